```python
import jax
import jax.numpy as jnp
from jax import lax
import numpy as np


D_MODEL = 2048
BATCH = 4
SEQ = 4096
DEPTH = 1

N_ATTN_HEADS = 8
ATTN_HEAD_DIM = 128
ATTN_WIDTH = N_ATTN_HEADS * ATTN_HEAD_DIM
MOBA_BLOCK = 256
MOBA_TOPK = 3
MOBA_Q_CHUNK = 16
RET_HEADS = 4
RET_HEAD_DIM = 256
RET_WIDTH = RET_HEADS * RET_HEAD_DIM
RET_CHUNK = 128
MIX_WIDTH = ATTN_WIDTH + RET_WIDTH
IN_WIDTH = 3 * ATTN_WIDTH + 4 * RET_WIDTH
N_GROUPS = 4
EXPERTS_PER_GROUP = 8
N_EXPERTS = N_GROUPS * EXPERTS_PER_GROUP
INNER_TOP_K = 2
D_EXPERT = 512
MOE_ROW_BLOCK = 128
N_MOD = 6
NORM_EPS = 1e-6
F32 = jnp.float32

kernel_name = 'hymba_moba_retnet_hmoe_block'


def rmsnorm(x, w):
    xf = x.astype(F32)
    y = xf * lax.rsqrt(jnp.mean(xf * xf, axis=-1, keepdims=True) + NORM_EPS)
    return (y * w.astype(F32)).astype(x.dtype)


def modulate(h, shift, scale):
    return h * (1 + scale[:, None, :]) + shift[:, None, :]


def split_heads(z, n_heads, head_dim):
    b, t, _ = z.shape
    return z.reshape(b, t, n_heads, head_dim).transpose(0, 2, 1, 3)


def merge_heads(z):
    b, h, t, dh = z.shape
    return z.transpose(0, 2, 1, 3).reshape(b, t, h * dh)


def moba_attention(q, k, v):
    b, h, t, dh = q.shape
    nb = -(-t // MOBA_BLOCK)
    tp = nb * MOBA_BLOCK
    pad = ((0, 0), (0, 0), (0, tp - t), (0, 0))
    k_p = jnp.pad(k, pad)
    v_p = jnp.pad(v, pad)
    kb = k_p.reshape(b, h, nb, MOBA_BLOCK, dh)
    vb = v_p.reshape(b, h, nb, MOBA_BLOCK, dh)
    k_mean = jnp.mean(kb.astype(F32), axis=3)
    q_block = jnp.arange(t) // MOBA_BLOCK
    past = jnp.arange(nb)[None, :] < q_block[:, None]
    gate = jnp.einsum('bhtd,bhnd->bhtn', q.astype(F32), k_mean)
    gate = jnp.where(past[None, None], gate, -jnp.inf)
    n_sel = min(MOBA_TOPK, nb)
    gate_top, sel = lax.top_k(gate, n_sel)
    valid = jnp.isfinite(gate_top)
    slopes = jnp.exp2(-8.0 * jnp.arange(1, h + 1, dtype=F32) / h)
    scale = dh ** -0.5
    b_idx = jnp.arange(b)[:, None, None, None]
    h_idx = jnp.arange(h)[None, :, None, None]
    offs = jnp.arange(MOBA_BLOCK)
    n_keys_sel = n_sel * MOBA_BLOCK

    def chunk(ci):
        t0 = ci * MOBA_Q_CHUNK
        qc = lax.dynamic_slice_in_dim(q, t0, MOBA_Q_CHUNK, axis=2)
        sel_c = lax.dynamic_slice_in_dim(sel, t0, MOBA_Q_CHUNK, axis=2)
        valid_c = lax.dynamic_slice_in_dim(valid, t0, MOBA_Q_CHUNK, axis=2)
        tq_i = t0 + jnp.arange(MOBA_Q_CHUNK)
        tq = tq_i.astype(F32)
        k_sel = kb[b_idx, h_idx, sel_c]
        v_sel = vb[b_idx, h_idx, sel_c]
        s_sel = jnp.einsum('bhqd,bhqjsd->bhqjs', qc, k_sel).astype(F32) * scale
        pos_sel = (sel_c[..., None] * MOBA_BLOCK + offs).astype(F32)
        s_sel = s_sel - slopes[None, :, None, None, None] * (tq[None, None, :, None, None] - pos_sel)
        s_sel = jnp.where(valid_c[..., None], s_sel, -jnp.inf).reshape(b, h, MOBA_Q_CHUNK, n_keys_sel)
        b0 = (t0 // MOBA_BLOCK) * MOBA_BLOCK
        k_own = lax.dynamic_slice_in_dim(k_p, b0, MOBA_BLOCK, axis=2)
        v_own = lax.dynamic_slice_in_dim(v_p, b0, MOBA_BLOCK, axis=2)
        pos_own = b0 + offs
        s_own = jnp.einsum('bhqd,bhsd->bhqs', qc, k_own).astype(F32) * scale
        s_own = s_own - slopes[None, :, None, None] * (tq[:, None] - pos_own.astype(F32)[None, :])[None, None]
        causal = pos_own[None, :] <= tq_i[:, None]
        s_own = jnp.where(causal[None, None], s_own, -jnp.inf)
        logits = jnp.concatenate([s_sel, s_own], axis=-1)
        p = jax.nn.softmax(logits, axis=-1).astype(v.dtype)
        out = jnp.einsum('bhqs,bhqsd->bhqd', p[..., :n_keys_sel],
                         v_sel.reshape(b, h, MOBA_Q_CHUNK, n_keys_sel, dh))
        out = out + jnp.einsum('bhqs,bhsd->bhqd', p[..., n_keys_sel:], v_own)
        return out

    out = lax.map(chunk, jnp.arange(t // MOBA_Q_CHUNK))
    return out.transpose(1, 2, 0, 3, 4).reshape(b, h, t, dh)


def multiscale_retention(q, k, v):
    b, h, t, dk = q.shape
    dv = v.shape[-1]
    n = t // RET_CHUNK
    dt = q.dtype
    log_gamma = jnp.log1p(-jnp.exp2(-5.0 - jnp.arange(h, dtype=F32)))
    pos = jnp.arange(RET_CHUNK, dtype=F32)
    diff = pos[:, None] - pos[None, :]
    decay_mask = jnp.where(diff >= 0, jnp.exp(jnp.maximum(diff, 0.0) * log_gamma[:, None, None]), 0.0).astype(dt)
    k_decay = jnp.exp((RET_CHUNK - 1 - pos) * log_gamma[:, None]).astype(dt)
    q_decay = jnp.exp((pos + 1) * log_gamma[:, None]).astype(dt)
    chunk_decay = jnp.exp(RET_CHUNK * log_gamma)
    qc = q.reshape(b, h, n, RET_CHUNK, dk)
    kc = (k * (dk ** -0.5)).reshape(b, h, n, RET_CHUNK, dk)
    vc = v.reshape(b, h, n, RET_CHUNK, dv)
    scores = jnp.einsum('bhnid,bhnjd->bhnij', qc, kc) * decay_mask[None, :, None]
    inner = jnp.einsum('bhnij,bhnje->bhnie', scores, vc)
    chunk_kv = jnp.einsum('bhnjd,bhnje->nbhde', kc * k_decay[None, :, None, :, None], vc).astype(F32)

    def step(state, kv_n):
        return state * chunk_decay[None, :, None, None] + kv_n, state

    _, state_prev = lax.scan(step, jnp.zeros((b, h, dk, dv), F32), chunk_kv)
    cross = jnp.einsum('bhnid,nbhde->bhnie', qc, state_prev.astype(dt)) * q_decay[None, :, None, :, None]
    return (inner + cross).reshape(b, h, t, dv)


def grouped_expert_ffn(hf, expert_id, weights, w1, w3, w2):
    n, d = hf.shape
    eid = expert_id.reshape(-1)
    tok = jnp.repeat(jnp.arange(n, dtype=jnp.int32), expert_id.shape[1])
    wts = weights.reshape(-1)
    a = eid.shape[0]
    p_rows = a + N_EXPERTS * MOE_ROW_BLOCK
    n_blk = p_rows // MOE_ROW_BLOCK
    order = jnp.argsort(eid)
    eid_s = eid[order]
    counts = jnp.bincount(eid, length=N_EXPERTS)
    starts = jnp.cumsum(counts) - counts
    padded = (counts + MOE_ROW_BLOCK - 1) // MOE_ROW_BLOCK * MOE_ROW_BLOCK
    pad_end = jnp.cumsum(padded)
    pad_start = pad_end - padded
    dest = pad_start[eid_s] + jnp.arange(a) - starts[eid_s]
    row_tok = jnp.zeros((p_rows,), jnp.int32).at[dest].set(tok[order])
    row_w = jnp.zeros((p_rows,), wts.dtype).at[dest].set(wts[order])
    blk_expert = jnp.minimum(jnp.searchsorted(pad_end, jnp.arange(n_blk) * MOE_ROW_BLOCK, side='right'), N_EXPERTS - 1)

    def block_ffn(args):
        rows, e = args
        xb = hf[rows]
        return (jax.nn.silu(xb @ w1[e]) * (xb @ w3[e])) @ w2[e]

    y_rows = lax.map(block_ffn, (row_tok.reshape(n_blk, MOE_ROW_BLOCK), blk_expert))
    y_rows = y_rows.reshape(p_rows, d) * row_w[:, None]
    return jax.ops.segment_sum(y_rows, row_tok, num_segments=n)


def hier_moe(h, w_group, b_group, w_router, b_router, w1, w3, w2):
    b, t, d = h.shape
    n = b * t
    hf = h.reshape(n, d)
    g_prob = jax.nn.softmax((hf @ w_group + b_group).astype(F32), axis=-1)
    g_sel = jnp.argmax(g_prob, axis=-1)
    g_w = jnp.take_along_axis(g_prob, g_sel[:, None], axis=1)
    e_logits = jnp.einsum('nd,gde->nge', hf, w_router) + b_router
    e_logits = jnp.take_along_axis(e_logits, g_sel[:, None, None], axis=1)[:, 0]
    e_prob = jax.nn.softmax(e_logits.astype(F32), axis=-1)
    top_p, top_e = lax.top_k(e_prob, INNER_TOP_K)
    weights = g_w * top_p / jnp.sum(top_p, axis=-1, keepdims=True)
    expert_id = g_sel[:, None] * EXPERTS_PER_GROUP + top_e
    y = grouped_expert_ffn(hf, expert_id, weights.astype(h.dtype), w1, w3, w2)
    return y.reshape(b, t, d)


def setup_inputs(seed: int = 0) -> dict:
    key = jax.random.key(seed)
    ks = jax.random.split(key, 17)
    nrm = jax.random.normal
    return {
        'x': nrm(ks[0], (BATCH, SEQ, D_MODEL), F32),
        'c': nrm(ks[1], (BATCH, D_MODEL), F32),
        'w_ada': nrm(ks[2], (DEPTH, D_MODEL, N_MOD * D_MODEL), F32) * D_MODEL ** -0.5,
        'b_ada': nrm(ks[3], (DEPTH, N_MOD * D_MODEL), F32) * 0.01,
        'norm_mix_w': 1.0 + 0.02 * nrm(ks[4], (DEPTH, D_MODEL), F32),
        'w_in': nrm(ks[5], (DEPTH, D_MODEL, IN_WIDTH), F32) * D_MODEL ** -0.5,
        'ret_norm_w': 1.0 + 0.02 * nrm(ks[6], (DEPTH, RET_WIDTH), F32),
        'w_out': nrm(ks[7], (DEPTH, MIX_WIDTH, D_MODEL), F32) * MIX_WIDTH ** -0.5,
        'norm_ffn_w': 1.0 + 0.02 * nrm(ks[8], (DEPTH, D_MODEL), F32),
        'w_group': nrm(ks[9], (DEPTH, D_MODEL, N_GROUPS), F32) * D_MODEL ** -0.5,
        'b_group': nrm(ks[10], (DEPTH, N_GROUPS), F32) * 0.01,
        'w_router': nrm(ks[11], (DEPTH, N_GROUPS, D_MODEL, EXPERTS_PER_GROUP), F32) * D_MODEL ** -0.5,
        'b_router': nrm(ks[12], (DEPTH, N_GROUPS, EXPERTS_PER_GROUP), F32) * 0.01,
        'w1': nrm(ks[13], (DEPTH, N_EXPERTS, D_MODEL, D_EXPERT), F32) * D_MODEL ** -0.5,
        'w3': nrm(ks[14], (DEPTH, N_EXPERTS, D_MODEL, D_EXPERT), F32) * D_MODEL ** -0.5,
        'w2': nrm(ks[15], (DEPTH, N_EXPERTS, D_EXPERT, D_MODEL), F32) * D_EXPERT ** -0.5,
        'norm_final_w': 1.0 + 0.02 * nrm(ks[16], (D_MODEL,), F32),
    }


def reference(x, c, w_ada, b_ada, norm_mix_w, w_in, ret_norm_w, w_out, norm_ffn_w,
              w_group, b_group, w_router, b_router, w1, w3, w2, norm_final_w):
    b, t, d = x.shape
    c_act = jax.nn.silu(c)
    splits = [ATTN_WIDTH, 2 * ATTN_WIDTH, 3 * ATTN_WIDTH,
              3 * ATTN_WIDTH + RET_WIDTH, 3 * ATTN_WIDTH + 2 * RET_WIDTH, 3 * ATTN_WIDTH + 3 * RET_WIDTH]
    for l in range(DEPTH):
        mod = c_act @ w_ada[l] + b_ada[l]
        shift_m, scale_m, gate_m, shift_f, scale_f, gate_f = jnp.split(mod, N_MOD, axis=-1)
        h = modulate(rmsnorm(x, norm_mix_w[l]), shift_m, scale_m)
        proj = h @ w_in[l]
        qa, ka, va, qr, kr, vr, gr = jnp.split(proj, splits, axis=-1)
        attn = moba_attention(split_heads(qa, N_ATTN_HEADS, ATTN_HEAD_DIM),
                              split_heads(ka, N_ATTN_HEADS, ATTN_HEAD_DIM),
                              split_heads(va, N_ATTN_HEADS, ATTN_HEAD_DIM))
        ret = multiscale_retention(split_heads(qr, RET_HEADS, RET_HEAD_DIM),
                                   split_heads(kr, RET_HEADS, RET_HEAD_DIM),
                                   split_heads(vr, RET_HEADS, RET_HEAD_DIM))
        rf = ret.astype(F32)
        rf = rf * lax.rsqrt(jnp.mean(rf * rf, axis=-1, keepdims=True) + NORM_EPS)
        ret_out = merge_heads(rf.astype(x.dtype)) * ret_norm_w[l] * jax.nn.silu(gr)
        mix = jnp.concatenate([merge_heads(attn), ret_out.astype(x.dtype)], axis=-1) @ w_out[l]
        x = x + gate_m[:, None, :] * mix
        h2 = modulate(rmsnorm(x, norm_ffn_w[l]), shift_f, scale_f)
        x = x + gate_f[:, None, :] * hier_moe(h2, w_group[l], b_group[l], w_router[l], b_router[l],
                                             w1[l], w3[l], w2[l])
    return rmsnorm(x, norm_final_w)
```

```python
import functools

import jax
import jax.numpy as jnp
from jax import lax
from jax.experimental import pallas as pl
from jax.experimental.pallas import tpu as pltpu

F32 = jnp.float32
BF16 = jnp.bfloat16
I32 = jnp.int32

N_ATTN_HEADS = 8
ATTN_HEAD_DIM = 128
ATTN_WIDTH = N_ATTN_HEADS * ATTN_HEAD_DIM
MOBA_BLOCK = 256
MOBA_TOPK = 3
RET_HEADS = 4
RET_HEAD_DIM = 256
RET_WIDTH = RET_HEADS * RET_HEAD_DIM
N_GROUPS = 4
EXPERTS_PER_GROUP = 8
N_EXPERTS = N_GROUPS * EXPERTS_PER_GROUP
N_MOD = 6
NORM_EPS = 1e-6

LANES = 128
RET_CHUNK = 256
FFN_TILE = 256
MASK_VALUE = -1e30
VMEM_LIMIT = 56 * 1024 * 1024

_DN_LAST = (((1,), (1,)), ((), ()))
_DN_FIRST = (((0,), (0,)), ((), ()))


def _cparams(sem):
    return pltpu.CompilerParams(dimension_semantics=sem, vmem_limit_bytes=VMEM_LIMIT)


def _silu(v):
    return v * jax.nn.sigmoid(v)


def _adaln_kernel(c_ref, w_ref, b_ref, o_ref):
    ca = _silu(c_ref[...]).astype(BF16)
    o_ref[...] = jnp.dot(ca, w_ref[...].astype(BF16), preferred_element_type=F32) + b_ref[...]


def _adaln(c_pad, w_ada, b_ada, tn=1024):
    rows, d = c_pad.shape
    n = w_ada.shape[1]
    return pl.pallas_call(
        _adaln_kernel,
        grid=(n // tn,),
        in_specs=[pl.BlockSpec((rows, d), lambda j: (0, 0)),
                  pl.BlockSpec((d, tn), lambda j: (0, j)),
                  pl.BlockSpec((1, tn), lambda j: (0, j))],
        out_specs=pl.BlockSpec((rows, tn), lambda j: (0, j)),
        out_shape=jax.ShapeDtypeStruct((rows, n), F32),
        compiler_params=_cparams(("arbitrary",)),
        name="adaln",
    )(c_pad, w_ada, b_ada)


def _inproj_kernel(x_ref, mod_ref, nw_ref, w_ref, o_ref, km_ref, h_ref, *, tm, tn):
    j = pl.program_id(1)

    @pl.when(j == 0)
    def _():
        x = x_ref[...]
        y = x * lax.rsqrt(jnp.mean(x * x, axis=-1, keepdims=True) + NORM_EPS) * nw_ref[...]
        m = mod_ref[0]
        h_ref[...] = (y * (1.0 + m[1:2]) + m[0:1]).astype(BF16)

    acc = jnp.dot(h_ref[...], w_ref[...], preferred_element_type=F32)
    o_ref[...] = acc.astype(BF16)

    @pl.when(j == 1)
    def _():
        km_ref[0] = jnp.sum(acc.reshape(tm // MOBA_BLOCK, MOBA_BLOCK, tn), axis=1) * (1.0 / MOBA_BLOCK)


def _inproj(x2d, mod3, nw, w_bf, seq, tm, tn=ATTN_WIDTH):
    n, d = x2d.shape
    width = w_bf.shape[1]
    per_b = seq // tm
    kern = functools.partial(_inproj_kernel, tm=tm, tn=tn)
    return pl.pallas_call(
        kern,
        grid=(n // tm, width // tn),
        in_specs=[pl.BlockSpec((tm, d), lambda i, j: (i, 0)),
                  pl.BlockSpec((1, N_MOD, d), lambda i, j: (i // per_b, 0, 0)),
                  pl.BlockSpec((1, d), lambda i, j: (0, 0)),
                  pl.BlockSpec((d, tn), lambda i, j: (0, j))],
        out_specs=[pl.BlockSpec((tm, tn), lambda i, j: (i, j)),
                   pl.BlockSpec((1, tm // MOBA_BLOCK, tn), lambda i, j: (i, 0, 0))],
        out_shape=[jax.ShapeDtypeStruct((n, width), BF16),
                   jax.ShapeDtypeStruct((n // tm, tm // MOBA_BLOCK, tn), F32)],
        scratch_shapes=[pltpu.VMEM((tm, d), BF16)],
        compiler_params=_cparams(("arbitrary", "arbitrary")),
        name="inproj",
    )(x2d, mod3, nw, w_bf)


def _moba_kernel(slopes_ref, q_ref, k_ref, v_ref, km_ref, o_ref, vt_ref, sel_ref, *, nb):
    h = pl.program_id(1)
    i = pl.program_id(2)
    blk = MOBA_BLOCK

    @pl.when(i == 0)
    def _():
        vt_ref[...] = v_ref[...].astype(F32).T.astype(BF16)

    slope = slopes_ref[h]
    scale = ATTN_HEAD_DIM ** -0.5
    q = q_ref[...]

    km = km_ref[0]
    km_hi = km.astype(BF16)
    km_lo = (km - km_hi.astype(F32)).astype(BF16)
    gate = (lax.dot_general(km_hi, q, _DN_LAST, preferred_element_type=F32)
            + lax.dot_general(km_lo, q, _DN_LAST, preferred_element_type=F32))
    bidx = lax.broadcasted_iota(I32, (nb, blk), 0)
    past = bidx < i
    gate = jnp.where(past, gate, -jnp.inf)
    rank = jnp.zeros((nb, blk), F32)
    for m in range(nb):
        gm = gate[m:m + 1, :]
        beats = jnp.where(gm > gate, 1.0, jnp.where((gm == gate) & (m < bidx), 1.0, 0.0))
        rank = rank + jnp.where(m < i, beats, 0.0)
    sel_ref[...] = jnp.where(past & (rank < MOBA_TOPK), 1.0, 0.0)

    qpos = lax.broadcasted_iota(I32, (blk, blk), 1)
    kpos = lax.broadcasted_iota(I32, (blk, blk), 0)
    dist = (qpos - kpos).astype(F32)

    i0 = pl.multiple_of(i * blk, blk)
    s = lax.dot_general(k_ref[pl.ds(i0, blk), :], q, _DN_LAST, preferred_element_type=F32) * scale
    s = jnp.where(qpos >= kpos, s - slope * dist, MASK_VALUE)
    m0 = jnp.max(s, axis=0, keepdims=True)
    p = jnp.exp(s - m0)
    l0 = jnp.sum(p, axis=0, keepdims=True)
    acc0 = jnp.dot(vt_ref[:, pl.ds(i0, blk)], p.astype(BF16), preferred_element_type=F32)

    def body(j, carry):
        m_run, l_run, acc = carry
        j0 = pl.multiple_of(j * blk, blk)
        sj = lax.dot_general(k_ref[pl.ds(j0, blk), :], q, _DN_LAST, preferred_element_type=F32) * scale
        off = ((i - j) * blk).astype(F32)
        sj = sj - slope * (dist + off)
        sj = jnp.where(sel_ref[pl.ds(j, 1), :] > 0.0, sj, MASK_VALUE)
        m_new = jnp.maximum(m_run, jnp.max(sj, axis=0, keepdims=True))
        alpha = jnp.exp(m_run - m_new)
        pj = jnp.exp(sj - m_new)
        l_new = alpha * l_run + jnp.sum(pj, axis=0, keepdims=True)
        acc_new = alpha * acc + jnp.dot(vt_ref[:, pl.ds(j0, blk)], pj.astype(BF16),
                                        preferred_element_type=F32)
        return m_new, l_new, acc_new

    _, l_fin, acc = lax.fori_loop(0, i, body, (m0, l0, acc0))
    o_ref[...] = (acc / l_fin).T.astype(BF16)


def _moba(proj, kmean, slopes, batch, seq):
    n = proj.shape[0]
    nb = seq // MOBA_BLOCK
    dh = ATTN_HEAD_DIM
    kern = functools.partial(_moba_kernel, nb=nb)
    return pl.pallas_call(
        kern,
        grid=(batch, N_ATTN_HEADS, nb),
        in_specs=[pl.BlockSpec(memory_space=pltpu.SMEM),
                  pl.BlockSpec((MOBA_BLOCK, dh), lambda b, h, i: (b * nb + i, h)),
                  pl.BlockSpec((seq, dh), lambda b, h, i: (b, N_ATTN_HEADS + h)),
                  pl.BlockSpec((seq, dh), lambda b, h, i: (b, 2 * N_ATTN_HEADS + h)),
                  pl.BlockSpec((1, nb, dh), lambda b, h, i: (b, 0, h))],
        out_specs=pl.BlockSpec((MOBA_BLOCK, dh), lambda b, h, i: (b * nb + i, h)),
        out_shape=jax.ShapeDtypeStruct((n, ATTN_WIDTH), BF16),
        scratch_shapes=[pltpu.VMEM((dh, seq), BF16), pltpu.VMEM((nb, MOBA_BLOCK), F32)],
        compiler_params=_cparams(("arbitrary", "arbitrary", "arbitrary")),
        name="moba",
    )(slopes, proj, proj, proj, kmean)


def _ret_kernel(lg_ref, q_ref, k_ref, v_ref, g_ref, rnw_ref, o_ref,
                state_ref, dmask_ref, qdec_ref, kdec_ref, cdec_ref):
    h = pl.program_id(1)
    n = pl.program_id(2)
    c = RET_CHUNK
    dk = RET_HEAD_DIM
    kscale = dk ** -0.5

    @pl.when(n == 0)
    def _():
        lg = lg_ref[h]
        state_ref[...] = jnp.zeros_like(state_ref)
        ti = lax.broadcasted_iota(I32, (c, c), 0)
        si = lax.broadcasted_iota(I32, (c, c), 1)
        diff = (ti - si).astype(F32)
        dmask_ref[...] = jnp.where(ti >= si, jnp.exp(jnp.maximum(diff, 0.0) * lg), 0.0)
        pos = lax.broadcasted_iota(I32, (c, dk), 0).astype(F32)
        qdec_ref[...] = jnp.exp((pos + 1.0) * lg)
        kdec_ref[...] = jnp.exp((c - 1.0 - pos) * lg)
        cdec_ref[...] = jnp.exp(jnp.full((1, dk), float(c), F32) * lg)

    q = q_ref[...]
    k = k_ref[...]
    v = v_ref[...]
    scores = lax.dot_general(q, k, _DN_LAST, preferred_element_type=F32) * kscale * dmask_ref[...]
    inner = jnp.dot(scores.astype(BF16), v, preferred_element_type=F32)
    state = state_ref[...]
    cross = jnp.dot(q, state.astype(BF16), preferred_element_type=F32) * qdec_ref[...]
    kd = (k.astype(F32) * kscale * kdec_ref[...]).astype(BF16)
    kv = lax.dot_general(kd, v, _DN_FIRST, preferred_element_type=F32)
    state_ref[...] = state * cdec_ref[...] + kv

    r = inner + cross
    rn = r * lax.rsqrt(jnp.mean(r * r, axis=-1, keepdims=True) + NORM_EPS)
    g = g_ref[...].astype(F32)
    o_ref[...] = (rn * rnw_ref[...] * _silu(g)).astype(BF16)


def _retention(proj, log_gamma, rnw, batch, seq):
    n = proj.shape[0]
    c = RET_CHUNK
    nc = seq // c
    dk = RET_HEAD_DIM
    base = 3 * ATTN_WIDTH // dk

    def col(off):
        return lambda b, h, t: (b * nc + t, base + off * RET_HEADS + h)

    return pl.pallas_call(
        _ret_kernel,
        grid=(batch, RET_HEADS, nc),
        in_specs=[pl.BlockSpec(memory_space=pltpu.SMEM),
                  pl.BlockSpec((c, dk), col(0)),
                  pl.BlockSpec((c, dk), col(1)),
                  pl.BlockSpec((c, dk), col(2)),
                  pl.BlockSpec((c, dk), col(3)),
                  pl.BlockSpec((1, dk), lambda b, h, t: (0, h))],
        out_specs=pl.BlockSpec((c, dk), lambda b, h, t: (b * nc + t, h)),
        out_shape=jax.ShapeDtypeStruct((n, RET_WIDTH), BF16),
        scratch_shapes=[pltpu.VMEM((dk, dk), F32), pltpu.VMEM((c, c), F32),
                        pltpu.VMEM((c, dk), F32), pltpu.VMEM((c, dk), F32),
                        pltpu.VMEM((1, dk), F32)],
        compiler_params=_cparams(("arbitrary", "arbitrary", "arbitrary")),
        name="retention",
    )(log_gamma, proj, proj, proj, proj, rnw)


def _outproj_kernel(a_ref, r_ref, w_ref, x_ref, mod_ref, nw_ref, wr_ref, br_ref,
                    x1_ref, hp_ref, lg_ref):
    mix = jnp.dot(jnp.concatenate([a_ref[...], r_ref[...]], axis=1), w_ref[...],
                  preferred_element_type=F32)
    m = mod_ref[0]
    x1 = x_ref[...] + m[2:3] * mix
    x1_ref[...] = x1
    y = x1 * lax.rsqrt(jnp.mean(x1 * x1, axis=-1, keepdims=True) + NORM_EPS) * nw_ref[...]
    h2 = y * (1.0 + m[4:5]) + m[3:4]
    hp_ref[...] = h2
    hi = h2.astype(BF16)
    lo = (h2 - hi.astype(F32)).astype(BF16)
    part = jnp.dot(jnp.concatenate([hi, lo], axis=1), wr_ref[...], preferred_element_type=F32)
    lg_ref[...] = part + pltpu.roll(part, LANES // 2, axis=1) + br_ref[...]


def _outproj(attn, ret, w_bf, x2d, mod3, nw, wr_cat, br, seq, tm):
    n, d = x2d.shape
    per_b = seq // tm
    return pl.pallas_call(
        _outproj_kernel,
        grid=(n // tm,),
        in_specs=[pl.BlockSpec((tm, ATTN_WIDTH), lambda i: (i, 0)),
                  pl.BlockSpec((tm, RET_WIDTH), lambda i: (i, 0)),
                  pl.BlockSpec((d, d), lambda i: (0, 0)),
                  pl.BlockSpec((tm, d), lambda i: (i, 0)),
                  pl.BlockSpec((1, N_MOD, d), lambda i: (i // per_b, 0, 0)),
                  pl.BlockSpec((1, d), lambda i: (0, 0)),
                  pl.BlockSpec((2 * d, LANES), lambda i: (0, 0)),
                  pl.BlockSpec((1, LANES), lambda i: (0, 0))],
        out_specs=[pl.BlockSpec((tm, d), lambda i: (i, 0)),
                   pl.BlockSpec((tm, d), lambda i: (i, 0)),
                   pl.BlockSpec((tm, LANES), lambda i: (i, 0))],
        out_shape=[jax.ShapeDtypeStruct((n, d), F32),
                   jax.ShapeDtypeStruct((n, d), F32),
                   jax.ShapeDtypeStruct((n, LANES), F32)],
        compiler_params=_cparams(("arbitrary",)),
        name="outproj",
    )(attn, ret, w_bf, x2d, mod3, nw, wr_cat, br)


_L_E0, _L_E1, _L_W0, _L_W1, _L_D0, _L_D1 = 0, 1, 2, 3, 4, 5
_PLAN_ROWS = 256


def _first_lane_where(cond, lane):
    return jnp.min(jnp.where(cond, lane, float(LANES)), axis=1, keepdims=True)


def _plan_kernel(lg_ref, info_ref, meta_ref, *, n_tok, n_tiles):
    tb = _PLAN_ROWS
    lane = lax.broadcasted_iota(I32, (tb, LANES), 1).astype(F32)
    row = lax.broadcasted_iota(I32, (tb, tb), 0)
    colm = lax.broadcasted_iota(I32, (tb, tb), 1)
    tri = jnp.where(colm < row, 1.0, 0.0).astype(BF16)
    n_g, n_e = N_GROUPS, EXPERTS_PER_GROUP

    def route(blk_i, carry):
        r0 = pl.multiple_of(blk_i * tb, tb)
        lg = lg_ref[pl.ds(r0, tb), :]
        gl = jnp.where(lane < n_g, lg, -jnp.inf)
        ge = jnp.exp(gl - jnp.max(gl, axis=1, keepdims=True))
        gp = ge / jnp.sum(ge, axis=1, keepdims=True)
        g_w = jnp.max(gp, axis=1, keepdims=True)
        g_sel = _first_lane_where((gp == g_w) & (lane < n_g), lane)
        lo_lane = n_g + n_e * g_sel
        in_grp = (lane >= lo_lane) & (lane < lo_lane + n_e)
        el = jnp.where(in_grp, lg, -jnp.inf)
        ee = jnp.exp(el - jnp.max(el, axis=1, keepdims=True))
        ep = jnp.where(in_grp, ee / jnp.sum(ee, axis=1, keepdims=True), -1.0)
        p1 = jnp.max(ep, axis=1, keepdims=True)
        i1 = _first_lane_where(ep == p1, lane)
        ep2 = jnp.where(lane == i1, -1.0, ep)
        p2 = jnp.max(ep2, axis=1, keepdims=True)
        i2 = _first_lane_where(ep2 == p2, lane)
        denom = p1 + p2
        w0 = g_w * p1 / denom
        w1 = g_w * p2 / denom
        e0 = i1 - n_g
        e1 = i2 - n_g
        oh0 = jnp.where(lane == e0, 1.0, 0.0)
        oh1 = jnp.where(lane == e1, 1.0, 0.0)
        oh = oh0 + oh1
        before = jnp.dot(tri, oh.astype(BF16), preferred_element_type=F32) + carry
        rank0 = jnp.sum(oh0 * before, axis=1, keepdims=True)
        rank1 = jnp.sum(oh1 * before, axis=1, keepdims=True)
        info = jnp.where(lane == _L_E0, e0,
               jnp.where(lane == _L_E1, e1,
               jnp.where(lane == _L_W0, w0,
               jnp.where(lane == _L_W1, w1,
               jnp.where(lane == _L_D0, rank0,
               jnp.where(lane == _L_D1, rank1, 0.0))))))
        info_ref[pl.ds(r0, tb), :] = info
        return carry + jnp.sum(oh, axis=0, keepdims=True)

    counts = lax.fori_loop(0, n_tok // tb, route, jnp.zeros((1, LANES), F32))

    lane1 = lax.broadcasted_iota(I32, (1, LANES), 1)
    padded = jnp.floor((counts + (FFN_TILE - 1.0)) * (1.0 / FFN_TILE)) * FFN_TILE
    pad_end = padded
    sh = 1
    while sh < N_EXPERTS:
        pad_end = pad_end + jnp.where(lane1 >= sh, pltpu.roll(pad_end, sh, axis=1), 0.0)
        sh *= 2
    pad_start = pad_end - padded

    def place(blk_i, _):
        r0 = pl.multiple_of(blk_i * tb, tb)
        info = info_ref[pl.ds(r0, tb), :]
        e0 = info[:, _L_E0:_L_E0 + 1]
        e1 = info[:, _L_E1:_L_E1 + 1]
        s0 = jnp.sum(jnp.where(lane == e0, pad_start, 0.0), axis=1, keepdims=True)
        s1 = jnp.sum(jnp.where(lane == e1, pad_start, 0.0), axis=1, keepdims=True)
        add = jnp.where(lane == _L_D0, s0, jnp.where(lane == _L_D1, s1, 0.0))
        info_ref[pl.ds(r0, tb), :] = info + add
        return 0

    lax.fori_loop(0, n_tok // tb, place, 0)

    sq_r = lax.broadcasted_iota(I32, (LANES, LANES), 0)
    sq_c = lax.broadcasted_iota(I32, (LANES, LANES), 1)
    pe_col = jnp.sum(jnp.where(sq_r == sq_c, pad_end, 0.0), axis=1, keepdims=True)
    mt = meta_ref.shape[1]
    t_start = lax.broadcasted_iota(I32, (LANES, mt), 1).astype(F32) * FFN_TILE
    e_row = lax.broadcasted_iota(I32, (LANES, mt), 0)
    ended = jnp.where((pe_col <= t_start) & (e_row < N_EXPERTS), 1.0, 0.0)
    tile_e = jnp.minimum(jnp.sum(ended, axis=0, keepdims=True), N_EXPERTS - 1.0)
    total = jnp.sum(jnp.where(lane1 == N_EXPERTS - 1, pad_end, 0.0), axis=1, keepdims=True)
    n_valid = jnp.broadcast_to(total * (1.0 / FFN_TILE), (1, mt))
    mrow = lax.broadcasted_iota(I32, (8, mt), 0)
    meta = jnp.where(mrow == 0, tile_e, jnp.where(mrow == 1, n_valid, 0.0))
    meta_ref[...] = meta.astype(I32)


def _plan(logits, n_tiles):
    n_tok = logits.shape[0]
    mt = -(-n_tiles // LANES) * LANES
    kern = functools.partial(_plan_kernel, n_tok=n_tok, n_tiles=n_tiles)
    return pl.pallas_call(
        kern,
        out_shape=[jax.ShapeDtypeStruct((n_tok, LANES), F32),
                   jax.ShapeDtypeStruct((8, mt), I32)],
        compiler_params=pltpu.CompilerParams(vmem_limit_bytes=VMEM_LIMIT),
        name="plan",
    )(logits)


_DMA_UNROLL = 8


def _dispatch_kernel(d0_ref, d1_ref, h_ref, xs_in_ref, xs_ref, sem, *, td):
    del xs_in_ref
    base = pl.program_id(0) * td

    def row_copy(r, dst_row):
        return pltpu.make_async_copy(h_ref.at[pl.ds(r, 1)], xs_ref.at[pl.ds(dst_row, 1)], sem)

    def body(u, _):
        for s in range(_DMA_UNROLL):
            r = u * _DMA_UNROLL + s
            row_copy(r, d0_ref[base + r]).start()
            row_copy(r, d1_ref[base + r]).start()
        return 0

    lax.fori_loop(0, td // _DMA_UNROLL, body, 0)
    for _ in range(2):
        pltpu.make_async_copy(h_ref, xs_ref.at[pl.ds(0, td)], sem).wait()


def _dispatch(d0, d1, h2p, xs_zero, td=256):
    n, w = h2p.shape
    kern = functools.partial(_dispatch_kernel, td=td)
    return pl.pallas_call(
        kern,
        grid_spec=pltpu.PrefetchScalarGridSpec(
            num_scalar_prefetch=2,
            grid=(n // td,),
            in_specs=[pl.BlockSpec((td, w), lambda i, d0, d1: (i, 0)),
                      pl.BlockSpec(memory_space=pl.ANY)],
            out_specs=pl.BlockSpec(memory_space=pl.ANY),
            scratch_shapes=[pltpu.SemaphoreType.DMA(())]),
        out_shape=jax.ShapeDtypeStruct(xs_zero.shape, xs_zero.dtype),
        input_output_aliases={3: 0},
        compiler_params=_cparams(("arbitrary",)),
        name="dispatch",
    )(d0, d1, h2p, xs_zero)


def _ffn_kernel(te_ref, nv_ref, xs_ref, w1_ref, w3_ref, w2_ref, y_ref):
    valid = pl.program_id(0) < nv_ref[0]

    @pl.when(valid)
    def _():
        x = xs_ref[...].astype(BF16)
        a = jnp.dot(x, w1_ref[0], preferred_element_type=F32)
        b = jnp.dot(x, w3_ref[0], preferred_element_type=F32)
        hmid = (_silu(a) * b).astype(BF16)
        y_ref[...] = jnp.dot(hmid, w2_ref[0], preferred_element_type=F32)

    @pl.when(jnp.logical_not(valid))
    def _():
        y_ref[...] = jnp.zeros_like(y_ref)


def _ffn(tile_e, n_valid, xs, w1, w3, w2):
    p_rows, w = xs.shape
    _, d, de = w1.shape
    n_tiles = p_rows // FFN_TILE

    def tile(i, te, nv):
        return jnp.minimum(i, nv[0] - 1)

    return pl.pallas_call(
        _ffn_kernel,
        grid_spec=pltpu.PrefetchScalarGridSpec(
            num_scalar_prefetch=2,
            grid=(n_tiles,),
            in_specs=[pl.BlockSpec((FFN_TILE, w), lambda i, te, nv: (tile(i, te, nv), 0)),
                      pl.BlockSpec((1, d, de), lambda i, te, nv: (te[tile(i, te, nv)], 0, 0)),
                      pl.BlockSpec((1, d, de), lambda i, te, nv: (te[tile(i, te, nv)], 0, 0)),
                      pl.BlockSpec((1, de, d), lambda i, te, nv: (te[tile(i, te, nv)], 0, 0))],
            out_specs=pl.BlockSpec((FFN_TILE, d), lambda i, te, nv: (i, 0))),
        out_shape=jax.ShapeDtypeStruct((p_rows, d), F32),
        compiler_params=_cparams(("arbitrary",)),
        name="ffn",
    )(tile_e, n_valid, xs, w1, w3, w2)


def _combine_kernel(d0_ref, d1_ref, x1_ref, info_ref, mod_ref, nw_ref, y_ref, o_ref,
                    ya_ref, yb_ref, sem, *, tc):
    base = pl.program_id(0) * tc

    def body(u, _):
        for s in range(_DMA_UNROLL):
            r = u * _DMA_UNROLL + s
            pltpu.make_async_copy(y_ref.at[pl.ds(d0_ref[base + r], 1)], ya_ref.at[pl.ds(r, 1)], sem).start()
            pltpu.make_async_copy(y_ref.at[pl.ds(d1_ref[base + r], 1)], yb_ref.at[pl.ds(r, 1)], sem).start()
        return 0

    lax.fori_loop(0, tc // _DMA_UNROLL, body, 0)
    pltpu.make_async_copy(y_ref.at[pl.ds(0, tc)], ya_ref, sem).wait()
    pltpu.make_async_copy(y_ref.at[pl.ds(0, tc)], yb_ref, sem).wait()

    info = info_ref[...]
    w0 = info[:, _L_W0:_L_W0 + 1]
    w1 = info[:, _L_W1:_L_W1 + 1]
    moe = ya_ref[...] * w0 + yb_ref[...] * w1
    x2 = x1_ref[...] + mod_ref[0][5:6] * moe
    o_ref[...] = x2 * lax.rsqrt(jnp.mean(x2 * x2, axis=-1, keepdims=True) + NORM_EPS) * nw_ref[...]


def _combine(d0, d1, x1, info, mod3, nfw, y, seq, tc=256):
    n, d = x1.shape
    per_b = seq // tc
    kern = functools.partial(_combine_kernel, tc=tc)
    return pl.pallas_call(
        kern,
        grid_spec=pltpu.PrefetchScalarGridSpec(
            num_scalar_prefetch=2,
            grid=(n // tc,),
            in_specs=[pl.BlockSpec((tc, d), lambda i, d0, d1: (i, 0)),
                      pl.BlockSpec((tc, LANES), lambda i, d0, d1: (i, 0)),
                      pl.BlockSpec((1, N_MOD, d), lambda i, d0, d1: (i // per_b, 0, 0)),
                      pl.BlockSpec((1, d), lambda i, d0, d1: (0, 0)),
                      pl.BlockSpec(memory_space=pl.ANY)],
            out_specs=pl.BlockSpec((tc, d), lambda i, d0, d1: (i, 0)),
            scratch_shapes=[pltpu.VMEM((tc, d), F32), pltpu.VMEM((tc, d), F32),
                            pltpu.SemaphoreType.DMA(())]),
        out_shape=jax.ShapeDtypeStruct((n, d), F32),
        compiler_params=_cparams(("arbitrary",)),
        name="combine",
    )(d0, d1, x1, info, mod3, nfw, y)


def _split_hi_lo(w):
    hi = w.astype(BF16)
    lo = (w - hi.astype(F32)).astype(BF16)
    return hi, lo


def kernel(x, c, w_ada, b_ada, norm_mix_w, w_in, ret_norm_w, w_out, norm_ffn_w,
           w_group, b_group, w_router, b_router, w1, w3, w2, norm_final_w):
    batch, seq, d = x.shape
    n = batch * seq
    depth = w_ada.shape[0]
    assert depth == 1, "the final rmsnorm is fused into the (single) layer's last kernel"
    tm = min(1024, seq)
    half = LANES // 2

    slopes = jnp.exp2(-8.0 * jnp.arange(1, N_ATTN_HEADS + 1, dtype=F32) / N_ATTN_HEADS)
    log_gamma = jnp.log1p(-jnp.exp2(-5.0 - jnp.arange(RET_HEADS, dtype=F32)))
    n_tiles = (2 * n) // FFN_TILE + N_EXPERTS
    p_rows = n_tiles * FFN_TILE

    c_pad = jnp.zeros((8, d), F32).at[:batch].set(c)
    x2d = x.reshape(n, d)
    for l in range(depth):
        mod = _adaln(c_pad, w_ada[l], b_ada[l].reshape(1, -1))
        mod3 = mod[:batch].reshape(batch, N_MOD, d)

        proj, kmean = _inproj(x2d, mod3, norm_mix_w[l].reshape(1, d), w_in[l].astype(BF16), seq, tm)
        kmean = kmean.reshape(batch, seq // MOBA_BLOCK, ATTN_WIDTH)
        attn = _moba(proj, kmean, slopes, batch, seq)
        ret = _retention(proj, log_gamma, ret_norm_w[l].reshape(1, RET_WIDTH), batch, seq)

        wr = jnp.concatenate([w_group[l], jnp.transpose(w_router[l], (1, 0, 2)).reshape(d, N_EXPERTS)], axis=1)
        wr = jnp.pad(wr, ((0, 0), (0, half - wr.shape[1])))
        wr_hi, wr_lo = _split_hi_lo(wr)
        wr_cat = jnp.concatenate([jnp.concatenate([wr_hi, wr_lo], axis=1),
                                  jnp.concatenate([wr_hi, jnp.zeros_like(wr_lo)], axis=1)], axis=0)
        br = jnp.concatenate([b_group[l], b_router[l].reshape(-1)])
        br = jnp.pad(br, (0, LANES - br.shape[0])).reshape(1, LANES)

        x1, h2p, logits = _outproj(attn, ret, w_out[l].astype(BF16), x2d, mod3,
                                   norm_ffn_w[l].reshape(1, d), wr_cat, br, seq, min(512, seq))

        info, meta = _plan(logits, n_tiles)
        d0 = info[:, _L_D0].astype(I32)
        d1 = info[:, _L_D1].astype(I32)
        tile_e = meta[0, :n_tiles]
        n_valid = meta[1, :1]

        xs = _dispatch(d0, d1, h2p, jnp.zeros((p_rows, d), F32))
        y = _ffn(tile_e, n_valid, xs, w1[l].astype(BF16), w3[l].astype(BF16), w2[l].astype(BF16))
        nfw = norm_final_w.reshape(1, d)
        x2d = _combine(d0, d1, x1, info, mod3, nfw, y, seq)
    return x2d.reshape(batch, seq, d)
```

```python
import functools

import jax
import jax.numpy as jnp
from jax import lax
from jax.experimental import pallas as pl
from jax.experimental.pallas import tpu as pltpu

F32 = jnp.float32
BF16 = jnp.bfloat16
I32 = jnp.int32

N_ATTN_HEADS = 8
ATTN_HEAD_DIM = 128
ATTN_WIDTH = N_ATTN_HEADS * ATTN_HEAD_DIM
MOBA_BLOCK = 256
MOBA_TOPK = 3
RET_HEADS = 4
RET_HEAD_DIM = 256
RET_WIDTH = RET_HEADS * RET_HEAD_DIM
N_GROUPS = 4
EXPERTS_PER_GROUP = 8
N_EXPERTS = N_GROUPS * EXPERTS_PER_GROUP
N_MOD = 6
NORM_EPS = 1e-6

LANES = 128
RET_CHUNK = 256
FFN_TILE = 256
MASK_VALUE = -1e30
VMEM_LIMIT = 56 * 1024 * 1024

_DN_LAST = (((1,), (1,)), ((), ()))
_DN_FIRST = (((0,), (0,)), ((), ()))


def _cparams(sem):
    return pltpu.CompilerParams(dimension_semantics=sem, vmem_limit_bytes=VMEM_LIMIT)


def _silu(v):
    return v * jax.nn.sigmoid(v)


def _adaln_kernel(c_ref, w_ref, b_ref, o_ref):
    ca = _silu(c_ref[...]).astype(BF16)
    o_ref[...] = jnp.dot(ca, w_ref[...].astype(BF16), preferred_element_type=F32) + b_ref[...]


def _adaln(c_pad, w_ada, b_ada, tn=1024):
    rows, d = c_pad.shape
    n = w_ada.shape[1]
    return pl.pallas_call(
        _adaln_kernel,
        grid=(n // tn,),
        in_specs=[pl.BlockSpec((rows, d), lambda j: (0, 0)),
                  pl.BlockSpec((d, tn), lambda j: (0, j)),
                  pl.BlockSpec((1, tn), lambda j: (0, j))],
        out_specs=pl.BlockSpec((rows, tn), lambda j: (0, j)),
        out_shape=jax.ShapeDtypeStruct((rows, n), F32),
        compiler_params=_cparams(("arbitrary",)),
        name="adaln",
    )(c_pad, w_ada, b_ada)


def _inproj_kernel(x_ref, mod_ref, nw_ref, w_ref, o_ref, km_ref, h_ref, *, tm, tn):
    j = pl.program_id(1)

    @pl.when(j == 0)
    def _():
        x = x_ref[...]
        y = x * lax.rsqrt(jnp.mean(x * x, axis=-1, keepdims=True) + NORM_EPS) * nw_ref[...]
        m = mod_ref[0]
        h_ref[...] = (y * (1.0 + m[1:2]) + m[0:1]).astype(BF16)

    acc = jnp.dot(h_ref[...], w_ref[...], preferred_element_type=F32)
    o_ref[...] = acc.astype(BF16)

    @pl.when(j == 1)
    def _():
        km_ref[0] = jnp.sum(acc.reshape(tm // MOBA_BLOCK, MOBA_BLOCK, tn), axis=1) * (1.0 / MOBA_BLOCK)


def _inproj(x2d, mod3, nw, w_bf, seq, tm, tn=ATTN_WIDTH):
    n, d = x2d.shape
    width = w_bf.shape[1]
    per_b = seq // tm
    kern = functools.partial(_inproj_kernel, tm=tm, tn=tn)
    return pl.pallas_call(
        kern,
        grid=(n // tm, width // tn),
        in_specs=[pl.BlockSpec((tm, d), lambda i, j: (i, 0)),
                  pl.BlockSpec((1, N_MOD, d), lambda i, j: (i // per_b, 0, 0)),
                  pl.BlockSpec((1, d), lambda i, j: (0, 0)),
                  pl.BlockSpec((d, tn), lambda i, j: (0, j))],
        out_specs=[pl.BlockSpec((tm, tn), lambda i, j: (i, j)),
                   pl.BlockSpec((1, tm // MOBA_BLOCK, tn), lambda i, j: (i, 0, 0))],
        out_shape=[jax.ShapeDtypeStruct((n, width), BF16),
                   jax.ShapeDtypeStruct((n // tm, tm // MOBA_BLOCK, tn), F32)],
        scratch_shapes=[pltpu.VMEM((tm, d), BF16)],
        compiler_params=_cparams(("arbitrary", "arbitrary")),
        name="inproj",
    )(x2d, mod3, nw, w_bf)


def _moba_kernel(slopes_ref, q_ref, k_ref, v_ref, km_ref, o_ref,
                 vt_ref, sel_ref, bias_ref, acc_ref, *, nb, hps):
    hg = pl.program_id(1)
    i = pl.program_id(2)
    blk = MOBA_BLOCK
    dh = ATTN_HEAD_DIM
    log2e = 1.4426950408889634
    qk_scale = (dh ** -0.5) * log2e

    qpos = lax.broadcasted_iota(I32, (blk, blk), 1)
    kpos = lax.broadcasted_iota(I32, (blk, blk), 0)

    @pl.when(i == 0)
    def _():
        vt_ref[...] = v_ref[...].astype(F32).T.astype(BF16)
        dist = (qpos - kpos).astype(F32)
        for g in range(hps):
            bias_ref[g] = dist * (-log2e * slopes_ref[hg * hps + g])

    bidx = lax.broadcasted_iota(I32, (nb, blk), 0)
    past = bidx < i
    i0 = pl.multiple_of(i * blk, blk)
    heads = [slice(g * dh, (g + 1) * dh) for g in range(hps)]
    gates, raw_own = [], []
    for cols in heads:
        q = q_ref[:, cols]
        km = km_ref[0, :, cols]
        km_hi = km.astype(BF16)
        km_lo = (km - km_hi.astype(F32)).astype(BF16)
        gates.append(lax.dot_general(km_hi, q, _DN_LAST, preferred_element_type=F32)
                     + lax.dot_general(km_lo, q, _DN_LAST, preferred_element_type=F32))
        raw_own.append(lax.dot_general(k_ref[pl.ds(i0, blk), cols], q, _DN_LAST,
                                       preferred_element_type=F32))
    own_bias = [bias_ref[g] for g in range(hps)]

    carry0, acc0, sels = [], [], []
    for g, cols in enumerate(heads):
        gate = jnp.where(past, gates[g], -jnp.inf)
        rank = jnp.zeros((nb, blk), F32)
        for m in range(nb):
            gm = gate[m:m + 1, :]
            beats = jnp.where(gm > gate, 1.0, jnp.where((gm == gate) & (m < bidx), 1.0, 0.0))
            rank = rank + jnp.where(m < i, beats, 0.0)
        sels.append(jnp.where(past & (rank < MOBA_TOPK), 1.0, 0.0))

        s = jnp.where(qpos >= kpos, raw_own[g] * qk_scale + own_bias[g], MASK_VALUE)
        m0 = jnp.max(s, axis=0, keepdims=True)
        p = jnp.exp2(s - m0)
        acc0.append(jnp.dot(vt_ref[cols, pl.ds(i0, blk)], p.astype(BF16), preferred_element_type=F32))
        carry0 += [m0, jnp.sum(p, axis=0, keepdims=True)]
    for g in range(hps):
        acc_ref[g] = acc0[g]
        sel_ref[g] = sels[g]

    def body(j, carry):
        j0 = pl.multiple_of(j * blk, blk)
        off = ((i - j) * blk).astype(F32) * log2e
        raw = [lax.dot_general(k_ref[pl.ds(j0, blk), cols], q_ref[:, cols], _DN_LAST,
                               preferred_element_type=F32) for cols in heads]
        chosen = [sel_ref[g, pl.ds(j, 1), :] > 0.0 for g in range(hps)]
        acc_old = [acc_ref[g] for g in range(hps)]
        out, acc_new = [], []
        for g, cols in enumerate(heads):
            m_run, l_run = carry[2 * g], carry[2 * g + 1]
            shift = -slopes_ref[hg * hps + g] * off
            s = raw[g] * qk_scale + bias_ref[g]
            m_new = jnp.maximum(m_run, jnp.where(chosen[g], jnp.max(s, axis=0, keepdims=True) + shift,
                                                 MASK_VALUE))
            alpha = jnp.exp2(m_run - m_new)
            p = jnp.exp2(s - jnp.where(chosen[g], m_new - shift, -MASK_VALUE))
            acc_new.append(alpha * acc_old[g] + jnp.dot(vt_ref[cols, pl.ds(j0, blk)], p.astype(BF16),
                                                        preferred_element_type=F32))
            out += [m_new, alpha * l_run + jnp.sum(p, axis=0, keepdims=True)]
        for g in range(hps):
            acc_ref[g] = acc_new[g]
        return tuple(out)

    fin = lax.fori_loop(0, i, body, tuple(carry0))
    for g in range(hps):
        o_ref[:, g * dh:(g + 1) * dh] = (acc_ref[g] / fin[2 * g + 1]).T.astype(BF16)


MOBA_HEADS_PER_STEP = 4


def _moba(proj, kmean, slopes, batch, seq):
    n = proj.shape[0]
    nb = seq // MOBA_BLOCK
    hps = MOBA_HEADS_PER_STEP
    w = hps * ATTN_HEAD_DIM
    ng = N_ATTN_HEADS // hps
    kern = functools.partial(_moba_kernel, nb=nb, hps=hps)
    return pl.pallas_call(
        kern,
        grid=(batch, ng, nb),
        in_specs=[pl.BlockSpec(memory_space=pltpu.SMEM),
                  pl.BlockSpec((MOBA_BLOCK, w), lambda b, h, i: (b * nb + i, h)),
                  pl.BlockSpec((seq, w), lambda b, h, i: (b, ng + h)),
                  pl.BlockSpec((seq, w), lambda b, h, i: (b, 2 * ng + h)),
                  pl.BlockSpec((1, nb, w), lambda b, h, i: (b, 0, h))],
        out_specs=pl.BlockSpec((MOBA_BLOCK, w), lambda b, h, i: (b * nb + i, h)),
        out_shape=jax.ShapeDtypeStruct((n, ATTN_WIDTH), BF16),
        scratch_shapes=[pltpu.VMEM((w, seq), BF16),
                        pltpu.VMEM((hps, nb, MOBA_BLOCK), F32),
                        pltpu.VMEM((hps, MOBA_BLOCK, MOBA_BLOCK), F32),
                        pltpu.VMEM((hps, ATTN_HEAD_DIM, MOBA_BLOCK), F32)],
        compiler_params=_cparams(("arbitrary", "arbitrary", "arbitrary")),
        name="moba",
    )(slopes, proj, proj, proj, kmean)


def _ret_kernel(lg_ref, q_ref, k_ref, v_ref, g_ref, rnw_ref, o_ref,
                state_ref, dmask_ref, qdec_ref, kdec_ref, cdec_ref):
    h = pl.program_id(1)
    n = pl.program_id(2)
    c = RET_CHUNK
    dk = RET_HEAD_DIM
    kscale = dk ** -0.5

    @pl.when(n == 0)
    def _():
        lg = lg_ref[h]
        state_ref[...] = jnp.zeros_like(state_ref)
        ti = lax.broadcasted_iota(I32, (c, c), 0)
        si = lax.broadcasted_iota(I32, (c, c), 1)
        diff = (ti - si).astype(F32)
        dmask_ref[...] = jnp.where(ti >= si, jnp.exp(jnp.maximum(diff, 0.0) * lg), 0.0)
        pos = lax.broadcasted_iota(I32, (c, dk), 0).astype(F32)
        qdec_ref[...] = jnp.exp((pos + 1.0) * lg)
        kdec_ref[...] = jnp.exp((c - 1.0 - pos) * lg)
        cdec_ref[...] = jnp.exp(jnp.full((1, dk), float(c), F32) * lg)

    q = q_ref[...]
    k = k_ref[...]
    v = v_ref[...]
    scores = lax.dot_general(q, k, _DN_LAST, preferred_element_type=F32) * kscale * dmask_ref[...]
    inner = jnp.dot(scores.astype(BF16), v, preferred_element_type=F32)
    state = state_ref[...]
    cross = jnp.dot(q, state.astype(BF16), preferred_element_type=F32) * qdec_ref[...]
    kd = (k.astype(F32) * kscale * kdec_ref[...]).astype(BF16)
    kv = lax.dot_general(kd, v, _DN_FIRST, preferred_element_type=F32)
    state_ref[...] = state * cdec_ref[...] + kv

    r = inner + cross
    rn = r * lax.rsqrt(jnp.mean(r * r, axis=-1, keepdims=True) + NORM_EPS)
    g = g_ref[...].astype(F32)
    o_ref[...] = (rn * rnw_ref[...] * _silu(g)).astype(BF16)


def _retention(proj, log_gamma, rnw, batch, seq):
    n = proj.shape[0]
    c = RET_CHUNK
    nc = seq // c
    dk = RET_HEAD_DIM
    base = 3 * ATTN_WIDTH // dk

    def col(off):
        return lambda b, h, t: (b * nc + t, base + off * RET_HEADS + h)

    return pl.pallas_call(
        _ret_kernel,
        grid=(batch, RET_HEADS, nc),
        in_specs=[pl.BlockSpec(memory_space=pltpu.SMEM),
                  pl.BlockSpec((c, dk), col(0)),
                  pl.BlockSpec((c, dk), col(1)),
                  pl.BlockSpec((c, dk), col(2)),
                  pl.BlockSpec((c, dk), col(3)),
                  pl.BlockSpec((1, dk), lambda b, h, t: (0, h))],
        out_specs=pl.BlockSpec((c, dk), lambda b, h, t: (b * nc + t, h)),
        out_shape=jax.ShapeDtypeStruct((n, RET_WIDTH), BF16),
        scratch_shapes=[pltpu.VMEM((dk, dk), F32), pltpu.VMEM((c, c), F32),
                        pltpu.VMEM((c, dk), F32), pltpu.VMEM((c, dk), F32),
                        pltpu.VMEM((1, dk), F32)],
        compiler_params=_cparams(("arbitrary", "arbitrary", "arbitrary")),
        name="retention",
    )(log_gamma, proj, proj, proj, proj, rnw)


def _outproj_kernel(a_ref, r_ref, w_ref, x_ref, mod_ref, nw_ref, wr_ref, br_ref,
                    x1_ref, hp_ref, lg_ref):
    mix = jnp.dot(jnp.concatenate([a_ref[...], r_ref[...]], axis=1), w_ref[...],
                  preferred_element_type=F32)
    m = mod_ref[0]
    x1 = x_ref[...] + m[2:3] * mix
    x1_ref[...] = x1
    y = x1 * lax.rsqrt(jnp.mean(x1 * x1, axis=-1, keepdims=True) + NORM_EPS) * nw_ref[...]
    h2 = y * (1.0 + m[4:5]) + m[3:4]
    hp_ref[...] = h2
    hi = h2.astype(BF16)
    lo = (h2 - hi.astype(F32)).astype(BF16)
    part = jnp.dot(jnp.concatenate([hi, lo], axis=1), wr_ref[...], preferred_element_type=F32)
    lg_ref[...] = part + pltpu.roll(part, LANES // 2, axis=1) + br_ref[...]


def _outproj(attn, ret, w_bf, x2d, mod3, nw, wr_cat, br, seq, tm):
    n, d = x2d.shape
    per_b = seq // tm
    return pl.pallas_call(
        _outproj_kernel,
        grid=(n // tm,),
        in_specs=[pl.BlockSpec((tm, ATTN_WIDTH), lambda i: (i, 0)),
                  pl.BlockSpec((tm, RET_WIDTH), lambda i: (i, 0)),
                  pl.BlockSpec((d, d), lambda i: (0, 0)),
                  pl.BlockSpec((tm, d), lambda i: (i, 0)),
                  pl.BlockSpec((1, N_MOD, d), lambda i: (i // per_b, 0, 0)),
                  pl.BlockSpec((1, d), lambda i: (0, 0)),
                  pl.BlockSpec((2 * d, LANES), lambda i: (0, 0)),
                  pl.BlockSpec((1, LANES), lambda i: (0, 0))],
        out_specs=[pl.BlockSpec((tm, d), lambda i: (i, 0)),
                   pl.BlockSpec((tm, d), lambda i: (i, 0)),
                   pl.BlockSpec((tm, LANES), lambda i: (i, 0))],
        out_shape=[jax.ShapeDtypeStruct((n, d), F32),
                   jax.ShapeDtypeStruct((n, d), F32),
                   jax.ShapeDtypeStruct((n, LANES), F32)],
        compiler_params=_cparams(("arbitrary",)),
        name="outproj",
    )(attn, ret, w_bf, x2d, mod3, nw, wr_cat, br)


_L_E0, _L_E1, _L_W0, _L_W1, _L_D0, _L_D1 = 0, 1, 2, 3, 4, 5
_PLAN_ROWS = 256


def _first_lane_where(cond, lane):
    return jnp.min(jnp.where(cond, lane, float(LANES)), axis=1, keepdims=True)


def _plan_kernel(lg_ref, info_ref, meta_ref, *, n_tok, n_tiles):
    tb = _PLAN_ROWS
    lane = lax.broadcasted_iota(I32, (tb, LANES), 1).astype(F32)
    row = lax.broadcasted_iota(I32, (tb, tb), 0)
    colm = lax.broadcasted_iota(I32, (tb, tb), 1)
    tri = jnp.where(colm < row, 1.0, 0.0).astype(BF16)
    n_g, n_e = N_GROUPS, EXPERTS_PER_GROUP

    def route(blk_i, carry):
        r0 = pl.multiple_of(blk_i * tb, tb)
        lg = lg_ref[pl.ds(r0, tb), :]
        gl = jnp.where(lane < n_g, lg, -jnp.inf)
        ge = jnp.exp(gl - jnp.max(gl, axis=1, keepdims=True))
        gp = ge / jnp.sum(ge, axis=1, keepdims=True)
        g_w = jnp.max(gp, axis=1, keepdims=True)
        g_sel = _first_lane_where((gp == g_w) & (lane < n_g), lane)
        lo_lane = n_g + n_e * g_sel
        in_grp = (lane >= lo_lane) & (lane < lo_lane + n_e)
        el = jnp.where(in_grp, lg, -jnp.inf)
        ee = jnp.exp(el - jnp.max(el, axis=1, keepdims=True))
        ep = jnp.where(in_grp, ee / jnp.sum(ee, axis=1, keepdims=True), -1.0)
        p1 = jnp.max(ep, axis=1, keepdims=True)
        i1 = _first_lane_where(ep == p1, lane)
        ep2 = jnp.where(lane == i1, -1.0, ep)
        p2 = jnp.max(ep2, axis=1, keepdims=True)
        i2 = _first_lane_where(ep2 == p2, lane)
        denom = p1 + p2
        w0 = g_w * p1 / denom
        w1 = g_w * p2 / denom
        e0 = i1 - n_g
        e1 = i2 - n_g
        oh0 = jnp.where(lane == e0, 1.0, 0.0)
        oh1 = jnp.where(lane == e1, 1.0, 0.0)
        oh = oh0 + oh1
        before = jnp.dot(tri, oh.astype(BF16), preferred_element_type=F32) + carry
        rank0 = jnp.sum(oh0 * before, axis=1, keepdims=True)
        rank1 = jnp.sum(oh1 * before, axis=1, keepdims=True)
        info = jnp.where(lane == _L_E0, e0,
               jnp.where(lane == _L_E1, e1,
               jnp.where(lane == _L_W0, w0,
               jnp.where(lane == _L_W1, w1,
               jnp.where(lane == _L_D0, rank0,
               jnp.where(lane == _L_D1, rank1, 0.0))))))
        info_ref[pl.ds(r0, tb), :] = info
        return carry + jnp.sum(oh, axis=0, keepdims=True)

    counts = lax.fori_loop(0, n_tok // tb, route, jnp.zeros((1, LANES), F32))

    lane1 = lax.broadcasted_iota(I32, (1, LANES), 1)
    padded = jnp.floor((counts + (FFN_TILE - 1.0)) * (1.0 / FFN_TILE)) * FFN_TILE
    pad_end = padded
    sh = 1
    while sh < N_EXPERTS:
        pad_end = pad_end + jnp.where(lane1 >= sh, pltpu.roll(pad_end, sh, axis=1), 0.0)
        sh *= 2
    pad_start = pad_end - padded

    def place(blk_i, _):
        r0 = pl.multiple_of(blk_i * tb, tb)
        info = info_ref[pl.ds(r0, tb), :]
        e0 = info[:, _L_E0:_L_E0 + 1]
        e1 = info[:, _L_E1:_L_E1 + 1]
        s0 = jnp.sum(jnp.where(lane == e0, pad_start, 0.0), axis=1, keepdims=True)
        s1 = jnp.sum(jnp.where(lane == e1, pad_start, 0.0), axis=1, keepdims=True)
        add = jnp.where(lane == _L_D0, s0, jnp.where(lane == _L_D1, s1, 0.0))
        info_ref[pl.ds(r0, tb), :] = info + add
        return 0

    lax.fori_loop(0, n_tok // tb, place, 0)

    sq_r = lax.broadcasted_iota(I32, (LANES, LANES), 0)
    sq_c = lax.broadcasted_iota(I32, (LANES, LANES), 1)
    pe_col = jnp.sum(jnp.where(sq_r == sq_c, pad_end, 0.0), axis=1, keepdims=True)
    mt = meta_ref.shape[1]
    t_start = lax.broadcasted_iota(I32, (LANES, mt), 1).astype(F32) * FFN_TILE
    e_row = lax.broadcasted_iota(I32, (LANES, mt), 0)
    ended = jnp.where((pe_col <= t_start) & (e_row < N_EXPERTS), 1.0, 0.0)
    tile_e = jnp.minimum(jnp.sum(ended, axis=0, keepdims=True), N_EXPERTS - 1.0)
    total = jnp.sum(jnp.where(lane1 == N_EXPERTS - 1, pad_end, 0.0), axis=1, keepdims=True)
    n_valid = jnp.broadcast_to(total * (1.0 / FFN_TILE), (1, mt))
    mrow = lax.broadcasted_iota(I32, (8, mt), 0)
    meta = jnp.where(mrow == 0, tile_e, jnp.where(mrow == 1, n_valid, 0.0))
    meta_ref[...] = meta.astype(I32)


def _plan(logits, n_tiles):
    n_tok = logits.shape[0]
    mt = -(-n_tiles // LANES) * LANES
    kern = functools.partial(_plan_kernel, n_tok=n_tok, n_tiles=n_tiles)
    return pl.pallas_call(
        kern,
        out_shape=[jax.ShapeDtypeStruct((n_tok, LANES), F32),
                   jax.ShapeDtypeStruct((8, mt), I32)],
        compiler_params=pltpu.CompilerParams(vmem_limit_bytes=VMEM_LIMIT),
        name="plan",
    )(logits)


_DMA_UNROLL = 8


def _dispatch_kernel(d0_ref, d1_ref, h_ref, xs_in_ref, xs_ref, sem, *, td):
    del xs_in_ref
    base = pl.program_id(0) * td

    def row_copy(r, dst_row):
        return pltpu.make_async_copy(h_ref.at[pl.ds(r, 1)], xs_ref.at[pl.ds(dst_row, 1)], sem)

    def body(u, _):
        for s in range(_DMA_UNROLL):
            r = u * _DMA_UNROLL + s
            row_copy(r, d0_ref[base + r]).start()
            row_copy(r, d1_ref[base + r]).start()
        return 0

    lax.fori_loop(0, td // _DMA_UNROLL, body, 0)
    for _ in range(2):
        pltpu.make_async_copy(h_ref, xs_ref.at[pl.ds(0, td)], sem).wait()


def _dispatch(d0, d1, h2p, xs_zero, td=256):
    n, w = h2p.shape
    kern = functools.partial(_dispatch_kernel, td=td)
    return pl.pallas_call(
        kern,
        grid_spec=pltpu.PrefetchScalarGridSpec(
            num_scalar_prefetch=2,
            grid=(n // td,),
            in_specs=[pl.BlockSpec((td, w), lambda i, d0, d1: (i, 0)),
                      pl.BlockSpec(memory_space=pl.ANY)],
            out_specs=pl.BlockSpec(memory_space=pl.ANY),
            scratch_shapes=[pltpu.SemaphoreType.DMA(())]),
        out_shape=jax.ShapeDtypeStruct(xs_zero.shape, xs_zero.dtype),
        input_output_aliases={3: 0},
        compiler_params=_cparams(("arbitrary",)),
        name="dispatch",
    )(d0, d1, h2p, xs_zero)


def _ffn_kernel(te_ref, nv_ref, xs_ref, w1_ref, w3_ref, w2_ref, y_ref):
    valid = pl.program_id(0) < nv_ref[0]

    @pl.when(valid)
    def _():
        x = xs_ref[...].astype(BF16)
        a = jnp.dot(x, w1_ref[0], preferred_element_type=F32)
        b = jnp.dot(x, w3_ref[0], preferred_element_type=F32)
        hmid = (_silu(a) * b).astype(BF16)
        y_ref[...] = jnp.dot(hmid, w2_ref[0], preferred_element_type=F32)

    @pl.when(jnp.logical_not(valid))
    def _():
        y_ref[...] = jnp.zeros_like(y_ref)


def _ffn(tile_e, n_valid, xs, w1, w3, w2):
    p_rows, w = xs.shape
    _, d, de = w1.shape
    n_tiles = p_rows // FFN_TILE

    def tile(i, te, nv):
        return jnp.minimum(i, nv[0] - 1)

    return pl.pallas_call(
        _ffn_kernel,
        grid_spec=pltpu.PrefetchScalarGridSpec(
            num_scalar_prefetch=2,
            grid=(n_tiles,),
            in_specs=[pl.BlockSpec((FFN_TILE, w), lambda i, te, nv: (tile(i, te, nv), 0)),
                      pl.BlockSpec((1, d, de), lambda i, te, nv: (te[tile(i, te, nv)], 0, 0)),
                      pl.BlockSpec((1, d, de), lambda i, te, nv: (te[tile(i, te, nv)], 0, 0)),
                      pl.BlockSpec((1, de, d), lambda i, te, nv: (te[tile(i, te, nv)], 0, 0))],
            out_specs=pl.BlockSpec((FFN_TILE, d), lambda i, te, nv: (i, 0))),
        out_shape=jax.ShapeDtypeStruct((p_rows, d), F32),
        compiler_params=_cparams(("arbitrary",)),
        name="ffn",
    )(tile_e, n_valid, xs, w1, w3, w2)


def _combine_kernel(d0_ref, d1_ref, x1_ref, info_ref, mod_ref, nw_ref, y_ref, o_ref,
                    ya_ref, yb_ref, sem, *, tc):
    base = pl.program_id(0) * tc

    def body(u, _):
        for s in range(_DMA_UNROLL):
            r = u * _DMA_UNROLL + s
            pltpu.make_async_copy(y_ref.at[pl.ds(d0_ref[base + r], 1)], ya_ref.at[pl.ds(r, 1)], sem).start()
            pltpu.make_async_copy(y_ref.at[pl.ds(d1_ref[base + r], 1)], yb_ref.at[pl.ds(r, 1)], sem).start()
        return 0

    lax.fori_loop(0, tc // _DMA_UNROLL, body, 0)
    pltpu.make_async_copy(y_ref.at[pl.ds(0, tc)], ya_ref, sem).wait()
    pltpu.make_async_copy(y_ref.at[pl.ds(0, tc)], yb_ref, sem).wait()

    info = info_ref[...]
    w0 = info[:, _L_W0:_L_W0 + 1]
    w1 = info[:, _L_W1:_L_W1 + 1]
    moe = ya_ref[...] * w0 + yb_ref[...] * w1
    x2 = x1_ref[...] + mod_ref[0][5:6] * moe
    o_ref[...] = x2 * lax.rsqrt(jnp.mean(x2 * x2, axis=-1, keepdims=True) + NORM_EPS) * nw_ref[...]


def _combine(d0, d1, x1, info, mod3, nfw, y, seq, tc=256):
    n, d = x1.shape
    per_b = seq // tc
    kern = functools.partial(_combine_kernel, tc=tc)
    return pl.pallas_call(
        kern,
        grid_spec=pltpu.PrefetchScalarGridSpec(
            num_scalar_prefetch=2,
            grid=(n // tc,),
            in_specs=[pl.BlockSpec((tc, d), lambda i, d0, d1: (i, 0)),
                      pl.BlockSpec((tc, LANES), lambda i, d0, d1: (i, 0)),
                      pl.BlockSpec((1, N_MOD, d), lambda i, d0, d1: (i // per_b, 0, 0)),
                      pl.BlockSpec((1, d), lambda i, d0, d1: (0, 0)),
                      pl.BlockSpec(memory_space=pl.ANY)],
            out_specs=pl.BlockSpec((tc, d), lambda i, d0, d1: (i, 0)),
            scratch_shapes=[pltpu.VMEM((tc, d), F32), pltpu.VMEM((tc, d), F32),
                            pltpu.SemaphoreType.DMA(())]),
        out_shape=jax.ShapeDtypeStruct((n, d), F32),
        compiler_params=_cparams(("arbitrary",)),
        name="combine",
    )(d0, d1, x1, info, mod3, nfw, y)


def _split_hi_lo(w):
    hi = w.astype(BF16)
    lo = (w - hi.astype(F32)).astype(BF16)
    return hi, lo


def kernel(x, c, w_ada, b_ada, norm_mix_w, w_in, ret_norm_w, w_out, norm_ffn_w,
           w_group, b_group, w_router, b_router, w1, w3, w2, norm_final_w):
    batch, seq, d = x.shape
    n = batch * seq
    depth = w_ada.shape[0]
    assert depth == 1, "the final rmsnorm is fused into the (single) layer's last kernel"
    tm = min(1024, seq)
    half = LANES // 2

    slopes = jnp.exp2(-8.0 * jnp.arange(1, N_ATTN_HEADS + 1, dtype=F32) / N_ATTN_HEADS)
    log_gamma = jnp.log1p(-jnp.exp2(-5.0 - jnp.arange(RET_HEADS, dtype=F32)))
    n_tiles = (2 * n) // FFN_TILE + N_EXPERTS
    p_rows = n_tiles * FFN_TILE

    c_pad = jnp.zeros((8, d), F32).at[:batch].set(c)
    x2d = x.reshape(n, d)
    for l in range(depth):
        mod = _adaln(c_pad, w_ada[l], b_ada[l].reshape(1, -1))
        mod3 = mod[:batch].reshape(batch, N_MOD, d)

        proj, kmean = _inproj(x2d, mod3, norm_mix_w[l].reshape(1, d), w_in[l].astype(BF16), seq, tm)
        kmean = kmean.reshape(batch, seq // MOBA_BLOCK, ATTN_WIDTH)
        attn = _moba(proj, kmean, slopes, batch, seq)
        ret = _retention(proj, log_gamma, ret_norm_w[l].reshape(1, RET_WIDTH), batch, seq)

        wr = jnp.concatenate([w_group[l], jnp.transpose(w_router[l], (1, 0, 2)).reshape(d, N_EXPERTS)], axis=1)
        wr = jnp.pad(wr, ((0, 0), (0, half - wr.shape[1])))
        wr_hi, wr_lo = _split_hi_lo(wr)
        wr_cat = jnp.concatenate([jnp.concatenate([wr_hi, wr_lo], axis=1),
                                  jnp.concatenate([wr_hi, jnp.zeros_like(wr_lo)], axis=1)], axis=0)
        br = jnp.concatenate([b_group[l], b_router[l].reshape(-1)])
        br = jnp.pad(br, (0, LANES - br.shape[0])).reshape(1, LANES)

        x1, h2p, logits = _outproj(attn, ret, w_out[l].astype(BF16), x2d, mod3,
                                   norm_ffn_w[l].reshape(1, d), wr_cat, br, seq, min(512, seq))

        info, meta = _plan(logits, n_tiles)
        d0 = info[:, _L_D0].astype(I32)
        d1 = info[:, _L_D1].astype(I32)
        tile_e = meta[0, :n_tiles]
        n_valid = meta[1, :1]

        xs = _dispatch(d0, d1, h2p, jnp.zeros((p_rows, d), F32))
        y = _ffn(tile_e, n_valid, xs, w1[l].astype(BF16), w3[l].astype(BF16), w2[l].astype(BF16))
        nfw = norm_final_w.reshape(1, d)
        x2d = _combine(d0, d1, x1, info, mod3, nfw, y, seq)
    return x2d.reshape(batch, seq, d)
```

```python
import functools

import jax
import jax.numpy as jnp
from jax import lax
from jax.experimental import pallas as pl
from jax.experimental.pallas import tpu as pltpu

F32 = jnp.float32
BF16 = jnp.bfloat16
I32 = jnp.int32

N_ATTN_HEADS = 8
ATTN_HEAD_DIM = 128
ATTN_WIDTH = N_ATTN_HEADS * ATTN_HEAD_DIM
MOBA_BLOCK = 256
MOBA_TOPK = 3
RET_HEADS = 4
RET_HEAD_DIM = 256
RET_WIDTH = RET_HEADS * RET_HEAD_DIM
N_GROUPS = 4
EXPERTS_PER_GROUP = 8
N_EXPERTS = N_GROUPS * EXPERTS_PER_GROUP
N_MOD = 6
NORM_EPS = 1e-6

LANES = 128
RET_CHUNK = 256
FFN_TILE = 256
MASK_VALUE = -1e30
VMEM_LIMIT = 56 * 1024 * 1024

_DN_LAST = (((1,), (1,)), ((), ()))
_DN_FIRST = (((0,), (0,)), ((), ()))


def _cparams(sem):
    return pltpu.CompilerParams(dimension_semantics=sem, vmem_limit_bytes=VMEM_LIMIT)


def _silu(v):
    return v * jax.nn.sigmoid(v)


def _adaln_kernel(c_ref, w_ref, b_ref, o_ref):
    ca = _silu(c_ref[...]).astype(BF16)
    o_ref[...] = jnp.dot(ca, w_ref[...].astype(BF16), preferred_element_type=F32) + b_ref[...]


def _adaln(c_pad, w_ada, b_ada, tn=1024):
    rows, d = c_pad.shape
    n = w_ada.shape[1]
    return pl.pallas_call(
        _adaln_kernel,
        grid=(n // tn,),
        in_specs=[pl.BlockSpec((rows, d), lambda j: (0, 0)),
                  pl.BlockSpec((d, tn), lambda j: (0, j)),
                  pl.BlockSpec((1, tn), lambda j: (0, j))],
        out_specs=pl.BlockSpec((rows, tn), lambda j: (0, j)),
        out_shape=jax.ShapeDtypeStruct((rows, n), F32),
        compiler_params=_cparams(("arbitrary",)),
        name="adaln",
    )(c_pad, w_ada, b_ada)


def _inproj_kernel(x_ref, mod_ref, nw_ref, w_ref, o_ref, km_ref, h_ref, *, tm, tn):
    j = pl.program_id(1)

    @pl.when(j == 0)
    def _():
        x = x_ref[...]
        y = x * lax.rsqrt(jnp.mean(x * x, axis=-1, keepdims=True) + NORM_EPS) * nw_ref[...]
        m = mod_ref[0]
        h_ref[...] = (y * (1.0 + m[1:2]) + m[0:1]).astype(BF16)

    acc = jnp.dot(h_ref[...], w_ref[...].astype(BF16), preferred_element_type=F32)
    o_ref[...] = acc.astype(BF16)

    @pl.when(j == 1)
    def _():
        km_ref[0] = jnp.sum(acc.reshape(tm // MOBA_BLOCK, MOBA_BLOCK, tn), axis=1) * (1.0 / MOBA_BLOCK)


def _inproj(x2d, mod3, nw, w_bf, seq, tm, tn=ATTN_WIDTH):
    n, d = x2d.shape
    width = w_bf.shape[1]
    per_b = seq // tm
    kern = functools.partial(_inproj_kernel, tm=tm, tn=tn)
    return pl.pallas_call(
        kern,
        grid=(n // tm, width // tn),
        in_specs=[pl.BlockSpec((tm, d), lambda i, j: (i, 0)),
                  pl.BlockSpec((1, N_MOD, d), lambda i, j: (i // per_b, 0, 0)),
                  pl.BlockSpec((1, d), lambda i, j: (0, 0)),
                  pl.BlockSpec((d, tn), lambda i, j: (0, j))],
        out_specs=[pl.BlockSpec((tm, tn), lambda i, j: (i, j)),
                   pl.BlockSpec((1, tm // MOBA_BLOCK, tn), lambda i, j: (i, 0, 0))],
        out_shape=[jax.ShapeDtypeStruct((n, width), BF16),
                   jax.ShapeDtypeStruct((n // tm, tm // MOBA_BLOCK, tn), F32)],
        scratch_shapes=[pltpu.VMEM((tm, d), BF16)],
        compiler_params=_cparams(("arbitrary", "arbitrary")),
        name="inproj",
    )(x2d, mod3, nw, w_bf)


def _moba_kernel(slopes_ref, q_ref, k_ref, v_ref, km_ref, o_ref,
                 vt_ref, sel_ref, bias_ref, acc_ref, *, nb, hps):
    hg = pl.program_id(1)
    i = pl.program_id(2)
    blk = MOBA_BLOCK
    dh = ATTN_HEAD_DIM
    log2e = 1.4426950408889634
    qk_scale = (dh ** -0.5) * log2e

    qpos = lax.broadcasted_iota(I32, (blk, blk), 1)
    kpos = lax.broadcasted_iota(I32, (blk, blk), 0)

    @pl.when(i == 0)
    def _():
        vt_ref[...] = v_ref[...].astype(F32).T.astype(BF16)
        dist = (qpos - kpos).astype(F32)
        for g in range(hps):
            bias_ref[g] = dist * (-log2e * slopes_ref[hg * hps + g])

    bidx = lax.broadcasted_iota(I32, (nb, blk), 0)
    past = bidx < i
    i0 = pl.multiple_of(i * blk, blk)
    heads = [slice(g * dh, (g + 1) * dh) for g in range(hps)]
    gates, raw_own = [], []
    for cols in heads:
        q = q_ref[:, cols]
        km = km_ref[0, :, cols]
        km_hi = km.astype(BF16)
        km_lo = (km - km_hi.astype(F32)).astype(BF16)
        gates.append(lax.dot_general(km_hi, q, _DN_LAST, preferred_element_type=F32)
                     + lax.dot_general(km_lo, q, _DN_LAST, preferred_element_type=F32))
        raw_own.append(lax.dot_general(k_ref[pl.ds(i0, blk), cols], q, _DN_LAST,
                                       preferred_element_type=F32))
    own_bias = [bias_ref[g] for g in range(hps)]

    carry0, acc0, sels = [], [], []
    for g, cols in enumerate(heads):
        gate = jnp.where(past, gates[g], -jnp.inf)
        rank = jnp.zeros((nb, blk), F32)
        for m in range(nb):
            gm = gate[m:m + 1, :]
            beats = jnp.where(gm > gate, 1.0, jnp.where((gm == gate) & (m < bidx), 1.0, 0.0))
            rank = rank + jnp.where(m < i, beats, 0.0)
        sels.append(jnp.where(past & (rank < MOBA_TOPK), 1.0, 0.0))

        s = jnp.where(qpos >= kpos, raw_own[g] * qk_scale + own_bias[g], MASK_VALUE)
        m0 = jnp.max(s, axis=0, keepdims=True)
        p = jnp.exp2(s - m0)
        acc0.append(jnp.dot(vt_ref[cols, pl.ds(i0, blk)], p.astype(BF16), preferred_element_type=F32))
        carry0 += [m0, jnp.sum(p, axis=0, keepdims=True)]
    for g in range(hps):
        acc_ref[g] = acc0[g]
        sel_ref[g] = sels[g]

    def body(j, carry):
        j0 = pl.multiple_of(j * blk, blk)
        off = ((i - j) * blk).astype(F32) * log2e
        raw = [lax.dot_general(k_ref[pl.ds(j0, blk), cols], q_ref[:, cols], _DN_LAST,
                               preferred_element_type=F32) for cols in heads]
        chosen = [sel_ref[g, pl.ds(j, 1), :] > 0.0 for g in range(hps)]
        acc_old = [acc_ref[g] for g in range(hps)]
        out, acc_new = [], []
        for g, cols in enumerate(heads):
            m_run, l_run = carry[2 * g], carry[2 * g + 1]
            shift = -slopes_ref[hg * hps + g] * off
            s = raw[g] * qk_scale + bias_ref[g]
            m_new = jnp.maximum(m_run, jnp.where(chosen[g], jnp.max(s, axis=0, keepdims=True) + shift,
                                                 MASK_VALUE))
            alpha = jnp.exp2(m_run - m_new)
            p = jnp.exp2(s - jnp.where(chosen[g], m_new - shift, -MASK_VALUE))
            acc_new.append(alpha * acc_old[g] + jnp.dot(vt_ref[cols, pl.ds(j0, blk)], p.astype(BF16),
                                                        preferred_element_type=F32))
            out += [m_new, alpha * l_run + jnp.sum(p, axis=0, keepdims=True)]
        for g in range(hps):
            acc_ref[g] = acc_new[g]
        return tuple(out)

    fin = lax.fori_loop(0, i, body, tuple(carry0))
    for g in range(hps):
        o_ref[:, g * dh:(g + 1) * dh] = (acc_ref[g] / fin[2 * g + 1]).T.astype(BF16)


MOBA_HEADS_PER_STEP = 4


def _moba(proj, kmean, slopes, batch, seq):
    n = proj.shape[0]
    nb = seq // MOBA_BLOCK
    hps = MOBA_HEADS_PER_STEP
    w = hps * ATTN_HEAD_DIM
    ng = N_ATTN_HEADS // hps
    kern = functools.partial(_moba_kernel, nb=nb, hps=hps)
    return pl.pallas_call(
        kern,
        grid=(batch, ng, nb),
        in_specs=[pl.BlockSpec(memory_space=pltpu.SMEM),
                  pl.BlockSpec((MOBA_BLOCK, w), lambda b, h, i: (b * nb + i, h)),
                  pl.BlockSpec((seq, w), lambda b, h, i: (b, ng + h)),
                  pl.BlockSpec((seq, w), lambda b, h, i: (b, 2 * ng + h)),
                  pl.BlockSpec((1, nb, w), lambda b, h, i: (b, 0, h))],
        out_specs=pl.BlockSpec((MOBA_BLOCK, w), lambda b, h, i: (b * nb + i, h)),
        out_shape=jax.ShapeDtypeStruct((n, ATTN_WIDTH), BF16),
        scratch_shapes=[pltpu.VMEM((w, seq), BF16),
                        pltpu.VMEM((hps, nb, MOBA_BLOCK), F32),
                        pltpu.VMEM((hps, MOBA_BLOCK, MOBA_BLOCK), F32),
                        pltpu.VMEM((hps, ATTN_HEAD_DIM, MOBA_BLOCK), F32)],
        compiler_params=_cparams(("arbitrary", "arbitrary", "arbitrary")),
        name="moba",
    )(slopes, proj, proj, proj, kmean)


def _ret_kernel(lg_ref, q_ref, k_ref, v_ref, g_ref, rnw_ref, o_ref,
                state_ref, dmask_ref, qdec_ref, kdec_ref, cdec_ref):
    n = pl.program_id(1)
    c = RET_CHUNK
    dk = RET_HEAD_DIM
    kscale = dk ** -0.5

    @pl.when(n == 0)
    def _():
        state_ref[...] = jnp.zeros_like(state_ref)
        ti = lax.broadcasted_iota(I32, (c, c), 0)
        si = lax.broadcasted_iota(I32, (c, c), 1)
        diff = jnp.maximum(ti - si, 0).astype(F32)
        pos = lax.broadcasted_iota(I32, (c, dk), 0).astype(F32)
        for h in range(RET_HEADS):
            lg = lg_ref[h]
            dmask_ref[h] = jnp.where(ti >= si, jnp.exp(diff * lg) * kscale, 0.0)
            qdec_ref[h] = jnp.exp((pos + 1.0) * lg)
            kdec_ref[h] = jnp.exp((c - 1.0 - pos) * lg) * kscale
            cdec_ref[h] = jnp.exp(jnp.full((1, dk), float(c), F32) * lg)

    heads = [slice(h * dk, (h + 1) * dk) for h in range(RET_HEADS)]
    qs = [q_ref[:, cols] for cols in heads]
    ks = [k_ref[:, cols] for cols in heads]
    vs = [v_ref[:, cols] for cols in heads]
    states = [state_ref[h] for h in range(RET_HEADS)]
    raw = [lax.dot_general(qs[h], ks[h], _DN_LAST, preferred_element_type=F32) for h in range(RET_HEADS)]
    cross = [jnp.dot(qs[h], states[h].astype(BF16), preferred_element_type=F32) for h in range(RET_HEADS)]
    outs, new_states = [], []
    for h, cols in enumerate(heads):
        inner = jnp.dot((raw[h] * dmask_ref[h]).astype(BF16), vs[h], preferred_element_type=F32)
        kd = (ks[h].astype(F32) * kdec_ref[h]).astype(BF16)
        kv = lax.dot_general(kd, vs[h], _DN_FIRST, preferred_element_type=F32)
        new_states.append(states[h] * cdec_ref[h] + kv)
        r = inner + cross[h] * qdec_ref[h]
        rn = r * lax.rsqrt(jnp.mean(r * r, axis=-1, keepdims=True) + NORM_EPS)
        g = g_ref[:, cols].astype(F32)
        outs.append((rn * rnw_ref[:, cols] * _silu(g)).astype(BF16))
    for h, cols in enumerate(heads):
        state_ref[h] = new_states[h]
        o_ref[:, cols] = outs[h]


def _retention(proj, log_gamma, rnw, batch, seq):
    n = proj.shape[0]
    c = RET_CHUNK
    nc = seq // c
    dk = RET_HEAD_DIM
    base = 3 * ATTN_WIDTH // RET_WIDTH

    def col(off):
        return lambda b, t: (b * nc + t, base + off)

    return pl.pallas_call(
        _ret_kernel,
        grid=(batch, nc),
        in_specs=[pl.BlockSpec(memory_space=pltpu.SMEM),
                  pl.BlockSpec((c, RET_WIDTH), col(0)),
                  pl.BlockSpec((c, RET_WIDTH), col(1)),
                  pl.BlockSpec((c, RET_WIDTH), col(2)),
                  pl.BlockSpec((c, RET_WIDTH), col(3)),
                  pl.BlockSpec((1, RET_WIDTH), lambda b, t: (0, 0))],
        out_specs=pl.BlockSpec((c, RET_WIDTH), lambda b, t: (b * nc + t, 0)),
        out_shape=jax.ShapeDtypeStruct((n, RET_WIDTH), BF16),
        scratch_shapes=[pltpu.VMEM((RET_HEADS, dk, dk), F32), pltpu.VMEM((RET_HEADS, c, c), F32),
                        pltpu.VMEM((RET_HEADS, c, dk), F32), pltpu.VMEM((RET_HEADS, c, dk), F32),
                        pltpu.VMEM((RET_HEADS, 1, dk), F32)],
        compiler_params=_cparams(("arbitrary", "arbitrary")),
        name="retention",
    )(log_gamma, proj, proj, proj, proj, rnw)


def _outproj_kernel(a_ref, r_ref, w_ref, x_ref, mod_ref, nw_ref, wr_ref, br_ref,
                    x1_ref, hp_ref, lg_ref, wb_ref):
    @pl.when(pl.program_id(0) == 0)
    def _():
        wb_ref[...] = w_ref[...].astype(BF16)

    mix = jnp.dot(jnp.concatenate([a_ref[...], r_ref[...]], axis=1), wb_ref[...],
                  preferred_element_type=F32)
    m = mod_ref[0]
    x1 = x_ref[...] + m[2:3] * mix
    x1_ref[...] = x1
    y = x1 * lax.rsqrt(jnp.mean(x1 * x1, axis=-1, keepdims=True) + NORM_EPS) * nw_ref[...]
    h2 = y * (1.0 + m[4:5]) + m[3:4]
    hp_ref[...] = h2
    hi = h2.astype(BF16)
    lo = (h2 - hi.astype(F32)).astype(BF16)
    part = jnp.dot(jnp.concatenate([hi, lo], axis=1), wr_ref[...], preferred_element_type=F32)
    lg_ref[...] = part + pltpu.roll(part, LANES // 2, axis=1) + br_ref[...]


def _outproj(attn, ret, w_bf, x2d, mod3, nw, wr_cat, br, seq, tm):
    n, d = x2d.shape
    per_b = seq // tm
    return pl.pallas_call(
        _outproj_kernel,
        grid=(n // tm,),
        in_specs=[pl.BlockSpec((tm, ATTN_WIDTH), lambda i: (i, 0)),
                  pl.BlockSpec((tm, RET_WIDTH), lambda i: (i, 0)),
                  pl.BlockSpec((d, d), lambda i: (0, 0), pipeline_mode=pl.Buffered(1)),
                  pl.BlockSpec((tm, d), lambda i: (i, 0)),
                  pl.BlockSpec((1, N_MOD, d), lambda i: (i // per_b, 0, 0)),
                  pl.BlockSpec((1, d), lambda i: (0, 0)),
                  pl.BlockSpec((2 * d, LANES), lambda i: (0, 0)),
                  pl.BlockSpec((1, LANES), lambda i: (0, 0))],
        out_specs=[pl.BlockSpec((tm, d), lambda i: (i, 0)),
                   pl.BlockSpec((tm, d), lambda i: (i, 0)),
                   pl.BlockSpec((tm, LANES), lambda i: (i, 0))],
        out_shape=[jax.ShapeDtypeStruct((n, d), F32),
                   jax.ShapeDtypeStruct((n, d), F32),
                   jax.ShapeDtypeStruct((n, LANES), F32)],
        scratch_shapes=[pltpu.VMEM((d, d), BF16)],
        compiler_params=_cparams(("arbitrary",)),
        name="outproj",
    )(attn, ret, w_bf, x2d, mod3, nw, wr_cat, br)


_L_E0, _L_E1, _L_W0, _L_W1, _L_D0, _L_D1 = 0, 1, 2, 3, 4, 5
_PLAN_ROWS = 256


def _first_lane_where(cond, lane):
    return jnp.min(jnp.where(cond, lane, float(LANES)), axis=1, keepdims=True)


def _plan_kernel(lg_ref, info_ref, meta_ref, *, n_tok, n_tiles):
    tb = _PLAN_ROWS
    lane = lax.broadcasted_iota(I32, (tb, LANES), 1).astype(F32)
    row = lax.broadcasted_iota(I32, (tb, tb), 0)
    colm = lax.broadcasted_iota(I32, (tb, tb), 1)
    tri = jnp.where(colm < row, 1.0, 0.0).astype(BF16)
    n_g, n_e = N_GROUPS, EXPERTS_PER_GROUP

    def route(blk_i, carry):
        r0 = pl.multiple_of(blk_i * tb, tb)
        lg = lg_ref[pl.ds(r0, tb), :]
        gl = jnp.where(lane < n_g, lg, -jnp.inf)
        ge = jnp.exp(gl - jnp.max(gl, axis=1, keepdims=True))
        gp = ge / jnp.sum(ge, axis=1, keepdims=True)
        g_w = jnp.max(gp, axis=1, keepdims=True)
        g_sel = _first_lane_where((gp == g_w) & (lane < n_g), lane)
        lo_lane = n_g + n_e * g_sel
        in_grp = (lane >= lo_lane) & (lane < lo_lane + n_e)
        el = jnp.where(in_grp, lg, -jnp.inf)
        ee = jnp.exp(el - jnp.max(el, axis=1, keepdims=True))
        ep = jnp.where(in_grp, ee / jnp.sum(ee, axis=1, keepdims=True), -1.0)
        p1 = jnp.max(ep, axis=1, keepdims=True)
        i1 = _first_lane_where(ep == p1, lane)
        ep2 = jnp.where(lane == i1, -1.0, ep)
        p2 = jnp.max(ep2, axis=1, keepdims=True)
        i2 = _first_lane_where(ep2 == p2, lane)
        denom = p1 + p2
        w0 = g_w * p1 / denom
        w1 = g_w * p2 / denom
        e0 = i1 - n_g
        e1 = i2 - n_g
        oh0 = jnp.where(lane == e0, 1.0, 0.0)
        oh1 = jnp.where(lane == e1, 1.0, 0.0)
        oh = oh0 + oh1
        before = jnp.dot(tri, oh.astype(BF16), preferred_element_type=F32) + carry
        rank0 = jnp.sum(oh0 * before, axis=1, keepdims=True)
        rank1 = jnp.sum(oh1 * before, axis=1, keepdims=True)
        info = jnp.where(lane == _L_E0, e0,
               jnp.where(lane == _L_E1, e1,
               jnp.where(lane == _L_W0, w0,
               jnp.where(lane == _L_W1, w1,
               jnp.where(lane == _L_D0, rank0,
               jnp.where(lane == _L_D1, rank1, 0.0))))))
        info_ref[pl.ds(r0, tb), :] = info
        return carry + jnp.sum(oh, axis=0, keepdims=True)

    counts = lax.fori_loop(0, n_tok // tb, route, jnp.zeros((1, LANES), F32))

    lane1 = lax.broadcasted_iota(I32, (1, LANES), 1)
    padded = jnp.floor((counts + (FFN_TILE - 1.0)) * (1.0 / FFN_TILE)) * FFN_TILE
    pad_end = padded
    sh = 1
    while sh < N_EXPERTS:
        pad_end = pad_end + jnp.where(lane1 >= sh, pltpu.roll(pad_end, sh, axis=1), 0.0)
        sh *= 2
    pad_start = pad_end - padded

    def place(blk_i, _):
        r0 = pl.multiple_of(blk_i * tb, tb)
        info = info_ref[pl.ds(r0, tb), :]
        e0 = info[:, _L_E0:_L_E0 + 1]
        e1 = info[:, _L_E1:_L_E1 + 1]
        s0 = jnp.sum(jnp.where(lane == e0, pad_start, 0.0), axis=1, keepdims=True)
        s1 = jnp.sum(jnp.where(lane == e1, pad_start, 0.0), axis=1, keepdims=True)
        add = jnp.where(lane == _L_D0, s0, jnp.where(lane == _L_D1, s1, 0.0))
        info_ref[pl.ds(r0, tb), :] = info + add
        return 0

    lax.fori_loop(0, n_tok // tb, place, 0)

    sq_r = lax.broadcasted_iota(I32, (LANES, LANES), 0)
    sq_c = lax.broadcasted_iota(I32, (LANES, LANES), 1)
    pe_col = jnp.sum(jnp.where(sq_r == sq_c, pad_end, 0.0), axis=1, keepdims=True)
    mt = meta_ref.shape[1]
    t_start = lax.broadcasted_iota(I32, (LANES, mt), 1).astype(F32) * FFN_TILE
    e_row = lax.broadcasted_iota(I32, (LANES, mt), 0)
    ended = jnp.where((pe_col <= t_start) & (e_row < N_EXPERTS), 1.0, 0.0)
    tile_e = jnp.minimum(jnp.sum(ended, axis=0, keepdims=True), N_EXPERTS - 1.0)
    total = jnp.sum(jnp.where(lane1 == N_EXPERTS - 1, pad_end, 0.0), axis=1, keepdims=True)
    n_valid = jnp.broadcast_to(total * (1.0 / FFN_TILE), (1, mt))
    mrow = lax.broadcasted_iota(I32, (8, mt), 0)
    meta = jnp.where(mrow == 0, tile_e, jnp.where(mrow == 1, n_valid, 0.0))
    meta_ref[...] = meta.astype(I32)


def _plan(logits, n_tiles):
    n_tok = logits.shape[0]
    mt = -(-n_tiles // LANES) * LANES
    kern = functools.partial(_plan_kernel, n_tok=n_tok, n_tiles=n_tiles)
    return pl.pallas_call(
        kern,
        out_shape=[jax.ShapeDtypeStruct((n_tok, LANES), F32),
                   jax.ShapeDtypeStruct((8, mt), I32)],
        compiler_params=pltpu.CompilerParams(vmem_limit_bytes=VMEM_LIMIT),
        name="plan",
    )(logits)


_DMA_UNROLL = 8


def _dispatch_kernel(d0_ref, d1_ref, h_ref, xs_in_ref, xs_ref, sem, *, td):
    del xs_in_ref
    base = pl.program_id(0) * td

    def row_copy(r, dst_row):
        return pltpu.make_async_copy(h_ref.at[pl.ds(r, 1)], xs_ref.at[pl.ds(dst_row, 1)], sem)

    def body(u, _):
        for s in range(_DMA_UNROLL):
            r = u * _DMA_UNROLL + s
            row_copy(r, d0_ref[base + r]).start()
            row_copy(r, d1_ref[base + r]).start()
        return 0

    lax.fori_loop(0, td // _DMA_UNROLL, body, 0)
    for _ in range(2):
        pltpu.make_async_copy(h_ref, xs_ref.at[pl.ds(0, td)], sem).wait()


def _dispatch(d0, d1, h2p, xs_zero, td=256):
    n, w = h2p.shape
    kern = functools.partial(_dispatch_kernel, td=td)
    return pl.pallas_call(
        kern,
        grid_spec=pltpu.PrefetchScalarGridSpec(
            num_scalar_prefetch=2,
            grid=(n // td,),
            in_specs=[pl.BlockSpec((td, w), lambda i, d0, d1: (i, 0)),
                      pl.BlockSpec(memory_space=pl.ANY)],
            out_specs=pl.BlockSpec(memory_space=pl.ANY),
            scratch_shapes=[pltpu.SemaphoreType.DMA(())]),
        out_shape=jax.ShapeDtypeStruct(xs_zero.shape, xs_zero.dtype),
        input_output_aliases={3: 0},
        compiler_params=_cparams(("arbitrary",)),
        name="dispatch",
    )(d0, d1, h2p, xs_zero)


def _ffn_kernel(te_ref, nv_ref, xs_ref, w1_ref, w3_ref, w2_ref, y_ref, w1b_ref, w3b_ref, w2b_ref):
    i = pl.program_id(0)
    valid = i < nv_ref[0]

    @pl.when(valid & ((i == 0) | (te_ref[i] != te_ref[jnp.maximum(i, 1) - 1])))
    def _():
        w1b_ref[...] = w1_ref[0].astype(BF16)
        w3b_ref[...] = w3_ref[0].astype(BF16)
        w2b_ref[...] = w2_ref[0].astype(BF16)

    @pl.when(valid)
    def _():
        x = xs_ref[...].astype(BF16)
        a = jnp.dot(x, w1b_ref[...], preferred_element_type=F32)
        b = jnp.dot(x, w3b_ref[...], preferred_element_type=F32)
        hmid = (_silu(a) * b).astype(BF16)
        y_ref[...] = jnp.dot(hmid, w2b_ref[...], preferred_element_type=F32)

    @pl.when(jnp.logical_not(valid))
    def _():
        y_ref[...] = jnp.zeros_like(y_ref)


def _ffn(tile_e, n_valid, xs, w1, w3, w2):
    p_rows, w = xs.shape
    _, d, de = w1.shape
    n_tiles = p_rows // FFN_TILE

    def tile(i, te, nv):
        return jnp.minimum(i, nv[0] - 1)

    return pl.pallas_call(
        _ffn_kernel,
        grid_spec=pltpu.PrefetchScalarGridSpec(
            num_scalar_prefetch=2,
            grid=(n_tiles,),
            in_specs=[pl.BlockSpec((FFN_TILE, w), lambda i, te, nv: (tile(i, te, nv), 0)),
                      pl.BlockSpec((1, d, de), lambda i, te, nv: (te[tile(i, te, nv)], 0, 0)),
                      pl.BlockSpec((1, d, de), lambda i, te, nv: (te[tile(i, te, nv)], 0, 0)),
                      pl.BlockSpec((1, de, d), lambda i, te, nv: (te[tile(i, te, nv)], 0, 0))],
            out_specs=pl.BlockSpec((FFN_TILE, d), lambda i, te, nv: (i, 0)),
            scratch_shapes=[pltpu.VMEM((d, de), BF16), pltpu.VMEM((d, de), BF16),
                            pltpu.VMEM((de, d), BF16)]),
        out_shape=jax.ShapeDtypeStruct((p_rows, d), F32),
        compiler_params=_cparams(("arbitrary",)),
        name="ffn",
    )(tile_e, n_valid, xs, w1, w3, w2)


def _combine_kernel(d0_ref, d1_ref, x1_ref, info_ref, mod_ref, nw_ref, y_ref, o_ref,
                    ya_ref, yb_ref, sem, *, tc):
    base = pl.program_id(0) * tc

    def body(u, _):
        for s in range(_DMA_UNROLL):
            r = u * _DMA_UNROLL + s
            pltpu.make_async_copy(y_ref.at[pl.ds(d0_ref[base + r], 1)], ya_ref.at[pl.ds(r, 1)], sem).start()
            pltpu.make_async_copy(y_ref.at[pl.ds(d1_ref[base + r], 1)], yb_ref.at[pl.ds(r, 1)], sem).start()
        return 0

    lax.fori_loop(0, tc // _DMA_UNROLL, body, 0)
    pltpu.make_async_copy(y_ref.at[pl.ds(0, tc)], ya_ref, sem).wait()
    pltpu.make_async_copy(y_ref.at[pl.ds(0, tc)], yb_ref, sem).wait()

    info = info_ref[...]
    w0 = info[:, _L_W0:_L_W0 + 1]
    w1 = info[:, _L_W1:_L_W1 + 1]
    moe = ya_ref[...] * w0 + yb_ref[...] * w1
    x2 = x1_ref[...] + mod_ref[0][5:6] * moe
    o_ref[...] = x2 * lax.rsqrt(jnp.mean(x2 * x2, axis=-1, keepdims=True) + NORM_EPS) * nw_ref[...]


def _combine(d0, d1, x1, info, mod3, nfw, y, seq, tc=256):
    n, d = x1.shape
    per_b = seq // tc
    kern = functools.partial(_combine_kernel, tc=tc)
    return pl.pallas_call(
        kern,
        grid_spec=pltpu.PrefetchScalarGridSpec(
            num_scalar_prefetch=2,
            grid=(n // tc,),
            in_specs=[pl.BlockSpec((tc, d), lambda i, d0, d1: (i, 0)),
                      pl.BlockSpec((tc, LANES), lambda i, d0, d1: (i, 0)),
                      pl.BlockSpec((1, N_MOD, d), lambda i, d0, d1: (i // per_b, 0, 0)),
                      pl.BlockSpec((1, d), lambda i, d0, d1: (0, 0)),
                      pl.BlockSpec(memory_space=pl.ANY)],
            out_specs=pl.BlockSpec((tc, d), lambda i, d0, d1: (i, 0)),
            scratch_shapes=[pltpu.VMEM((tc, d), F32), pltpu.VMEM((tc, d), F32),
                            pltpu.SemaphoreType.DMA(())]),
        out_shape=jax.ShapeDtypeStruct((n, d), F32),
        compiler_params=_cparams(("arbitrary",)),
        name="combine",
    )(d0, d1, x1, info, mod3, nfw, y)


def _split_hi_lo(w):
    hi = w.astype(BF16)
    lo = (w - hi.astype(F32)).astype(BF16)
    return hi, lo


def kernel(x, c, w_ada, b_ada, norm_mix_w, w_in, ret_norm_w, w_out, norm_ffn_w,
           w_group, b_group, w_router, b_router, w1, w3, w2, norm_final_w):
    batch, seq, d = x.shape
    n = batch * seq
    depth = w_ada.shape[0]
    assert depth == 1, "the final rmsnorm is fused into the (single) layer's last kernel"
    tm = min(1024, seq)
    half = LANES // 2

    slopes = jnp.exp2(-8.0 * jnp.arange(1, N_ATTN_HEADS + 1, dtype=F32) / N_ATTN_HEADS)
    log_gamma = jnp.log1p(-jnp.exp2(-5.0 - jnp.arange(RET_HEADS, dtype=F32)))
    n_tiles = (2 * n) // FFN_TILE + N_EXPERTS
    p_rows = n_tiles * FFN_TILE

    c_pad = jnp.zeros((8, d), F32).at[:batch].set(c)
    x2d = x.reshape(n, d)
    for l in range(depth):
        mod = _adaln(c_pad, w_ada[l], b_ada[l].reshape(1, -1))
        mod3 = mod[:batch].reshape(batch, N_MOD, d)

        proj, kmean = _inproj(x2d, mod3, norm_mix_w[l].reshape(1, d), w_in[l], seq, tm)
        kmean = kmean.reshape(batch, seq // MOBA_BLOCK, ATTN_WIDTH)
        attn = _moba(proj, kmean, slopes, batch, seq)
        ret = _retention(proj, log_gamma, ret_norm_w[l].reshape(1, RET_WIDTH), batch, seq)

        wr = jnp.concatenate([w_group[l], jnp.transpose(w_router[l], (1, 0, 2)).reshape(d, N_EXPERTS)], axis=1)
        wr = jnp.pad(wr, ((0, 0), (0, half - wr.shape[1])))
        wr_hi, wr_lo = _split_hi_lo(wr)
        wr_cat = jnp.concatenate([jnp.concatenate([wr_hi, wr_lo], axis=1),
                                  jnp.concatenate([wr_hi, jnp.zeros_like(wr_lo)], axis=1)], axis=0)
        br = jnp.concatenate([b_group[l], b_router[l].reshape(-1)])
        br = jnp.pad(br, (0, LANES - br.shape[0])).reshape(1, LANES)

        x1, h2p, logits = _outproj(attn, ret, w_out[l], x2d, mod3,
                                   norm_ffn_w[l].reshape(1, d), wr_cat, br, seq, min(512, seq))

        info, meta = _plan(logits, n_tiles)
        d0 = info[:, _L_D0].astype(I32)
        d1 = info[:, _L_D1].astype(I32)
        tile_e = meta[0, :n_tiles]
        n_valid = meta[1, :1]

        xs = _dispatch(d0, d1, h2p, jnp.zeros((p_rows, d), F32))
        y = _ffn(tile_e, n_valid, xs, w1[l], w3[l], w2[l])
        nfw = norm_final_w.reshape(1, d)
        x2d = _combine(d0, d1, x1, info, mod3, nfw, y, seq)
    return x2d.reshape(batch, seq, d)
```

```python
import functools

import jax
import jax.numpy as jnp
from jax import lax
from jax.experimental import pallas as pl
from jax.experimental.pallas import tpu as pltpu

F32 = jnp.float32
BF16 = jnp.bfloat16
I32 = jnp.int32

N_ATTN_HEADS = 8
ATTN_HEAD_DIM = 128
ATTN_WIDTH = N_ATTN_HEADS * ATTN_HEAD_DIM
MOBA_BLOCK = 256
MOBA_TOPK = 3
RET_HEADS = 4
RET_HEAD_DIM = 256
RET_WIDTH = RET_HEADS * RET_HEAD_DIM
N_GROUPS = 4
EXPERTS_PER_GROUP = 8
N_EXPERTS = N_GROUPS * EXPERTS_PER_GROUP
N_MOD = 6
NORM_EPS = 1e-6

LANES = 128
RET_CHUNK = 256
FFN_TILE = 256
MASK_VALUE = -1e30
VMEM_LIMIT = 56 * 1024 * 1024

_DN_LAST = (((1,), (1,)), ((), ()))
_DN_FIRST = (((0,), (0,)), ((), ()))


def _cparams(sem):
    return pltpu.CompilerParams(dimension_semantics=sem, vmem_limit_bytes=VMEM_LIMIT)


def _silu(v):
    return v * jax.nn.sigmoid(v)


def _adaln_kernel(c_ref, w_ref, b_ref, o_ref):
    ca = _silu(c_ref[...]).astype(BF16)
    o_ref[...] = jnp.dot(ca, w_ref[...].astype(BF16), preferred_element_type=F32) + b_ref[...]


def _adaln(c_pad, w_ada, b_ada, tn=1024):
    rows, d = c_pad.shape
    n = w_ada.shape[1]
    return pl.pallas_call(
        _adaln_kernel,
        grid=(n // tn,),
        in_specs=[pl.BlockSpec((rows, d), lambda j: (0, 0)),
                  pl.BlockSpec((d, tn), lambda j: (0, j)),
                  pl.BlockSpec((1, tn), lambda j: (0, j))],
        out_specs=pl.BlockSpec((rows, tn), lambda j: (0, j)),
        out_shape=jax.ShapeDtypeStruct((rows, n), F32),
        compiler_params=_cparams(("arbitrary",)),
        name="adaln",
    )(c_pad, w_ada, b_ada)


def _inproj_kernel(x_ref, mod_ref, nw_ref, w_ref, o_ref, km_ref, h_ref, *, tm, tn):
    j = pl.program_id(1)

    @pl.when(j == 0)
    def _():
        x = x_ref[...]
        y = x * lax.rsqrt(jnp.mean(x * x, axis=-1, keepdims=True) + NORM_EPS) * nw_ref[...]
        m = mod_ref[0]
        h_ref[...] = (y * (1.0 + m[1:2]) + m[0:1]).astype(BF16)

    acc = jnp.dot(h_ref[...], w_ref[...].astype(BF16), preferred_element_type=F32)
    o_ref[...] = acc.astype(BF16)

    @pl.when(j == 1)
    def _():
        km_ref[0] = jnp.sum(acc.reshape(tm // MOBA_BLOCK, MOBA_BLOCK, tn), axis=1) * (1.0 / MOBA_BLOCK)


def _inproj(x2d, mod3, nw, w_bf, seq, tm, tn=ATTN_WIDTH):
    n, d = x2d.shape
    width = w_bf.shape[1]
    per_b = seq // tm
    kern = functools.partial(_inproj_kernel, tm=tm, tn=tn)
    return pl.pallas_call(
        kern,
        grid=(n // tm, width // tn),
        in_specs=[pl.BlockSpec((tm, d), lambda i, j: (i, 0)),
                  pl.BlockSpec((1, N_MOD, d), lambda i, j: (i // per_b, 0, 0)),
                  pl.BlockSpec((1, d), lambda i, j: (0, 0)),
                  pl.BlockSpec((d, tn), lambda i, j: (0, j))],
        out_specs=[pl.BlockSpec((tm, tn), lambda i, j: (i, j)),
                   pl.BlockSpec((1, tm // MOBA_BLOCK, tn), lambda i, j: (i, 0, 0))],
        out_shape=[jax.ShapeDtypeStruct((n, width), BF16),
                   jax.ShapeDtypeStruct((n // tm, tm // MOBA_BLOCK, tn), F32)],
        scratch_shapes=[pltpu.VMEM((tm, d), BF16)],
        compiler_params=_cparams(("arbitrary", "arbitrary")),
        name="inproj",
    )(x2d, mod3, nw, w_bf)


def _moba_kernel(slopes_ref, q_ref, k_ref, v_ref, km_ref, o_ref,
                 vt_ref, sel_ref, bias_ref, acc_ref, *, nb, hps):
    hg = pl.program_id(1)
    i = pl.program_id(2)
    blk = MOBA_BLOCK
    dh = ATTN_HEAD_DIM
    log2e = 1.4426950408889634
    qk_scale = (dh ** -0.5) * log2e

    qpos = lax.broadcasted_iota(I32, (blk, blk), 1)
    kpos = lax.broadcasted_iota(I32, (blk, blk), 0)

    @pl.when(i == 0)
    def _():
        vt_ref[...] = v_ref[...].astype(F32).T.astype(BF16)
        dist = (qpos - kpos).astype(F32)
        for g in range(hps):
            bias_ref[g] = dist * (-log2e * slopes_ref[hg * hps + g])

    bidx = lax.broadcasted_iota(I32, (nb, blk), 0)
    past = bidx < i
    i0 = pl.multiple_of(i * blk, blk)
    heads = [slice(g * dh, (g + 1) * dh) for g in range(hps)]
    gates, raw_own = [], []
    for cols in heads:
        q = q_ref[:, cols]
        km = km_ref[0, :, cols]
        km_hi = km.astype(BF16)
        km_lo = (km - km_hi.astype(F32)).astype(BF16)
        gates.append(lax.dot_general(km_hi, q, _DN_LAST, preferred_element_type=F32)
                     + lax.dot_general(km_lo, q, _DN_LAST, preferred_element_type=F32))
        raw_own.append(lax.dot_general(k_ref[pl.ds(i0, blk), cols], q, _DN_LAST,
                                       preferred_element_type=F32))
    own_bias = [bias_ref[g] for g in range(hps)]

    carry0, acc0, sels = [], [], []
    for g, cols in enumerate(heads):
        gate = jnp.where(past, gates[g], -jnp.inf)
        rank = jnp.zeros((nb, blk), F32)
        for m in range(nb):
            gm = gate[m:m + 1, :]
            beats = jnp.where(gm > gate, 1.0, jnp.where((gm == gate) & (m < bidx), 1.0, 0.0))
            rank = rank + jnp.where(m < i, beats, 0.0)
        sels.append(jnp.where(past & (rank < MOBA_TOPK), 1.0, 0.0))

        s = jnp.where(qpos >= kpos, raw_own[g] * qk_scale + own_bias[g], MASK_VALUE)
        m0 = jnp.max(s, axis=0, keepdims=True)
        p = jnp.exp2(s - m0)
        acc0.append(jnp.dot(vt_ref[cols, pl.ds(i0, blk)], p.astype(BF16), preferred_element_type=F32))
        carry0 += [m0, jnp.sum(p, axis=0, keepdims=True)]
    for g in range(hps):
        acc_ref[g] = acc0[g]
        sel_ref[g] = sels[g]

    def body(j, carry):
        j0 = pl.multiple_of(j * blk, blk)
        off = ((i - j) * blk).astype(F32) * log2e
        raw = [lax.dot_general(k_ref[pl.ds(j0, blk), cols], q_ref[:, cols], _DN_LAST,
                               preferred_element_type=F32) for cols in heads]
        chosen = [sel_ref[g, pl.ds(j, 1), :] > 0.0 for g in range(hps)]
        acc_old = [acc_ref[g] for g in range(hps)]
        out, acc_new = [], []
        for g, cols in enumerate(heads):
            m_run, l_run = carry[2 * g], carry[2 * g + 1]
            shift = -slopes_ref[hg * hps + g] * off
            s = raw[g] * qk_scale + bias_ref[g]
            m_new = jnp.maximum(m_run, jnp.where(chosen[g], jnp.max(s, axis=0, keepdims=True) + shift,
                                                 MASK_VALUE))
            alpha = jnp.exp2(m_run - m_new)
            p = jnp.exp2(s - jnp.where(chosen[g], m_new - shift, -MASK_VALUE))
            acc_new.append(alpha * acc_old[g] + jnp.dot(vt_ref[cols, pl.ds(j0, blk)], p.astype(BF16),
                                                        preferred_element_type=F32))
            out += [m_new, alpha * l_run + jnp.sum(p, axis=0, keepdims=True)]
        for g in range(hps):
            acc_ref[g] = acc_new[g]
        return tuple(out)

    fin = lax.fori_loop(0, i, body, tuple(carry0))
    for g in range(hps):
        o_ref[:, g * dh:(g + 1) * dh] = (acc_ref[g] / fin[2 * g + 1]).T.astype(BF16)


MOBA_HEADS_PER_STEP = 4


def _moba(proj, kmean, slopes, batch, seq):
    n = proj.shape[0]
    nb = seq // MOBA_BLOCK
    hps = MOBA_HEADS_PER_STEP
    w = hps * ATTN_HEAD_DIM
    ng = N_ATTN_HEADS // hps
    kern = functools.partial(_moba_kernel, nb=nb, hps=hps)
    return pl.pallas_call(
        kern,
        grid=(batch, ng, nb),
        in_specs=[pl.BlockSpec(memory_space=pltpu.SMEM),
                  pl.BlockSpec((MOBA_BLOCK, w), lambda b, h, i: (b * nb + i, h)),
                  pl.BlockSpec((seq, w), lambda b, h, i: (b, ng + h)),
                  pl.BlockSpec((seq, w), lambda b, h, i: (b, 2 * ng + h)),
                  pl.BlockSpec((1, nb, w), lambda b, h, i: (b, 0, h))],
        out_specs=pl.BlockSpec((MOBA_BLOCK, w), lambda b, h, i: (b * nb + i, h)),
        out_shape=jax.ShapeDtypeStruct((n, ATTN_WIDTH), BF16),
        scratch_shapes=[pltpu.VMEM((w, seq), BF16),
                        pltpu.VMEM((hps, nb, MOBA_BLOCK), F32),
                        pltpu.VMEM((hps, MOBA_BLOCK, MOBA_BLOCK), F32),
                        pltpu.VMEM((hps, ATTN_HEAD_DIM, MOBA_BLOCK), F32)],
        compiler_params=_cparams(("arbitrary", "arbitrary", "arbitrary")),
        name="moba",
    )(slopes, proj, proj, proj, kmean)


def _ret_kernel(lg_ref, q_ref, k_ref, v_ref, g_ref, rnw_ref, o_ref,
                state_ref, dmask_ref, qdec_ref, kdec_ref, cdec_ref):
    n = pl.program_id(1)
    c = RET_CHUNK
    dk = RET_HEAD_DIM
    kscale = dk ** -0.5

    @pl.when(n == 0)
    def _():
        state_ref[...] = jnp.zeros_like(state_ref)
        ti = lax.broadcasted_iota(I32, (c, c), 0)
        si = lax.broadcasted_iota(I32, (c, c), 1)
        diff = jnp.maximum(ti - si, 0).astype(F32)
        pos = lax.broadcasted_iota(I32, (c, dk), 0).astype(F32)
        for h in range(RET_HEADS):
            lg = lg_ref[h]
            dmask_ref[h] = jnp.where(ti >= si, jnp.exp(diff * lg) * kscale, 0.0)
            qdec_ref[h] = jnp.exp((pos + 1.0) * lg)
            kdec_ref[h] = jnp.exp((c - 1.0 - pos) * lg) * kscale
            cdec_ref[h] = jnp.exp(jnp.full((1, dk), float(c), F32) * lg)

    heads = [slice(h * dk, (h + 1) * dk) for h in range(RET_HEADS)]
    qs = [q_ref[:, cols] for cols in heads]
    ks = [k_ref[:, cols] for cols in heads]
    vs = [v_ref[:, cols] for cols in heads]
    states = [state_ref[h] for h in range(RET_HEADS)]
    raw = [lax.dot_general(qs[h], ks[h], _DN_LAST, preferred_element_type=F32) for h in range(RET_HEADS)]
    cross = [jnp.dot(qs[h], states[h].astype(BF16), preferred_element_type=F32) for h in range(RET_HEADS)]
    outs, new_states = [], []
    for h, cols in enumerate(heads):
        inner = jnp.dot((raw[h] * dmask_ref[h]).astype(BF16), vs[h], preferred_element_type=F32)
        kd = (ks[h].astype(F32) * kdec_ref[h]).astype(BF16)
        kv = lax.dot_general(kd, vs[h], _DN_FIRST, preferred_element_type=F32)
        new_states.append(states[h] * cdec_ref[h] + kv)
        r = inner + cross[h] * qdec_ref[h]
        rn = r * lax.rsqrt(jnp.mean(r * r, axis=-1, keepdims=True) + NORM_EPS)
        g = g_ref[:, cols].astype(F32)
        outs.append((rn * rnw_ref[:, cols] * _silu(g)).astype(BF16))
    for h, cols in enumerate(heads):
        state_ref[h] = new_states[h]
        o_ref[:, cols] = outs[h]


def _retention(proj, log_gamma, rnw, batch, seq):
    n = proj.shape[0]
    c = RET_CHUNK
    nc = seq // c
    dk = RET_HEAD_DIM
    base = 3 * ATTN_WIDTH // RET_WIDTH

    def col(off):
        return lambda b, t: (b * nc + t, base + off)

    return pl.pallas_call(
        _ret_kernel,
        grid=(batch, nc),
        in_specs=[pl.BlockSpec(memory_space=pltpu.SMEM),
                  pl.BlockSpec((c, RET_WIDTH), col(0)),
                  pl.BlockSpec((c, RET_WIDTH), col(1)),
                  pl.BlockSpec((c, RET_WIDTH), col(2)),
                  pl.BlockSpec((c, RET_WIDTH), col(3)),
                  pl.BlockSpec((1, RET_WIDTH), lambda b, t: (0, 0))],
        out_specs=pl.BlockSpec((c, RET_WIDTH), lambda b, t: (b * nc + t, 0)),
        out_shape=jax.ShapeDtypeStruct((n, RET_WIDTH), BF16),
        scratch_shapes=[pltpu.VMEM((RET_HEADS, dk, dk), F32), pltpu.VMEM((RET_HEADS, c, c), F32),
                        pltpu.VMEM((RET_HEADS, c, dk), F32), pltpu.VMEM((RET_HEADS, c, dk), F32),
                        pltpu.VMEM((RET_HEADS, 1, dk), F32)],
        compiler_params=_cparams(("arbitrary", "arbitrary")),
        name="retention",
    )(log_gamma, proj, proj, proj, proj, rnw)


def _outproj_kernel(a_ref, r_ref, w_ref, x_ref, mod_ref, nw_ref, wr_ref, br_ref,
                    x1_ref, hp_ref, lg_ref, wb_ref):
    @pl.when(pl.program_id(0) == 0)
    def _():
        wb_ref[...] = w_ref[...].astype(BF16)

    mix = jnp.dot(jnp.concatenate([a_ref[...], r_ref[...]], axis=1), wb_ref[...],
                  preferred_element_type=F32)
    m = mod_ref[0]
    x1 = x_ref[...] + m[2:3] * mix
    x1_ref[...] = x1
    y = x1 * lax.rsqrt(jnp.mean(x1 * x1, axis=-1, keepdims=True) + NORM_EPS) * nw_ref[...]
    h2 = y * (1.0 + m[4:5]) + m[3:4]
    hp_ref[...] = h2
    hi = h2.astype(BF16)
    lo = (h2 - hi.astype(F32)).astype(BF16)
    part = jnp.dot(jnp.concatenate([hi, lo], axis=1), wr_ref[...], preferred_element_type=F32)
    lg_ref[...] = part + pltpu.roll(part, LANES // 2, axis=1) + br_ref[...]


def _outproj(attn, ret, w_bf, x2d, mod3, nw, wr_cat, br, seq, tm):
    n, d = x2d.shape
    per_b = seq // tm
    return pl.pallas_call(
        _outproj_kernel,
        grid=(n // tm,),
        in_specs=[pl.BlockSpec((tm, ATTN_WIDTH), lambda i: (i, 0)),
                  pl.BlockSpec((tm, RET_WIDTH), lambda i: (i, 0)),
                  pl.BlockSpec((d, d), lambda i: (0, 0), pipeline_mode=pl.Buffered(1)),
                  pl.BlockSpec((tm, d), lambda i: (i, 0)),
                  pl.BlockSpec((1, N_MOD, d), lambda i: (i // per_b, 0, 0)),
                  pl.BlockSpec((1, d), lambda i: (0, 0)),
                  pl.BlockSpec((2 * d, LANES), lambda i: (0, 0)),
                  pl.BlockSpec((1, LANES), lambda i: (0, 0))],
        out_specs=[pl.BlockSpec((tm, d), lambda i: (i, 0)),
                   pl.BlockSpec((tm, d), lambda i: (i, 0)),
                   pl.BlockSpec((tm, LANES), lambda i: (i, 0))],
        out_shape=[jax.ShapeDtypeStruct((n, d), F32),
                   jax.ShapeDtypeStruct((n, d), F32),
                   jax.ShapeDtypeStruct((n, LANES), F32)],
        scratch_shapes=[pltpu.VMEM((d, d), BF16)],
        compiler_params=_cparams(("arbitrary",)),
        name="outproj",
    )(attn, ret, w_bf, x2d, mod3, nw, wr_cat, br)


_L_E0, _L_E1, _L_W0, _L_W1, _L_D0, _L_D1 = 0, 1, 2, 3, 4, 5
_M_TILE_E, _M_NVALID, _M_NTILES, _M_COUNT, _M_START = 0, 1, 2, 3, 4
_PLAN_ROWS = 256


def _first_lane_where(cond, lane):
    return jnp.min(jnp.where(cond, lane, float(LANES)), axis=1, keepdims=True)


def _plan_kernel(lg_ref, info_ref, meta_ref, *, n_tok, n_tiles):
    tb = _PLAN_ROWS
    lane = lax.broadcasted_iota(I32, (tb, LANES), 1).astype(F32)
    row = lax.broadcasted_iota(I32, (tb, tb), 0)
    colm = lax.broadcasted_iota(I32, (tb, tb), 1)
    tri = jnp.where(colm < row, 1.0, 0.0).astype(BF16)
    n_g, n_e = N_GROUPS, EXPERTS_PER_GROUP

    def route(blk_i, carry):
        r0 = pl.multiple_of(blk_i * tb, tb)
        lg = lg_ref[pl.ds(r0, tb), :]
        gl = jnp.where(lane < n_g, lg, -jnp.inf)
        ge = jnp.exp(gl - jnp.max(gl, axis=1, keepdims=True))
        gp = ge / jnp.sum(ge, axis=1, keepdims=True)
        g_w = jnp.max(gp, axis=1, keepdims=True)
        g_sel = _first_lane_where((gp == g_w) & (lane < n_g), lane)
        lo_lane = n_g + n_e * g_sel
        in_grp = (lane >= lo_lane) & (lane < lo_lane + n_e)
        el = jnp.where(in_grp, lg, -jnp.inf)
        ee = jnp.exp(el - jnp.max(el, axis=1, keepdims=True))
        ep = jnp.where(in_grp, ee / jnp.sum(ee, axis=1, keepdims=True), -1.0)
        p1 = jnp.max(ep, axis=1, keepdims=True)
        i1 = _first_lane_where(ep == p1, lane)
        ep2 = jnp.where(lane == i1, -1.0, ep)
        p2 = jnp.max(ep2, axis=1, keepdims=True)
        i2 = _first_lane_where(ep2 == p2, lane)
        denom = p1 + p2
        w0 = g_w * p1 / denom
        w1 = g_w * p2 / denom
        e0 = i1 - n_g
        e1 = i2 - n_g
        oh0 = jnp.where(lane == e0, 1.0, 0.0)
        oh1 = jnp.where(lane == e1, 1.0, 0.0)
        oh = oh0 + oh1
        before = jnp.dot(tri, oh.astype(BF16), preferred_element_type=F32) + carry
        rank0 = jnp.sum(oh0 * before, axis=1, keepdims=True)
        rank1 = jnp.sum(oh1 * before, axis=1, keepdims=True)
        info = jnp.where(lane == _L_E0, e0,
               jnp.where(lane == _L_E1, e1,
               jnp.where(lane == _L_W0, w0,
               jnp.where(lane == _L_W1, w1,
               jnp.where(lane == _L_D0, rank0,
               jnp.where(lane == _L_D1, rank1, 0.0))))))
        info_ref[pl.ds(r0, tb), :] = info
        return carry + jnp.sum(oh, axis=0, keepdims=True)

    counts = lax.fori_loop(0, n_tok // tb, route, jnp.zeros((1, LANES), F32))

    lane1 = lax.broadcasted_iota(I32, (1, LANES), 1)
    padded = jnp.floor((counts + (FFN_TILE - 1.0)) * (1.0 / FFN_TILE)) * FFN_TILE
    pad_end = padded
    sh = 1
    while sh < N_EXPERTS:
        pad_end = pad_end + jnp.where(lane1 >= sh, pltpu.roll(pad_end, sh, axis=1), 0.0)
        sh *= 2
    pad_start = pad_end - padded

    def place(blk_i, _):
        r0 = pl.multiple_of(blk_i * tb, tb)
        info = info_ref[pl.ds(r0, tb), :]
        e0 = info[:, _L_E0:_L_E0 + 1]
        e1 = info[:, _L_E1:_L_E1 + 1]
        s0 = jnp.sum(jnp.where(lane == e0, pad_start, 0.0), axis=1, keepdims=True)
        s1 = jnp.sum(jnp.where(lane == e1, pad_start, 0.0), axis=1, keepdims=True)
        add = jnp.where(lane == _L_D0, s0, jnp.where(lane == _L_D1, s1, 0.0))
        info_ref[pl.ds(r0, tb), :] = info + add
        return 0

    lax.fori_loop(0, n_tok // tb, place, 0)

    sq_r = lax.broadcasted_iota(I32, (LANES, LANES), 0)
    sq_c = lax.broadcasted_iota(I32, (LANES, LANES), 1)
    pe_col = jnp.sum(jnp.where(sq_r == sq_c, pad_end, 0.0), axis=1, keepdims=True)
    mt = meta_ref.shape[1]
    t_start = lax.broadcasted_iota(I32, (LANES, mt), 1).astype(F32) * FFN_TILE
    e_row = lax.broadcasted_iota(I32, (LANES, mt), 0)
    ended = jnp.where((pe_col <= t_start) & (e_row < N_EXPERTS), 1.0, 0.0)
    tile_e = jnp.minimum(jnp.sum(ended, axis=0, keepdims=True), N_EXPERTS - 1.0)
    total = jnp.sum(jnp.where(lane1 == N_EXPERTS - 1, pad_end, 0.0), axis=1, keepdims=True)
    n_valid = jnp.broadcast_to(total * (1.0 / FFN_TILE), (1, mt))
    def per_expert(v):
        return v if mt == LANES else jnp.concatenate([v, jnp.zeros((1, mt - LANES), F32)], axis=1)

    mrow = lax.broadcasted_iota(I32, (8, mt), 0)
    meta = jnp.where(mrow == _M_TILE_E, tile_e,
           jnp.where(mrow == _M_NVALID, n_valid,
           jnp.where(mrow == _M_NTILES, per_expert(padded * (1.0 / FFN_TILE)),
           jnp.where(mrow == _M_COUNT, per_expert(counts),
           jnp.where(mrow == _M_START, per_expert(pad_start), 0.0)))))
    meta_ref[...] = meta.astype(I32)


def _plan(logits, n_tiles):
    n_tok = logits.shape[0]
    mt = -(-n_tiles // LANES) * LANES
    kern = functools.partial(_plan_kernel, n_tok=n_tok, n_tiles=n_tiles)
    return pl.pallas_call(
        kern,
        out_shape=[jax.ShapeDtypeStruct((n_tok, LANES), F32),
                   jax.ShapeDtypeStruct((8, mt), I32)],
        compiler_params=pltpu.CompilerParams(vmem_limit_bytes=VMEM_LIMIT),
        name="plan",
    )(logits)


_DMA_UNROLL = 8


def _invert_kernel(d0_ref, d1_ref, nv_ref, ntl_ref, cnt_ref, start_ref, tok_ref, *, n_tok, p_rows):
    def zero_rows(lo, hi):
        def zbody(p, _):
            tok_ref[p] = 0
            return 0
        lax.fori_loop(lo, hi, zbody, 0)

    def pad_body(e, _):
        zero_rows(start_ref[e] + cnt_ref[e], start_ref[e] + ntl_ref[e] * FFN_TILE)
        return 0

    lax.fori_loop(0, N_EXPERTS, pad_body, 0)
    zero_rows(nv_ref[0] * FFN_TILE, p_rows)

    def body(u, _):
        for s in range(_DMA_UNROLL):
            t = u * _DMA_UNROLL + s
            tok_ref[d0_ref[t]] = t
            tok_ref[d1_ref[t]] = t
        return 0

    lax.fori_loop(0, n_tok // _DMA_UNROLL, body, 0)


def _invert(d0, d1, n_valid, ntl, cnt, start, p_rows):
    n_tok = d0.shape[0]
    kern = functools.partial(_invert_kernel, n_tok=n_tok, p_rows=p_rows)
    return pl.pallas_call(
        kern,
        grid_spec=pltpu.PrefetchScalarGridSpec(
            num_scalar_prefetch=6,
            grid=(1,),
            in_specs=[],
            out_specs=pl.BlockSpec(memory_space=pltpu.SMEM)),
        out_shape=jax.ShapeDtypeStruct((p_rows,), I32),
        compiler_params=_cparams(("arbitrary",)),
        name="invert",
    )(d0, d1, n_valid, ntl, cnt, start)


def _ffn_kernel(te_ref, nv_ref, ntl_ref, tok_ref, h_ref, w1_ref, w3_ref, w2_ref, y_ref,
                xbuf, xb_ref, w1f, w3f, w2f, w1b, w3b, w2b, gsem, wsem, run_ref, *, n_tiles):
    i = pl.program_id(0)
    nv = nv_ref[0]
    valid = i < nv
    slot = i % 2

    def row_copy(tile, r, dst_slot):
        tok = tok_ref[tile * FFN_TILE + r]
        return pltpu.make_async_copy(h_ref.at[pl.ds(tok, 1)], xbuf.at[dst_slot, pl.ds(r, 1)],
                                     gsem.at[dst_slot])

    def weight_copies(e, ws):
        return (pltpu.make_async_copy(w1_ref.at[e], w1f.at[ws], wsem.at[ws]),
                pltpu.make_async_copy(w3_ref.at[e], w3f.at[ws], wsem.at[ws]),
                pltpu.make_async_copy(w2_ref.at[e], w2f.at[ws], wsem.at[ws]))

    @pl.when(i == 0)
    def _():
        run_ref[0] = 0
        for cp in weight_copies(te_ref[0], 0):
            cp.start()

        def first_rows(u, _):
            for s in range(_DMA_UNROLL):
                row_copy(0, u * _DMA_UNROLL + s, 0).start()
            return 0
        lax.fori_loop(0, FFN_TILE // _DMA_UNROLL, first_rows, 0)

    @pl.when(valid & ((i == 0) | (te_ref[i] != te_ref[jnp.maximum(i, 1) - 1])))
    def _():
        run = run_ref[0]
        ws = run % 2
        for cp in weight_copies(te_ref[i], ws):
            cp.wait()
        w1b[...] = w1f[ws].astype(BF16)
        w3b[...] = w3f[ws].astype(BF16)
        w2b[...] = w2f[ws].astype(BF16)
        nxt = i + ntl_ref[te_ref[i]]

        @pl.when(nxt < nv)
        def _():
            for cp in weight_copies(te_ref[jnp.minimum(nxt, n_tiles - 1)], 1 - ws):
                cp.start()
        run_ref[0] = run + 1

    @pl.when(valid)
    def _():
        pltpu.make_async_copy(h_ref.at[pl.ds(0, FFN_TILE)], xbuf.at[slot], gsem.at[slot]).wait()
        xb_ref[...] = xbuf[slot].astype(BF16)
        nxt_tile = jnp.minimum(i + 1, n_tiles - 1)
        for r in range(FFN_TILE):
            row_copy(nxt_tile, r, 1 - slot).start()
        x = xb_ref[...]
        a = jnp.dot(x, w1b[...], preferred_element_type=F32)
        b = jnp.dot(x, w3b[...], preferred_element_type=F32)
        hmid = (_silu(a) * b).astype(BF16)
        y_ref[...] = jnp.dot(hmid, w2b[...], preferred_element_type=F32)

    def drain(s):
        pltpu.make_async_copy(h_ref.at[pl.ds(0, FFN_TILE)], xbuf.at[s], gsem.at[s]).wait()

    @pl.when(valid & (i == n_tiles - 1))
    def _():
        drain(1 - slot)

    @pl.when(jnp.logical_not(valid))
    def _():
        @pl.when(i == nv)
        def _():
            drain(slot)
        y_ref[...] = jnp.zeros_like(y_ref)


def _ffn(tile_e, n_valid, ntl, row_tok, h2, w1, w3, w2):
    p_rows = row_tok.shape[0]
    _, d, de = w1.shape
    n_tiles = p_rows // FFN_TILE
    kern = functools.partial(_ffn_kernel, n_tiles=n_tiles)
    any_spec = pl.BlockSpec(memory_space=pl.ANY)
    return pl.pallas_call(
        kern,
        grid_spec=pltpu.PrefetchScalarGridSpec(
            num_scalar_prefetch=4,
            grid=(n_tiles,),
            in_specs=[any_spec, any_spec, any_spec, any_spec],
            out_specs=pl.BlockSpec((FFN_TILE, d), lambda i, *_: (i, 0)),
            scratch_shapes=[pltpu.VMEM((2, FFN_TILE, d), F32), pltpu.VMEM((FFN_TILE, d), BF16),
                            pltpu.VMEM((2, d, de), F32), pltpu.VMEM((2, d, de), F32),
                            pltpu.VMEM((2, de, d), F32),
                            pltpu.VMEM((d, de), BF16), pltpu.VMEM((d, de), BF16),
                            pltpu.VMEM((de, d), BF16),
                            pltpu.SemaphoreType.DMA((2,)), pltpu.SemaphoreType.DMA((2,)),
                            pltpu.SMEM((1,), I32)]),
        out_shape=jax.ShapeDtypeStruct((p_rows, d), F32),
        compiler_params=_cparams(("arbitrary",)),
        name="ffn",
    )(tile_e, n_valid, ntl, row_tok, h2, w1, w3, w2)


def _combine_kernel(d0_ref, d1_ref, x1_ref, info_ref, mod_ref, nw_ref, y_ref, o_ref,
                    ya0, yb0, ya1, yb1, sem, *, th, n_steps):
    i = pl.program_id(0)
    bufs = ((ya0, yb0), (ya1, yb1))

    def row_copies(tok, r, half):
        ya, yb = bufs[half]
        return (pltpu.make_async_copy(y_ref.at[pl.ds(d0_ref[tok], 1)], ya.at[pl.ds(r, 1)], sem.at[half]),
                pltpu.make_async_copy(y_ref.at[pl.ds(d1_ref[tok], 1)], yb.at[pl.ds(r, 1)], sem.at[half]))

    def wait_half(half):
        for buf in bufs[half]:
            pltpu.make_async_copy(y_ref.at[pl.ds(0, th)], buf, sem.at[half]).wait()

    def combine_half(half):
        ya, yb = bufs[half]
        rows = pl.ds(half * th, th)
        info = info_ref[rows, :]
        moe = ya[...] * info[:, _L_W0:_L_W0 + 1] + yb[...] * info[:, _L_W1:_L_W1 + 1]
        x2 = x1_ref[rows, :] + mod_ref[0][5:6] * moe
        o_ref[rows, :] = x2 * lax.rsqrt(jnp.mean(x2 * x2, axis=-1, keepdims=True) + NORM_EPS) * nw_ref[...]

    @pl.when(i == 0)
    def _():
        def first_rows(u, _):
            for s in range(_DMA_UNROLL):
                r = u * _DMA_UNROLL + s
                for cp in row_copies(r, r, 0):
                    cp.start()
            return 0
        lax.fori_loop(0, th // _DMA_UNROLL, first_rows, 0)

    base = i * 2 * th
    wait_half(0)
    for r in range(th):
        for cp in row_copies(base + th + r, r, 1):
            cp.start()
    combine_half(0)

    nxt = jnp.minimum(i + 1, n_steps - 1) * 2 * th
    wait_half(1)
    for r in range(th):
        for cp in row_copies(nxt + r, r, 0):
            cp.start()
    combine_half(1)

    @pl.when(i == n_steps - 1)
    def _():
        wait_half(0)


def _combine(d0, d1, x1, info, mod3, nfw, y, seq, th=256):
    n, d = x1.shape
    tc = 2 * th
    per_b = seq // tc
    kern = functools.partial(_combine_kernel, th=th, n_steps=n // tc)
    return pl.pallas_call(
        kern,
        grid_spec=pltpu.PrefetchScalarGridSpec(
            num_scalar_prefetch=2,
            grid=(n // tc,),
            in_specs=[pl.BlockSpec((tc, d), lambda i, d0, d1: (i, 0)),
                      pl.BlockSpec((tc, LANES), lambda i, d0, d1: (i, 0)),
                      pl.BlockSpec((1, N_MOD, d), lambda i, d0, d1: (i // per_b, 0, 0)),
                      pl.BlockSpec((1, d), lambda i, d0, d1: (0, 0)),
                      pl.BlockSpec(memory_space=pl.ANY)],
            out_specs=pl.BlockSpec((tc, d), lambda i, d0, d1: (i, 0)),
            scratch_shapes=[pltpu.VMEM((th, d), F32), pltpu.VMEM((th, d), F32),
                            pltpu.VMEM((th, d), F32), pltpu.VMEM((th, d), F32),
                            pltpu.SemaphoreType.DMA((2,))]),
        out_shape=jax.ShapeDtypeStruct((n, d), F32),
        compiler_params=_cparams(("arbitrary",)),
        name="combine",
    )(d0, d1, x1, info, mod3, nfw, y)


def _split_hi_lo(w):
    hi = w.astype(BF16)
    lo = (w - hi.astype(F32)).astype(BF16)
    return hi, lo


def kernel(x, c, w_ada, b_ada, norm_mix_w, w_in, ret_norm_w, w_out, norm_ffn_w,
           w_group, b_group, w_router, b_router, w1, w3, w2, norm_final_w):
    batch, seq, d = x.shape
    n = batch * seq
    depth = w_ada.shape[0]
    assert depth == 1, "the final rmsnorm is fused into the (single) layer's last kernel"
    tm = min(1024, seq)
    half = LANES // 2

    slopes = jnp.exp2(-8.0 * jnp.arange(1, N_ATTN_HEADS + 1, dtype=F32) / N_ATTN_HEADS)
    log_gamma = jnp.log1p(-jnp.exp2(-5.0 - jnp.arange(RET_HEADS, dtype=F32)))
    n_tiles = (2 * n) // FFN_TILE + N_EXPERTS
    p_rows = n_tiles * FFN_TILE

    c_pad = jnp.zeros((8, d), F32).at[:batch].set(c)
    x2d = x.reshape(n, d)
    for l in range(depth):
        mod = _adaln(c_pad, w_ada[l], b_ada[l].reshape(1, -1))
        mod3 = mod[:batch].reshape(batch, N_MOD, d)

        proj, kmean = _inproj(x2d, mod3, norm_mix_w[l].reshape(1, d), w_in[l], seq, tm)
        kmean = kmean.reshape(batch, seq // MOBA_BLOCK, ATTN_WIDTH)
        attn = _moba(proj, kmean, slopes, batch, seq)
        ret = _retention(proj, log_gamma, ret_norm_w[l].reshape(1, RET_WIDTH), batch, seq)

        wr = jnp.concatenate([w_group[l], jnp.transpose(w_router[l], (1, 0, 2)).reshape(d, N_EXPERTS)], axis=1)
        wr = jnp.pad(wr, ((0, 0), (0, half - wr.shape[1])))
        wr_hi, wr_lo = _split_hi_lo(wr)
        wr_cat = jnp.concatenate([jnp.concatenate([wr_hi, wr_lo], axis=1),
                                  jnp.concatenate([wr_hi, jnp.zeros_like(wr_lo)], axis=1)], axis=0)
        br = jnp.concatenate([b_group[l], b_router[l].reshape(-1)])
        br = jnp.pad(br, (0, LANES - br.shape[0])).reshape(1, LANES)

        x1, h2p, logits = _outproj(attn, ret, w_out[l], x2d, mod3,
                                   norm_ffn_w[l].reshape(1, d), wr_cat, br, seq, min(512, seq))

        info, meta = _plan(logits, n_tiles)
        d0 = info[:, _L_D0].astype(I32)
        d1 = info[:, _L_D1].astype(I32)
        tile_e = meta[_M_TILE_E, :n_tiles]
        n_valid = meta[_M_NVALID, :1]
        ntl = meta[_M_NTILES, :N_EXPERTS]

        row_tok = _invert(d0, d1, n_valid, ntl, meta[_M_COUNT, :N_EXPERTS],
                          meta[_M_START, :N_EXPERTS], p_rows)
        y = _ffn(tile_e, n_valid, ntl, row_tok, h2p, w1[l], w3[l], w2[l])
        nfw = norm_final_w.reshape(1, d)
        x2d = _combine(d0, d1, x1, info, mod3, nfw, y, seq)
    return x2d.reshape(batch, seq, d)
```

```python
import functools

import jax
import jax.numpy as jnp
from jax import lax
from jax.experimental import pallas as pl
from jax.experimental.pallas import tpu as pltpu

F32 = jnp.float32
BF16 = jnp.bfloat16
I32 = jnp.int32

N_ATTN_HEADS = 8
ATTN_HEAD_DIM = 128
ATTN_WIDTH = N_ATTN_HEADS * ATTN_HEAD_DIM
MOBA_BLOCK = 256
MOBA_TOPK = 3
RET_HEADS = 4
RET_HEAD_DIM = 256
RET_WIDTH = RET_HEADS * RET_HEAD_DIM
N_GROUPS = 4
EXPERTS_PER_GROUP = 8
N_EXPERTS = N_GROUPS * EXPERTS_PER_GROUP
N_MOD = 6
NORM_EPS = 1e-6

LANES = 128
RET_CHUNK = 256
FFN_TILE = 256
MASK_VALUE = -1e30
VMEM_LIMIT = 56 * 1024 * 1024

_DN_LAST = (((1,), (1,)), ((), ()))
_DN_FIRST = (((0,), (0,)), ((), ()))


def _cparams(sem):
    return pltpu.CompilerParams(dimension_semantics=sem, vmem_limit_bytes=VMEM_LIMIT)


def _silu(v):
    return v * jax.nn.sigmoid(v)


def _adaln_kernel(c_ref, w_ref, b_ref, o_ref):
    ca = _silu(c_ref[...]).astype(BF16)
    o_ref[...] = jnp.dot(ca, w_ref[...].astype(BF16), preferred_element_type=F32) + b_ref[...]


def _adaln(c_pad, w_ada, b_ada, tn=1024):
    rows, d = c_pad.shape
    n = w_ada.shape[1]
    return pl.pallas_call(
        _adaln_kernel,
        grid=(n // tn,),
        in_specs=[pl.BlockSpec((rows, d), lambda j: (0, 0)),
                  pl.BlockSpec((d, tn), lambda j: (0, j)),
                  pl.BlockSpec((1, tn), lambda j: (0, j))],
        out_specs=pl.BlockSpec((rows, tn), lambda j: (0, j)),
        out_shape=jax.ShapeDtypeStruct((rows, n), F32),
        compiler_params=_cparams(("arbitrary",)),
        name="adaln",
    )(c_pad, w_ada, b_ada)


def _inproj_kernel(x_ref, mod_ref, nw_ref, w_ref, o_ref, km_ref, h_ref, *, tm, tn):
    j = pl.program_id(1)

    @pl.when(j == 0)
    def _():
        x = x_ref[...]
        y = x * lax.rsqrt(jnp.mean(x * x, axis=-1, keepdims=True) + NORM_EPS) * nw_ref[...]
        m = mod_ref[0]
        h_ref[...] = (y * (1.0 + m[1:2]) + m[0:1]).astype(BF16)

    acc = jnp.dot(h_ref[...], w_ref[...].astype(BF16), preferred_element_type=F32)
    o_ref[...] = acc.astype(BF16)

    @pl.when(j == 1)
    def _():
        km_ref[0] = jnp.sum(acc.reshape(tm // MOBA_BLOCK, MOBA_BLOCK, tn), axis=1) * (1.0 / MOBA_BLOCK)


def _inproj(x2d, mod3, nw, w_bf, seq, tm, tn=ATTN_WIDTH):
    n, d = x2d.shape
    width = w_bf.shape[1]
    per_b = seq // tm
    kern = functools.partial(_inproj_kernel, tm=tm, tn=tn)
    return pl.pallas_call(
        kern,
        grid=(n // tm, width // tn),
        in_specs=[pl.BlockSpec((tm, d), lambda i, j: (i, 0)),
                  pl.BlockSpec((1, N_MOD, d), lambda i, j: (i // per_b, 0, 0)),
                  pl.BlockSpec((1, d), lambda i, j: (0, 0)),
                  pl.BlockSpec((d, tn), lambda i, j: (0, j))],
        out_specs=[pl.BlockSpec((tm, tn), lambda i, j: (i, j)),
                   pl.BlockSpec((1, tm // MOBA_BLOCK, tn), lambda i, j: (i, 0, 0))],
        out_shape=[jax.ShapeDtypeStruct((n, width), BF16),
                   jax.ShapeDtypeStruct((n // tm, tm // MOBA_BLOCK, tn), F32)],
        scratch_shapes=[pltpu.VMEM((tm, d), BF16)],
        compiler_params=_cparams(("arbitrary", "arbitrary")),
        name="inproj",
    )(x2d, mod3, nw, w_bf)


def _moba_kernel(slopes_ref, q_ref, k_ref, v_ref, km_ref, o_ref,
                 vt_ref, sel_ref, bias_ref, acc_ref, *, nb, hps):
    hg = pl.program_id(1)
    i = pl.program_id(2)
    blk = MOBA_BLOCK
    dh = ATTN_HEAD_DIM
    log2e = 1.4426950408889634
    qk_scale = (dh ** -0.5) * log2e

    qpos = lax.broadcasted_iota(I32, (blk, blk), 1)
    kpos = lax.broadcasted_iota(I32, (blk, blk), 0)

    @pl.when(i == 0)
    def _():
        vt_ref[...] = v_ref[...].astype(F32).T.astype(BF16)
        dist = (qpos - kpos).astype(F32)
        for g in range(hps):
            bias_ref[g] = dist * (-log2e * slopes_ref[hg * hps + g])

    bidx = lax.broadcasted_iota(I32, (nb, blk), 0)
    past = bidx < i
    i0 = pl.multiple_of(i * blk, blk)
    heads = [slice(g * dh, (g + 1) * dh) for g in range(hps)]
    gates, raw_own = [], []
    for cols in heads:
        q = q_ref[:, cols]
        km = km_ref[0, :, cols]
        km_hi = km.astype(BF16)
        km_lo = (km - km_hi.astype(F32)).astype(BF16)
        gates.append(lax.dot_general(km_hi, q, _DN_LAST, preferred_element_type=F32)
                     + lax.dot_general(km_lo, q, _DN_LAST, preferred_element_type=F32))
        raw_own.append(lax.dot_general(k_ref[pl.ds(i0, blk), cols], q, _DN_LAST,
                                       preferred_element_type=F32))
    own_bias = [bias_ref[g] for g in range(hps)]

    carry0, acc0, sels = [], [], []
    for g, cols in enumerate(heads):
        gate = jnp.where(past, gates[g], -jnp.inf)
        rank = jnp.zeros((nb, blk), F32)
        for m in range(nb):
            gm = gate[m:m + 1, :]
            beats = jnp.where(gm > gate, 1.0, jnp.where((gm == gate) & (m < bidx), 1.0, 0.0))
            rank = rank + jnp.where(m < i, beats, 0.0)
        sels.append(jnp.where(past & (rank < MOBA_TOPK), 1.0, 0.0))

        s = jnp.where(qpos >= kpos, raw_own[g] * qk_scale + own_bias[g], MASK_VALUE)
        m0 = jnp.max(s, axis=0, keepdims=True)
        p = jnp.exp2(s - m0)
        acc0.append(jnp.dot(vt_ref[cols, pl.ds(i0, blk)], p.astype(BF16), preferred_element_type=F32))
        carry0 += [m0, jnp.sum(p, axis=0, keepdims=True)]
    for g in range(hps):
        acc_ref[g] = acc0[g]
        sel_ref[g] = sels[g]

    def body(j, carry):
        j0 = pl.multiple_of(j * blk, blk)
        off = ((i - j) * blk).astype(F32) * log2e
        raw = [lax.dot_general(k_ref[pl.ds(j0, blk), cols], q_ref[:, cols], _DN_LAST,
                               preferred_element_type=F32) for cols in heads]
        chosen = [sel_ref[g, pl.ds(j, 1), :] > 0.0 for g in range(hps)]
        acc_old = [acc_ref[g] for g in range(hps)]
        out, acc_new = [], []
        for g, cols in enumerate(heads):
            m_run, l_run = carry[2 * g], carry[2 * g + 1]
            shift = -slopes_ref[hg * hps + g] * off
            s = raw[g] * qk_scale + bias_ref[g]
            m_new = jnp.maximum(m_run, jnp.where(chosen[g], jnp.max(s, axis=0, keepdims=True) + shift,
                                                 MASK_VALUE))
            alpha = jnp.exp2(m_run - m_new)
            p = jnp.exp2(s - jnp.where(chosen[g], m_new - shift, -MASK_VALUE))
            acc_new.append(alpha * acc_old[g] + jnp.dot(vt_ref[cols, pl.ds(j0, blk)], p.astype(BF16),
                                                        preferred_element_type=F32))
            out += [m_new, alpha * l_run + jnp.sum(p, axis=0, keepdims=True)]
        for g in range(hps):
            acc_ref[g] = acc_new[g]
        return tuple(out)

    fin = lax.fori_loop(0, i, body, tuple(carry0))
    for g in range(hps):
        o_ref[:, g * dh:(g + 1) * dh] = (acc_ref[g] / fin[2 * g + 1]).T.astype(BF16)


MOBA_HEADS_PER_STEP = 4


def _moba(proj, kmean, slopes, batch, seq):
    n = proj.shape[0]
    nb = seq // MOBA_BLOCK
    hps = MOBA_HEADS_PER_STEP
    w = hps * ATTN_HEAD_DIM
    ng = N_ATTN_HEADS // hps
    kern = functools.partial(_moba_kernel, nb=nb, hps=hps)
    return pl.pallas_call(
        kern,
        grid=(batch, ng, nb),
        in_specs=[pl.BlockSpec(memory_space=pltpu.SMEM),
                  pl.BlockSpec((MOBA_BLOCK, w), lambda b, h, i: (b * nb + i, h)),
                  pl.BlockSpec((seq, w), lambda b, h, i: (b, ng + h)),
                  pl.BlockSpec((seq, w), lambda b, h, i: (b, 2 * ng + h)),
                  pl.BlockSpec((1, nb, w), lambda b, h, i: (b, 0, h))],
        out_specs=pl.BlockSpec((MOBA_BLOCK, w), lambda b, h, i: (b * nb + i, h)),
        out_shape=jax.ShapeDtypeStruct((n, ATTN_WIDTH), BF16),
        scratch_shapes=[pltpu.VMEM((w, seq), BF16),
                        pltpu.VMEM((hps, nb, MOBA_BLOCK), F32),
                        pltpu.VMEM((hps, MOBA_BLOCK, MOBA_BLOCK), F32),
                        pltpu.VMEM((hps, ATTN_HEAD_DIM, MOBA_BLOCK), F32)],
        compiler_params=_cparams(("arbitrary", "arbitrary", "arbitrary")),
        name="moba",
    )(slopes, proj, proj, proj, kmean)


def _ret_kernel(lg_ref, q_ref, k_ref, v_ref, g_ref, rnw_ref, o_ref,
                state_ref, dmask_ref, qdec_ref, kdec_ref, cdec_ref):
    n = pl.program_id(1)
    c = RET_CHUNK
    dk = RET_HEAD_DIM
    kscale = dk ** -0.5

    @pl.when(n == 0)
    def _():
        state_ref[...] = jnp.zeros_like(state_ref)
        ti = lax.broadcasted_iota(I32, (c, c), 0)
        si = lax.broadcasted_iota(I32, (c, c), 1)
        diff = jnp.maximum(ti - si, 0).astype(F32)
        pos = lax.broadcasted_iota(I32, (c, dk), 0).astype(F32)
        for h in range(RET_HEADS):
            lg = lg_ref[h]
            dmask_ref[h] = jnp.where(ti >= si, jnp.exp(diff * lg) * kscale, 0.0)
            qdec_ref[h] = jnp.exp((pos + 1.0) * lg)
            kdec_ref[h] = jnp.exp((c - 1.0 - pos) * lg) * kscale
            cdec_ref[h] = jnp.exp(jnp.full((1, dk), float(c), F32) * lg)

    heads = [slice(h * dk, (h + 1) * dk) for h in range(RET_HEADS)]
    qs = [q_ref[:, cols] for cols in heads]
    ks = [k_ref[:, cols] for cols in heads]
    vs = [v_ref[:, cols] for cols in heads]
    states = [state_ref[h] for h in range(RET_HEADS)]
    raw = [lax.dot_general(qs[h], ks[h], _DN_LAST, preferred_element_type=F32) for h in range(RET_HEADS)]
    cross = [jnp.dot(qs[h], states[h].astype(BF16), preferred_element_type=F32) for h in range(RET_HEADS)]
    outs, new_states = [], []
    for h, cols in enumerate(heads):
        inner = jnp.dot((raw[h] * dmask_ref[h]).astype(BF16), vs[h], preferred_element_type=F32)
        kd = (ks[h].astype(F32) * kdec_ref[h]).astype(BF16)
        kv = lax.dot_general(kd, vs[h], _DN_FIRST, preferred_element_type=F32)
        new_states.append(states[h] * cdec_ref[h] + kv)
        r = inner + cross[h] * qdec_ref[h]
        rn = r * lax.rsqrt(jnp.mean(r * r, axis=-1, keepdims=True) + NORM_EPS)
        g = g_ref[:, cols].astype(F32)
        outs.append((rn * rnw_ref[:, cols] * _silu(g)).astype(BF16))
    for h, cols in enumerate(heads):
        state_ref[h] = new_states[h]
        o_ref[:, cols] = outs[h]


def _retention(proj, log_gamma, rnw, batch, seq):
    n = proj.shape[0]
    c = RET_CHUNK
    nc = seq // c
    dk = RET_HEAD_DIM
    base = 3 * ATTN_WIDTH // RET_WIDTH

    def col(off):
        return lambda b, t: (b * nc + t, base + off)

    return pl.pallas_call(
        _ret_kernel,
        grid=(batch, nc),
        in_specs=[pl.BlockSpec(memory_space=pltpu.SMEM),
                  pl.BlockSpec((c, RET_WIDTH), col(0)),
                  pl.BlockSpec((c, RET_WIDTH), col(1)),
                  pl.BlockSpec((c, RET_WIDTH), col(2)),
                  pl.BlockSpec((c, RET_WIDTH), col(3)),
                  pl.BlockSpec((1, RET_WIDTH), lambda b, t: (0, 0))],
        out_specs=pl.BlockSpec((c, RET_WIDTH), lambda b, t: (b * nc + t, 0)),
        out_shape=jax.ShapeDtypeStruct((n, RET_WIDTH), BF16),
        scratch_shapes=[pltpu.VMEM((RET_HEADS, dk, dk), F32), pltpu.VMEM((RET_HEADS, c, c), F32),
                        pltpu.VMEM((RET_HEADS, c, dk), F32), pltpu.VMEM((RET_HEADS, c, dk), F32),
                        pltpu.VMEM((RET_HEADS, 1, dk), F32)],
        compiler_params=_cparams(("arbitrary", "arbitrary")),
        name="retention",
    )(log_gamma, proj, proj, proj, proj, rnw)


def _outproj_kernel(a_ref, r_ref, w_ref, x_ref, mod_ref, nw_ref, wr_ref, br_ref,
                    x1_ref, hp_ref, lg_ref, wb_ref):
    @pl.when(pl.program_id(0) == 0)
    def _():
        wb_ref[...] = w_ref[...].astype(BF16)

    mix = jnp.dot(jnp.concatenate([a_ref[...], r_ref[...]], axis=1), wb_ref[...],
                  preferred_element_type=F32)
    m = mod_ref[0]
    x1 = x_ref[...] + m[2:3] * mix
    x1_ref[...] = x1
    y = x1 * lax.rsqrt(jnp.mean(x1 * x1, axis=-1, keepdims=True) + NORM_EPS) * nw_ref[...]
    h2 = y * (1.0 + m[4:5]) + m[3:4]
    hp_ref[...] = h2
    hi = h2.astype(BF16)
    lo = (h2 - hi.astype(F32)).astype(BF16)
    part = jnp.dot(jnp.concatenate([hi, lo], axis=1), wr_ref[...], preferred_element_type=F32)
    lg_ref[...] = part + pltpu.roll(part, LANES // 2, axis=1) + br_ref[...]


def _outproj(attn, ret, w_bf, x2d, mod3, nw, wr_cat, br, seq, tm):
    n, d = x2d.shape
    per_b = seq // tm
    return pl.pallas_call(
        _outproj_kernel,
        grid=(n // tm,),
        in_specs=[pl.BlockSpec((tm, ATTN_WIDTH), lambda i: (i, 0)),
                  pl.BlockSpec((tm, RET_WIDTH), lambda i: (i, 0)),
                  pl.BlockSpec((d, d), lambda i: (0, 0), pipeline_mode=pl.Buffered(1)),
                  pl.BlockSpec((tm, d), lambda i: (i, 0)),
                  pl.BlockSpec((1, N_MOD, d), lambda i: (i // per_b, 0, 0)),
                  pl.BlockSpec((1, d), lambda i: (0, 0)),
                  pl.BlockSpec((2 * d, LANES), lambda i: (0, 0)),
                  pl.BlockSpec((1, LANES), lambda i: (0, 0))],
        out_specs=[pl.BlockSpec((tm, d), lambda i: (i, 0)),
                   pl.BlockSpec((tm, d), lambda i: (i, 0)),
                   pl.BlockSpec((tm, LANES), lambda i: (i, 0))],
        out_shape=[jax.ShapeDtypeStruct((n, d), F32),
                   jax.ShapeDtypeStruct((n, d), F32),
                   jax.ShapeDtypeStruct((n, LANES), F32)],
        scratch_shapes=[pltpu.VMEM((d, d), BF16)],
        compiler_params=_cparams(("arbitrary",)),
        name="outproj",
    )(attn, ret, w_bf, x2d, mod3, nw, wr_cat, br)


_L_E0, _L_E1, _L_W0, _L_W1, _L_D0, _L_D1 = 0, 1, 2, 3, 4, 5
_M_TILE_E, _M_NVALID, _M_NTILES, _M_COUNT, _M_START = 0, 1, 2, 3, 4
_PLAN_ROWS = 256


def _first_lane_where(cond, lane):
    return jnp.min(jnp.where(cond, lane, float(LANES)), axis=1, keepdims=True)


def _plan_kernel(lg_ref, info_ref, meta_ref, *, n_tok, n_tiles):
    tb = _PLAN_ROWS
    lane = lax.broadcasted_iota(I32, (tb, LANES), 1).astype(F32)
    row = lax.broadcasted_iota(I32, (tb, tb), 0)
    colm = lax.broadcasted_iota(I32, (tb, tb), 1)
    tri = jnp.where(colm < row, 1.0, 0.0).astype(BF16)
    n_g, n_e = N_GROUPS, EXPERTS_PER_GROUP

    def route(blk_i, carry):
        r0 = pl.multiple_of(blk_i * tb, tb)
        lg = lg_ref[pl.ds(r0, tb), :]
        gl = jnp.where(lane < n_g, lg, -jnp.inf)
        ge = jnp.exp(gl - jnp.max(gl, axis=1, keepdims=True))
        gp = ge / jnp.sum(ge, axis=1, keepdims=True)
        g_w = jnp.max(gp, axis=1, keepdims=True)
        g_sel = _first_lane_where((gp == g_w) & (lane < n_g), lane)
        lo_lane = n_g + n_e * g_sel
        in_grp = (lane >= lo_lane) & (lane < lo_lane + n_e)
        el = jnp.where(in_grp, lg, -jnp.inf)
        ee = jnp.exp(el - jnp.max(el, axis=1, keepdims=True))
        ep = jnp.where(in_grp, ee / jnp.sum(ee, axis=1, keepdims=True), -1.0)
        p1 = jnp.max(ep, axis=1, keepdims=True)
        i1 = _first_lane_where(ep == p1, lane)
        ep2 = jnp.where(lane == i1, -1.0, ep)
        p2 = jnp.max(ep2, axis=1, keepdims=True)
        i2 = _first_lane_where(ep2 == p2, lane)
        denom = p1 + p2
        w0 = g_w * p1 / denom
        w1 = g_w * p2 / denom
        e0 = i1 - n_g
        e1 = i2 - n_g
        oh0 = jnp.where(lane == e0, 1.0, 0.0)
        oh1 = jnp.where(lane == e1, 1.0, 0.0)
        oh = oh0 + oh1
        before = jnp.dot(tri, oh.astype(BF16), preferred_element_type=F32) + carry
        rank0 = jnp.sum(oh0 * before, axis=1, keepdims=True)
        rank1 = jnp.sum(oh1 * before, axis=1, keepdims=True)
        info = jnp.where(lane == _L_E0, e0,
               jnp.where(lane == _L_E1, e1,
               jnp.where(lane == _L_W0, w0,
               jnp.where(lane == _L_W1, w1,
               jnp.where(lane == _L_D0, rank0,
               jnp.where(lane == _L_D1, rank1, 0.0))))))
        info_ref[pl.ds(r0, tb), :] = info
        return carry + jnp.sum(oh, axis=0, keepdims=True)

    counts = lax.fori_loop(0, n_tok // tb, route, jnp.zeros((1, LANES), F32))

    lane1 = lax.broadcasted_iota(I32, (1, LANES), 1)
    padded = jnp.floor((counts + (FFN_TILE - 1.0)) * (1.0 / FFN_TILE)) * FFN_TILE
    pad_end = padded
    sh = 1
    while sh < N_EXPERTS:
        pad_end = pad_end + jnp.where(lane1 >= sh, pltpu.roll(pad_end, sh, axis=1), 0.0)
        sh *= 2
    pad_start = pad_end - padded

    def place(blk_i, _):
        r0 = pl.multiple_of(blk_i * tb, tb)
        info = info_ref[pl.ds(r0, tb), :]
        e0 = info[:, _L_E0:_L_E0 + 1]
        e1 = info[:, _L_E1:_L_E1 + 1]
        s0 = jnp.sum(jnp.where(lane == e0, pad_start, 0.0), axis=1, keepdims=True)
        s1 = jnp.sum(jnp.where(lane == e1, pad_start, 0.0), axis=1, keepdims=True)
        add = jnp.where(lane == _L_D0, s0, jnp.where(lane == _L_D1, s1, 0.0))
        info_ref[pl.ds(r0, tb), :] = info + add
        return 0

    lax.fori_loop(0, n_tok // tb, place, 0)

    sq_r = lax.broadcasted_iota(I32, (LANES, LANES), 0)
    sq_c = lax.broadcasted_iota(I32, (LANES, LANES), 1)
    pe_col = jnp.sum(jnp.where(sq_r == sq_c, pad_end, 0.0), axis=1, keepdims=True)
    mt = meta_ref.shape[1]
    t_start = lax.broadcasted_iota(I32, (LANES, mt), 1).astype(F32) * FFN_TILE
    e_row = lax.broadcasted_iota(I32, (LANES, mt), 0)
    ended = jnp.where((pe_col <= t_start) & (e_row < N_EXPERTS), 1.0, 0.0)
    tile_e = jnp.minimum(jnp.sum(ended, axis=0, keepdims=True), N_EXPERTS - 1.0)
    total = jnp.sum(jnp.where(lane1 == N_EXPERTS - 1, pad_end, 0.0), axis=1, keepdims=True)
    n_valid = jnp.broadcast_to(total * (1.0 / FFN_TILE), (1, mt))
    def per_expert(v):
        return v if mt == LANES else jnp.concatenate([v, jnp.zeros((1, mt - LANES), F32)], axis=1)

    mrow = lax.broadcasted_iota(I32, (8, mt), 0)
    meta = jnp.where(mrow == _M_TILE_E, tile_e,
           jnp.where(mrow == _M_NVALID, n_valid,
           jnp.where(mrow == _M_NTILES, per_expert(padded * (1.0 / FFN_TILE)),
           jnp.where(mrow == _M_COUNT, per_expert(counts),
           jnp.where(mrow == _M_START, per_expert(pad_start), 0.0)))))
    meta_ref[...] = meta.astype(I32)


def _plan(logits, n_tiles):
    n_tok = logits.shape[0]
    mt = -(-n_tiles // LANES) * LANES
    kern = functools.partial(_plan_kernel, n_tok=n_tok, n_tiles=n_tiles)
    return pl.pallas_call(
        kern,
        out_shape=[jax.ShapeDtypeStruct((n_tok, LANES), F32),
                   jax.ShapeDtypeStruct((8, mt), I32)],
        compiler_params=pltpu.CompilerParams(vmem_limit_bytes=VMEM_LIMIT),
        name="plan",
    )(logits)


_DMA_UNROLL = 8


def _invert_kernel(d0_ref, d1_ref, nv_ref, ntl_ref, cnt_ref, start_ref, tok_ref, *, n_tok, p_rows):
    def zero_rows(lo, hi):
        def zbody(u, _):
            for s in range(_DMA_UNROLL):
                tok_ref[jnp.minimum(lo + u * _DMA_UNROLL + s, hi - 1)] = 0
            return 0
        lax.fori_loop(0, (hi - lo + _DMA_UNROLL - 1) // _DMA_UNROLL, zbody, 0)

    def pad_body(e, _):
        zero_rows(start_ref[e] + cnt_ref[e], start_ref[e] + ntl_ref[e] * FFN_TILE)
        return 0

    lax.fori_loop(0, N_EXPERTS, pad_body, 0)
    zero_rows(nv_ref[0] * FFN_TILE, p_rows)

    def body(u, _):
        t0 = u * _DMA_UNROLL
        rows = [(d0_ref[t0 + s], d1_ref[t0 + s]) for s in range(_DMA_UNROLL)]
        for s, (r0, r1) in enumerate(rows):
            tok_ref[r0] = t0 + s
            tok_ref[r1] = t0 + s
        return 0

    lax.fori_loop(0, n_tok // _DMA_UNROLL, body, 0)


def _invert(d0, d1, n_valid, ntl, cnt, start, p_rows):
    n_tok = d0.shape[0]
    kern = functools.partial(_invert_kernel, n_tok=n_tok, p_rows=p_rows)
    return pl.pallas_call(
        kern,
        grid_spec=pltpu.PrefetchScalarGridSpec(
            num_scalar_prefetch=6,
            grid=(1,),
            in_specs=[],
            out_specs=pl.BlockSpec(memory_space=pltpu.SMEM)),
        out_shape=jax.ShapeDtypeStruct((p_rows,), I32),
        compiler_params=_cparams(("arbitrary",)),
        name="invert",
    )(d0, d1, n_valid, ntl, cnt, start)


_FFN_SLOTS = 3


def _ffn_kernel(te_ref, nv_ref, ntl_ref, tok_ref, h_ref, w1_ref, w3_ref, w2_ref, y_ref,
                xbuf, xb_ref, w1f, w3f, w2f, w1b, w3b, w2b, gsem, wsem, run_ref, *, n_tiles):
    i = pl.program_id(0)
    nv = nv_ref[0]
    valid = i < nv
    slot = i % _FFN_SLOTS

    def row_copy(tile, r, dst_slot):
        tok = tok_ref[tile * FFN_TILE + r]
        return pltpu.make_async_copy(h_ref.at[pl.ds(tok, 1)], xbuf.at[dst_slot, pl.ds(r, 1)],
                                     gsem.at[dst_slot])

    def weight_copies(e, ws):
        return (pltpu.make_async_copy(w1_ref.at[e], w1f.at[ws], wsem.at[ws]),
                pltpu.make_async_copy(w3_ref.at[e], w3f.at[ws], wsem.at[ws]),
                pltpu.make_async_copy(w2_ref.at[e], w2f.at[ws], wsem.at[ws]))

    @pl.when(i == 0)
    def _():
        run_ref[0] = 0
        for cp in weight_copies(te_ref[0], 0):
            cp.start()

        def first_rows(u, _):
            for s in range(_DMA_UNROLL):
                for t in range(_FFN_SLOTS - 1):
                    row_copy(t, u * _DMA_UNROLL + s, t).start()
            return 0
        lax.fori_loop(0, FFN_TILE // _DMA_UNROLL, first_rows, 0)

    @pl.when(valid & ((i == 0) | (te_ref[i] != te_ref[jnp.maximum(i, 1) - 1])))
    def _():
        run = run_ref[0]
        ws = run % 2
        for cp in weight_copies(te_ref[i], ws):
            cp.wait()
        w1b[...] = w1f[ws].astype(BF16)
        w3b[...] = w3f[ws].astype(BF16)
        w2b[...] = w2f[ws].astype(BF16)
        nxt = i + ntl_ref[te_ref[i]]

        @pl.when(nxt < nv)
        def _():
            for cp in weight_copies(te_ref[jnp.minimum(nxt, n_tiles - 1)], 1 - ws):
                cp.start()
        run_ref[0] = run + 1

    @pl.when(valid)
    def _():
        pltpu.make_async_copy(h_ref.at[pl.ds(0, FFN_TILE)], xbuf.at[slot], gsem.at[slot]).wait()
        xb_ref[...] = xbuf[slot].astype(BF16)
        ahead = _FFN_SLOTS - 1
        nxt_tile = jnp.minimum(i + ahead, n_tiles - 1)
        nxt_slot = (i + ahead) % _FFN_SLOTS
        for r in range(FFN_TILE):
            row_copy(nxt_tile, r, nxt_slot).start()
        x = xb_ref[...]
        a = jnp.dot(x, w1b[...], preferred_element_type=F32)
        b = jnp.dot(x, w3b[...], preferred_element_type=F32)
        hmid = (_silu(a) * b).astype(BF16)
        y_ref[...] = jnp.dot(hmid, w2b[...], preferred_element_type=F32)

    @pl.when(jnp.logical_not(valid))
    def _():
        y_ref[...] = jnp.zeros_like(y_ref)

    @pl.when(i == n_tiles - 1)
    def _():
        for t in range(_FFN_SLOTS - 1):
            s = (nv + t) % _FFN_SLOTS
            pltpu.make_async_copy(h_ref.at[pl.ds(0, FFN_TILE)], xbuf.at[s], gsem.at[s]).wait()


def _ffn(tile_e, n_valid, ntl, row_tok, h2, w1, w3, w2):
    p_rows = row_tok.shape[0]
    _, d, de = w1.shape
    n_tiles = p_rows // FFN_TILE
    kern = functools.partial(_ffn_kernel, n_tiles=n_tiles)
    any_spec = pl.BlockSpec(memory_space=pl.ANY)
    return pl.pallas_call(
        kern,
        grid_spec=pltpu.PrefetchScalarGridSpec(
            num_scalar_prefetch=4,
            grid=(n_tiles,),
            in_specs=[any_spec, any_spec, any_spec, any_spec],
            out_specs=pl.BlockSpec((FFN_TILE, d), lambda i, *_: (i, 0)),
            scratch_shapes=[pltpu.VMEM((_FFN_SLOTS, FFN_TILE, d), F32), pltpu.VMEM((FFN_TILE, d), BF16),
                            pltpu.VMEM((2, d, de), F32), pltpu.VMEM((2, d, de), F32),
                            pltpu.VMEM((2, de, d), F32),
                            pltpu.VMEM((d, de), BF16), pltpu.VMEM((d, de), BF16),
                            pltpu.VMEM((de, d), BF16),
                            pltpu.SemaphoreType.DMA((_FFN_SLOTS,)), pltpu.SemaphoreType.DMA((2,)),
                            pltpu.SMEM((1,), I32)]),
        out_shape=jax.ShapeDtypeStruct((p_rows, d), F32),
        compiler_params=_cparams(("arbitrary",)),
        name="ffn",
    )(tile_e, n_valid, ntl, row_tok, h2, w1, w3, w2)


def _combine_kernel(d0_ref, d1_ref, x1_ref, info_ref, mod_ref, nw_ref, y_ref, o_ref,
                    *scratch, tp, n_phases):
    i = pl.program_id(0)
    ns = _COMBINE_SETS
    sem = scratch[-1]
    bufs = [(scratch[2 * q], scratch[2 * q + 1]) for q in range(ns)]

    def row_copies(phase, r, q):
        tok = phase * tp + r
        ya, yb = bufs[q]
        return (pltpu.make_async_copy(y_ref.at[pl.ds(d0_ref[tok], 1)], ya.at[pl.ds(r, 1)], sem.at[q]),
                pltpu.make_async_copy(y_ref.at[pl.ds(d1_ref[tok], 1)], yb.at[pl.ds(r, 1)], sem.at[q]))

    def wait_set(q):
        for buf in bufs[q]:
            pltpu.make_async_copy(y_ref.at[pl.ds(0, tp)], buf, sem.at[q]).wait()

    @pl.when(i == 0)
    def _():
        def first_rows(u, _):
            for s in range(_DMA_UNROLL):
                for q in range(ns - 1):
                    for cp in row_copies(q, u * _DMA_UNROLL + s, q):
                        cp.start()
            return 0
        lax.fori_loop(0, tp // _DMA_UNROLL, first_rows, 0)

    for q in range(ns):
        wait_set(q)
        ahead = jnp.minimum(i * ns + q + ns - 1, n_phases - 1)
        for r in range(tp):
            for cp in row_copies(ahead, r, (q + ns - 1) % ns):
                cp.start()
        ya, yb = bufs[q]
        rows = pl.ds(q * tp, tp)
        info = info_ref[rows, :]
        moe = ya[...] * info[:, _L_W0:_L_W0 + 1] + yb[...] * info[:, _L_W1:_L_W1 + 1]
        x2 = x1_ref[rows, :] + mod_ref[0][5:6] * moe
        o_ref[rows, :] = x2 * lax.rsqrt(jnp.mean(x2 * x2, axis=-1, keepdims=True) + NORM_EPS) * nw_ref[...]

    @pl.when(i == n_phases // ns - 1)
    def _():
        for q in range(ns - 1):
            wait_set(q)


_COMBINE_SETS = 4


def _combine(d0, d1, x1, info, mod3, nfw, y, seq, tp=128):
    n, d = x1.shape
    tc = _COMBINE_SETS * tp
    per_b = seq // tc
    kern = functools.partial(_combine_kernel, tp=tp, n_phases=n // tp)
    return pl.pallas_call(
        kern,
        grid_spec=pltpu.PrefetchScalarGridSpec(
            num_scalar_prefetch=2,
            grid=(n // tc,),
            in_specs=[pl.BlockSpec((tc, d), lambda i, d0, d1: (i, 0)),
                      pl.BlockSpec((tc, LANES), lambda i, d0, d1: (i, 0)),
                      pl.BlockSpec((1, N_MOD, d), lambda i, d0, d1: (i // per_b, 0, 0)),
                      pl.BlockSpec((1, d), lambda i, d0, d1: (0, 0)),
                      pl.BlockSpec(memory_space=pl.ANY)],
            out_specs=pl.BlockSpec((tc, d), lambda i, d0, d1: (i, 0)),
            scratch_shapes=[pltpu.VMEM((tp, d), F32)] * (2 * _COMBINE_SETS)
                           + [pltpu.SemaphoreType.DMA((_COMBINE_SETS,))]),
        out_shape=jax.ShapeDtypeStruct((n, d), F32),
        compiler_params=_cparams(("arbitrary",)),
        name="combine",
    )(d0, d1, x1, info, mod3, nfw, y)


def _split_hi_lo(w):
    hi = w.astype(BF16)
    lo = (w - hi.astype(F32)).astype(BF16)
    return hi, lo


def kernel(x, c, w_ada, b_ada, norm_mix_w, w_in, ret_norm_w, w_out, norm_ffn_w,
           w_group, b_group, w_router, b_router, w1, w3, w2, norm_final_w):
    batch, seq, d = x.shape
    n = batch * seq
    depth = w_ada.shape[0]
    assert depth == 1, "the final rmsnorm is fused into the (single) layer's last kernel"
    tm = min(1024, seq)
    half = LANES // 2

    slopes = jnp.exp2(-8.0 * jnp.arange(1, N_ATTN_HEADS + 1, dtype=F32) / N_ATTN_HEADS)
    log_gamma = jnp.log1p(-jnp.exp2(-5.0 - jnp.arange(RET_HEADS, dtype=F32)))
    n_tiles = (2 * n) // FFN_TILE + N_EXPERTS
    p_rows = n_tiles * FFN_TILE

    c_pad = jnp.zeros((8, d), F32).at[:batch].set(c)
    x2d = x.reshape(n, d)
    for l in range(depth):
        mod = _adaln(c_pad, w_ada[l], b_ada[l].reshape(1, -1))
        mod3 = mod[:batch].reshape(batch, N_MOD, d)

        proj, kmean = _inproj(x2d, mod3, norm_mix_w[l].reshape(1, d), w_in[l], seq, tm)
        kmean = kmean.reshape(batch, seq // MOBA_BLOCK, ATTN_WIDTH)
        attn = _moba(proj, kmean, slopes, batch, seq)
        ret = _retention(proj, log_gamma, ret_norm_w[l].reshape(1, RET_WIDTH), batch, seq)

        wr = jnp.concatenate([w_group[l], jnp.transpose(w_router[l], (1, 0, 2)).reshape(d, N_EXPERTS)], axis=1)
        wr = jnp.pad(wr, ((0, 0), (0, half - wr.shape[1])))
        wr_hi, wr_lo = _split_hi_lo(wr)
        wr_cat = jnp.concatenate([jnp.concatenate([wr_hi, wr_lo], axis=1),
                                  jnp.concatenate([wr_hi, jnp.zeros_like(wr_lo)], axis=1)], axis=0)
        br = jnp.concatenate([b_group[l], b_router[l].reshape(-1)])
        br = jnp.pad(br, (0, LANES - br.shape[0])).reshape(1, LANES)

        x1, h2p, logits = _outproj(attn, ret, w_out[l], x2d, mod3,
                                   norm_ffn_w[l].reshape(1, d), wr_cat, br, seq, min(512, seq))

        info, meta = _plan(logits, n_tiles)
        d0 = info[:, _L_D0].astype(I32)
        d1 = info[:, _L_D1].astype(I32)
        tile_e = meta[_M_TILE_E, :n_tiles]
        n_valid = meta[_M_NVALID, :1]
        ntl = meta[_M_NTILES, :N_EXPERTS]

        row_tok = _invert(d0, d1, n_valid, ntl, meta[_M_COUNT, :N_EXPERTS],
                          meta[_M_START, :N_EXPERTS], p_rows)
        y = _ffn(tile_e, n_valid, ntl, row_tok, h2p, w1[l], w3[l], w2[l])
        nfw = norm_final_w.reshape(1, d)
        x2d = _combine(d0, d1, x1, info, mod3, nfw, y, seq)
    return x2d.reshape(batch, seq, d)
```

```python
import functools

import jax
import jax.numpy as jnp
from jax import lax
from jax.experimental import pallas as pl
from jax.experimental.pallas import tpu as pltpu

F32 = jnp.float32
BF16 = jnp.bfloat16
I32 = jnp.int32

N_ATTN_HEADS = 8
ATTN_HEAD_DIM = 128
ATTN_WIDTH = N_ATTN_HEADS * ATTN_HEAD_DIM
MOBA_BLOCK = 256
MOBA_TOPK = 3
RET_HEADS = 4
RET_HEAD_DIM = 256
RET_WIDTH = RET_HEADS * RET_HEAD_DIM
N_GROUPS = 4
EXPERTS_PER_GROUP = 8
N_EXPERTS = N_GROUPS * EXPERTS_PER_GROUP
N_MOD = 6
NORM_EPS = 1e-6

LANES = 128
RET_CHUNK = 256
FFN_TILE = 256
MASK_VALUE = -1e30
VMEM_LIMIT = 56 * 1024 * 1024

_DN_LAST = (((1,), (1,)), ((), ()))
_DN_FIRST = (((0,), (0,)), ((), ()))


def _cparams(sem):
    return pltpu.CompilerParams(dimension_semantics=sem, vmem_limit_bytes=VMEM_LIMIT)


def _silu(v):
    return v * jax.nn.sigmoid(v)


def _adaln_kernel(c_ref, w_ref, b_ref, o_ref):
    ca = _silu(c_ref[...]).astype(BF16)
    o_ref[...] = jnp.dot(ca, w_ref[...].astype(BF16), preferred_element_type=F32) + b_ref[...]


def _adaln(c_pad, w_ada, b_ada, tn=1024):
    rows, d = c_pad.shape
    n = w_ada.shape[1]
    return pl.pallas_call(
        _adaln_kernel,
        grid=(n // tn,),
        in_specs=[pl.BlockSpec((rows, d), lambda j: (0, 0)),
                  pl.BlockSpec((d, tn), lambda j: (0, j)),
                  pl.BlockSpec((1, tn), lambda j: (0, j))],
        out_specs=pl.BlockSpec((rows, tn), lambda j: (0, j)),
        out_shape=jax.ShapeDtypeStruct((rows, n), F32),
        compiler_params=_cparams(("arbitrary",)),
        name="adaln",
    )(c_pad, w_ada, b_ada)


def _inproj_kernel(x_ref, mod_ref, nw_ref, w_ref, o_ref, km_ref, h_ref, *, tm, tn):
    j = pl.program_id(1)

    @pl.when(j == 0)
    def _():
        x = x_ref[...]
        y = x * lax.rsqrt(jnp.mean(x * x, axis=-1, keepdims=True) + NORM_EPS) * nw_ref[...]
        m = mod_ref[0]
        h_ref[...] = (y * (1.0 + m[1:2]) + m[0:1]).astype(BF16)

    acc = jnp.dot(h_ref[...], w_ref[...].astype(BF16), preferred_element_type=F32)
    o_ref[...] = acc.astype(BF16)

    @pl.when(j == 1)
    def _():
        km_ref[0] = jnp.sum(acc.reshape(tm // MOBA_BLOCK, MOBA_BLOCK, tn), axis=1) * (1.0 / MOBA_BLOCK)


def _inproj(x2d, mod3, nw, w_bf, seq, tm, tn=ATTN_WIDTH):
    n, d = x2d.shape
    width = w_bf.shape[1]
    per_b = seq // tm
    kern = functools.partial(_inproj_kernel, tm=tm, tn=tn)
    return pl.pallas_call(
        kern,
        grid=(n // tm, width // tn),
        in_specs=[pl.BlockSpec((tm, d), lambda i, j: (i, 0)),
                  pl.BlockSpec((1, N_MOD, d), lambda i, j: (i // per_b, 0, 0)),
                  pl.BlockSpec((1, d), lambda i, j: (0, 0)),
                  pl.BlockSpec((d, tn), lambda i, j: (0, j))],
        out_specs=[pl.BlockSpec((tm, tn), lambda i, j: (i, j)),
                   pl.BlockSpec((1, tm // MOBA_BLOCK, tn), lambda i, j: (i, 0, 0))],
        out_shape=[jax.ShapeDtypeStruct((n, width), BF16),
                   jax.ShapeDtypeStruct((n // tm, tm // MOBA_BLOCK, tn), F32)],
        scratch_shapes=[pltpu.VMEM((tm, d), BF16)],
        compiler_params=_cparams(("arbitrary", "arbitrary")),
        name="inproj",
    )(x2d, mod3, nw, w_bf)


def _moba_kernel(slopes_ref, q_ref, k_ref, v_ref, km_ref, o_ref,
                 vt_ref, sel_ref, bias_ref, acc_ref, *, nb, hps):
    hg = pl.program_id(1)
    i = pl.program_id(2)
    blk = MOBA_BLOCK
    dh = ATTN_HEAD_DIM
    log2e = 1.4426950408889634
    qk_scale = (dh ** -0.5) * log2e

    qpos = lax.broadcasted_iota(I32, (blk, blk), 1)
    kpos = lax.broadcasted_iota(I32, (blk, blk), 0)

    @pl.when(i == 0)
    def _():
        vt_ref[...] = v_ref[...].astype(F32).T.astype(BF16)
        dist = (qpos - kpos).astype(F32)
        for g in range(hps):
            bias_ref[g] = dist * (-log2e * slopes_ref[hg * hps + g])

    bidx = lax.broadcasted_iota(I32, (nb, blk), 0)
    past = bidx < i
    i0 = pl.multiple_of(i * blk, blk)
    heads = [slice(g * dh, (g + 1) * dh) for g in range(hps)]
    gates, raw_own = [], []
    for cols in heads:
        q = q_ref[:, cols]
        km = km_ref[0, :, cols]
        km_hi = km.astype(BF16)
        km_lo = (km - km_hi.astype(F32)).astype(BF16)
        gates.append(lax.dot_general(km_hi, q, _DN_LAST, preferred_element_type=F32)
                     + lax.dot_general(km_lo, q, _DN_LAST, preferred_element_type=F32))
        raw_own.append(lax.dot_general(k_ref[pl.ds(i0, blk), cols], q, _DN_LAST,
                                       preferred_element_type=F32))
    own_bias = [bias_ref[g] for g in range(hps)]

    carry0, acc0, sels = [], [], []
    for g, cols in enumerate(heads):
        gate = jnp.where(past, gates[g], -jnp.inf)
        rank = jnp.zeros((nb, blk), F32)
        for m in range(nb):
            gm = gate[m:m + 1, :]
            beats = jnp.where(gm > gate, 1.0, jnp.where((gm == gate) & (m < bidx), 1.0, 0.0))
            rank = rank + jnp.where(m < i, beats, 0.0)
        sels.append(jnp.where(past & (rank < MOBA_TOPK), 1.0, 0.0))

        s = jnp.where(qpos >= kpos, raw_own[g] * qk_scale + own_bias[g], MASK_VALUE)
        m0 = jnp.max(s, axis=0, keepdims=True)
        p = jnp.exp2(s - m0)
        acc0.append(jnp.dot(vt_ref[cols, pl.ds(i0, blk)], p.astype(BF16), preferred_element_type=F32))
        carry0 += [m0, jnp.sum(p, axis=0, keepdims=True)]
    for g in range(hps):
        acc_ref[g] = acc0[g]
        sel_ref[g] = sels[g]

    def body(j, carry):
        j0 = pl.multiple_of(j * blk, blk)
        off = ((i - j) * blk).astype(F32) * log2e
        raw = [lax.dot_general(k_ref[pl.ds(j0, blk), cols], q_ref[:, cols], _DN_LAST,
                               preferred_element_type=F32) for cols in heads]
        chosen = [sel_ref[g, pl.ds(j, 1), :] > 0.0 for g in range(hps)]
        acc_old = [acc_ref[g] for g in range(hps)]
        out, acc_new = [], []
        for g, cols in enumerate(heads):
            m_run, l_run = carry[2 * g], carry[2 * g + 1]
            shift = -slopes_ref[hg * hps + g] * off
            s = raw[g] * qk_scale + bias_ref[g]
            m_new = jnp.maximum(m_run, jnp.where(chosen[g], jnp.max(s, axis=0, keepdims=True) + shift,
                                                 MASK_VALUE))
            alpha = jnp.exp2(m_run - m_new)
            p = jnp.exp2(s - jnp.where(chosen[g], m_new - shift, -MASK_VALUE))
            acc_new.append(alpha * acc_old[g] + jnp.dot(vt_ref[cols, pl.ds(j0, blk)], p.astype(BF16),
                                                        preferred_element_type=F32))
            out += [m_new, alpha * l_run + jnp.sum(p, axis=0, keepdims=True)]
        for g in range(hps):
            acc_ref[g] = acc_new[g]
        return tuple(out)

    fin = lax.fori_loop(0, i, body, tuple(carry0))
    for g in range(hps):
        o_ref[:, g * dh:(g + 1) * dh] = (acc_ref[g] / fin[2 * g + 1]).T.astype(BF16)


MOBA_HEADS_PER_STEP = 8


def _moba(proj, kmean, slopes, batch, seq):
    n = proj.shape[0]
    nb = seq // MOBA_BLOCK
    hps = MOBA_HEADS_PER_STEP
    w = hps * ATTN_HEAD_DIM
    ng = N_ATTN_HEADS // hps
    kern = functools.partial(_moba_kernel, nb=nb, hps=hps)
    return pl.pallas_call(
        kern,
        grid=(batch, ng, nb),
        in_specs=[pl.BlockSpec(memory_space=pltpu.SMEM),
                  pl.BlockSpec((MOBA_BLOCK, w), lambda b, h, i: (b * nb + i, h)),
                  pl.BlockSpec((seq, w), lambda b, h, i: (b, ng + h)),
                  pl.BlockSpec((seq, w), lambda b, h, i: (b, 2 * ng + h)),
                  pl.BlockSpec((1, nb, w), lambda b, h, i: (b, 0, h))],
        out_specs=pl.BlockSpec((MOBA_BLOCK, w), lambda b, h, i: (b * nb + i, h)),
        out_shape=jax.ShapeDtypeStruct((n, ATTN_WIDTH), BF16),
        scratch_shapes=[pltpu.VMEM((w, seq), BF16),
                        pltpu.VMEM((hps, nb, MOBA_BLOCK), F32),
                        pltpu.VMEM((hps, MOBA_BLOCK, MOBA_BLOCK), F32),
                        pltpu.VMEM((hps, ATTN_HEAD_DIM, MOBA_BLOCK), F32)],
        compiler_params=_cparams(("arbitrary", "arbitrary", "arbitrary")),
        name="moba",
    )(slopes, proj, proj, proj, kmean)


def _ret_kernel(lg_ref, q_ref, k_ref, v_ref, g_ref, rnw_ref, o_ref,
                state_ref, dmask_ref, qdec_ref, kdec_ref, cdec_ref):
    n = pl.program_id(1)
    c = RET_CHUNK
    dk = RET_HEAD_DIM
    kscale = dk ** -0.5

    @pl.when(n == 0)
    def _():
        state_ref[...] = jnp.zeros_like(state_ref)
        ti = lax.broadcasted_iota(I32, (c, c), 0)
        si = lax.broadcasted_iota(I32, (c, c), 1)
        diff = jnp.maximum(ti - si, 0).astype(F32)
        pos = lax.broadcasted_iota(I32, (c, dk), 0).astype(F32)
        for h in range(RET_HEADS):
            lg = lg_ref[h]
            dmask_ref[h] = jnp.where(ti >= si, jnp.exp(diff * lg) * kscale, 0.0)
            qdec_ref[h] = jnp.exp((pos + 1.0) * lg)
            kdec_ref[h] = jnp.exp((c - 1.0 - pos) * lg) * kscale
            cdec_ref[h] = jnp.exp(jnp.full((1, dk), float(c), F32) * lg)

    heads = [slice(h * dk, (h + 1) * dk) for h in range(RET_HEADS)]
    qs = [q_ref[:, cols] for cols in heads]
    ks = [k_ref[:, cols] for cols in heads]
    vs = [v_ref[:, cols] for cols in heads]
    states = [state_ref[h] for h in range(RET_HEADS)]
    raw = [lax.dot_general(qs[h], ks[h], _DN_LAST, preferred_element_type=F32) for h in range(RET_HEADS)]
    cross = [jnp.dot(qs[h], states[h].astype(BF16), preferred_element_type=F32) for h in range(RET_HEADS)]
    outs, new_states = [], []
    for h, cols in enumerate(heads):
        inner = jnp.dot((raw[h] * dmask_ref[h]).astype(BF16), vs[h], preferred_element_type=F32)
        kd = (ks[h].astype(F32) * kdec_ref[h]).astype(BF16)
        kv = lax.dot_general(kd, vs[h], _DN_FIRST, preferred_element_type=F32)
        new_states.append(states[h] * cdec_ref[h] + kv)
        r = inner + cross[h] * qdec_ref[h]
        rn = r * lax.rsqrt(jnp.mean(r * r, axis=-1, keepdims=True) + NORM_EPS)
        g = g_ref[:, cols].astype(F32)
        outs.append((rn * rnw_ref[:, cols] * _silu(g)).astype(BF16))
    for h, cols in enumerate(heads):
        state_ref[h] = new_states[h]
        o_ref[:, cols] = outs[h]


def _retention(proj, log_gamma, rnw, batch, seq):
    n = proj.shape[0]
    c = RET_CHUNK
    nc = seq // c
    dk = RET_HEAD_DIM
    base = 3 * ATTN_WIDTH // RET_WIDTH

    def col(off):
        return lambda b, t: (b * nc + t, base + off)

    return pl.pallas_call(
        _ret_kernel,
        grid=(batch, nc),
        in_specs=[pl.BlockSpec(memory_space=pltpu.SMEM),
                  pl.BlockSpec((c, RET_WIDTH), col(0)),
                  pl.BlockSpec((c, RET_WIDTH), col(1)),
                  pl.BlockSpec((c, RET_WIDTH), col(2)),
                  pl.BlockSpec((c, RET_WIDTH), col(3)),
                  pl.BlockSpec((1, RET_WIDTH), lambda b, t: (0, 0))],
        out_specs=pl.BlockSpec((c, RET_WIDTH), lambda b, t: (b * nc + t, 0)),
        out_shape=jax.ShapeDtypeStruct((n, RET_WIDTH), BF16),
        scratch_shapes=[pltpu.VMEM((RET_HEADS, dk, dk), F32), pltpu.VMEM((RET_HEADS, c, c), F32),
                        pltpu.VMEM((RET_HEADS, c, dk), F32), pltpu.VMEM((RET_HEADS, c, dk), F32),
                        pltpu.VMEM((RET_HEADS, 1, dk), F32)],
        compiler_params=_cparams(("arbitrary", "arbitrary")),
        name="retention",
    )(log_gamma, proj, proj, proj, proj, rnw)


def _outproj_kernel(a_ref, r_ref, w_ref, x_ref, mod_ref, nw_ref, wr_ref, br_ref,
                    x1_ref, hp_ref, lg_ref, wb_ref):
    @pl.when(pl.program_id(0) == 0)
    def _():
        wb_ref[...] = w_ref[...].astype(BF16)

    mix = jnp.dot(jnp.concatenate([a_ref[...], r_ref[...]], axis=1), wb_ref[...],
                  preferred_element_type=F32)
    m = mod_ref[0]
    x1 = x_ref[...] + m[2:3] * mix
    x1_ref[...] = x1
    y = x1 * lax.rsqrt(jnp.mean(x1 * x1, axis=-1, keepdims=True) + NORM_EPS) * nw_ref[...]
    h2 = y * (1.0 + m[4:5]) + m[3:4]
    hp_ref[...] = h2
    hi = h2.astype(BF16)
    lo = (h2 - hi.astype(F32)).astype(BF16)
    part = jnp.dot(jnp.concatenate([hi, lo], axis=1), wr_ref[...], preferred_element_type=F32)
    lg_ref[...] = part + pltpu.roll(part, LANES // 2, axis=1) + br_ref[...]


def _outproj(attn, ret, w_bf, x2d, mod3, nw, wr_cat, br, seq, tm):
    n, d = x2d.shape
    per_b = seq // tm
    return pl.pallas_call(
        _outproj_kernel,
        grid=(n // tm,),
        in_specs=[pl.BlockSpec((tm, ATTN_WIDTH), lambda i: (i, 0)),
                  pl.BlockSpec((tm, RET_WIDTH), lambda i: (i, 0)),
                  pl.BlockSpec((d, d), lambda i: (0, 0), pipeline_mode=pl.Buffered(1)),
                  pl.BlockSpec((tm, d), lambda i: (i, 0)),
                  pl.BlockSpec((1, N_MOD, d), lambda i: (i // per_b, 0, 0)),
                  pl.BlockSpec((1, d), lambda i: (0, 0)),
                  pl.BlockSpec((2 * d, LANES), lambda i: (0, 0)),
                  pl.BlockSpec((1, LANES), lambda i: (0, 0))],
        out_specs=[pl.BlockSpec((tm, d), lambda i: (i, 0)),
                   pl.BlockSpec((tm, d), lambda i: (i, 0)),
                   pl.BlockSpec((tm, LANES), lambda i: (i, 0))],
        out_shape=[jax.ShapeDtypeStruct((n, d), F32),
                   jax.ShapeDtypeStruct((n, d), F32),
                   jax.ShapeDtypeStruct((n, LANES), F32)],
        scratch_shapes=[pltpu.VMEM((d, d), BF16)],
        compiler_params=_cparams(("arbitrary",)),
        name="outproj",
    )(attn, ret, w_bf, x2d, mod3, nw, wr_cat, br)


_L_E0, _L_E1, _L_W0, _L_W1, _L_D0, _L_D1 = 0, 1, 2, 3, 4, 5
_M_TILE_E, _M_NVALID, _M_NTILES, _M_COUNT, _M_START = 0, 1, 2, 3, 4
_PLAN_ROWS = 256
_PLAN_GROUP = 4


def _first_lane_where(cond, lane):
    return jnp.min(jnp.where(cond, lane, float(LANES)), axis=1, keepdims=True)


def _plan_kernel(lg_ref, info_ref, meta_ref, *, n_tok, n_tiles):
    tb = _PLAN_ROWS
    lane = lax.broadcasted_iota(I32, (tb, LANES), 1).astype(F32)
    row = lax.broadcasted_iota(I32, (tb, tb), 0)
    colm = lax.broadcasted_iota(I32, (tb, tb), 1)
    tri = jnp.where(colm < row, 1.0, 0.0).astype(BF16)
    n_g, n_e = N_GROUPS, EXPERTS_PER_GROUP

    def route_block(r0):
        lg = lg_ref[pl.ds(r0, tb), :]
        gl = jnp.where(lane < n_g, lg, -jnp.inf)
        ge = jnp.exp(gl - jnp.max(gl, axis=1, keepdims=True))
        gp = ge / jnp.sum(ge, axis=1, keepdims=True)
        g_w = jnp.max(gp, axis=1, keepdims=True)
        g_sel = _first_lane_where((gp == g_w) & (lane < n_g), lane)
        lo_lane = n_g + n_e * g_sel
        in_grp = (lane >= lo_lane) & (lane < lo_lane + n_e)
        el = jnp.where(in_grp, lg, -jnp.inf)
        ee = jnp.exp(el - jnp.max(el, axis=1, keepdims=True))
        ep = jnp.where(in_grp, ee / jnp.sum(ee, axis=1, keepdims=True), -1.0)
        p1 = jnp.max(ep, axis=1, keepdims=True)
        i1 = _first_lane_where(ep == p1, lane)
        ep2 = jnp.where(lane == i1, -1.0, ep)
        p2 = jnp.max(ep2, axis=1, keepdims=True)
        i2 = _first_lane_where(ep2 == p2, lane)
        denom = p1 + p2
        w0 = g_w * p1 / denom
        w1 = g_w * p2 / denom
        e0 = i1 - n_g
        e1 = i2 - n_g
        oh0 = jnp.where(lane == e0, 1.0, 0.0)
        oh1 = jnp.where(lane == e1, 1.0, 0.0)
        oh = oh0 + oh1
        local = jnp.dot(tri, oh.astype(BF16), preferred_element_type=F32)
        info = jnp.where(lane == _L_E0, e0,
               jnp.where(lane == _L_E1, e1,
               jnp.where(lane == _L_W0, w0,
               jnp.where(lane == _L_W1, w1, 0.0))))
        return info, oh0, oh1, local, jnp.sum(oh, axis=0, keepdims=True)

    group = _PLAN_GROUP if (n_tok // tb) % _PLAN_GROUP == 0 else 1

    def route(it, carry):
        starts = [pl.multiple_of((it * group + k) * tb, tb) for k in range(group)]
        blocks = [route_block(r0) for r0 in starts]
        infos = []
        for info, oh0, oh1, local, total in blocks:
            before = local + carry
            rank0 = jnp.sum(oh0 * before, axis=1, keepdims=True)
            rank1 = jnp.sum(oh1 * before, axis=1, keepdims=True)
            infos.append(info + jnp.where(lane == _L_D0, rank0, jnp.where(lane == _L_D1, rank1, 0.0)))
            carry = carry + total
        for r0, info in zip(starts, infos):
            info_ref[pl.ds(r0, tb), :] = info
        return carry

    counts = lax.fori_loop(0, n_tok // (tb * group), route, jnp.zeros((1, LANES), F32))

    lane1 = lax.broadcasted_iota(I32, (1, LANES), 1)
    padded = jnp.floor((counts + (FFN_TILE - 1.0)) * (1.0 / FFN_TILE)) * FFN_TILE
    pad_end = padded
    sh = 1
    while sh < N_EXPERTS:
        pad_end = pad_end + jnp.where(lane1 >= sh, pltpu.roll(pad_end, sh, axis=1), 0.0)
        sh *= 2
    pad_start = pad_end - padded

    def place(it, _):
        starts = [pl.multiple_of((it * group + k) * tb, tb) for k in range(group)]
        infos = [info_ref[pl.ds(r0, tb), :] for r0 in starts]
        placed = []
        for info in infos:
            e0 = info[:, _L_E0:_L_E0 + 1]
            e1 = info[:, _L_E1:_L_E1 + 1]
            s0 = jnp.sum(jnp.where(lane == e0, pad_start, 0.0), axis=1, keepdims=True)
            s1 = jnp.sum(jnp.where(lane == e1, pad_start, 0.0), axis=1, keepdims=True)
            placed.append(info + jnp.where(lane == _L_D0, s0, jnp.where(lane == _L_D1, s1, 0.0)))
        for r0, info in zip(starts, placed):
            info_ref[pl.ds(r0, tb), :] = info
        return 0

    lax.fori_loop(0, n_tok // (tb * group), place, 0)

    sq_r = lax.broadcasted_iota(I32, (LANES, LANES), 0)
    sq_c = lax.broadcasted_iota(I32, (LANES, LANES), 1)
    pe_col = jnp.sum(jnp.where(sq_r == sq_c, pad_end, 0.0), axis=1, keepdims=True)
    mt = meta_ref.shape[1]
    t_start = lax.broadcasted_iota(I32, (LANES, mt), 1).astype(F32) * FFN_TILE
    e_row = lax.broadcasted_iota(I32, (LANES, mt), 0)
    ended = jnp.where((pe_col <= t_start) & (e_row < N_EXPERTS), 1.0, 0.0)
    tile_e = jnp.minimum(jnp.sum(ended, axis=0, keepdims=True), N_EXPERTS - 1.0)
    total = jnp.sum(jnp.where(lane1 == N_EXPERTS - 1, pad_end, 0.0), axis=1, keepdims=True)
    n_valid = jnp.broadcast_to(total * (1.0 / FFN_TILE), (1, mt))
    def per_expert(v):
        return v if mt == LANES else jnp.concatenate([v, jnp.zeros((1, mt - LANES), F32)], axis=1)

    mrow = lax.broadcasted_iota(I32, (8, mt), 0)
    meta = jnp.where(mrow == _M_TILE_E, tile_e,
           jnp.where(mrow == _M_NVALID, n_valid,
           jnp.where(mrow == _M_NTILES, per_expert(padded * (1.0 / FFN_TILE)),
           jnp.where(mrow == _M_COUNT, per_expert(counts),
           jnp.where(mrow == _M_START, per_expert(pad_start), 0.0)))))
    meta_ref[...] = meta.astype(I32)


def _plan(logits, n_tiles):
    n_tok = logits.shape[0]
    mt = -(-n_tiles // LANES) * LANES
    kern = functools.partial(_plan_kernel, n_tok=n_tok, n_tiles=n_tiles)
    return pl.pallas_call(
        kern,
        out_shape=[jax.ShapeDtypeStruct((n_tok, LANES), F32),
                   jax.ShapeDtypeStruct((8, mt), I32)],
        compiler_params=pltpu.CompilerParams(vmem_limit_bytes=VMEM_LIMIT),
        name="plan",
    )(logits)


_DMA_UNROLL = 8


def _invert_kernel(d0_ref, d1_ref, nv_ref, ntl_ref, cnt_ref, start_ref, tok_ref, *, n_tok, p_rows):
    def zero_rows(lo, hi):
        def zbody(u, _):
            for s in range(_DMA_UNROLL):
                tok_ref[jnp.minimum(lo + u * _DMA_UNROLL + s, hi - 1)] = 0
            return 0
        lax.fori_loop(0, (hi - lo + _DMA_UNROLL - 1) // _DMA_UNROLL, zbody, 0)

    def pad_body(e, _):
        zero_rows(start_ref[e] + cnt_ref[e], start_ref[e] + ntl_ref[e] * FFN_TILE)
        return 0

    lax.fori_loop(0, N_EXPERTS, pad_body, 0)
    zero_rows(nv_ref[0] * FFN_TILE, p_rows)

    def body(u, _):
        t0 = u * _DMA_UNROLL
        rows = [(d0_ref[t0 + s], d1_ref[t0 + s]) for s in range(_DMA_UNROLL)]
        for s, (r0, r1) in enumerate(rows):
            tok_ref[r0] = t0 + s
            tok_ref[r1] = t0 + s
        return 0

    lax.fori_loop(0, n_tok // _DMA_UNROLL, body, 0)


def _invert(d0, d1, n_valid, ntl, cnt, start, p_rows):
    n_tok = d0.shape[0]
    kern = functools.partial(_invert_kernel, n_tok=n_tok, p_rows=p_rows)
    return pl.pallas_call(
        kern,
        grid_spec=pltpu.PrefetchScalarGridSpec(
            num_scalar_prefetch=6,
            grid=(1,),
            in_specs=[],
            out_specs=pl.BlockSpec(memory_space=pltpu.SMEM)),
        out_shape=jax.ShapeDtypeStruct((p_rows,), I32),
        compiler_params=_cparams(("arbitrary",)),
        name="invert",
    )(d0, d1, n_valid, ntl, cnt, start)


_FFN_SLOTS = 3
_BULK_DMA_PRIORITY = 1


def _ffn_kernel(te_ref, nv_ref, ntl_ref, tok_ref, h_ref, w1_ref, w3_ref, w2_ref, y_ref,
                xbuf, xb_ref, w1f, w3f, w2f, w1b, w3b, w2b, gsem, wsem, run_ref, *, n_tiles):
    i = pl.program_id(0)
    nv = nv_ref[0]
    valid = i < nv
    slot = i % _FFN_SLOTS

    def row_copy(tile, r, dst_slot):
        tok = tok_ref[tile * FFN_TILE + r]
        return pltpu.make_async_copy(h_ref.at[pl.ds(tok, 1)], xbuf.at[dst_slot, pl.ds(r, 1)],
                                     gsem.at[dst_slot])

    def weight_copies(e, ws):
        return (pltpu.make_async_copy(w1_ref.at[e], w1f.at[ws], wsem.at[ws]),
                pltpu.make_async_copy(w3_ref.at[e], w3f.at[ws], wsem.at[ws]),
                pltpu.make_async_copy(w2_ref.at[e], w2f.at[ws], wsem.at[ws]))

    @pl.when(i == 0)
    def _():
        run_ref[0] = 0
        for cp in weight_copies(te_ref[0], 0):
            cp.start(priority=_BULK_DMA_PRIORITY)

        def first_rows(u, _):
            for s in range(_DMA_UNROLL):
                for t in range(_FFN_SLOTS - 1):
                    row_copy(t, u * _DMA_UNROLL + s, t).start()
            return 0
        lax.fori_loop(0, FFN_TILE // _DMA_UNROLL, first_rows, 0)

    @pl.when(valid & ((i == 0) | (te_ref[i] != te_ref[jnp.maximum(i, 1) - 1])))
    def _():
        run = run_ref[0]
        ws = run % 2
        for cp in weight_copies(te_ref[i], ws):
            cp.wait()
        w1b[...] = w1f[ws].astype(BF16)
        w3b[...] = w3f[ws].astype(BF16)
        w2b[...] = w2f[ws].astype(BF16)
        nxt = i + ntl_ref[te_ref[i]]

        @pl.when(nxt < nv)
        def _():
            for cp in weight_copies(te_ref[jnp.minimum(nxt, n_tiles - 1)], 1 - ws):
                cp.start(priority=_BULK_DMA_PRIORITY)
        run_ref[0] = run + 1

    @pl.when(valid)
    def _():
        pltpu.make_async_copy(h_ref.at[pl.ds(0, FFN_TILE)], xbuf.at[slot], gsem.at[slot]).wait()
        xb_ref[...] = xbuf[slot].astype(BF16)
        ahead = _FFN_SLOTS - 1
        nxt_tile = jnp.minimum(i + ahead, n_tiles - 1)
        nxt_slot = (i + ahead) % _FFN_SLOTS
        for r in range(FFN_TILE):
            row_copy(nxt_tile, r, nxt_slot).start()
        x = xb_ref[...]
        a = jnp.dot(x, w1b[...], preferred_element_type=F32)
        b = jnp.dot(x, w3b[...], preferred_element_type=F32)
        hmid = (_silu(a) * b).astype(BF16)
        y_ref[...] = jnp.dot(hmid, w2b[...], preferred_element_type=F32)

    @pl.when(jnp.logical_not(valid))
    def _():
        y_ref[...] = jnp.zeros_like(y_ref)

    @pl.when(i == n_tiles - 1)
    def _():
        for t in range(_FFN_SLOTS - 1):
            s = (nv + t) % _FFN_SLOTS
            pltpu.make_async_copy(h_ref.at[pl.ds(0, FFN_TILE)], xbuf.at[s], gsem.at[s]).wait()


def _ffn(tile_e, n_valid, ntl, row_tok, h2, w1, w3, w2):
    p_rows = row_tok.shape[0]
    _, d, de = w1.shape
    n_tiles = p_rows // FFN_TILE
    kern = functools.partial(_ffn_kernel, n_tiles=n_tiles)
    any_spec = pl.BlockSpec(memory_space=pl.ANY)
    return pl.pallas_call(
        kern,
        grid_spec=pltpu.PrefetchScalarGridSpec(
            num_scalar_prefetch=4,
            grid=(n_tiles,),
            in_specs=[any_spec, any_spec, any_spec, any_spec],
            out_specs=pl.BlockSpec((FFN_TILE, d), lambda i, *_: (i, 0)),
            scratch_shapes=[pltpu.VMEM((_FFN_SLOTS, FFN_TILE, d), F32), pltpu.VMEM((FFN_TILE, d), BF16),
                            pltpu.VMEM((2, d, de), F32), pltpu.VMEM((2, d, de), F32),
                            pltpu.VMEM((2, de, d), F32),
                            pltpu.VMEM((d, de), BF16), pltpu.VMEM((d, de), BF16),
                            pltpu.VMEM((de, d), BF16),
                            pltpu.SemaphoreType.DMA((_FFN_SLOTS,)), pltpu.SemaphoreType.DMA((2,)),
                            pltpu.SMEM((1,), I32)]),
        out_shape=jax.ShapeDtypeStruct((p_rows, d), F32),
        compiler_params=_cparams(("arbitrary",)),
        name="ffn",
    )(tile_e, n_valid, ntl, row_tok, h2, w1, w3, w2)


def _combine_kernel(d0_ref, d1_ref, x1_ref, info_ref, mod_ref, nw_ref, y_ref, o_ref,
                    *scratch, tp, n_phases):
    i = pl.program_id(0)
    ns = _COMBINE_SETS
    sem = scratch[-1]
    bufs = [(scratch[2 * q], scratch[2 * q + 1]) for q in range(ns)]

    def row_copies(phase, r, q):
        tok = phase * tp + r
        ya, yb = bufs[q]
        return (pltpu.make_async_copy(y_ref.at[pl.ds(d0_ref[tok], 1)], ya.at[pl.ds(r, 1)], sem.at[q]),
                pltpu.make_async_copy(y_ref.at[pl.ds(d1_ref[tok], 1)], yb.at[pl.ds(r, 1)], sem.at[q]))

    def wait_set(q):
        for buf in bufs[q]:
            pltpu.make_async_copy(y_ref.at[pl.ds(0, tp)], buf, sem.at[q]).wait()

    @pl.when(i == 0)
    def _():
        def first_rows(u, _):
            for s in range(_DMA_UNROLL):
                for q in range(ns - 1):
                    for cp in row_copies(q, u * _DMA_UNROLL + s, q):
                        cp.start()
            return 0
        lax.fori_loop(0, tp // _DMA_UNROLL, first_rows, 0)

    for q in range(ns):
        wait_set(q)
        ahead = jnp.minimum(i * ns + q + ns - 1, n_phases - 1)
        for r in range(tp):
            for prio, cp in enumerate(row_copies(ahead, r, (q + ns - 1) % ns)):
                cp.start(priority=prio)
        ya, yb = bufs[q]
        rows = pl.ds(q * tp, tp)
        info = info_ref[rows, :]
        moe = ya[...] * info[:, _L_W0:_L_W0 + 1] + yb[...] * info[:, _L_W1:_L_W1 + 1]
        x2 = x1_ref[rows, :] + mod_ref[0][5:6] * moe
        o_ref[rows, :] = x2 * lax.rsqrt(jnp.mean(x2 * x2, axis=-1, keepdims=True) + NORM_EPS) * nw_ref[...]

    @pl.when(i == n_phases // ns - 1)
    def _():
        for q in range(ns - 1):
            wait_set(q)


_COMBINE_SETS = 4


def _combine(d0, d1, x1, info, mod3, nfw, y, seq, tp=128):
    n, d = x1.shape
    tc = _COMBINE_SETS * tp
    per_b = seq // tc
    kern = functools.partial(_combine_kernel, tp=tp, n_phases=n // tp)
    return pl.pallas_call(
        kern,
        grid_spec=pltpu.PrefetchScalarGridSpec(
            num_scalar_prefetch=2,
            grid=(n // tc,),
            in_specs=[pl.BlockSpec((tc, d), lambda i, d0, d1: (i, 0)),
                      pl.BlockSpec((tc, LANES), lambda i, d0, d1: (i, 0)),
                      pl.BlockSpec((1, N_MOD, d), lambda i, d0, d1: (i // per_b, 0, 0)),
                      pl.BlockSpec((1, d), lambda i, d0, d1: (0, 0)),
                      pl.BlockSpec(memory_space=pl.ANY)],
            out_specs=pl.BlockSpec((tc, d), lambda i, d0, d1: (i, 0)),
            scratch_shapes=[pltpu.VMEM((tp, d), F32)] * (2 * _COMBINE_SETS)
                           + [pltpu.SemaphoreType.DMA((_COMBINE_SETS,))]),
        out_shape=jax.ShapeDtypeStruct((n, d), F32),
        compiler_params=_cparams(("arbitrary",)),
        name="combine",
    )(d0, d1, x1, info, mod3, nfw, y)


def _split_hi_lo(w):
    hi = w.astype(BF16)
    lo = (w - hi.astype(F32)).astype(BF16)
    return hi, lo


def kernel(x, c, w_ada, b_ada, norm_mix_w, w_in, ret_norm_w, w_out, norm_ffn_w,
           w_group, b_group, w_router, b_router, w1, w3, w2, norm_final_w):
    batch, seq, d = x.shape
    n = batch * seq
    depth = w_ada.shape[0]
    assert depth == 1, "the final rmsnorm is fused into the (single) layer's last kernel"
    tm = min(1024, seq)
    half = LANES // 2

    slopes = jnp.exp2(-8.0 * jnp.arange(1, N_ATTN_HEADS + 1, dtype=F32) / N_ATTN_HEADS)
    log_gamma = jnp.log1p(-jnp.exp2(-5.0 - jnp.arange(RET_HEADS, dtype=F32)))
    n_tiles = (2 * n) // FFN_TILE + N_EXPERTS
    p_rows = n_tiles * FFN_TILE

    c_pad = jnp.zeros((8, d), F32).at[:batch].set(c)
    x2d = x.reshape(n, d)
    for l in range(depth):
        mod = _adaln(c_pad, w_ada[l], b_ada[l].reshape(1, -1))
        mod3 = mod[:batch].reshape(batch, N_MOD, d)

        proj, kmean = _inproj(x2d, mod3, norm_mix_w[l].reshape(1, d), w_in[l], seq, tm)
        kmean = kmean.reshape(batch, seq // MOBA_BLOCK, ATTN_WIDTH)
        attn = _moba(proj, kmean, slopes, batch, seq)
        ret = _retention(proj, log_gamma, ret_norm_w[l].reshape(1, RET_WIDTH), batch, seq)

        wr = jnp.concatenate([w_group[l], jnp.transpose(w_router[l], (1, 0, 2)).reshape(d, N_EXPERTS)], axis=1)
        wr = jnp.pad(wr, ((0, 0), (0, half - wr.shape[1])))
        wr_hi, wr_lo = _split_hi_lo(wr)
        wr_cat = jnp.concatenate([jnp.concatenate([wr_hi, wr_lo], axis=1),
                                  jnp.concatenate([wr_hi, jnp.zeros_like(wr_lo)], axis=1)], axis=0)
        br = jnp.concatenate([b_group[l], b_router[l].reshape(-1)])
        br = jnp.pad(br, (0, LANES - br.shape[0])).reshape(1, LANES)

        x1, h2p, logits = _outproj(attn, ret, w_out[l], x2d, mod3,
                                   norm_ffn_w[l].reshape(1, d), wr_cat, br, seq, min(512, seq))

        info, meta = _plan(logits, n_tiles)
        d0 = info[:, _L_D0].astype(I32)
        d1 = info[:, _L_D1].astype(I32)
        tile_e = meta[_M_TILE_E, :n_tiles]
        n_valid = meta[_M_NVALID, :1]
        ntl = meta[_M_NTILES, :N_EXPERTS]

        row_tok = _invert(d0, d1, n_valid, ntl, meta[_M_COUNT, :N_EXPERTS],
                          meta[_M_START, :N_EXPERTS], p_rows)
        y = _ffn(tile_e, n_valid, ntl, row_tok, h2p, w1[l], w3[l], w2[l])
        nfw = norm_final_w.reshape(1, d)
        x2d = _combine(d0, d1, x1, info, mod3, nfw, y, seq)
    return x2d.reshape(batch, seq, d)
```

```python
import functools

import jax
import jax.numpy as jnp
from jax import lax
from jax.experimental import pallas as pl
from jax.experimental.pallas import tpu as pltpu

F32 = jnp.float32
BF16 = jnp.bfloat16
I32 = jnp.int32

N_ATTN_HEADS = 8
ATTN_HEAD_DIM = 128
ATTN_WIDTH = N_ATTN_HEADS * ATTN_HEAD_DIM
MOBA_BLOCK = 256
MOBA_TOPK = 3
RET_HEADS = 4
RET_HEAD_DIM = 256
RET_WIDTH = RET_HEADS * RET_HEAD_DIM
N_GROUPS = 4
EXPERTS_PER_GROUP = 8
N_EXPERTS = N_GROUPS * EXPERTS_PER_GROUP
N_MOD = 6
NORM_EPS = 1e-6

LANES = 128
RET_CHUNK = 256
FFN_TILE = 256
MASK_VALUE = -1e30
LOG2E = 1.4426950408889634
MOBA_QK_SCALE = ATTN_HEAD_DIM ** -0.5 * LOG2E
VMEM_LIMIT = 56 * 1024 * 1024

_DN_LAST = (((1,), (1,)), ((), ()))
_DN_FIRST = (((0,), (0,)), ((), ()))


def _cparams(sem):
    return pltpu.CompilerParams(dimension_semantics=sem, vmem_limit_bytes=VMEM_LIMIT)


def _silu(v):
    return v * jax.nn.sigmoid(v)


def _store_token_major(ref, value):
    rows, d = value.shape
    nc = d // LANES
    for c in range(nc):
        ref[pl.ds(c, rows, stride=nc), :] = value[:, c * LANES:(c + 1) * LANES]


def _load_token_major(ref, rows, d, dtype):
    nc = d // LANES
    return jnp.concatenate([ref[pl.ds(c, rows, stride=nc), :].astype(dtype) for c in range(nc)], axis=1)


def _adaln_kernel(c_ref, w_ref, b_ref, o_ref):
    ca = _silu(c_ref[...]).astype(BF16)
    o_ref[...] = jnp.dot(ca, w_ref[...].astype(BF16), preferred_element_type=F32) + b_ref[...]


def _adaln(c_pad, w_ada, b_ada, tn=1024):
    rows, d = c_pad.shape
    n = w_ada.shape[1]
    return pl.pallas_call(
        _adaln_kernel,
        grid=(n // tn,),
        in_specs=[pl.BlockSpec((rows, d), lambda j: (0, 0)),
                  pl.BlockSpec((d, tn), lambda j: (0, j)),
                  pl.BlockSpec((1, tn), lambda j: (0, j))],
        out_specs=pl.BlockSpec((rows, tn), lambda j: (0, j)),
        out_shape=jax.ShapeDtypeStruct((rows, n), F32),
        compiler_params=_cparams(("arbitrary",)),
        name="adaln",
    )(c_pad, w_ada, b_ada)


def _inproj_kernel(x_ref, mod_ref, nw_ref, w_ref, o_ref, km_ref, h_ref, *, tm, tn):
    j = pl.program_id(1)

    @pl.when(j == 0)
    def _():
        x = x_ref[...]
        y = x * lax.rsqrt(jnp.mean(x * x, axis=-1, keepdims=True) + NORM_EPS) * nw_ref[...]
        m = mod_ref[0]
        h_ref[...] = (y * (1.0 + m[1:2]) + m[0:1]).astype(BF16)

    acc = jnp.dot(h_ref[...], w_ref[...].astype(BF16), preferred_element_type=F32)
    o_ref[...] = (acc * jnp.where(j == 0, MOBA_QK_SCALE, 1.0)).astype(BF16)

    @pl.when(j == 1)
    def _():
        km_ref[0] = jnp.sum(acc.reshape(tm // MOBA_BLOCK, MOBA_BLOCK, tn), axis=1) * (1.0 / MOBA_BLOCK)


def _inproj(x2d, mod3, nw, w_bf, seq, tm, tn=ATTN_WIDTH):
    n, d = x2d.shape
    width = w_bf.shape[1]
    per_b = seq // tm
    kern = functools.partial(_inproj_kernel, tm=tm, tn=tn)
    return pl.pallas_call(
        kern,
        grid=(n // tm, width // tn),
        in_specs=[pl.BlockSpec((tm, d), lambda i, j: (i, 0)),
                  pl.BlockSpec((1, N_MOD, d), lambda i, j: (i // per_b, 0, 0)),
                  pl.BlockSpec((1, d), lambda i, j: (0, 0)),
                  pl.BlockSpec((d, tn), lambda i, j: (0, j))],
        out_specs=[pl.BlockSpec((tm, tn), lambda i, j: (i, j)),
                   pl.BlockSpec((1, tm // MOBA_BLOCK, tn), lambda i, j: (i, 0, 0))],
        out_shape=[jax.ShapeDtypeStruct((n, width), BF16),
                   jax.ShapeDtypeStruct((n // tm, tm // MOBA_BLOCK, tn), F32)],
        scratch_shapes=[pltpu.VMEM((tm, d), BF16)],
        compiler_params=_cparams(("arbitrary", "arbitrary")),
        name="inproj",
    )(x2d, mod3, nw, w_bf)


def _moba_kernel(slopes_ref, q_ref, k_ref, v_ref, km_ref, o_ref,
                 vt_ref, sel_ref, bias_ref, acc_ref, *, nb, hps):
    hg = pl.program_id(1)
    i = pl.program_id(2)
    blk = MOBA_BLOCK
    dh = ATTN_HEAD_DIM
    log2e = LOG2E

    qpos = lax.broadcasted_iota(I32, (blk, blk), 1)
    kpos = lax.broadcasted_iota(I32, (blk, blk), 0)

    @pl.when(i == 0)
    def _():
        vt_ref[...] = v_ref[...].astype(F32).T.astype(BF16)
        dist = (qpos - kpos).astype(F32)
        for g in range(hps):
            bias_ref[g] = dist * (-log2e * slopes_ref[hg * hps + g])

    bidx = lax.broadcasted_iota(I32, (nb, blk), 0)
    past = bidx < i
    i0 = pl.multiple_of(i * blk, blk)
    heads = [slice(g * dh, (g + 1) * dh) for g in range(hps)]
    gates, raw_own = [], []
    for cols in heads:
        q = q_ref[:, cols]
        km = km_ref[0, :, cols]
        km_hi = km.astype(BF16)
        km_lo = (km - km_hi.astype(F32)).astype(BF16)
        gates.append(lax.dot_general(km_hi, q, _DN_LAST, preferred_element_type=F32)
                     + lax.dot_general(km_lo, q, _DN_LAST, preferred_element_type=F32))
        raw_own.append(lax.dot_general(k_ref[pl.ds(i0, blk), cols], q, _DN_LAST,
                                       preferred_element_type=F32))
    own_bias = [bias_ref[g] for g in range(hps)]

    carry0, acc0, sels = [], [], []
    for g, cols in enumerate(heads):
        gate = jnp.where(past, gates[g], -jnp.inf)
        rank = jnp.zeros((nb, blk), F32)
        for m in range(nb):
            gm = gate[m:m + 1, :]
            beats = jnp.where(gm > gate, 1.0, jnp.where((gm == gate) & (m < bidx), 1.0, 0.0))
            rank = rank + jnp.where(m < i, beats, 0.0)
        sels.append(jnp.where(past & (rank < MOBA_TOPK), 1.0, 0.0))

        s = jnp.where(qpos >= kpos, own_bias[g] + raw_own[g], MASK_VALUE)
        m0 = jnp.max(s, axis=0, keepdims=True)
        p = jnp.exp2(s - m0)
        acc0.append(jnp.dot(vt_ref[cols, pl.ds(i0, blk)], p.astype(BF16), preferred_element_type=F32))
        carry0 += [m0, jnp.sum(p, axis=0, keepdims=True)]
    for g in range(hps):
        acc_ref[g] = acc0[g]
        sel_ref[g] = sels[g]

    def body(j, carry):
        j0 = pl.multiple_of(j * blk, blk)
        off = ((i - j) * blk).astype(F32) * log2e
        raw = [lax.dot_general(k_ref[pl.ds(j0, blk), cols], q_ref[:, cols], _DN_LAST,
                               preferred_element_type=F32) for cols in heads]
        chosen = [sel_ref[g, pl.ds(j, 1), :] > 0.0 for g in range(hps)]
        acc_old = [acc_ref[g] for g in range(hps)]
        out, acc_new = [], []
        for g, cols in enumerate(heads):
            m_run, l_run = carry[2 * g], carry[2 * g + 1]
            shift = -slopes_ref[hg * hps + g] * off
            s = bias_ref[g] + raw[g]
            m_new = jnp.maximum(m_run, jnp.where(chosen[g], jnp.max(s, axis=0, keepdims=True) + shift,
                                                 MASK_VALUE))
            alpha = jnp.exp2(m_run - m_new)
            p = jnp.exp2(s - jnp.where(chosen[g], m_new - shift, -MASK_VALUE))
            acc_new.append(alpha * acc_old[g] + jnp.dot(vt_ref[cols, pl.ds(j0, blk)], p.astype(BF16),
                                                        preferred_element_type=F32))
            out += [m_new, alpha * l_run + jnp.sum(p, axis=0, keepdims=True)]
        for g in range(hps):
            acc_ref[g] = acc_new[g]
        return tuple(out)

    fin = lax.fori_loop(0, i, body, tuple(carry0))
    for g in range(hps):
        o_ref[:, g * dh:(g + 1) * dh] = (acc_ref[g] / fin[2 * g + 1]).T.astype(BF16)


MOBA_HEADS_PER_STEP = 8


def _moba(proj, kmean, slopes, batch, seq):
    n = proj.shape[0]
    nb = seq // MOBA_BLOCK
    hps = MOBA_HEADS_PER_STEP
    w = hps * ATTN_HEAD_DIM
    ng = N_ATTN_HEADS // hps
    kern = functools.partial(_moba_kernel, nb=nb, hps=hps)
    return pl.pallas_call(
        kern,
        grid=(batch, ng, nb),
        in_specs=[pl.BlockSpec(memory_space=pltpu.SMEM),
                  pl.BlockSpec((MOBA_BLOCK, w), lambda b, h, i: (b * nb + i, h)),
                  pl.BlockSpec((seq, w), lambda b, h, i: (b, ng + h)),
                  pl.BlockSpec((seq, w), lambda b, h, i: (b, 2 * ng + h)),
                  pl.BlockSpec((1, nb, w), lambda b, h, i: (b, 0, h))],
        out_specs=pl.BlockSpec((MOBA_BLOCK, w), lambda b, h, i: (b * nb + i, h)),
        out_shape=jax.ShapeDtypeStruct((n, ATTN_WIDTH), BF16),
        scratch_shapes=[pltpu.VMEM((w, seq), BF16),
                        pltpu.VMEM((hps, nb, MOBA_BLOCK), F32),
                        pltpu.VMEM((hps, MOBA_BLOCK, MOBA_BLOCK), F32),
                        pltpu.VMEM((hps, ATTN_HEAD_DIM, MOBA_BLOCK), F32)],
        compiler_params=_cparams(("arbitrary", "arbitrary", "arbitrary")),
        name="moba",
    )(slopes, proj, proj, proj, kmean)


def _ret_kernel(lg_ref, q_ref, k_ref, v_ref, g_ref, rnw_ref, o_ref,
                state_ref, dmask_ref, qdec_ref, kdec_ref, cdec_ref):
    n = pl.program_id(1)
    c = RET_CHUNK
    dk = RET_HEAD_DIM
    kscale = dk ** -0.5

    @pl.when(n == 0)
    def _():
        state_ref[...] = jnp.zeros_like(state_ref)
        ti = lax.broadcasted_iota(I32, (c, c), 0)
        si = lax.broadcasted_iota(I32, (c, c), 1)
        diff = jnp.maximum(ti - si, 0).astype(F32)
        pos = lax.broadcasted_iota(I32, (c, dk), 0).astype(F32)
        for h in range(RET_HEADS):
            lg = lg_ref[h]
            dmask_ref[h] = jnp.where(ti >= si, jnp.exp(diff * lg) * kscale, 0.0)
            qdec_ref[h] = jnp.exp((pos + 1.0) * lg)
            kdec_ref[h] = jnp.exp((c - 1.0 - pos) * lg) * kscale
            cdec_ref[h] = jnp.exp(jnp.full((1, dk), float(c), F32) * lg)

    heads = [slice(h * dk, (h + 1) * dk) for h in range(RET_HEADS)]
    qs = [q_ref[:, cols] for cols in heads]
    ks = [k_ref[:, cols] for cols in heads]
    vs = [v_ref[:, cols] for cols in heads]
    states = [state_ref[h] for h in range(RET_HEADS)]
    raw = [lax.dot_general(qs[h], ks[h], _DN_LAST, preferred_element_type=F32) for h in range(RET_HEADS)]
    cross = [jnp.dot(qs[h], states[h].astype(BF16), preferred_element_type=F32) for h in range(RET_HEADS)]
    outs, new_states = [], []
    for h, cols in enumerate(heads):
        inner = jnp.dot((raw[h] * dmask_ref[h]).astype(BF16), vs[h], preferred_element_type=F32)
        kd = (ks[h].astype(F32) * kdec_ref[h]).astype(BF16)
        kv = lax.dot_general(kd, vs[h], _DN_FIRST, preferred_element_type=F32)
        new_states.append(states[h] * cdec_ref[h] + kv)
        r = inner + cross[h] * qdec_ref[h]
        rn = r * lax.rsqrt(jnp.mean(r * r, axis=-1, keepdims=True) + NORM_EPS)
        g = g_ref[:, cols].astype(F32)
        outs.append((rn * rnw_ref[:, cols] * _silu(g)).astype(BF16))
    for h, cols in enumerate(heads):
        state_ref[h] = new_states[h]
        o_ref[:, cols] = outs[h]


def _retention(proj, log_gamma, rnw, batch, seq):
    n = proj.shape[0]
    c = RET_CHUNK
    nc = seq // c
    dk = RET_HEAD_DIM
    base = 3 * ATTN_WIDTH // RET_WIDTH

    def col(off):
        return lambda b, t: (b * nc + t, base + off)

    return pl.pallas_call(
        _ret_kernel,
        grid=(batch, nc),
        in_specs=[pl.BlockSpec(memory_space=pltpu.SMEM),
                  pl.BlockSpec((c, RET_WIDTH), col(0)),
                  pl.BlockSpec((c, RET_WIDTH), col(1)),
                  pl.BlockSpec((c, RET_WIDTH), col(2)),
                  pl.BlockSpec((c, RET_WIDTH), col(3)),
                  pl.BlockSpec((1, RET_WIDTH), lambda b, t: (0, 0))],
        out_specs=pl.BlockSpec((c, RET_WIDTH), lambda b, t: (b * nc + t, 0)),
        out_shape=jax.ShapeDtypeStruct((n, RET_WIDTH), BF16),
        scratch_shapes=[pltpu.VMEM((RET_HEADS, dk, dk), F32), pltpu.VMEM((RET_HEADS, c, c), F32),
                        pltpu.VMEM((RET_HEADS, c, dk), F32), pltpu.VMEM((RET_HEADS, c, dk), F32),
                        pltpu.VMEM((RET_HEADS, 1, dk), F32)],
        compiler_params=_cparams(("arbitrary", "arbitrary")),
        name="retention",
    )(log_gamma, proj, proj, proj, proj, rnw)


def _outproj_kernel(a_ref, r_ref, w_ref, x_ref, mod_ref, nw_ref, wr_ref, br_ref,
                    x1_ref, hp_ref, lg_ref, wb_ref):
    @pl.when(pl.program_id(0) == 0)
    def _():
        wb_ref[...] = w_ref[...].astype(BF16)

    mix = jnp.dot(jnp.concatenate([a_ref[...], r_ref[...]], axis=1), wb_ref[...],
                  preferred_element_type=F32)
    m = mod_ref[0]
    x1 = x_ref[...] + m[2:3] * mix
    x1_ref[...] = x1
    y = x1 * lax.rsqrt(jnp.mean(x1 * x1, axis=-1, keepdims=True) + NORM_EPS) * nw_ref[...]
    h2 = y * (1.0 + m[4:5]) + m[3:4]
    _store_token_major(hp_ref, h2)
    hi = h2.astype(BF16)
    lo = (h2 - hi.astype(F32)).astype(BF16)
    part = jnp.dot(jnp.concatenate([hi, lo], axis=1), wr_ref[...], preferred_element_type=F32)
    lg_ref[...] = part + pltpu.roll(part, LANES // 2, axis=1) + br_ref[...]


def _outproj(attn, ret, w_bf, x2d, mod3, nw, wr_cat, br, seq, tm):
    n, d = x2d.shape
    per_b = seq // tm
    return pl.pallas_call(
        _outproj_kernel,
        grid=(n // tm,),
        in_specs=[pl.BlockSpec((tm, ATTN_WIDTH), lambda i: (i, 0)),
                  pl.BlockSpec((tm, RET_WIDTH), lambda i: (i, 0)),
                  pl.BlockSpec((d, d), lambda i: (0, 0), pipeline_mode=pl.Buffered(1)),
                  pl.BlockSpec((tm, d), lambda i: (i, 0)),
                  pl.BlockSpec((1, N_MOD, d), lambda i: (i // per_b, 0, 0)),
                  pl.BlockSpec((1, d), lambda i: (0, 0)),
                  pl.BlockSpec((2 * d, LANES), lambda i: (0, 0)),
                  pl.BlockSpec((1, LANES), lambda i: (0, 0))],
        out_specs=[pl.BlockSpec((tm, d), lambda i: (i, 0)),
                   pl.BlockSpec((tm * (d // LANES), LANES), lambda i: (i, 0)),
                   pl.BlockSpec((tm, LANES), lambda i: (i, 0))],
        out_shape=[jax.ShapeDtypeStruct((n, d), F32),
                   jax.ShapeDtypeStruct((n * (d // LANES), LANES), F32),
                   jax.ShapeDtypeStruct((n, LANES), F32)],
        scratch_shapes=[pltpu.VMEM((d, d), BF16)],
        compiler_params=_cparams(("arbitrary",)),
        name="outproj",
    )(attn, ret, w_bf, x2d, mod3, nw, wr_cat, br)


_L_E0, _L_E1, _L_W0, _L_W1, _L_D0, _L_D1 = 0, 1, 2, 3, 4, 5
_M_TILE_E, _M_NVALID, _M_NTILES, _M_COUNT, _M_START = 0, 1, 2, 3, 4
_PLAN_ROWS = 256
_PLAN_GROUP = 4


def _first_lane_where(cond, lane):
    return jnp.min(jnp.where(cond, lane, float(LANES)), axis=1, keepdims=True)


def _plan_kernel(lg_ref, info_ref, meta_ref, *, n_tok, n_tiles):
    tb = _PLAN_ROWS
    lane = lax.broadcasted_iota(I32, (tb, LANES), 1).astype(F32)
    row = lax.broadcasted_iota(I32, (tb, tb), 0)
    colm = lax.broadcasted_iota(I32, (tb, tb), 1)
    tri = jnp.where(colm < row, 1.0, 0.0).astype(BF16)
    n_g, n_e = N_GROUPS, EXPERTS_PER_GROUP

    def route_block(r0):
        lg = lg_ref[pl.ds(r0, tb), :]
        gl = jnp.where(lane < n_g, lg, -jnp.inf)
        ge = jnp.exp(gl - jnp.max(gl, axis=1, keepdims=True))
        gp = ge / jnp.sum(ge, axis=1, keepdims=True)
        g_w = jnp.max(gp, axis=1, keepdims=True)
        g_sel = _first_lane_where((gp == g_w) & (lane < n_g), lane)
        lo_lane = n_g + n_e * g_sel
        in_grp = (lane >= lo_lane) & (lane < lo_lane + n_e)
        el = jnp.where(in_grp, lg, -jnp.inf)
        ee = jnp.exp(el - jnp.max(el, axis=1, keepdims=True))
        ep = jnp.where(in_grp, ee / jnp.sum(ee, axis=1, keepdims=True), -1.0)
        p1 = jnp.max(ep, axis=1, keepdims=True)
        i1 = _first_lane_where(ep == p1, lane)
        ep2 = jnp.where(lane == i1, -1.0, ep)
        p2 = jnp.max(ep2, axis=1, keepdims=True)
        i2 = _first_lane_where(ep2 == p2, lane)
        denom = p1 + p2
        w0 = g_w * p1 / denom
        w1 = g_w * p2 / denom
        e0 = i1 - n_g
        e1 = i2 - n_g
        oh0 = jnp.where(lane == e0, 1.0, 0.0)
        oh1 = jnp.where(lane == e1, 1.0, 0.0)
        oh = oh0 + oh1
        local = jnp.dot(tri, oh.astype(BF16), preferred_element_type=F32)
        info = jnp.where(lane == _L_E0, e0,
               jnp.where(lane == _L_E1, e1,
               jnp.where(lane == _L_W0, w0,
               jnp.where(lane == _L_W1, w1, 0.0))))
        return info, oh0, oh1, local, jnp.sum(oh, axis=0, keepdims=True)

    group = _PLAN_GROUP if (n_tok // tb) % _PLAN_GROUP == 0 else 1

    def route(it, carry):
        starts = [pl.multiple_of((it * group + k) * tb, tb) for k in range(group)]
        blocks = [route_block(r0) for r0 in starts]
        infos = []
        for info, oh0, oh1, local, total in blocks:
            before = local + carry
            rank0 = jnp.sum(oh0 * before, axis=1, keepdims=True)
            rank1 = jnp.sum(oh1 * before, axis=1, keepdims=True)
            infos.append(info + jnp.where(lane == _L_D0, rank0, jnp.where(lane == _L_D1, rank1, 0.0)))
            carry = carry + total
        for r0, info in zip(starts, infos):
            info_ref[pl.ds(r0, tb), :] = info
        return carry

    counts = lax.fori_loop(0, n_tok // (tb * group), route, jnp.zeros((1, LANES), F32))

    lane1 = lax.broadcasted_iota(I32, (1, LANES), 1)
    padded = jnp.floor((counts + (FFN_TILE - 1.0)) * (1.0 / FFN_TILE)) * FFN_TILE
    pad_end = padded
    sh = 1
    while sh < N_EXPERTS:
        pad_end = pad_end + jnp.where(lane1 >= sh, pltpu.roll(pad_end, sh, axis=1), 0.0)
        sh *= 2
    pad_start = pad_end - padded

    def place(it, _):
        starts = [pl.multiple_of((it * group + k) * tb, tb) for k in range(group)]
        infos = [info_ref[pl.ds(r0, tb), :] for r0 in starts]
        placed = []
        for info in infos:
            e0 = info[:, _L_E0:_L_E0 + 1]
            e1 = info[:, _L_E1:_L_E1 + 1]
            s0 = jnp.sum(jnp.where(lane == e0, pad_start, 0.0), axis=1, keepdims=True)
            s1 = jnp.sum(jnp.where(lane == e1, pad_start, 0.0), axis=1, keepdims=True)
            placed.append(info + jnp.where(lane == _L_D0, s0, jnp.where(lane == _L_D1, s1, 0.0)))
        for r0, info in zip(starts, placed):
            info_ref[pl.ds(r0, tb), :] = info
        return 0

    lax.fori_loop(0, n_tok // (tb * group), place, 0)

    sq_r = lax.broadcasted_iota(I32, (LANES, LANES), 0)
    sq_c = lax.broadcasted_iota(I32, (LANES, LANES), 1)
    pe_col = jnp.sum(jnp.where(sq_r == sq_c, pad_end, 0.0), axis=1, keepdims=True)
    mt = meta_ref.shape[1]
    t_start = lax.broadcasted_iota(I32, (LANES, mt), 1).astype(F32) * FFN_TILE
    e_row = lax.broadcasted_iota(I32, (LANES, mt), 0)
    ended = jnp.where((pe_col <= t_start) & (e_row < N_EXPERTS), 1.0, 0.0)
    tile_e = jnp.minimum(jnp.sum(ended, axis=0, keepdims=True), N_EXPERTS - 1.0)
    total = jnp.sum(jnp.where(lane1 == N_EXPERTS - 1, pad_end, 0.0), axis=1, keepdims=True)
    n_valid = jnp.broadcast_to(total * (1.0 / FFN_TILE), (1, mt))
    def per_expert(v):
        return v if mt == LANES else jnp.concatenate([v, jnp.zeros((1, mt - LANES), F32)], axis=1)

    mrow = lax.broadcasted_iota(I32, (8, mt), 0)
    meta = jnp.where(mrow == _M_TILE_E, tile_e,
           jnp.where(mrow == _M_NVALID, n_valid,
           jnp.where(mrow == _M_NTILES, per_expert(padded * (1.0 / FFN_TILE)),
           jnp.where(mrow == _M_COUNT, per_expert(counts),
           jnp.where(mrow == _M_START, per_expert(pad_start), 0.0)))))
    meta_ref[...] = meta.astype(I32)


def _plan(logits, n_tiles):
    n_tok = logits.shape[0]
    mt = -(-n_tiles // LANES) * LANES
    kern = functools.partial(_plan_kernel, n_tok=n_tok, n_tiles=n_tiles)
    return pl.pallas_call(
        kern,
        out_shape=[jax.ShapeDtypeStruct((n_tok, LANES), F32),
                   jax.ShapeDtypeStruct((8, mt), I32)],
        compiler_params=pltpu.CompilerParams(vmem_limit_bytes=VMEM_LIMIT),
        name="plan",
    )(logits)


_DMA_UNROLL = 8


def _invert_kernel(d0_ref, d1_ref, nv_ref, ntl_ref, cnt_ref, start_ref, tok_ref, *, n_tok, p_rows):
    def zero_rows(lo, hi):
        def zbody(u, _):
            for s in range(_DMA_UNROLL):
                tok_ref[jnp.minimum(lo + u * _DMA_UNROLL + s, hi - 1)] = 0
            return 0
        lax.fori_loop(0, (hi - lo + _DMA_UNROLL - 1) // _DMA_UNROLL, zbody, 0)

    def pad_body(e, _):
        zero_rows(start_ref[e] + cnt_ref[e], start_ref[e] + ntl_ref[e] * FFN_TILE)
        return 0

    lax.fori_loop(0, N_EXPERTS, pad_body, 0)
    zero_rows(nv_ref[0] * FFN_TILE, p_rows)

    def body(u, _):
        t0 = u * _DMA_UNROLL
        rows = [(d0_ref[t0 + s], d1_ref[t0 + s]) for s in range(_DMA_UNROLL)]
        for s, (r0, r1) in enumerate(rows):
            tok_ref[r0] = t0 + s
            tok_ref[r1] = t0 + s
        return 0

    lax.fori_loop(0, n_tok // _DMA_UNROLL, body, 0)


def _invert(d0, d1, n_valid, ntl, cnt, start, p_rows):
    n_tok = d0.shape[0]
    kern = functools.partial(_invert_kernel, n_tok=n_tok, p_rows=p_rows)
    return pl.pallas_call(
        kern,
        grid_spec=pltpu.PrefetchScalarGridSpec(
            num_scalar_prefetch=6,
            grid=(1,),
            in_specs=[],
            out_specs=pl.BlockSpec(memory_space=pltpu.SMEM)),
        out_shape=jax.ShapeDtypeStruct((p_rows,), I32),
        compiler_params=_cparams(("arbitrary",)),
        name="invert",
    )(d0, d1, n_valid, ntl, cnt, start)


_FFN_SLOTS = 3
_BULK_DMA_PRIORITY = 1


def _ffn_kernel(te_ref, nv_ref, ntl_ref, tok_ref, h_ref, w1_ref, w3_ref, w2_ref, y_ref,
                xbuf, xb_ref, w1f, w3f, w2f, w1b, w3b, w2b, gsem, wsem, run_ref, *, n_tiles):
    i = pl.program_id(0)
    nv = nv_ref[0]
    valid = i < nv
    slot = i % _FFN_SLOTS
    nc = xb_ref.shape[1] // LANES

    def row_copy(tile, r, dst_slot):
        tok = tok_ref[tile * FFN_TILE + r]
        src = pl.ds(pl.multiple_of(tok * nc, nc), nc)
        return pltpu.make_async_copy(h_ref.at[src], xbuf.at[dst_slot, pl.ds(r * nc, nc)], gsem.at[dst_slot])

    def weight_copies(e, ws):
        return (pltpu.make_async_copy(w1_ref.at[e], w1f.at[ws], wsem.at[ws]),
                pltpu.make_async_copy(w3_ref.at[e], w3f.at[ws], wsem.at[ws]),
                pltpu.make_async_copy(w2_ref.at[e], w2f.at[ws], wsem.at[ws]))

    @pl.when(i == 0)
    def _():
        run_ref[0] = 0
        for cp in weight_copies(te_ref[0], 0):
            cp.start(priority=_BULK_DMA_PRIORITY)

        def first_rows(u, _):
            for s in range(_DMA_UNROLL):
                for t in range(_FFN_SLOTS - 1):
                    row_copy(t, u * _DMA_UNROLL + s, t).start()
            return 0
        lax.fori_loop(0, FFN_TILE // _DMA_UNROLL, first_rows, 0)

    @pl.when(valid & ((i == 0) | (te_ref[i] != te_ref[jnp.maximum(i, 1) - 1])))
    def _():
        run = run_ref[0]
        ws = run % 2
        for cp in weight_copies(te_ref[i], ws):
            cp.wait()
        w1b[...] = w1f[ws].astype(BF16)
        w3b[...] = w3f[ws].astype(BF16)
        w2b[...] = w2f[ws].astype(BF16)
        nxt = i + ntl_ref[te_ref[i]]

        @pl.when(nxt < nv)
        def _():
            for cp in weight_copies(te_ref[jnp.minimum(nxt, n_tiles - 1)], 1 - ws):
                cp.start(priority=_BULK_DMA_PRIORITY)
        run_ref[0] = run + 1

    @pl.when(valid)
    def _():
        pltpu.make_async_copy(h_ref.at[pl.ds(0, FFN_TILE * nc)], xbuf.at[slot], gsem.at[slot]).wait()
        xb_ref[...] = _load_token_major(xbuf.at[slot], FFN_TILE, xb_ref.shape[1], BF16)
        ahead = _FFN_SLOTS - 1
        nxt_tile = jnp.minimum(i + ahead, n_tiles - 1)
        nxt_slot = (i + ahead) % _FFN_SLOTS
        for r in range(FFN_TILE):
            row_copy(nxt_tile, r, nxt_slot).start()
        x = xb_ref[...]
        a = jnp.dot(x, w1b[...], preferred_element_type=F32)
        b = jnp.dot(x, w3b[...], preferred_element_type=F32)
        hmid = (_silu(a) * b).astype(BF16)
        y_ref[...] = jnp.dot(hmid, w2b[...], preferred_element_type=F32)

    @pl.when(jnp.logical_not(valid))
    def _():
        y_ref[...] = jnp.zeros_like(y_ref)

    @pl.when(i == n_tiles - 1)
    def _():
        for t in range(_FFN_SLOTS - 1):
            s = (nv + t) % _FFN_SLOTS
            pltpu.make_async_copy(h_ref.at[pl.ds(0, FFN_TILE * nc)], xbuf.at[s], gsem.at[s]).wait()


def _ffn(tile_e, n_valid, ntl, row_tok, h2, w1, w3, w2):
    p_rows = row_tok.shape[0]
    _, d, de = w1.shape
    n_tiles = p_rows // FFN_TILE
    kern = functools.partial(_ffn_kernel, n_tiles=n_tiles)
    any_spec = pl.BlockSpec(memory_space=pl.ANY)
    return pl.pallas_call(
        kern,
        grid_spec=pltpu.PrefetchScalarGridSpec(
            num_scalar_prefetch=4,
            grid=(n_tiles,),
            in_specs=[any_spec, any_spec, any_spec, any_spec],
            out_specs=pl.BlockSpec((FFN_TILE, d), lambda i, *_: (i, 0)),
            scratch_shapes=[pltpu.VMEM((_FFN_SLOTS, FFN_TILE * (d // LANES), LANES), F32),
                            pltpu.VMEM((FFN_TILE, d), BF16),
                            pltpu.VMEM((2, d, de), F32), pltpu.VMEM((2, d, de), F32),
                            pltpu.VMEM((2, de, d), F32),
                            pltpu.VMEM((d, de), BF16), pltpu.VMEM((d, de), BF16),
                            pltpu.VMEM((de, d), BF16),
                            pltpu.SemaphoreType.DMA((_FFN_SLOTS,)), pltpu.SemaphoreType.DMA((2,)),
                            pltpu.SMEM((1,), I32)]),
        out_shape=jax.ShapeDtypeStruct((p_rows, d), F32),
        compiler_params=_cparams(("arbitrary",)),
        name="ffn",
    )(tile_e, n_valid, ntl, row_tok, h2, w1, w3, w2)


def _combine_kernel(d0_ref, d1_ref, x1_ref, info_ref, mod_ref, nw_ref, y_ref, o_ref,
                    *scratch, tp, n_phases):
    i = pl.program_id(0)
    ns = _COMBINE_SETS
    sem = scratch[-1]
    bufs = [(scratch[2 * q], scratch[2 * q + 1]) for q in range(ns)]

    def row_copies(phase, r, q):
        tok = phase * tp + r
        ya, yb = bufs[q]
        return (pltpu.make_async_copy(y_ref.at[pl.ds(d0_ref[tok], 1)], ya.at[pl.ds(r, 1)], sem.at[q]),
                pltpu.make_async_copy(y_ref.at[pl.ds(d1_ref[tok], 1)], yb.at[pl.ds(r, 1)], sem.at[q]))

    def wait_set(q):
        for buf in bufs[q]:
            pltpu.make_async_copy(y_ref.at[pl.ds(0, tp)], buf, sem.at[q]).wait()

    @pl.when(i == 0)
    def _():
        def first_rows(u, _):
            for s in range(_DMA_UNROLL):
                for q in range(ns - 1):
                    for cp in row_copies(q, u * _DMA_UNROLL + s, q):
                        cp.start()
            return 0
        lax.fori_loop(0, tp // _DMA_UNROLL, first_rows, 0)

    for q in range(ns):
        wait_set(q)
        ahead = jnp.minimum(i * ns + q + ns - 1, n_phases - 1)
        for r in range(tp):
            for prio, cp in enumerate(row_copies(ahead, r, (q + ns - 1) % ns)):
                cp.start(priority=prio)
        ya, yb = bufs[q]
        rows = pl.ds(q * tp, tp)
        info = info_ref[rows, :]
        moe = ya[...] * info[:, _L_W0:_L_W0 + 1] + yb[...] * info[:, _L_W1:_L_W1 + 1]
        x2 = x1_ref[rows, :] + mod_ref[0][5:6] * moe
        o_ref[rows, :] = x2 * lax.rsqrt(jnp.mean(x2 * x2, axis=-1, keepdims=True) + NORM_EPS) * nw_ref[...]

    @pl.when(i == n_phases // ns - 1)
    def _():
        for q in range(ns - 1):
            wait_set(q)


_COMBINE_SETS = 4


def _combine(d0, d1, x1, info, mod3, nfw, y, seq, tp=128):
    n, d = x1.shape
    tc = _COMBINE_SETS * tp
    per_b = seq // tc
    kern = functools.partial(_combine_kernel, tp=tp, n_phases=n // tp)
    return pl.pallas_call(
        kern,
        grid_spec=pltpu.PrefetchScalarGridSpec(
            num_scalar_prefetch=2,
            grid=(n // tc,),
            in_specs=[pl.BlockSpec((tc, d), lambda i, d0, d1: (i, 0)),
                      pl.BlockSpec((tc, LANES), lambda i, d0, d1: (i, 0)),
                      pl.BlockSpec((1, N_MOD, d), lambda i, d0, d1: (i // per_b, 0, 0)),
                      pl.BlockSpec((1, d), lambda i, d0, d1: (0, 0)),
                      pl.BlockSpec(memory_space=pl.ANY)],
            out_specs=pl.BlockSpec((tc, d), lambda i, d0, d1: (i, 0)),
            scratch_shapes=[pltpu.VMEM((tp, d), F32)] * (2 * _COMBINE_SETS)
                           + [pltpu.SemaphoreType.DMA((_COMBINE_SETS,))]),
        out_shape=jax.ShapeDtypeStruct((n, d), F32),
        compiler_params=_cparams(("arbitrary",)),
        name="combine",
    )(d0, d1, x1, info, mod3, nfw, y)


def _split_hi_lo(w):
    hi = w.astype(BF16)
    lo = (w - hi.astype(F32)).astype(BF16)
    return hi, lo


def kernel(x, c, w_ada, b_ada, norm_mix_w, w_in, ret_norm_w, w_out, norm_ffn_w,
           w_group, b_group, w_router, b_router, w1, w3, w2, norm_final_w):
    batch, seq, d = x.shape
    n = batch * seq
    depth = w_ada.shape[0]
    assert depth == 1, "the final rmsnorm is fused into the (single) layer's last kernel"
    tm = min(1024, seq)
    half = LANES // 2

    slopes = jnp.exp2(-8.0 * jnp.arange(1, N_ATTN_HEADS + 1, dtype=F32) / N_ATTN_HEADS)
    log_gamma = jnp.log1p(-jnp.exp2(-5.0 - jnp.arange(RET_HEADS, dtype=F32)))
    n_tiles = (2 * n) // FFN_TILE + N_EXPERTS
    p_rows = n_tiles * FFN_TILE

    c_pad = jnp.zeros((8, d), F32).at[:batch].set(c)
    x2d = x.reshape(n, d)
    for l in range(depth):
        mod = _adaln(c_pad, w_ada[l], b_ada[l].reshape(1, -1))
        mod3 = mod[:batch].reshape(batch, N_MOD, d)

        proj, kmean = _inproj(x2d, mod3, norm_mix_w[l].reshape(1, d), w_in[l], seq, tm)
        kmean = kmean.reshape(batch, seq // MOBA_BLOCK, ATTN_WIDTH)
        attn = _moba(proj, kmean, slopes, batch, seq)
        ret = _retention(proj, log_gamma, ret_norm_w[l].reshape(1, RET_WIDTH), batch, seq)

        wr = jnp.concatenate([w_group[l], jnp.transpose(w_router[l], (1, 0, 2)).reshape(d, N_EXPERTS)], axis=1)
        wr = jnp.pad(wr, ((0, 0), (0, half - wr.shape[1])))
        wr_hi, wr_lo = _split_hi_lo(wr)
        wr_cat = jnp.concatenate([jnp.concatenate([wr_hi, wr_lo], axis=1),
                                  jnp.concatenate([wr_hi, jnp.zeros_like(wr_lo)], axis=1)], axis=0)
        br = jnp.concatenate([b_group[l], b_router[l].reshape(-1)])
        br = jnp.pad(br, (0, LANES - br.shape[0])).reshape(1, LANES)

        x1, h2p, logits = _outproj(attn, ret, w_out[l], x2d, mod3,
                                   norm_ffn_w[l].reshape(1, d), wr_cat, br, seq, min(256, seq))

        info, meta = _plan(logits, n_tiles)
        d0 = info[:, _L_D0].astype(I32)
        d1 = info[:, _L_D1].astype(I32)
        tile_e = meta[_M_TILE_E, :n_tiles]
        n_valid = meta[_M_NVALID, :1]
        ntl = meta[_M_NTILES, :N_EXPERTS]

        row_tok = _invert(d0, d1, n_valid, ntl, meta[_M_COUNT, :N_EXPERTS],
                          meta[_M_START, :N_EXPERTS], p_rows)
        y = _ffn(tile_e, n_valid, ntl, row_tok, h2p, w1[l], w3[l], w2[l])
        nfw = norm_final_w.reshape(1, d)
        x2d = _combine(d0, d1, x1, info, mod3, nfw, y, seq)
    return x2d.reshape(batch, seq, d)
```

```python
import functools

import jax
import jax.numpy as jnp
from jax import lax
from jax.experimental import pallas as pl
from jax.experimental.pallas import tpu as pltpu

F32 = jnp.float32
BF16 = jnp.bfloat16
I32 = jnp.int32

N_ATTN_HEADS = 8
ATTN_HEAD_DIM = 128
ATTN_WIDTH = N_ATTN_HEADS * ATTN_HEAD_DIM
MOBA_BLOCK = 256
MOBA_TOPK = 3
RET_HEADS = 4
RET_HEAD_DIM = 256
RET_WIDTH = RET_HEADS * RET_HEAD_DIM
N_GROUPS = 4
EXPERTS_PER_GROUP = 8
N_EXPERTS = N_GROUPS * EXPERTS_PER_GROUP
N_MOD = 6
NORM_EPS = 1e-6

LANES = 128
RET_CHUNK = 256
FFN_TILE = 256
INPROJ_ROWS = 1024
OUTPROJ_ROWS = 256
MASK_VALUE = -1e30
LOG2E = 1.4426950408889634
MOBA_QK_SCALE = ATTN_HEAD_DIM ** -0.5 * LOG2E
VMEM_LIMIT = 56 * 1024 * 1024

_DN_LAST = (((1,), (1,)), ((), ()))
_DN_FIRST = (((0,), (0,)), ((), ()))


def _cparams(sem):
    return pltpu.CompilerParams(dimension_semantics=sem, vmem_limit_bytes=VMEM_LIMIT)


def _silu(v):
    return v * jax.nn.sigmoid(v)


def _store_token_major(ref, value):
    rows, d = value.shape
    nc = d // LANES
    for c in range(nc):
        ref[pl.ds(c, rows, stride=nc), :] = value[:, c * LANES:(c + 1) * LANES]


def _load_token_major(ref, rows, d, dtype):
    nc = d // LANES
    return jnp.concatenate([ref[pl.ds(c, rows, stride=nc), :].astype(dtype) for c in range(nc)], axis=1)


def _adaln_kernel(c_ref, w_ref, b_ref, o_ref):
    ca = _silu(c_ref[...]).astype(BF16)
    o_ref[...] = jnp.dot(ca, w_ref[...].astype(BF16), preferred_element_type=F32) + b_ref[...]


def _adaln(c_pad, w_ada, b_ada, tn=1024):
    rows, d = c_pad.shape
    n = w_ada.shape[1]
    return pl.pallas_call(
        _adaln_kernel,
        grid=(n // tn,),
        in_specs=[pl.BlockSpec((rows, d), lambda j: (0, 0)),
                  pl.BlockSpec((d, tn), lambda j: (0, j)),
                  pl.BlockSpec((1, tn), lambda j: (0, j))],
        out_specs=pl.BlockSpec((rows, tn), lambda j: (0, j)),
        out_shape=jax.ShapeDtypeStruct((rows, n), F32),
        compiler_params=_cparams(("arbitrary",)),
        name="adaln",
    )(c_pad, w_ada, b_ada)


def _inproj_kernel(x_ref, mod_ref, nw_ref, w_ref, o_ref, km_ref, h_ref, *, tm, tn):
    j = pl.program_id(1)

    @pl.when(j == 0)
    def _():
        x = x_ref[...]
        y = x * lax.rsqrt(jnp.mean(x * x, axis=-1, keepdims=True) + NORM_EPS) * nw_ref[...]
        m = mod_ref[0]
        h_ref[...] = (y * (1.0 + m[1:2]) + m[0:1]).astype(BF16)

    acc = jnp.dot(h_ref[...], w_ref[...].astype(BF16), preferred_element_type=F32)
    o_ref[...] = (acc * jnp.where(j == 0, MOBA_QK_SCALE, 1.0)).astype(BF16)

    @pl.when(j == 1)
    def _():
        km_ref[0] = jnp.sum(acc.reshape(tm // MOBA_BLOCK, MOBA_BLOCK, tn), axis=1) * (1.0 / MOBA_BLOCK)


def _inproj(x2d, mod3, nw, w_bf, seq, tm, tn=ATTN_WIDTH):
    n, d = x2d.shape
    width = w_bf.shape[1]
    per_b = seq // tm
    kern = functools.partial(_inproj_kernel, tm=tm, tn=tn)
    return pl.pallas_call(
        kern,
        grid=(n // tm, width // tn),
        in_specs=[pl.BlockSpec((tm, d), lambda i, j: (i, 0)),
                  pl.BlockSpec((1, N_MOD, d), lambda i, j: (i // per_b, 0, 0)),
                  pl.BlockSpec((1, d), lambda i, j: (0, 0)),
                  pl.BlockSpec((d, tn), lambda i, j: (0, j))],
        out_specs=[pl.BlockSpec((tm, tn), lambda i, j: (i, j)),
                   pl.BlockSpec((1, tm // MOBA_BLOCK, tn), lambda i, j: (i, 0, 0))],
        out_shape=[jax.ShapeDtypeStruct((n, width), BF16),
                   jax.ShapeDtypeStruct((n // tm, tm // MOBA_BLOCK, tn), F32)],
        scratch_shapes=[pltpu.VMEM((tm, d), BF16)],
        compiler_params=_cparams(("arbitrary", "arbitrary")),
        name="inproj",
    )(x2d, mod3, nw, w_bf)


def _moba_kernel(slopes_ref, q_ref, k_ref, v_ref, km_ref, o_ref,
                 vt_ref, sel_ref, bias_ref, acc_ref, *, nb, hps):
    hg = pl.program_id(1)
    i = pl.program_id(2)
    blk = MOBA_BLOCK
    dh = ATTN_HEAD_DIM
    log2e = LOG2E

    qpos = lax.broadcasted_iota(I32, (blk, blk), 1)
    kpos = lax.broadcasted_iota(I32, (blk, blk), 0)

    @pl.when(i == 0)
    def _():
        vt_ref[...] = v_ref[...].astype(F32).T.astype(BF16)
        dist = (qpos - kpos).astype(F32)
        for g in range(hps):
            bias_ref[g] = dist * (-log2e * slopes_ref[hg * hps + g])

    bidx = lax.broadcasted_iota(I32, (nb, blk), 0)
    past = bidx < i
    i0 = pl.multiple_of(i * blk, blk)
    heads = [slice(g * dh, (g + 1) * dh) for g in range(hps)]
    gates, raw_own = [], []
    for cols in heads:
        q = q_ref[:, cols]
        km = km_ref[0, :, cols]
        km_hi = km.astype(BF16)
        km_lo = (km - km_hi.astype(F32)).astype(BF16)
        gates.append(lax.dot_general(km_hi, q, _DN_LAST, preferred_element_type=F32)
                     + lax.dot_general(km_lo, q, _DN_LAST, preferred_element_type=F32))
        raw_own.append(lax.dot_general(k_ref[pl.ds(i0, blk), cols], q, _DN_LAST,
                                       preferred_element_type=F32))
    own_bias = [bias_ref[g] for g in range(hps)]

    carry0, acc0, sels = [], [], []
    for g, cols in enumerate(heads):
        gate = jnp.where(past, gates[g], -jnp.inf)
        rank = jnp.zeros((nb, blk), F32)
        for m in range(nb):
            gm = gate[m:m + 1, :]
            beats = jnp.where(gm > gate, 1.0, jnp.where((gm == gate) & (m < bidx), 1.0, 0.0))
            rank = rank + jnp.where(m < i, beats, 0.0)
        sels.append(jnp.where(past & (rank < MOBA_TOPK), 1.0, 0.0))

        s = jnp.where(qpos >= kpos, own_bias[g] + raw_own[g], MASK_VALUE)
        m0 = jnp.max(s, axis=0, keepdims=True)
        p = jnp.exp2(s - m0)
        acc0.append(jnp.dot(vt_ref[cols, pl.ds(i0, blk)], p.astype(BF16), preferred_element_type=F32))
        carry0 += [m0, jnp.sum(p, axis=0, keepdims=True)]
    for g in range(hps):
        acc_ref[g] = acc0[g]
        sel_ref[g] = sels[g]

    def body(j, carry):
        j0 = pl.multiple_of(j * blk, blk)
        off = ((i - j) * blk).astype(F32) * log2e
        raw = [lax.dot_general(k_ref[pl.ds(j0, blk), cols], q_ref[:, cols], _DN_LAST,
                               preferred_element_type=F32) for cols in heads]
        chosen = [sel_ref[g, pl.ds(j, 1), :] > 0.0 for g in range(hps)]
        acc_old = [acc_ref[g] for g in range(hps)]
        out, acc_new = [], []
        for g, cols in enumerate(heads):
            m_run, l_run = carry[2 * g], carry[2 * g + 1]
            shift = -slopes_ref[hg * hps + g] * off
            s = bias_ref[g] + raw[g]
            m_new = jnp.maximum(m_run, jnp.where(chosen[g], jnp.max(s, axis=0, keepdims=True) + shift,
                                                 MASK_VALUE))
            alpha = jnp.exp2(m_run - m_new)
            p = jnp.exp2(s - jnp.where(chosen[g], m_new - shift, -MASK_VALUE))
            acc_new.append(alpha * acc_old[g] + jnp.dot(vt_ref[cols, pl.ds(j0, blk)], p.astype(BF16),
                                                        preferred_element_type=F32))
            out += [m_new, alpha * l_run + jnp.sum(p, axis=0, keepdims=True)]
        for g in range(hps):
            acc_ref[g] = acc_new[g]
        return tuple(out)

    fin = lax.fori_loop(0, i, body, tuple(carry0))
    for g in range(hps):
        o_ref[:, g * dh:(g + 1) * dh] = (acc_ref[g] / fin[2 * g + 1]).T.astype(BF16)


MOBA_HEADS_PER_STEP = 8


def _moba(proj, kmean, slopes, batch, seq):
    n = proj.shape[0]
    nb = seq // MOBA_BLOCK
    hps = MOBA_HEADS_PER_STEP
    w = hps * ATTN_HEAD_DIM
    ng = N_ATTN_HEADS // hps
    kern = functools.partial(_moba_kernel, nb=nb, hps=hps)
    return pl.pallas_call(
        kern,
        grid=(batch, ng, nb),
        in_specs=[pl.BlockSpec(memory_space=pltpu.SMEM),
                  pl.BlockSpec((MOBA_BLOCK, w), lambda b, h, i: (b * nb + i, h)),
                  pl.BlockSpec((seq, w), lambda b, h, i: (b, ng + h)),
                  pl.BlockSpec((seq, w), lambda b, h, i: (b, 2 * ng + h)),
                  pl.BlockSpec((1, nb, w), lambda b, h, i: (b, 0, h))],
        out_specs=pl.BlockSpec((MOBA_BLOCK, w), lambda b, h, i: (b * nb + i, h)),
        out_shape=jax.ShapeDtypeStruct((n, ATTN_WIDTH), BF16),
        scratch_shapes=[pltpu.VMEM((w, seq), BF16),
                        pltpu.VMEM((hps, nb, MOBA_BLOCK), F32),
                        pltpu.VMEM((hps, MOBA_BLOCK, MOBA_BLOCK), F32),
                        pltpu.VMEM((hps, ATTN_HEAD_DIM, MOBA_BLOCK), F32)],
        compiler_params=_cparams(("arbitrary", "arbitrary", "arbitrary")),
        name="moba",
    )(slopes, proj, proj, proj, kmean)


def _ret_kernel(lg_ref, q_ref, k_ref, v_ref, g_ref, rnw_ref, o_ref,
                state_ref, dmask_ref, qdec_ref, kdec_ref, cdec_ref):
    n = pl.program_id(1)
    c = RET_CHUNK
    dk = RET_HEAD_DIM
    kscale = dk ** -0.5

    @pl.when(n == 0)
    def _():
        state_ref[...] = jnp.zeros_like(state_ref)
        ti = lax.broadcasted_iota(I32, (c, c), 0)
        si = lax.broadcasted_iota(I32, (c, c), 1)
        diff = jnp.maximum(ti - si, 0).astype(F32)
        pos = lax.broadcasted_iota(I32, (c, dk), 0).astype(F32)
        for h in range(RET_HEADS):
            lg = lg_ref[h]
            dmask_ref[h] = jnp.where(ti >= si, jnp.exp(diff * lg) * kscale, 0.0)
            qdec_ref[h] = jnp.exp((pos + 1.0) * lg)
            kdec_ref[h] = jnp.exp((c - 1.0 - pos) * lg) * kscale
            cdec_ref[h] = jnp.exp(jnp.full((1, dk), float(c), F32) * lg)

    heads = [slice(h * dk, (h + 1) * dk) for h in range(RET_HEADS)]
    qs = [q_ref[:, cols] for cols in heads]
    ks = [k_ref[:, cols] for cols in heads]
    vs = [v_ref[:, cols] for cols in heads]
    states = [state_ref[h] for h in range(RET_HEADS)]
    raw = [lax.dot_general(qs[h], ks[h], _DN_LAST, preferred_element_type=F32) for h in range(RET_HEADS)]
    cross = [jnp.dot(qs[h], states[h].astype(BF16), preferred_element_type=F32) for h in range(RET_HEADS)]
    outs, new_states = [], []
    for h, cols in enumerate(heads):
        inner = jnp.dot((raw[h] * dmask_ref[h]).astype(BF16), vs[h], preferred_element_type=F32)
        kd = (ks[h].astype(F32) * kdec_ref[h]).astype(BF16)
        kv = lax.dot_general(kd, vs[h], _DN_FIRST, preferred_element_type=F32)
        new_states.append(states[h] * cdec_ref[h] + kv)
        r = inner + cross[h] * qdec_ref[h]
        rn = r * lax.rsqrt(jnp.mean(r * r, axis=-1, keepdims=True) + NORM_EPS)
        g = g_ref[:, cols].astype(F32)
        outs.append((rn * rnw_ref[:, cols] * _silu(g)).astype(BF16))
    for h, cols in enumerate(heads):
        state_ref[h] = new_states[h]
        o_ref[:, cols] = outs[h]


def _retention(proj, log_gamma, rnw, batch, seq):
    n = proj.shape[0]
    c = RET_CHUNK
    nc = seq // c
    dk = RET_HEAD_DIM
    base = 3 * ATTN_WIDTH // RET_WIDTH

    def col(off):
        return lambda b, t: (b * nc + t, base + off)

    return pl.pallas_call(
        _ret_kernel,
        grid=(batch, nc),
        in_specs=[pl.BlockSpec(memory_space=pltpu.SMEM),
                  pl.BlockSpec((c, RET_WIDTH), col(0)),
                  pl.BlockSpec((c, RET_WIDTH), col(1)),
                  pl.BlockSpec((c, RET_WIDTH), col(2)),
                  pl.BlockSpec((c, RET_WIDTH), col(3)),
                  pl.BlockSpec((1, RET_WIDTH), lambda b, t: (0, 0))],
        out_specs=pl.BlockSpec((c, RET_WIDTH), lambda b, t: (b * nc + t, 0)),
        out_shape=jax.ShapeDtypeStruct((n, RET_WIDTH), BF16),
        scratch_shapes=[pltpu.VMEM((RET_HEADS, dk, dk), F32), pltpu.VMEM((RET_HEADS, c, c), F32),
                        pltpu.VMEM((RET_HEADS, c, dk), F32), pltpu.VMEM((RET_HEADS, c, dk), F32),
                        pltpu.VMEM((RET_HEADS, 1, dk), F32)],
        compiler_params=_cparams(("arbitrary", "arbitrary")),
        name="retention",
    )(log_gamma, proj, proj, proj, proj, rnw)


def _outproj_kernel(a_ref, r_ref, w_ref, x_ref, mod_ref, nw_ref, wr_ref, br_ref,
                    x1_ref, hp_ref, lg_ref, wb_ref):
    @pl.when(pl.program_id(0) == 0)
    def _():
        wb_ref[...] = w_ref[...].astype(BF16)

    mix = jnp.dot(jnp.concatenate([a_ref[...], r_ref[...]], axis=1), wb_ref[...],
                  preferred_element_type=F32)
    m = mod_ref[0]
    x1 = x_ref[...] + m[2:3] * mix
    x1_ref[...] = x1
    y = x1 * lax.rsqrt(jnp.mean(x1 * x1, axis=-1, keepdims=True) + NORM_EPS) * nw_ref[...]
    h2 = y * (1.0 + m[4:5]) + m[3:4]
    _store_token_major(hp_ref, h2)
    hi = h2.astype(BF16)
    lo = (h2 - hi.astype(F32)).astype(BF16)
    part = jnp.dot(jnp.concatenate([hi, lo], axis=1), wr_ref[...], preferred_element_type=F32)
    lg_ref[...] = part + pltpu.roll(part, LANES // 2, axis=1) + br_ref[...]


def _outproj(attn, ret, w_bf, x2d, mod3, nw, wr_cat, br, seq, tm):
    n, d = x2d.shape
    per_b = seq // tm
    return pl.pallas_call(
        _outproj_kernel,
        grid=(n // tm,),
        in_specs=[pl.BlockSpec((tm, ATTN_WIDTH), lambda i: (i, 0)),
                  pl.BlockSpec((tm, RET_WIDTH), lambda i: (i, 0)),
                  pl.BlockSpec((d, d), lambda i: (0, 0), pipeline_mode=pl.Buffered(1)),
                  pl.BlockSpec((tm, d), lambda i: (i, 0)),
                  pl.BlockSpec((1, N_MOD, d), lambda i: (i // per_b, 0, 0)),
                  pl.BlockSpec((1, d), lambda i: (0, 0)),
                  pl.BlockSpec((2 * d, LANES), lambda i: (0, 0)),
                  pl.BlockSpec((1, LANES), lambda i: (0, 0))],
        out_specs=[pl.BlockSpec((tm, d), lambda i: (i, 0)),
                   pl.BlockSpec((tm * (d // LANES), LANES), lambda i: (i, 0)),
                   pl.BlockSpec((tm, LANES), lambda i: (i, 0))],
        out_shape=[jax.ShapeDtypeStruct((n, d), F32),
                   jax.ShapeDtypeStruct((n * (d // LANES), LANES), F32),
                   jax.ShapeDtypeStruct((n, LANES), F32)],
        scratch_shapes=[pltpu.VMEM((d, d), BF16)],
        compiler_params=_cparams(("arbitrary",)),
        name="outproj",
    )(attn, ret, w_bf, x2d, mod3, nw, wr_cat, br)


_L_E0, _L_E1, _L_W0, _L_W1, _L_D0, _L_D1 = 0, 1, 2, 3, 4, 5
_ROUTER_EXPERT_LANE = 8
_M_TILE_E, _M_NVALID, _M_NTILES, _M_COUNT, _M_START = 0, 1, 2, 3, 4
_PLAN_ROWS = 256
_PLAN_GROUP = 4


def _first_row_where(cond, rows, limit):
    return jnp.min(jnp.where(cond, rows, limit), axis=0, keepdims=True)


def _col_from_row(v):
    r = lax.broadcasted_iota(I32, (LANES, LANES), 0)
    c = lax.broadcasted_iota(I32, (LANES, LANES), 1)
    return jnp.sum(jnp.where(r == c, v, 0.0), axis=1, keepdims=True)


def _row_from_col(v):
    r = lax.broadcasted_iota(I32, (LANES, LANES), 0)
    c = lax.broadcasted_iota(I32, (LANES, LANES), 1)
    return jnp.sum(jnp.where(r == c, v, 0.0), axis=0, keepdims=True)


def _rows_to_tile(rows, row8):
    tile = 0.0
    for k in reversed(range(len(rows))):
        tile = jnp.where(row8 == k, rows[k], tile)
    return tile


def _plan_kernel(lg_ref, info_ref, dest_ref, meta_ref, *, n_tok):
    tb = _PLAN_ROWS
    n_g, n_e = N_GROUPS, EXPERTS_PER_GROUP
    row8 = lax.broadcasted_iota(I32, (8, tb), 0).astype(F32)
    row_e = lax.broadcasted_iota(I32, (N_EXPERTS, tb), 0).astype(F32)
    earlier = lax.broadcasted_iota(I32, (tb, tb), 0) < lax.broadcasted_iota(I32, (tb, tb), 1)
    tri = jnp.where(earlier, 1.0, 0.0).astype(BF16)

    def route_block(r0):
        lt = lg_ref[pl.ds(r0, tb), :].T
        gl = jnp.where(row8 < n_g, lt[0:8, :], -jnp.inf)
        ge = jnp.exp(gl - jnp.max(gl, axis=0, keepdims=True))
        gp = ge / jnp.sum(ge, axis=0, keepdims=True)
        g_w = jnp.max(gp, axis=0, keepdims=True)
        g_sel = _first_row_where((gp == g_w) & (row8 < n_g), row8, 8.0)
        el = lt[_ROUTER_EXPERT_LANE:_ROUTER_EXPERT_LANE + n_e, :]
        for g in range(1, n_g):
            lo = _ROUTER_EXPERT_LANE + g * n_e
            el = jnp.where(g_sel == g, lt[lo:lo + n_e, :], el)
        ee = jnp.exp(el - jnp.max(el, axis=0, keepdims=True))
        ep = ee / jnp.sum(ee, axis=0, keepdims=True)
        p1 = jnp.max(ep, axis=0, keepdims=True)
        i1 = _first_row_where(ep == p1, row8, 8.0)
        ep2 = jnp.where(row8 == i1, -1.0, ep)
        p2 = jnp.max(ep2, axis=0, keepdims=True)
        i2 = _first_row_where(ep2 == p2, row8, 8.0)
        denom = p1 + p2
        w0 = g_w * p1 / denom
        w1 = g_w * p2 / denom
        e0 = g_sel * n_e + i1
        e1 = g_sel * n_e + i2
        oh0 = jnp.where(row_e == e0, 1.0, 0.0)
        oh1 = jnp.where(row_e == e1, 1.0, 0.0)
        oh = oh0 + oh1
        local = jnp.dot(oh.astype(BF16), tri, preferred_element_type=F32)
        return (e0, e1, w0, w1), oh0, oh1, local, jnp.sum(oh, axis=1, keepdims=True)

    group = _PLAN_GROUP if (n_tok // tb) % _PLAN_GROUP == 0 else 1

    def route(it, carry):
        starts = [pl.multiple_of((it * group + k) * tb, tb) for k in range(group)]
        blocks = [route_block(r0) for r0 in starts]
        tiles = []
        for (e0, e1, w0, w1), oh0, oh1, local, total in blocks:
            before = local + carry
            rank0 = jnp.sum(oh0 * before, axis=0, keepdims=True)
            rank1 = jnp.sum(oh1 * before, axis=0, keepdims=True)
            tiles.append(_rows_to_tile([e0, e1, w0, w1, rank0, rank1], row8))
            carry = carry + total
        for r0, tile in zip(starts, tiles):
            info_ref[:, pl.ds(r0, tb)] = tile
        return carry

    counts_col = lax.fori_loop(0, n_tok // (tb * group), route, jnp.zeros((N_EXPERTS, 1), F32))
    counts = _row_from_col(jnp.concatenate([counts_col, jnp.zeros((LANES - N_EXPERTS, 1), F32)], axis=0))

    lane1 = lax.broadcasted_iota(I32, (1, LANES), 1)
    padded = jnp.floor((counts + (FFN_TILE - 1.0)) * (1.0 / FFN_TILE)) * FFN_TILE
    pad_end = padded
    sh = 1
    while sh < N_EXPERTS:
        pad_end = pad_end + jnp.where(lane1 >= sh, pltpu.roll(pad_end, sh, axis=1), 0.0)
        sh *= 2
    pad_start = pad_end - padded

    start_col = _col_from_row(pad_start)[0:N_EXPERTS, :]

    def place(it, _):
        starts = [pl.multiple_of((it * group + k) * tb, tb) for k in range(group)]
        tiles = [info_ref[:, pl.ds(r0, tb)] for r0 in starts]
        placed = []
        for tile in tiles:
            s0 = jnp.sum(jnp.where(row_e == tile[_L_E0:_L_E0 + 1, :], start_col, 0.0), axis=0, keepdims=True)
            s1 = jnp.sum(jnp.where(row_e == tile[_L_E1:_L_E1 + 1, :], start_col, 0.0), axis=0, keepdims=True)
            placed.append(tile + jnp.where(row8 == _L_D0, s0, jnp.where(row8 == _L_D1, s1, 0.0)))
        for r0, tile in zip(starts, placed):
            info_ref[:, pl.ds(r0, tb)] = tile
            dest_ref[:, pl.ds(r0, tb)] = tile.astype(I32)
        return 0

    lax.fori_loop(0, n_tok // (tb * group), place, 0)

    pe_col = _col_from_row(pad_end)
    mt = meta_ref.shape[1]
    t_start = lax.broadcasted_iota(I32, (LANES, mt), 1).astype(F32) * FFN_TILE
    e_row = lax.broadcasted_iota(I32, (LANES, mt), 0)
    ended = jnp.where((pe_col <= t_start) & (e_row < N_EXPERTS), 1.0, 0.0)
    tile_e = jnp.minimum(jnp.sum(ended, axis=0, keepdims=True), N_EXPERTS - 1.0)
    total = jnp.sum(jnp.where(lane1 == N_EXPERTS - 1, pad_end, 0.0), axis=1, keepdims=True)
    n_valid = jnp.broadcast_to(total * (1.0 / FFN_TILE), (1, mt))
    def per_expert(v):
        return v if mt == LANES else jnp.concatenate([v, jnp.zeros((1, mt - LANES), F32)], axis=1)

    mrow = lax.broadcasted_iota(I32, (8, mt), 0)
    meta = jnp.where(mrow == _M_TILE_E, tile_e,
           jnp.where(mrow == _M_NVALID, n_valid,
           jnp.where(mrow == _M_NTILES, per_expert(padded * (1.0 / FFN_TILE)),
           jnp.where(mrow == _M_COUNT, per_expert(counts),
           jnp.where(mrow == _M_START, per_expert(pad_start), 0.0)))))
    meta_ref[...] = meta.astype(I32)


def _plan(logits, n_tiles):
    n_tok = logits.shape[0]
    mt = -(-n_tiles // LANES) * LANES
    kern = functools.partial(_plan_kernel, n_tok=n_tok)
    return pl.pallas_call(
        kern,
        out_shape=[jax.ShapeDtypeStruct((8, n_tok), F32),
                   jax.ShapeDtypeStruct((8, n_tok), I32),
                   jax.ShapeDtypeStruct((8, mt), I32)],
        compiler_params=pltpu.CompilerParams(vmem_limit_bytes=VMEM_LIMIT),
        name="plan",
    )(logits)


_DMA_UNROLL = 8


def _invert_kernel(d0_ref, d1_ref, nv_ref, ntl_ref, cnt_ref, start_ref, tok_ref, *, n_tok, p_rows):
    def zero_rows(lo, hi):
        def zbody(u, _):
            for s in range(_DMA_UNROLL):
                tok_ref[jnp.minimum(lo + u * _DMA_UNROLL + s, hi - 1)] = 0
            return 0
        lax.fori_loop(0, (hi - lo + _DMA_UNROLL - 1) // _DMA_UNROLL, zbody, 0)

    def pad_body(e, _):
        zero_rows(start_ref[e] + cnt_ref[e], start_ref[e] + ntl_ref[e] * FFN_TILE)
        return 0

    lax.fori_loop(0, N_EXPERTS, pad_body, 0)
    zero_rows(nv_ref[0] * FFN_TILE, p_rows)

    def body(u, _):
        t0 = u * _DMA_UNROLL
        rows = [(d0_ref[t0 + s], d1_ref[t0 + s]) for s in range(_DMA_UNROLL)]
        for s, (r0, r1) in enumerate(rows):
            tok_ref[r0] = t0 + s
            tok_ref[r1] = t0 + s
        return 0

    lax.fori_loop(0, n_tok // _DMA_UNROLL, body, 0)


def _invert(d0, d1, n_valid, ntl, cnt, start, p_rows):
    n_tok = d0.shape[0]
    kern = functools.partial(_invert_kernel, n_tok=n_tok, p_rows=p_rows)
    return pl.pallas_call(
        kern,
        grid_spec=pltpu.PrefetchScalarGridSpec(
            num_scalar_prefetch=6,
            grid=(1,),
            in_specs=[],
            out_specs=pl.BlockSpec(memory_space=pltpu.SMEM)),
        out_shape=jax.ShapeDtypeStruct((p_rows,), I32),
        compiler_params=_cparams(("arbitrary",)),
        name="invert",
    )(d0, d1, n_valid, ntl, cnt, start)


_FFN_SLOTS = 3
_BULK_DMA_PRIORITY = 1


def _ffn_kernel(te_ref, nv_ref, ntl_ref, tok_ref, h_ref, w1_ref, w3_ref, w2_ref, y_ref,
                xbuf, xb_ref, w1f, w3f, w2f, w1b, w3b, w2b, gsem, wsem, run_ref, *, n_tiles):
    i = pl.program_id(0)
    nv = nv_ref[0]
    valid = i < nv
    slot = i % _FFN_SLOTS
    nc = xb_ref.shape[1] // LANES

    def row_copy(tile, r, dst_slot):
        tok = tok_ref[tile * FFN_TILE + r]
        src = pl.ds(pl.multiple_of(tok * nc, nc), nc)
        return pltpu.make_async_copy(h_ref.at[src], xbuf.at[dst_slot, pl.ds(r * nc, nc)], gsem.at[dst_slot])

    def weight_copies(e, ws):
        return (pltpu.make_async_copy(w1_ref.at[e], w1f.at[ws], wsem.at[ws]),
                pltpu.make_async_copy(w3_ref.at[e], w3f.at[ws], wsem.at[ws]),
                pltpu.make_async_copy(w2_ref.at[e], w2f.at[ws], wsem.at[ws]))

    @pl.when(i == 0)
    def _():
        run_ref[0] = 0
        for cp in weight_copies(te_ref[0], 0):
            cp.start(priority=_BULK_DMA_PRIORITY)

        def first_rows(u, _):
            for s in range(_DMA_UNROLL):
                for t in range(_FFN_SLOTS - 1):
                    row_copy(t, u * _DMA_UNROLL + s, t).start()
            return 0
        lax.fori_loop(0, FFN_TILE // _DMA_UNROLL, first_rows, 0)

    @pl.when(valid & ((i == 0) | (te_ref[i] != te_ref[jnp.maximum(i, 1) - 1])))
    def _():
        run = run_ref[0]
        ws = run % 2
        for cp in weight_copies(te_ref[i], ws):
            cp.wait()
        w1b[...] = w1f[ws].astype(BF16)
        w3b[...] = w3f[ws].astype(BF16)
        w2b[...] = w2f[ws].astype(BF16)
        nxt = i + ntl_ref[te_ref[i]]

        @pl.when(nxt < nv)
        def _():
            for cp in weight_copies(te_ref[jnp.minimum(nxt, n_tiles - 1)], 1 - ws):
                cp.start(priority=_BULK_DMA_PRIORITY)
        run_ref[0] = run + 1

    @pl.when(valid)
    def _():
        pltpu.make_async_copy(h_ref.at[pl.ds(0, FFN_TILE * nc)], xbuf.at[slot], gsem.at[slot]).wait()
        xb_ref[...] = _load_token_major(xbuf.at[slot], FFN_TILE, xb_ref.shape[1], BF16)
        ahead = _FFN_SLOTS - 1
        nxt_tile = jnp.minimum(i + ahead, n_tiles - 1)
        nxt_slot = (i + ahead) % _FFN_SLOTS
        for r in range(FFN_TILE):
            row_copy(nxt_tile, r, nxt_slot).start()
        x = xb_ref[...]
        a = jnp.dot(x, w1b[...], preferred_element_type=F32)
        b = jnp.dot(x, w3b[...], preferred_element_type=F32)
        hmid = (_silu(a) * b).astype(BF16)
        y_ref[...] = jnp.dot(hmid, w2b[...], preferred_element_type=F32)

    @pl.when(jnp.logical_not(valid))
    def _():
        y_ref[...] = jnp.zeros_like(y_ref)

    @pl.when(i == n_tiles - 1)
    def _():
        for t in range(_FFN_SLOTS - 1):
            s = (nv + t) % _FFN_SLOTS
            pltpu.make_async_copy(h_ref.at[pl.ds(0, FFN_TILE * nc)], xbuf.at[s], gsem.at[s]).wait()


def _ffn(tile_e, n_valid, ntl, row_tok, h2, w1, w3, w2):
    p_rows = row_tok.shape[0]
    _, d, de = w1.shape
    n_tiles = p_rows // FFN_TILE
    kern = functools.partial(_ffn_kernel, n_tiles=n_tiles)
    any_spec = pl.BlockSpec(memory_space=pl.ANY)
    return pl.pallas_call(
        kern,
        grid_spec=pltpu.PrefetchScalarGridSpec(
            num_scalar_prefetch=4,
            grid=(n_tiles,),
            in_specs=[any_spec, any_spec, any_spec, any_spec],
            out_specs=pl.BlockSpec((FFN_TILE, d), lambda i, *_: (i, 0)),
            scratch_shapes=[pltpu.VMEM((_FFN_SLOTS, FFN_TILE * (d // LANES), LANES), F32),
                            pltpu.VMEM((FFN_TILE, d), BF16),
                            pltpu.VMEM((2, d, de), F32), pltpu.VMEM((2, d, de), F32),
                            pltpu.VMEM((2, de, d), F32),
                            pltpu.VMEM((d, de), BF16), pltpu.VMEM((d, de), BF16),
                            pltpu.VMEM((de, d), BF16),
                            pltpu.SemaphoreType.DMA((_FFN_SLOTS,)), pltpu.SemaphoreType.DMA((2,)),
                            pltpu.SMEM((1,), I32)]),
        out_shape=jax.ShapeDtypeStruct((p_rows, d), F32),
        compiler_params=_cparams(("arbitrary",)),
        name="ffn",
    )(tile_e, n_valid, ntl, row_tok, h2, w1, w3, w2)


def _combine_kernel(d0_ref, d1_ref, x1_ref, info_ref, mod_ref, nw_ref, y_ref, o_ref,
                    *scratch, tp, n_phases):
    i = pl.program_id(0)
    ns = _COMBINE_SETS
    sem = scratch[-1]
    bufs = [(scratch[2 * q], scratch[2 * q + 1]) for q in range(ns)]

    def row_copies(phase, r, q):
        tok = phase * tp + r
        ya, yb = bufs[q]
        return (pltpu.make_async_copy(y_ref.at[pl.ds(d0_ref[tok], 1)], ya.at[pl.ds(r, 1)], sem.at[q]),
                pltpu.make_async_copy(y_ref.at[pl.ds(d1_ref[tok], 1)], yb.at[pl.ds(r, 1)], sem.at[q]))

    def wait_set(q):
        for buf in bufs[q]:
            pltpu.make_async_copy(y_ref.at[pl.ds(0, tp)], buf, sem.at[q]).wait()

    @pl.when(i == 0)
    def _():
        def first_rows(u, _):
            for s in range(_DMA_UNROLL):
                for q in range(ns - 1):
                    for cp in row_copies(q, u * _DMA_UNROLL + s, q):
                        cp.start()
            return 0
        lax.fori_loop(0, tp // _DMA_UNROLL, first_rows, 0)

    for q in range(ns):
        wait_set(q)
        ahead = jnp.minimum(i * ns + q + ns - 1, n_phases - 1)
        for r in range(tp):
            for prio, cp in enumerate(row_copies(ahead, r, (q + ns - 1) % ns)):
                cp.start(priority=prio)
        ya, yb = bufs[q]
        rows = pl.ds(q * tp, tp)
        w0 = _col_from_row(info_ref[_L_W0:_L_W0 + 1, q * tp:(q + 1) * tp])
        w1 = _col_from_row(info_ref[_L_W1:_L_W1 + 1, q * tp:(q + 1) * tp])
        moe = ya[...] * w0 + yb[...] * w1
        x2 = x1_ref[rows, :] + mod_ref[0][5:6] * moe
        o_ref[rows, :] = x2 * lax.rsqrt(jnp.mean(x2 * x2, axis=-1, keepdims=True) + NORM_EPS) * nw_ref[...]

    @pl.when(i == n_phases // ns - 1)
    def _():
        for q in range(ns - 1):
            wait_set(q)


_COMBINE_SETS = 4


def _combine(d0, d1, x1, info, mod3, nfw, y, seq, tp=LANES):
    assert tp == LANES, "one phase's combine weights must fill exactly one lane-dense row"
    n, d = x1.shape
    tc = _COMBINE_SETS * tp
    per_b = seq // tc
    kern = functools.partial(_combine_kernel, tp=tp, n_phases=n // tp)
    return pl.pallas_call(
        kern,
        grid_spec=pltpu.PrefetchScalarGridSpec(
            num_scalar_prefetch=2,
            grid=(n // tc,),
            in_specs=[pl.BlockSpec((tc, d), lambda i, d0, d1: (i, 0)),
                      pl.BlockSpec((8, tc), lambda i, d0, d1: (0, i)),
                      pl.BlockSpec((1, N_MOD, d), lambda i, d0, d1: (i // per_b, 0, 0)),
                      pl.BlockSpec((1, d), lambda i, d0, d1: (0, 0)),
                      pl.BlockSpec(memory_space=pl.ANY)],
            out_specs=pl.BlockSpec((tc, d), lambda i, d0, d1: (i, 0)),
            scratch_shapes=[pltpu.VMEM((tp, d), F32)] * (2 * _COMBINE_SETS)
                           + [pltpu.SemaphoreType.DMA((_COMBINE_SETS,))]),
        out_shape=jax.ShapeDtypeStruct((n, d), F32),
        compiler_params=_cparams(("arbitrary",)),
        name="combine",
    )(d0, d1, x1, info, mod3, nfw, y)


def _split_hi_lo(w):
    hi = w.astype(BF16)
    lo = (w - hi.astype(F32)).astype(BF16)
    return hi, lo


def kernel(x, c, w_ada, b_ada, norm_mix_w, w_in, ret_norm_w, w_out, norm_ffn_w,
           w_group, b_group, w_router, b_router, w1, w3, w2, norm_final_w):
    batch, seq, d = x.shape
    n = batch * seq
    depth = w_ada.shape[0]
    assert depth == 1, "the final rmsnorm is fused into the (single) layer's last kernel"
    tm = min(INPROJ_ROWS, seq)
    half = LANES // 2

    slopes = jnp.exp2(-8.0 * jnp.arange(1, N_ATTN_HEADS + 1, dtype=F32) / N_ATTN_HEADS)
    log_gamma = jnp.log1p(-jnp.exp2(-5.0 - jnp.arange(RET_HEADS, dtype=F32)))
    n_tiles = (2 * n) // FFN_TILE + N_EXPERTS
    p_rows = n_tiles * FFN_TILE

    c_pad = jnp.zeros((8, d), F32).at[:batch].set(c)
    x2d = x.reshape(n, d)
    for l in range(depth):
        mod = _adaln(c_pad, w_ada[l], b_ada[l].reshape(1, -1))
        mod3 = mod[:batch].reshape(batch, N_MOD, d)

        proj, kmean = _inproj(x2d, mod3, norm_mix_w[l].reshape(1, d), w_in[l], seq, tm)
        kmean = kmean.reshape(batch, seq // MOBA_BLOCK, ATTN_WIDTH)
        attn = _moba(proj, kmean, slopes, batch, seq)
        ret = _retention(proj, log_gamma, ret_norm_w[l].reshape(1, RET_WIDTH), batch, seq)

        gap = _ROUTER_EXPERT_LANE - N_GROUPS
        wr = jnp.concatenate([w_group[l], jnp.zeros((d, gap), F32),
                              jnp.transpose(w_router[l], (1, 0, 2)).reshape(d, N_EXPERTS)], axis=1)
        wr = jnp.pad(wr, ((0, 0), (0, half - wr.shape[1])))
        wr_hi, wr_lo = _split_hi_lo(wr)
        wr_cat = jnp.concatenate([jnp.concatenate([wr_hi, wr_lo], axis=1),
                                  jnp.concatenate([wr_hi, jnp.zeros_like(wr_lo)], axis=1)], axis=0)
        br = jnp.concatenate([b_group[l], jnp.zeros((gap,), F32), b_router[l].reshape(-1)])
        br = jnp.pad(br, (0, LANES - br.shape[0])).reshape(1, LANES)

        x1, h2p, logits = _outproj(attn, ret, w_out[l], x2d, mod3,
                                   norm_ffn_w[l].reshape(1, d), wr_cat, br, seq, min(OUTPROJ_ROWS, seq))

        info, dest, meta = _plan(logits, n_tiles)
        d0 = dest[_L_D0]
        d1 = dest[_L_D1]
        tile_e = meta[_M_TILE_E, :n_tiles]
        n_valid = meta[_M_NVALID, :1]
        ntl = meta[_M_NTILES, :N_EXPERTS]

        row_tok = _invert(d0, d1, n_valid, ntl, meta[_M_COUNT, :N_EXPERTS],
                          meta[_M_START, :N_EXPERTS], p_rows)
        y = _ffn(tile_e, n_valid, ntl, row_tok, h2p, w1[l], w3[l], w2[l])
        nfw = norm_final_w.reshape(1, d)
        x2d = _combine(d0, d1, x1, info, mod3, nfw, y, seq)
    return x2d.reshape(batch, seq, d)
```

```python
import functools

import jax
import jax.numpy as jnp
from jax import lax
from jax.experimental import pallas as pl
from jax.experimental.pallas import tpu as pltpu

F32 = jnp.float32
BF16 = jnp.bfloat16
I32 = jnp.int32

N_ATTN_HEADS = 8
ATTN_HEAD_DIM = 128
ATTN_WIDTH = N_ATTN_HEADS * ATTN_HEAD_DIM
MOBA_BLOCK = 256
MOBA_TOPK = 3
RET_HEADS = 4
RET_HEAD_DIM = 256
RET_WIDTH = RET_HEADS * RET_HEAD_DIM
N_GROUPS = 4
EXPERTS_PER_GROUP = 8
N_EXPERTS = N_GROUPS * EXPERTS_PER_GROUP
N_MOD = 6
NORM_EPS = 1e-6

LANES = 128
RET_CHUNK = 256
FFN_TILE = 256
INPROJ_ROWS = 1024
OUTPROJ_ROWS = 256
MASK_VALUE = -1e30
LOG2E = 1.4426950408889634
MOBA_QK_SCALE = ATTN_HEAD_DIM ** -0.5 * LOG2E
_MOBA_EXTRA_ROWS = 16
VMEM_LIMIT = 56 * 1024 * 1024

_DN_LAST = (((1,), (1,)), ((), ()))
_DN_FIRST = (((0,), (0,)), ((), ()))


def _cparams(sem):
    return pltpu.CompilerParams(dimension_semantics=sem, vmem_limit_bytes=VMEM_LIMIT)


def _silu(v):
    return v * jax.nn.sigmoid(v)


def _store_token_major(ref, value):
    rows, d = value.shape
    nc = d // LANES
    for c in range(nc):
        ref[pl.ds(c, rows, stride=nc), :] = value[:, c * LANES:(c + 1) * LANES]


def _load_token_major(ref, rows, d, dtype):
    nc = d // LANES
    return jnp.concatenate([ref[pl.ds(c, rows, stride=nc), :].astype(dtype) for c in range(nc)], axis=1)


def _adaln_kernel(c_ref, w_ref, b_ref, o_ref):
    ca = _silu(c_ref[...]).astype(BF16)
    o_ref[...] = jnp.dot(ca, w_ref[...].astype(BF16), preferred_element_type=F32) + b_ref[...]


def _adaln(c_pad, w_ada, b_ada, tn=1024):
    rows, d = c_pad.shape
    n = w_ada.shape[1]
    return pl.pallas_call(
        _adaln_kernel,
        grid=(n // tn,),
        in_specs=[pl.BlockSpec((rows, d), lambda j: (0, 0)),
                  pl.BlockSpec((d, tn), lambda j: (0, j)),
                  pl.BlockSpec((1, tn), lambda j: (0, j))],
        out_specs=pl.BlockSpec((rows, tn), lambda j: (0, j)),
        out_shape=jax.ShapeDtypeStruct((rows, n), F32),
        compiler_params=_cparams(("arbitrary",)),
        name="adaln",
    )(c_pad, w_ada, b_ada)


def _inproj_kernel(x_ref, mod_ref, nw_ref, w_ref, o_ref, km_ref, h_ref, *, tm, tn):
    j = pl.program_id(1)

    @pl.when(j == 0)
    def _():
        x = x_ref[...]
        y = x * lax.rsqrt(jnp.mean(x * x, axis=-1, keepdims=True) + NORM_EPS) * nw_ref[...]
        m = mod_ref[0]
        h_ref[...] = (y * (1.0 + m[1:2]) + m[0:1]).astype(BF16)

    acc = jnp.dot(h_ref[...], w_ref[...].astype(BF16), preferred_element_type=F32)
    o_ref[...] = (acc * jnp.where(j == 0, MOBA_QK_SCALE, 1.0)).astype(BF16)

    @pl.when(j == 1)
    def _():
        km_ref[0] = jnp.sum(acc.reshape(tm // MOBA_BLOCK, MOBA_BLOCK, tn), axis=1) * (1.0 / MOBA_BLOCK)


def _inproj(x2d, mod3, nw, w_bf, seq, tm, tn=ATTN_WIDTH):
    n, d = x2d.shape
    width = w_bf.shape[1]
    per_b = seq // tm
    kern = functools.partial(_inproj_kernel, tm=tm, tn=tn)
    return pl.pallas_call(
        kern,
        grid=(n // tm, width // tn),
        in_specs=[pl.BlockSpec((tm, d), lambda i, j: (i, 0)),
                  pl.BlockSpec((1, N_MOD, d), lambda i, j: (i // per_b, 0, 0)),
                  pl.BlockSpec((1, d), lambda i, j: (0, 0)),
                  pl.BlockSpec((d, tn), lambda i, j: (0, j))],
        out_specs=[pl.BlockSpec((tm, tn), lambda i, j: (i, j)),
                   pl.BlockSpec((1, tm // MOBA_BLOCK, tn), lambda i, j: (i, 0, 0))],
        out_shape=[jax.ShapeDtypeStruct((n, width), BF16),
                   jax.ShapeDtypeStruct((n // tm, tm // MOBA_BLOCK, tn), F32)],
        scratch_shapes=[pltpu.VMEM((tm, d), BF16)],
        compiler_params=_cparams(("arbitrary", "arbitrary")),
        name="inproj",
    )(x2d, mod3, nw, w_bf)


def _moba_kernel(slopes_ref, q_ref, k_ref, v_ref, km_ref, o_ref,
                 vt_ref, sel_ref, bias_ref, acc_ref, *, nb, hps):
    hg = pl.program_id(1)
    i = pl.program_id(2)
    blk = MOBA_BLOCK
    dh = ATTN_HEAD_DIM
    log2e = LOG2E

    qpos = lax.broadcasted_iota(I32, (blk, blk), 1)
    kpos = lax.broadcasted_iota(I32, (blk, blk), 0)

    vt_rows = dh + _MOBA_EXTRA_ROWS

    @pl.when(i == 0)
    def _():
        vt = v_ref[...].astype(F32).T.astype(BF16)
        extra = lax.broadcasted_iota(I32, (_MOBA_EXTRA_ROWS, vt.shape[1]), 0)
        ones_row = jnp.where(extra == 0, 1.0, 0.0).astype(BF16)
        dist = (qpos - kpos).astype(F32)
        for g in range(hps):
            vt_ref[g * vt_rows:g * vt_rows + dh, :] = vt[g * dh:(g + 1) * dh, :]
            vt_ref[g * vt_rows + dh:(g + 1) * vt_rows, :] = ones_row
            bias_ref[g] = dist * (-log2e * slopes_ref[hg * hps + g])

    bidx = lax.broadcasted_iota(I32, (nb, blk), 0)
    past = bidx < i
    i0 = pl.multiple_of(i * blk, blk)
    heads = [slice(g * dh, (g + 1) * dh) for g in range(hps)]
    vrows = [slice(g * vt_rows, (g + 1) * vt_rows) for g in range(hps)]
    gates, raw_own = [], []
    for cols in heads:
        q = q_ref[:, cols]
        km = km_ref[0, :, cols]
        km_hi = km.astype(BF16)
        km_lo = (km - km_hi.astype(F32)).astype(BF16)
        gates.append(lax.dot_general(km_hi, q, _DN_LAST, preferred_element_type=F32)
                     + lax.dot_general(km_lo, q, _DN_LAST, preferred_element_type=F32))
        raw_own.append(lax.dot_general(k_ref[pl.ds(i0, blk), cols], q, _DN_LAST,
                                       preferred_element_type=F32))
    own_bias = [bias_ref[g] for g in range(hps)]

    carry0, acc0, sels = [], [], []
    for g, cols in enumerate(heads):
        gate = jnp.where(past, gates[g], -jnp.inf)
        rank = jnp.zeros((nb, blk), F32)
        for m in range(nb):
            gm = gate[m:m + 1, :]
            beats = jnp.where(gm > gate, 1.0, jnp.where((gm == gate) & (m < bidx), 1.0, 0.0))
            rank = rank + jnp.where(m < i, beats, 0.0)
        sels.append(jnp.where(past & (rank < MOBA_TOPK), 1.0, 0.0))

        s = jnp.where(qpos >= kpos, own_bias[g] + raw_own[g], MASK_VALUE)
        m0 = jnp.max(s, axis=0, keepdims=True)
        p = jnp.exp2(s - m0)
        acc0.append(jnp.dot(vt_ref[vrows[g], pl.ds(i0, blk)], p.astype(BF16), preferred_element_type=F32))
        carry0.append(m0)
    for g in range(hps):
        acc_ref[g] = acc0[g]
        sel_ref[g] = sels[g]

    def body(j, carry):
        j0 = pl.multiple_of(j * blk, blk)
        off = ((i - j) * blk).astype(F32) * log2e
        raw = [lax.dot_general(k_ref[pl.ds(j0, blk), cols], q_ref[:, cols], _DN_LAST,
                               preferred_element_type=F32) for cols in heads]
        chosen = [sel_ref[g, pl.ds(j, 1), :] > 0.0 for g in range(hps)]
        acc_old = [acc_ref[g] for g in range(hps)]
        out, acc_new = [], []
        for g in range(hps):
            m_run = carry[g]
            shift = -slopes_ref[hg * hps + g] * off
            s = bias_ref[g] + raw[g]
            m_new = jnp.maximum(m_run, jnp.where(chosen[g], jnp.max(s, axis=0, keepdims=True) + shift,
                                                 MASK_VALUE))
            alpha = jnp.exp2(m_run - m_new)
            p = jnp.exp2(s - jnp.where(chosen[g], m_new - shift, -MASK_VALUE))
            acc_new.append(alpha * acc_old[g] + jnp.dot(vt_ref[vrows[g], pl.ds(j0, blk)], p.astype(BF16),
                                                        preferred_element_type=F32))
            out.append(m_new)
        for g in range(hps):
            acc_ref[g] = acc_new[g]
        return tuple(out)

    lax.fori_loop(0, i, body, tuple(carry0))
    for g in range(hps):
        acc = acc_ref[g]
        o_ref[:, g * dh:(g + 1) * dh] = (acc[:dh] / acc[dh:dh + 1]).T.astype(BF16)


MOBA_HEADS_PER_STEP = 8


def _moba(proj, kmean, slopes, batch, seq):
    n = proj.shape[0]
    nb = seq // MOBA_BLOCK
    hps = MOBA_HEADS_PER_STEP
    vt_rows = ATTN_HEAD_DIM + _MOBA_EXTRA_ROWS
    w = hps * ATTN_HEAD_DIM
    ng = N_ATTN_HEADS // hps
    kern = functools.partial(_moba_kernel, nb=nb, hps=hps)
    return pl.pallas_call(
        kern,
        grid=(batch, ng, nb),
        in_specs=[pl.BlockSpec(memory_space=pltpu.SMEM),
                  pl.BlockSpec((MOBA_BLOCK, w), lambda b, h, i: (b * nb + i, h)),
                  pl.BlockSpec((seq, w), lambda b, h, i: (b, ng + h)),
                  pl.BlockSpec((seq, w), lambda b, h, i: (b, 2 * ng + h)),
                  pl.BlockSpec((1, nb, w), lambda b, h, i: (b, 0, h))],
        out_specs=pl.BlockSpec((MOBA_BLOCK, w), lambda b, h, i: (b * nb + i, h)),
        out_shape=jax.ShapeDtypeStruct((n, ATTN_WIDTH), BF16),
        scratch_shapes=[pltpu.VMEM((hps * vt_rows, seq), BF16),
                        pltpu.VMEM((hps, nb, MOBA_BLOCK), F32),
                        pltpu.VMEM((hps, MOBA_BLOCK, MOBA_BLOCK), F32),
                        pltpu.VMEM((hps, vt_rows, MOBA_BLOCK), F32)],
        compiler_params=_cparams(("arbitrary", "arbitrary", "arbitrary")),
        name="moba",
    )(slopes, proj, proj, proj, kmean)


def _ret_kernel(lg_ref, q_ref, k_ref, v_ref, g_ref, rnw_ref, o_ref,
                state_ref, dmask_ref, qdec_ref, kdec_ref, cdec_ref):
    n = pl.program_id(1)
    c = RET_CHUNK
    dk = RET_HEAD_DIM
    kscale = dk ** -0.5

    @pl.when(n == 0)
    def _():
        state_ref[...] = jnp.zeros_like(state_ref)
        ti = lax.broadcasted_iota(I32, (c, c), 0)
        si = lax.broadcasted_iota(I32, (c, c), 1)
        diff = jnp.maximum(ti - si, 0).astype(F32)
        pos = lax.broadcasted_iota(I32, (c, dk), 0).astype(F32)
        for h in range(RET_HEADS):
            lg = lg_ref[h]
            dmask_ref[h] = jnp.where(ti >= si, jnp.exp(diff * lg) * kscale, 0.0)
            qdec_ref[h] = jnp.exp((pos + 1.0) * lg)
            kdec_ref[h] = jnp.exp((c - 1.0 - pos) * lg) * kscale
            cdec_ref[h] = jnp.exp(jnp.full((1, dk), float(c), F32) * lg)

    heads = [slice(h * dk, (h + 1) * dk) for h in range(RET_HEADS)]
    qs = [q_ref[:, cols] for cols in heads]
    ks = [k_ref[:, cols] for cols in heads]
    vs = [v_ref[:, cols] for cols in heads]
    states = [state_ref[h] for h in range(RET_HEADS)]
    raw = [lax.dot_general(qs[h], ks[h], _DN_LAST, preferred_element_type=F32) for h in range(RET_HEADS)]
    cross = [jnp.dot(qs[h], states[h].astype(BF16), preferred_element_type=F32) for h in range(RET_HEADS)]
    outs, new_states = [], []
    for h, cols in enumerate(heads):
        inner = jnp.dot((raw[h] * dmask_ref[h]).astype(BF16), vs[h], preferred_element_type=F32)
        kd = (ks[h].astype(F32) * kdec_ref[h]).astype(BF16)
        kv = lax.dot_general(kd, vs[h], _DN_FIRST, preferred_element_type=F32)
        new_states.append(states[h] * cdec_ref[h] + kv)
        r = inner + cross[h] * qdec_ref[h]
        rn = r * lax.rsqrt(jnp.mean(r * r, axis=-1, keepdims=True) + NORM_EPS)
        g = g_ref[:, cols].astype(F32)
        outs.append((rn * rnw_ref[:, cols] * _silu(g)).astype(BF16))
    for h, cols in enumerate(heads):
        state_ref[h] = new_states[h]
        o_ref[:, cols] = outs[h]


def _retention(proj, log_gamma, rnw, batch, seq):
    n = proj.shape[0]
    c = RET_CHUNK
    nc = seq // c
    dk = RET_HEAD_DIM
    base = 3 * ATTN_WIDTH // RET_WIDTH

    def col(off):
        return lambda b, t: (b * nc + t, base + off)

    return pl.pallas_call(
        _ret_kernel,
        grid=(batch, nc),
        in_specs=[pl.BlockSpec(memory_space=pltpu.SMEM),
                  pl.BlockSpec((c, RET_WIDTH), col(0)),
                  pl.BlockSpec((c, RET_WIDTH), col(1)),
                  pl.BlockSpec((c, RET_WIDTH), col(2)),
                  pl.BlockSpec((c, RET_WIDTH), col(3)),
                  pl.BlockSpec((1, RET_WIDTH), lambda b, t: (0, 0))],
        out_specs=pl.BlockSpec((c, RET_WIDTH), lambda b, t: (b * nc + t, 0)),
        out_shape=jax.ShapeDtypeStruct((n, RET_WIDTH), BF16),
        scratch_shapes=[pltpu.VMEM((RET_HEADS, dk, dk), F32), pltpu.VMEM((RET_HEADS, c, c), F32),
                        pltpu.VMEM((RET_HEADS, c, dk), F32), pltpu.VMEM((RET_HEADS, c, dk), F32),
                        pltpu.VMEM((RET_HEADS, 1, dk), F32)],
        compiler_params=_cparams(("arbitrary", "arbitrary")),
        name="retention",
    )(log_gamma, proj, proj, proj, proj, rnw)


def _outproj_kernel(a_ref, r_ref, w_ref, x_ref, mod_ref, nw_ref, wr_ref, br_ref,
                    x1_ref, hp_ref, lg_ref, wb_ref):
    @pl.when(pl.program_id(0) == 0)
    def _():
        wb_ref[...] = w_ref[...].astype(BF16)

    mix = jnp.dot(jnp.concatenate([a_ref[...], r_ref[...]], axis=1), wb_ref[...],
                  preferred_element_type=F32)
    m = mod_ref[0]
    x1 = x_ref[...] + m[2:3] * mix
    x1_ref[...] = x1
    y = x1 * lax.rsqrt(jnp.mean(x1 * x1, axis=-1, keepdims=True) + NORM_EPS) * nw_ref[...]
    h2 = y * (1.0 + m[4:5]) + m[3:4]
    _store_token_major(hp_ref, h2)
    hi = h2.astype(BF16)
    lo = (h2 - hi.astype(F32)).astype(BF16)
    part = jnp.dot(jnp.concatenate([hi, lo], axis=1), wr_ref[...], preferred_element_type=F32)
    lg_ref[...] = part + pltpu.roll(part, LANES // 2, axis=1) + br_ref[...]


def _outproj(attn, ret, w_bf, x2d, mod3, nw, wr_cat, br, seq, tm):
    n, d = x2d.shape
    per_b = seq // tm
    return pl.pallas_call(
        _outproj_kernel,
        grid=(n // tm,),
        in_specs=[pl.BlockSpec((tm, ATTN_WIDTH), lambda i: (i, 0)),
                  pl.BlockSpec((tm, RET_WIDTH), lambda i: (i, 0)),
                  pl.BlockSpec((d, d), lambda i: (0, 0), pipeline_mode=pl.Buffered(1)),
                  pl.BlockSpec((tm, d), lambda i: (i, 0)),
                  pl.BlockSpec((1, N_MOD, d), lambda i: (i // per_b, 0, 0)),
                  pl.BlockSpec((1, d), lambda i: (0, 0)),
                  pl.BlockSpec((2 * d, LANES), lambda i: (0, 0)),
                  pl.BlockSpec((1, LANES), lambda i: (0, 0))],
        out_specs=[pl.BlockSpec((tm, d), lambda i: (i, 0)),
                   pl.BlockSpec((tm * (d // LANES), LANES), lambda i: (i, 0)),
                   pl.BlockSpec((tm, LANES), lambda i: (i, 0))],
        out_shape=[jax.ShapeDtypeStruct((n, d), F32),
                   jax.ShapeDtypeStruct((n * (d // LANES), LANES), F32),
                   jax.ShapeDtypeStruct((n, LANES), F32)],
        scratch_shapes=[pltpu.VMEM((d, d), BF16)],
        compiler_params=_cparams(("arbitrary",)),
        name="outproj",
    )(attn, ret, w_bf, x2d, mod3, nw, wr_cat, br)


_L_E0, _L_E1, _L_W0, _L_W1, _L_D0, _L_D1 = 0, 1, 2, 3, 4, 5
_ROUTER_EXPERT_LANE = 8
_M_TILE_E, _M_NVALID, _M_NTILES, _M_COUNT, _M_START = 0, 1, 2, 3, 4
_PLAN_ROWS = 256
_PLAN_GROUP = 4


def _first_row_where(cond, rows, limit):
    return jnp.min(jnp.where(cond, rows, limit), axis=0, keepdims=True)


def _col_from_row(v):
    r = lax.broadcasted_iota(I32, (LANES, LANES), 0)
    c = lax.broadcasted_iota(I32, (LANES, LANES), 1)
    return jnp.sum(jnp.where(r == c, v, 0.0), axis=1, keepdims=True)


def _row_from_col(v):
    r = lax.broadcasted_iota(I32, (LANES, LANES), 0)
    c = lax.broadcasted_iota(I32, (LANES, LANES), 1)
    return jnp.sum(jnp.where(r == c, v, 0.0), axis=0, keepdims=True)


def _rows_to_tile(rows, row8):
    tile = 0.0
    for k in reversed(range(len(rows))):
        tile = jnp.where(row8 == k, rows[k], tile)
    return tile


def _plan_kernel(lg_ref, info_ref, dest_ref, meta_ref, *, n_tok):
    tb = _PLAN_ROWS
    n_g, n_e = N_GROUPS, EXPERTS_PER_GROUP
    row8 = lax.broadcasted_iota(I32, (8, tb), 0).astype(F32)
    row_e = lax.broadcasted_iota(I32, (N_EXPERTS, tb), 0).astype(F32)
    earlier = lax.broadcasted_iota(I32, (tb, tb), 0) < lax.broadcasted_iota(I32, (tb, tb), 1)
    tri = jnp.where(earlier, 1.0, 0.0).astype(BF16)

    def route_block(r0):
        lt = lg_ref[pl.ds(r0, tb), :].T
        gl = jnp.where(row8 < n_g, lt[0:8, :], -jnp.inf)
        ge = jnp.exp(gl - jnp.max(gl, axis=0, keepdims=True))
        gp = ge / jnp.sum(ge, axis=0, keepdims=True)
        g_w = jnp.max(gp, axis=0, keepdims=True)
        g_sel = _first_row_where((gp == g_w) & (row8 < n_g), row8, 8.0)
        el = lt[_ROUTER_EXPERT_LANE:_ROUTER_EXPERT_LANE + n_e, :]
        for g in range(1, n_g):
            lo = _ROUTER_EXPERT_LANE + g * n_e
            el = jnp.where(g_sel == g, lt[lo:lo + n_e, :], el)
        ee = jnp.exp(el - jnp.max(el, axis=0, keepdims=True))
        ep = ee / jnp.sum(ee, axis=0, keepdims=True)
        p1 = jnp.max(ep, axis=0, keepdims=True)
        i1 = _first_row_where(ep == p1, row8, 8.0)
        ep2 = jnp.where(row8 == i1, -1.0, ep)
        p2 = jnp.max(ep2, axis=0, keepdims=True)
        i2 = _first_row_where(ep2 == p2, row8, 8.0)
        denom = p1 + p2
        w0 = g_w * p1 / denom
        w1 = g_w * p2 / denom
        e0 = g_sel * n_e + i1
        e1 = g_sel * n_e + i2
        oh0 = jnp.where(row_e == e0, 1.0, 0.0)
        oh1 = jnp.where(row_e == e1, 1.0, 0.0)
        oh = oh0 + oh1
        local = jnp.dot(oh.astype(BF16), tri, preferred_element_type=F32)
        return (e0, e1, w0, w1), oh0, oh1, local, jnp.sum(oh, axis=1, keepdims=True)

    group = _PLAN_GROUP if (n_tok // tb) % _PLAN_GROUP == 0 else 1

    def route(it, carry):
        starts = [pl.multiple_of((it * group + k) * tb, tb) for k in range(group)]
        blocks = [route_block(r0) for r0 in starts]
        tiles = []
        for (e0, e1, w0, w1), oh0, oh1, local, total in blocks:
            before = local + carry
            rank0 = jnp.sum(oh0 * before, axis=0, keepdims=True)
            rank1 = jnp.sum(oh1 * before, axis=0, keepdims=True)
            tiles.append(_rows_to_tile([e0, e1, w0, w1, rank0, rank1], row8))
            carry = carry + total
        for r0, tile in zip(starts, tiles):
            info_ref[:, pl.ds(r0, tb)] = tile
        return carry

    counts_col = lax.fori_loop(0, n_tok // (tb * group), route, jnp.zeros((N_EXPERTS, 1), F32))
    counts = _row_from_col(jnp.concatenate([counts_col, jnp.zeros((LANES - N_EXPERTS, 1), F32)], axis=0))

    lane1 = lax.broadcasted_iota(I32, (1, LANES), 1)
    padded = jnp.floor((counts + (FFN_TILE - 1.0)) * (1.0 / FFN_TILE)) * FFN_TILE
    pad_end = padded
    sh = 1
    while sh < N_EXPERTS:
        pad_end = pad_end + jnp.where(lane1 >= sh, pltpu.roll(pad_end, sh, axis=1), 0.0)
        sh *= 2
    pad_start = pad_end - padded

    start_col = _col_from_row(pad_start)[0:N_EXPERTS, :]

    def place(it, _):
        starts = [pl.multiple_of((it * group + k) * tb, tb) for k in range(group)]
        tiles = [info_ref[:, pl.ds(r0, tb)] for r0 in starts]
        placed = []
        for tile in tiles:
            s0 = jnp.sum(jnp.where(row_e == tile[_L_E0:_L_E0 + 1, :], start_col, 0.0), axis=0, keepdims=True)
            s1 = jnp.sum(jnp.where(row_e == tile[_L_E1:_L_E1 + 1, :], start_col, 0.0), axis=0, keepdims=True)
            placed.append(tile + jnp.where(row8 == _L_D0, s0, jnp.where(row8 == _L_D1, s1, 0.0)))
        for r0, tile in zip(starts, placed):
            info_ref[:, pl.ds(r0, tb)] = tile
            dest_ref[:, pl.ds(r0, tb)] = tile.astype(I32)
        return 0

    lax.fori_loop(0, n_tok // (tb * group), place, 0)

    pe_col = _col_from_row(pad_end)
    mt = meta_ref.shape[1]
    t_start = lax.broadcasted_iota(I32, (LANES, mt), 1).astype(F32) * FFN_TILE
    e_row = lax.broadcasted_iota(I32, (LANES, mt), 0)
    ended = jnp.where((pe_col <= t_start) & (e_row < N_EXPERTS), 1.0, 0.0)
    tile_e = jnp.minimum(jnp.sum(ended, axis=0, keepdims=True), N_EXPERTS - 1.0)
    total = jnp.sum(jnp.where(lane1 == N_EXPERTS - 1, pad_end, 0.0), axis=1, keepdims=True)
    n_valid = jnp.broadcast_to(total * (1.0 / FFN_TILE), (1, mt))
    def per_expert(v):
        return v if mt == LANES else jnp.concatenate([v, jnp.zeros((1, mt - LANES), F32)], axis=1)

    mrow = lax.broadcasted_iota(I32, (8, mt), 0)
    meta = jnp.where(mrow == _M_TILE_E, tile_e,
           jnp.where(mrow == _M_NVALID, n_valid,
           jnp.where(mrow == _M_NTILES, per_expert(padded * (1.0 / FFN_TILE)),
           jnp.where(mrow == _M_COUNT, per_expert(counts),
           jnp.where(mrow == _M_START, per_expert(pad_start), 0.0)))))
    meta_ref[...] = meta.astype(I32)


def _plan(logits, n_tiles):
    n_tok = logits.shape[0]
    mt = -(-n_tiles // LANES) * LANES
    kern = functools.partial(_plan_kernel, n_tok=n_tok)
    return pl.pallas_call(
        kern,
        out_shape=[jax.ShapeDtypeStruct((8, n_tok), F32),
                   jax.ShapeDtypeStruct((8, n_tok), I32),
                   jax.ShapeDtypeStruct((8, mt), I32)],
        compiler_params=pltpu.CompilerParams(vmem_limit_bytes=VMEM_LIMIT),
        name="plan",
    )(logits)


_DMA_UNROLL = 8


def _invert_kernel(d0_ref, d1_ref, nv_ref, ntl_ref, cnt_ref, start_ref, tok_ref, *, n_tok, p_rows):
    def zero_rows(lo, hi):
        def zbody(u, _):
            for s in range(_DMA_UNROLL):
                tok_ref[jnp.minimum(lo + u * _DMA_UNROLL + s, hi - 1)] = 0
            return 0
        lax.fori_loop(0, (hi - lo + _DMA_UNROLL - 1) // _DMA_UNROLL, zbody, 0)

    def pad_body(e, _):
        zero_rows(start_ref[e] + cnt_ref[e], start_ref[e] + ntl_ref[e] * FFN_TILE)
        return 0

    lax.fori_loop(0, N_EXPERTS, pad_body, 0)
    zero_rows(nv_ref[0] * FFN_TILE, p_rows)

    def body(u, _):
        t0 = u * _DMA_UNROLL
        rows = [(d0_ref[t0 + s], d1_ref[t0 + s]) for s in range(_DMA_UNROLL)]
        for s, (r0, r1) in enumerate(rows):
            tok_ref[r0] = t0 + s
            tok_ref[r1] = t0 + s
        return 0

    lax.fori_loop(0, n_tok // _DMA_UNROLL, body, 0)


def _invert(d0, d1, n_valid, ntl, cnt, start, p_rows):
    n_tok = d0.shape[0]
    kern = functools.partial(_invert_kernel, n_tok=n_tok, p_rows=p_rows)
    return pl.pallas_call(
        kern,
        grid_spec=pltpu.PrefetchScalarGridSpec(
            num_scalar_prefetch=6,
            grid=(1,),
            in_specs=[],
            out_specs=pl.BlockSpec(memory_space=pltpu.SMEM)),
        out_shape=jax.ShapeDtypeStruct((p_rows,), I32),
        compiler_params=_cparams(("arbitrary",)),
        name="invert",
    )(d0, d1, n_valid, ntl, cnt, start)


_FFN_SLOTS = 3
_BULK_DMA_PRIORITY = 1


def _ffn_kernel(te_ref, nv_ref, ntl_ref, tok_ref, h_ref, w1_ref, w3_ref, w2_ref, y_ref,
                xbuf, xb_ref, w1f, w3f, w2f, w1b, w3b, w2b, gsem, wsem, run_ref, *, n_tiles):
    i = pl.program_id(0)
    nv = nv_ref[0]
    valid = i < nv
    slot = i % _FFN_SLOTS
    nc = xb_ref.shape[1] // LANES

    def row_copy(tile, r, dst_slot):
        tok = tok_ref[tile * FFN_TILE + r]
        src = pl.ds(pl.multiple_of(tok * nc, nc), nc)
        return pltpu.make_async_copy(h_ref.at[src], xbuf.at[dst_slot, pl.ds(r * nc, nc)], gsem.at[dst_slot])

    def weight_copies(e, ws):
        return (pltpu.make_async_copy(w1_ref.at[e], w1f.at[ws], wsem.at[ws]),
                pltpu.make_async_copy(w3_ref.at[e], w3f.at[ws], wsem.at[ws]),
                pltpu.make_async_copy(w2_ref.at[e], w2f.at[ws], wsem.at[ws]))

    @pl.when(i == 0)
    def _():
        run_ref[0] = 0
        for cp in weight_copies(te_ref[0], 0):
            cp.start(priority=_BULK_DMA_PRIORITY)

        def first_rows(u, _):
            for s in range(_DMA_UNROLL):
                for t in range(_FFN_SLOTS - 1):
                    row_copy(t, u * _DMA_UNROLL + s, t).start()
            return 0
        lax.fori_loop(0, FFN_TILE // _DMA_UNROLL, first_rows, 0)

    @pl.when(valid & ((i == 0) | (te_ref[i] != te_ref[jnp.maximum(i, 1) - 1])))
    def _():
        run = run_ref[0]
        ws = run % 2
        for cp in weight_copies(te_ref[i], ws):
            cp.wait()
        w1b[...] = w1f[ws].astype(BF16)
        w3b[...] = w3f[ws].astype(BF16)
        w2b[...] = w2f[ws].astype(BF16)
        nxt = i + ntl_ref[te_ref[i]]

        @pl.when(nxt < nv)
        def _():
            for cp in weight_copies(te_ref[jnp.minimum(nxt, n_tiles - 1)], 1 - ws):
                cp.start(priority=_BULK_DMA_PRIORITY)
        run_ref[0] = run + 1

    @pl.when(valid)
    def _():
        pltpu.make_async_copy(h_ref.at[pl.ds(0, FFN_TILE * nc)], xbuf.at[slot], gsem.at[slot]).wait()
        xb_ref[...] = _load_token_major(xbuf.at[slot], FFN_TILE, xb_ref.shape[1], BF16)
        ahead = _FFN_SLOTS - 1
        nxt_tile = jnp.minimum(i + ahead, n_tiles - 1)
        nxt_slot = (i + ahead) % _FFN_SLOTS
        for r in range(FFN_TILE):
            row_copy(nxt_tile, r, nxt_slot).start()
        x = xb_ref[...]
        a = jnp.dot(x, w1b[...], preferred_element_type=F32)
        b = jnp.dot(x, w3b[...], preferred_element_type=F32)
        hmid = (_silu(a) * b).astype(BF16)
        y_ref[...] = jnp.dot(hmid, w2b[...], preferred_element_type=F32)

    @pl.when(jnp.logical_not(valid))
    def _():
        y_ref[...] = jnp.zeros_like(y_ref)

    @pl.when(i == n_tiles - 1)
    def _():
        for t in range(_FFN_SLOTS - 1):
            s = (nv + t) % _FFN_SLOTS
            pltpu.make_async_copy(h_ref.at[pl.ds(0, FFN_TILE * nc)], xbuf.at[s], gsem.at[s]).wait()


def _ffn(tile_e, n_valid, ntl, row_tok, h2, w1, w3, w2):
    p_rows = row_tok.shape[0]
    _, d, de = w1.shape
    n_tiles = p_rows // FFN_TILE
    kern = functools.partial(_ffn_kernel, n_tiles=n_tiles)
    any_spec = pl.BlockSpec(memory_space=pl.ANY)
    return pl.pallas_call(
        kern,
        grid_spec=pltpu.PrefetchScalarGridSpec(
            num_scalar_prefetch=4,
            grid=(n_tiles,),
            in_specs=[any_spec, any_spec, any_spec, any_spec],
            out_specs=pl.BlockSpec((FFN_TILE, d), lambda i, *_: (i, 0)),
            scratch_shapes=[pltpu.VMEM((_FFN_SLOTS, FFN_TILE * (d // LANES), LANES), F32),
                            pltpu.VMEM((FFN_TILE, d), BF16),
                            pltpu.VMEM((2, d, de), F32), pltpu.VMEM((2, d, de), F32),
                            pltpu.VMEM((2, de, d), F32),
                            pltpu.VMEM((d, de), BF16), pltpu.VMEM((d, de), BF16),
                            pltpu.VMEM((de, d), BF16),
                            pltpu.SemaphoreType.DMA((_FFN_SLOTS,)), pltpu.SemaphoreType.DMA((2,)),
                            pltpu.SMEM((1,), I32)]),
        out_shape=jax.ShapeDtypeStruct((p_rows, d), F32),
        compiler_params=_cparams(("arbitrary",)),
        name="ffn",
    )(tile_e, n_valid, ntl, row_tok, h2, w1, w3, w2)


def _combine_kernel(d0_ref, d1_ref, x1_ref, info_ref, mod_ref, nw_ref, y_ref, o_ref,
                    *scratch, tp, n_phases):
    i = pl.program_id(0)
    ns = _COMBINE_SETS
    sem = scratch[-1]
    bufs = [(scratch[2 * q], scratch[2 * q + 1]) for q in range(ns)]

    def row_copies(phase, r, q):
        tok = phase * tp + r
        ya, yb = bufs[q]
        return (pltpu.make_async_copy(y_ref.at[pl.ds(d0_ref[tok], 1)], ya.at[pl.ds(r, 1)], sem.at[q]),
                pltpu.make_async_copy(y_ref.at[pl.ds(d1_ref[tok], 1)], yb.at[pl.ds(r, 1)], sem.at[q]))

    def wait_set(q):
        for buf in bufs[q]:
            pltpu.make_async_copy(y_ref.at[pl.ds(0, tp)], buf, sem.at[q]).wait()

    @pl.when(i == 0)
    def _():
        def first_rows(u, _):
            for s in range(_DMA_UNROLL):
                for q in range(ns - 1):
                    for cp in row_copies(q, u * _DMA_UNROLL + s, q):
                        cp.start()
            return 0
        lax.fori_loop(0, tp // _DMA_UNROLL, first_rows, 0)

    for q in range(ns):
        wait_set(q)
        ahead = jnp.minimum(i * ns + q + ns - 1, n_phases - 1)
        for r in range(tp):
            for prio, cp in enumerate(row_copies(ahead, r, (q + ns - 1) % ns)):
                cp.start(priority=prio)
        ya, yb = bufs[q]
        rows = pl.ds(q * tp, tp)
        w0 = _col_from_row(info_ref[_L_W0:_L_W0 + 1, q * tp:(q + 1) * tp])
        w1 = _col_from_row(info_ref[_L_W1:_L_W1 + 1, q * tp:(q + 1) * tp])
        moe = ya[...] * w0 + yb[...] * w1
        x2 = x1_ref[rows, :] + mod_ref[0][5:6] * moe
        o_ref[rows, :] = x2 * lax.rsqrt(jnp.mean(x2 * x2, axis=-1, keepdims=True) + NORM_EPS) * nw_ref[...]

    @pl.when(i == n_phases // ns - 1)
    def _():
        for q in range(ns - 1):
            wait_set(q)


_COMBINE_SETS = 4


def _combine(d0, d1, x1, info, mod3, nfw, y, seq, tp=LANES):
    assert tp == LANES, "one phase's combine weights must fill exactly one lane-dense row"
    n, d = x1.shape
    tc = _COMBINE_SETS * tp
    per_b = seq // tc
    kern = functools.partial(_combine_kernel, tp=tp, n_phases=n // tp)
    return pl.pallas_call(
        kern,
        grid_spec=pltpu.PrefetchScalarGridSpec(
            num_scalar_prefetch=2,
            grid=(n // tc,),
            in_specs=[pl.BlockSpec((tc, d), lambda i, d0, d1: (i, 0)),
                      pl.BlockSpec((8, tc), lambda i, d0, d1: (0, i)),
                      pl.BlockSpec((1, N_MOD, d), lambda i, d0, d1: (i // per_b, 0, 0)),
                      pl.BlockSpec((1, d), lambda i, d0, d1: (0, 0)),
                      pl.BlockSpec(memory_space=pl.ANY)],
            out_specs=pl.BlockSpec((tc, d), lambda i, d0, d1: (i, 0)),
            scratch_shapes=[pltpu.VMEM((tp, d), F32)] * (2 * _COMBINE_SETS)
                           + [pltpu.SemaphoreType.DMA((_COMBINE_SETS,))]),
        out_shape=jax.ShapeDtypeStruct((n, d), F32),
        compiler_params=_cparams(("arbitrary",)),
        name="combine",
    )(d0, d1, x1, info, mod3, nfw, y)


def _split_hi_lo(w):
    hi = w.astype(BF16)
    lo = (w - hi.astype(F32)).astype(BF16)
    return hi, lo


def kernel(x, c, w_ada, b_ada, norm_mix_w, w_in, ret_norm_w, w_out, norm_ffn_w,
           w_group, b_group, w_router, b_router, w1, w3, w2, norm_final_w):
    batch, seq, d = x.shape
    n = batch * seq
    depth = w_ada.shape[0]
    assert depth == 1, "the final rmsnorm is fused into the (single) layer's last kernel"
    tm = min(INPROJ_ROWS, seq)
    half = LANES // 2

    slopes = jnp.exp2(-8.0 * jnp.arange(1, N_ATTN_HEADS + 1, dtype=F32) / N_ATTN_HEADS)
    log_gamma = jnp.log1p(-jnp.exp2(-5.0 - jnp.arange(RET_HEADS, dtype=F32)))
    n_tiles = (2 * n) // FFN_TILE + N_EXPERTS
    p_rows = n_tiles * FFN_TILE

    c_pad = jnp.zeros((8, d), F32).at[:batch].set(c)
    x2d = x.reshape(n, d)
    for l in range(depth):
        mod = _adaln(c_pad, w_ada[l], b_ada[l].reshape(1, -1))
        mod3 = mod[:batch].reshape(batch, N_MOD, d)

        proj, kmean = _inproj(x2d, mod3, norm_mix_w[l].reshape(1, d), w_in[l], seq, tm)
        kmean = kmean.reshape(batch, seq // MOBA_BLOCK, ATTN_WIDTH)
        attn = _moba(proj, kmean, slopes, batch, seq)
        ret = _retention(proj, log_gamma, ret_norm_w[l].reshape(1, RET_WIDTH), batch, seq)

        gap = _ROUTER_EXPERT_LANE - N_GROUPS
        wr = jnp.concatenate([w_group[l], jnp.zeros((d, gap), F32),
                              jnp.transpose(w_router[l], (1, 0, 2)).reshape(d, N_EXPERTS)], axis=1)
        wr = jnp.pad(wr, ((0, 0), (0, half - wr.shape[1])))
        wr_hi, wr_lo = _split_hi_lo(wr)
        wr_cat = jnp.concatenate([jnp.concatenate([wr_hi, wr_lo], axis=1),
                                  jnp.concatenate([wr_hi, jnp.zeros_like(wr_lo)], axis=1)], axis=0)
        br = jnp.concatenate([b_group[l], jnp.zeros((gap,), F32), b_router[l].reshape(-1)])
        br = jnp.pad(br, (0, LANES - br.shape[0])).reshape(1, LANES)

        x1, h2p, logits = _outproj(attn, ret, w_out[l], x2d, mod3,
                                   norm_ffn_w[l].reshape(1, d), wr_cat, br, seq, min(OUTPROJ_ROWS, seq))

        info, dest, meta = _plan(logits, n_tiles)
        d0 = dest[_L_D0]
        d1 = dest[_L_D1]
        tile_e = meta[_M_TILE_E, :n_tiles]
        n_valid = meta[_M_NVALID, :1]
        ntl = meta[_M_NTILES, :N_EXPERTS]

        row_tok = _invert(d0, d1, n_valid, ntl, meta[_M_COUNT, :N_EXPERTS],
                          meta[_M_START, :N_EXPERTS], p_rows)
        y = _ffn(tile_e, n_valid, ntl, row_tok, h2p, w1[l], w3[l], w2[l])
        nfw = norm_final_w.reshape(1, d)
        x2d = _combine(d0, d1, x1, info, mod3, nfw, y, seq)
    return x2d.reshape(batch, seq, d)
```

```python
import functools

import jax
import jax.numpy as jnp
from jax import lax
from jax.experimental import pallas as pl
from jax.experimental.pallas import tpu as pltpu

F32 = jnp.float32
BF16 = jnp.bfloat16
I32 = jnp.int32

N_ATTN_HEADS = 8
ATTN_HEAD_DIM = 128
ATTN_WIDTH = N_ATTN_HEADS * ATTN_HEAD_DIM
MOBA_BLOCK = 256
MOBA_TOPK = 3
RET_HEADS = 4
RET_HEAD_DIM = 256
RET_WIDTH = RET_HEADS * RET_HEAD_DIM
N_GROUPS = 4
EXPERTS_PER_GROUP = 8
N_EXPERTS = N_GROUPS * EXPERTS_PER_GROUP
N_MOD = 6
NORM_EPS = 1e-6

LANES = 128
SUBLANES = 8
RET_CHUNK = 256
FFN_TILE = 256
INPROJ_ROWS = 1024
OUTPROJ_ROWS = 256
MASK_VALUE = -1e30
LOG2E = 1.4426950408889634
MOBA_QK_SCALE = ATTN_HEAD_DIM ** -0.5 * LOG2E
_MOBA_EXTRA_ROWS = 16
VMEM_LIMIT = 56 * 1024 * 1024

_DN_LAST = (((1,), (1,)), ((), ()))
_DN_FIRST = (((0,), (0,)), ((), ()))


def _cparams(sem):
    return pltpu.CompilerParams(dimension_semantics=sem, vmem_limit_bytes=VMEM_LIMIT)


def _silu(v):
    return v * jax.nn.sigmoid(v)


def _store_token_major(ref, value):
    rows, d = value.shape
    nc = d // LANES
    for c in range(nc):
        ref[pl.ds(c, rows, stride=nc), :] = value[:, c * LANES:(c + 1) * LANES]


def _load_token_major(ref, rows, d, dtype):
    nc = d // LANES
    return jnp.concatenate([ref[pl.ds(c, rows, stride=nc), :].astype(dtype) for c in range(nc)], axis=1)


def _adaln_kernel(c_ref, w_ref, b_ref, o_ref):
    ca = _silu(c_ref[...]).astype(BF16)
    o_ref[...] = jnp.dot(ca, w_ref[...].astype(BF16), preferred_element_type=F32) + b_ref[...]


def _adaln(c_pad, w_ada, b_ada, tn=1024):
    rows, d = c_pad.shape
    n = w_ada.shape[1]
    return pl.pallas_call(
        _adaln_kernel,
        grid=(n // tn,),
        in_specs=[pl.BlockSpec((rows, d), lambda j: (0, 0)),
                  pl.BlockSpec((d, tn), lambda j: (0, j)),
                  pl.BlockSpec((1, tn), lambda j: (0, j))],
        out_specs=pl.BlockSpec((rows, tn), lambda j: (0, j)),
        out_shape=jax.ShapeDtypeStruct((rows, n), F32),
        compiler_params=_cparams(("arbitrary",)),
        name="adaln",
    )(c_pad, w_ada, b_ada)


def _inproj_kernel(x_ref, mod_ref, nw_ref, w_ref, o_ref, km_ref, h_ref, *, tm, tn):
    j = pl.program_id(1)

    @pl.when(j == 0)
    def _():
        x = x_ref[...]
        y = x * lax.rsqrt(jnp.mean(x * x, axis=-1, keepdims=True) + NORM_EPS) * nw_ref[...]
        m = mod_ref[0]
        h_ref[...] = (y * (1.0 + m[1:2]) + m[0:1]).astype(BF16)

    acc = jnp.dot(h_ref[...], w_ref[...].astype(BF16), preferred_element_type=F32)
    o_ref[...] = (acc * jnp.where(j == 0, MOBA_QK_SCALE, 1.0)).astype(BF16)

    @pl.when(j == 1)
    def _():
        km_ref[0] = jnp.sum(acc.reshape(tm // MOBA_BLOCK, MOBA_BLOCK, tn), axis=1) * (1.0 / MOBA_BLOCK)


def _inproj(x2d, mod3, nw, w_bf, seq, tm, tn=ATTN_WIDTH):
    n, d = x2d.shape
    width = w_bf.shape[1]
    per_b = seq // tm
    kern = functools.partial(_inproj_kernel, tm=tm, tn=tn)
    return pl.pallas_call(
        kern,
        grid=(n // tm, width // tn),
        in_specs=[pl.BlockSpec((tm, d), lambda i, j: (i, 0)),
                  pl.BlockSpec((1, N_MOD, d), lambda i, j: (i // per_b, 0, 0)),
                  pl.BlockSpec((1, d), lambda i, j: (0, 0)),
                  pl.BlockSpec((d, tn), lambda i, j: (0, j))],
        out_specs=[pl.BlockSpec((tm, tn), lambda i, j: (i, j)),
                   pl.BlockSpec((1, tm // MOBA_BLOCK, tn), lambda i, j: (i, 0, 0))],
        out_shape=[jax.ShapeDtypeStruct((n, width), BF16),
                   jax.ShapeDtypeStruct((n // tm, tm // MOBA_BLOCK, tn), F32)],
        scratch_shapes=[pltpu.VMEM((tm, d), BF16)],
        compiler_params=_cparams(("arbitrary", "arbitrary")),
        name="inproj",
    )(x2d, mod3, nw, w_bf)


def _moba_kernel(slopes_ref, q_ref, k_ref, v_ref, km_ref, o_ref,
                 vt_ref, sel_ref, bias_ref, acc_ref, *, nb, hps):
    hg = pl.program_id(1)
    i = pl.program_id(2)
    blk = MOBA_BLOCK
    dh = ATTN_HEAD_DIM
    log2e = LOG2E

    qpos = lax.broadcasted_iota(I32, (blk, blk), 1)
    kpos = lax.broadcasted_iota(I32, (blk, blk), 0)

    vt_rows = dh + _MOBA_EXTRA_ROWS

    @pl.when(i == 0)
    def _():
        vt = v_ref[...].astype(F32).T.astype(BF16)
        extra = lax.broadcasted_iota(I32, (_MOBA_EXTRA_ROWS, vt.shape[1]), 0)
        ones_row = jnp.where(extra == 0, 1.0, 0.0).astype(BF16)
        dist = (qpos - kpos).astype(F32)
        for g in range(hps):
            vt_ref[g * vt_rows:g * vt_rows + dh, :] = vt[g * dh:(g + 1) * dh, :]
            vt_ref[g * vt_rows + dh:(g + 1) * vt_rows, :] = ones_row
            bias_ref[g] = dist * (-log2e * slopes_ref[hg * hps + g])

    bidx = lax.broadcasted_iota(I32, (nb, blk), 0)
    past = bidx < i
    i0 = pl.multiple_of(i * blk, blk)
    heads = [slice(g * dh, (g + 1) * dh) for g in range(hps)]
    vrows = [slice(g * vt_rows, (g + 1) * vt_rows) for g in range(hps)]
    gates, raw_own = [], []
    for cols in heads:
        q = q_ref[:, cols]
        km = km_ref[0, :, cols]
        km_hi = km.astype(BF16)
        km_lo = (km - km_hi.astype(F32)).astype(BF16)
        gates.append(lax.dot_general(km_hi, q, _DN_LAST, preferred_element_type=F32)
                     + lax.dot_general(km_lo, q, _DN_LAST, preferred_element_type=F32))
        raw_own.append(lax.dot_general(k_ref[pl.ds(i0, blk), cols], q, _DN_LAST,
                                       preferred_element_type=F32))
    own_bias = [bias_ref[g] for g in range(hps)]

    carry0, acc0, sels = [], [], []
    for g, cols in enumerate(heads):
        gate = jnp.where(past, gates[g], -jnp.inf)
        rank = jnp.zeros((nb, blk), F32)
        for m in range(nb):
            gm = gate[m:m + 1, :]
            beats = jnp.where(gm > gate, 1.0, jnp.where((gm == gate) & (m < bidx), 1.0, 0.0))
            rank = rank + jnp.where(m < i, beats, 0.0)
        sels.append(jnp.where(past & (rank < MOBA_TOPK), 1.0, 0.0))

        s = jnp.where(qpos >= kpos, own_bias[g] + raw_own[g], MASK_VALUE)
        m0 = jnp.max(s, axis=0, keepdims=True)
        p = jnp.exp2(s - m0)
        acc0.append(jnp.dot(vt_ref[vrows[g], pl.ds(i0, blk)], p.astype(BF16), preferred_element_type=F32))
        carry0.append(m0)
    for g in range(hps):
        acc_ref[g] = acc0[g]
        sel_ref[g] = sels[g]

    def body(j, carry):
        j0 = pl.multiple_of(j * blk, blk)
        off = ((i - j) * blk).astype(F32) * log2e
        raw = [lax.dot_general(k_ref[pl.ds(j0, blk), cols], q_ref[:, cols], _DN_LAST,
                               preferred_element_type=F32) for cols in heads]
        chosen = [sel_ref[g, pl.ds(j, 1), :] > 0.0 for g in range(hps)]
        acc_old = [acc_ref[g] for g in range(hps)]
        out, acc_new = [], []
        for g in range(hps):
            m_run = carry[g]
            shift = -slopes_ref[hg * hps + g] * off
            s = bias_ref[g] + raw[g]
            m_new = jnp.maximum(m_run, jnp.where(chosen[g], jnp.max(s, axis=0, keepdims=True) + shift,
                                                 MASK_VALUE))
            alpha = jnp.exp2(m_run - m_new)
            p = jnp.exp2(s - jnp.where(chosen[g], m_new - shift, -MASK_VALUE))
            acc_new.append(alpha * acc_old[g] + jnp.dot(vt_ref[vrows[g], pl.ds(j0, blk)], p.astype(BF16),
                                                        preferred_element_type=F32))
            out.append(m_new)
        for g in range(hps):
            acc_ref[g] = acc_new[g]
        return tuple(out)

    lax.fori_loop(0, i, body, tuple(carry0))
    for g in range(hps):
        acc = acc_ref[g]
        o_ref[:, g * dh:(g + 1) * dh] = (acc[:dh] / acc[dh:dh + 1]).T.astype(BF16)


MOBA_HEADS_PER_STEP = 8


def _moba(proj, kmean, slopes, batch, seq):
    n = proj.shape[0]
    nb = seq // MOBA_BLOCK
    hps = MOBA_HEADS_PER_STEP
    vt_rows = ATTN_HEAD_DIM + _MOBA_EXTRA_ROWS
    w = hps * ATTN_HEAD_DIM
    ng = N_ATTN_HEADS // hps
    kern = functools.partial(_moba_kernel, nb=nb, hps=hps)
    return pl.pallas_call(
        kern,
        grid=(batch, ng, nb),
        in_specs=[pl.BlockSpec(memory_space=pltpu.SMEM),
                  pl.BlockSpec((MOBA_BLOCK, w), lambda b, h, i: (b * nb + i, h)),
                  pl.BlockSpec((seq, w), lambda b, h, i: (b, ng + h)),
                  pl.BlockSpec((seq, w), lambda b, h, i: (b, 2 * ng + h)),
                  pl.BlockSpec((1, nb, w), lambda b, h, i: (b, 0, h))],
        out_specs=pl.BlockSpec((MOBA_BLOCK, w), lambda b, h, i: (b * nb + i, h)),
        out_shape=jax.ShapeDtypeStruct((n, ATTN_WIDTH), BF16),
        scratch_shapes=[pltpu.VMEM((hps * vt_rows, seq), BF16),
                        pltpu.VMEM((hps, nb, MOBA_BLOCK), F32),
                        pltpu.VMEM((hps, MOBA_BLOCK, MOBA_BLOCK), F32),
                        pltpu.VMEM((hps, vt_rows, MOBA_BLOCK), F32)],
        compiler_params=_cparams(("arbitrary", "arbitrary", "arbitrary")),
        name="moba",
    )(slopes, proj, proj, proj, kmean)


def _ret_kernel(lg_ref, q_ref, k_ref, v_ref, g_ref, rnw_ref, o_ref,
                state_ref, dmask_ref, qdec_ref, kdec_ref, cdec_ref):
    n = pl.program_id(1)
    c = RET_CHUNK
    dk = RET_HEAD_DIM
    kscale = dk ** -0.5

    @pl.when(n == 0)
    def _():
        state_ref[...] = jnp.zeros_like(state_ref)
        ti = lax.broadcasted_iota(I32, (c, c), 0)
        si = lax.broadcasted_iota(I32, (c, c), 1)
        diff = jnp.maximum(ti - si, 0).astype(F32)
        pos = lax.broadcasted_iota(I32, (c, dk), 0).astype(F32)
        for h in range(RET_HEADS):
            lg = lg_ref[h]
            dmask_ref[h] = jnp.where(ti >= si, jnp.exp(diff * lg) * kscale, 0.0)
            qdec_ref[h] = jnp.exp((pos + 1.0) * lg)
            kdec_ref[h] = jnp.exp((c - 1.0 - pos) * lg) * kscale
            cdec_ref[h] = jnp.exp(jnp.full((1, dk), float(c), F32) * lg)

    heads = [slice(h * dk, (h + 1) * dk) for h in range(RET_HEADS)]
    qs = [q_ref[:, cols] for cols in heads]
    ks = [k_ref[:, cols] for cols in heads]
    vs = [v_ref[:, cols] for cols in heads]
    states = [state_ref[h] for h in range(RET_HEADS)]
    raw = [lax.dot_general(qs[h], ks[h], _DN_LAST, preferred_element_type=F32) for h in range(RET_HEADS)]
    cross = [jnp.dot(qs[h], states[h].astype(BF16), preferred_element_type=F32) for h in range(RET_HEADS)]
    outs, new_states = [], []
    for h, cols in enumerate(heads):
        inner = jnp.dot((raw[h] * dmask_ref[h]).astype(BF16), vs[h], preferred_element_type=F32)
        kd = (ks[h].astype(F32) * kdec_ref[h]).astype(BF16)
        kv = lax.dot_general(kd, vs[h], _DN_FIRST, preferred_element_type=F32)
        new_states.append(states[h] * cdec_ref[h] + kv)
        r = inner + cross[h] * qdec_ref[h]
        rn = r * lax.rsqrt(jnp.mean(r * r, axis=-1, keepdims=True) + NORM_EPS)
        g = g_ref[:, cols].astype(F32)
        outs.append((rn * rnw_ref[:, cols] * _silu(g)).astype(BF16))
    for h, cols in enumerate(heads):
        state_ref[h] = new_states[h]
        o_ref[:, cols] = outs[h]


def _retention(proj, log_gamma, rnw, batch, seq):
    n = proj.shape[0]
    c = RET_CHUNK
    nc = seq // c
    dk = RET_HEAD_DIM
    base = 3 * ATTN_WIDTH // RET_WIDTH

    def col(off):
        return lambda b, t: (b * nc + t, base + off)

    return pl.pallas_call(
        _ret_kernel,
        grid=(batch, nc),
        in_specs=[pl.BlockSpec(memory_space=pltpu.SMEM),
                  pl.BlockSpec((c, RET_WIDTH), col(0)),
                  pl.BlockSpec((c, RET_WIDTH), col(1)),
                  pl.BlockSpec((c, RET_WIDTH), col(2)),
                  pl.BlockSpec((c, RET_WIDTH), col(3)),
                  pl.BlockSpec((1, RET_WIDTH), lambda b, t: (0, 0))],
        out_specs=pl.BlockSpec((c, RET_WIDTH), lambda b, t: (b * nc + t, 0)),
        out_shape=jax.ShapeDtypeStruct((n, RET_WIDTH), BF16),
        scratch_shapes=[pltpu.VMEM((RET_HEADS, dk, dk), F32), pltpu.VMEM((RET_HEADS, c, c), F32),
                        pltpu.VMEM((RET_HEADS, c, dk), F32), pltpu.VMEM((RET_HEADS, c, dk), F32),
                        pltpu.VMEM((RET_HEADS, 1, dk), F32)],
        compiler_params=_cparams(("arbitrary", "arbitrary")),
        name="retention",
    )(log_gamma, proj, proj, proj, proj, rnw)


def _outproj_kernel(a_ref, r_ref, w_ref, x_ref, mod_ref, nw_ref, wr_ref, br_ref,
                    x1_ref, hp_ref, lg_ref, wb_ref):
    @pl.when(pl.program_id(0) == 0)
    def _():
        wb_ref[...] = w_ref[...].astype(BF16)

    mix = jnp.dot(jnp.concatenate([a_ref[...], r_ref[...]], axis=1), wb_ref[...],
                  preferred_element_type=F32)
    m = mod_ref[0]
    x1 = x_ref[...] + m[2:3] * mix
    x1_ref[...] = x1
    y = x1 * lax.rsqrt(jnp.mean(x1 * x1, axis=-1, keepdims=True) + NORM_EPS) * nw_ref[...]
    h2 = y * (1.0 + m[4:5]) + m[3:4]
    _store_token_major(hp_ref, h2)
    hi = h2.astype(BF16)
    lo = (h2 - hi.astype(F32)).astype(BF16)
    part = jnp.dot(jnp.concatenate([hi, lo], axis=1), wr_ref[...], preferred_element_type=F32)
    lg_ref[...] = part + pltpu.roll(part, LANES // 2, axis=1) + br_ref[...]


def _outproj(attn, ret, w_bf, x2d, mod3, nw, wr_cat, br, seq, tm):
    n, d = x2d.shape
    per_b = seq // tm
    return pl.pallas_call(
        _outproj_kernel,
        grid=(n // tm,),
        in_specs=[pl.BlockSpec((tm, ATTN_WIDTH), lambda i: (i, 0)),
                  pl.BlockSpec((tm, RET_WIDTH), lambda i: (i, 0)),
                  pl.BlockSpec((d, d), lambda i: (0, 0), pipeline_mode=pl.Buffered(1)),
                  pl.BlockSpec((tm, d), lambda i: (i, 0)),
                  pl.BlockSpec((1, N_MOD, d), lambda i: (i // per_b, 0, 0)),
                  pl.BlockSpec((1, d), lambda i: (0, 0)),
                  pl.BlockSpec((2 * d, LANES), lambda i: (0, 0)),
                  pl.BlockSpec((1, LANES), lambda i: (0, 0))],
        out_specs=[pl.BlockSpec((tm, d), lambda i: (i, 0)),
                   pl.BlockSpec((tm * (d // LANES), LANES), lambda i: (i, 0)),
                   pl.BlockSpec((tm, LANES), lambda i: (i, 0))],
        out_shape=[jax.ShapeDtypeStruct((n, d), F32),
                   jax.ShapeDtypeStruct((n * (d // LANES), LANES), F32),
                   jax.ShapeDtypeStruct((n, LANES), F32)],
        scratch_shapes=[pltpu.VMEM((d, d), BF16)],
        compiler_params=_cparams(("arbitrary",)),
        name="outproj",
    )(attn, ret, w_bf, x2d, mod3, nw, wr_cat, br)


_L_E0, _L_E1, _L_W0, _L_W1, _L_D0, _L_D1 = 0, 1, 2, 3, 4, 5
_ROUTER_EXPERT_LANE = 8
_M_TILE_E, _M_NVALID, _M_NTILES, _M_COUNT, _M_START = 0, 1, 2, 3, 4
_PLAN_ROWS = 256
_PLAN_GROUP = 4


def _first_row_where(cond, rows, limit):
    return jnp.min(jnp.where(cond, rows, limit), axis=0, keepdims=True)


def _col_from_row(v):
    r = lax.broadcasted_iota(I32, (LANES, LANES), 0)
    c = lax.broadcasted_iota(I32, (LANES, LANES), 1)
    return jnp.sum(jnp.where(r == c, v, 0.0), axis=1, keepdims=True)


def _row_from_col(v):
    r = lax.broadcasted_iota(I32, (LANES, LANES), 0)
    c = lax.broadcasted_iota(I32, (LANES, LANES), 1)
    return jnp.sum(jnp.where(r == c, v, 0.0), axis=0, keepdims=True)


def _rows_to_tile(rows, row8):
    tile = 0.0
    for k in reversed(range(len(rows))):
        tile = jnp.where(row8 == k, rows[k], tile)
    return tile


def _plan_kernel(lg_ref, info_ref, dest_ref, meta_ref, *, n_tok):
    tb = _PLAN_ROWS
    n_g, n_e = N_GROUPS, EXPERTS_PER_GROUP
    assert n_g <= SUBLANES and n_e == SUBLANES, "group and per-group logits each fill one sublane tile"
    no_row = float(SUBLANES)
    row8 = lax.broadcasted_iota(I32, (SUBLANES, tb), 0).astype(F32)
    row_e = lax.broadcasted_iota(I32, (N_EXPERTS, tb), 0).astype(F32)
    earlier = lax.broadcasted_iota(I32, (tb, tb), 0) < lax.broadcasted_iota(I32, (tb, tb), 1)
    tri = jnp.where(earlier, 1.0, 0.0).astype(BF16)

    def route_block(r0):
        lt = lg_ref[pl.ds(r0, tb), :].T
        gl = jnp.where(row8 < n_g, lt[0:SUBLANES, :], -jnp.inf)
        ge = jnp.exp(gl - jnp.max(gl, axis=0, keepdims=True))
        gp = ge / jnp.sum(ge, axis=0, keepdims=True)
        g_w = jnp.max(gp, axis=0, keepdims=True)
        g_sel = _first_row_where((gp == g_w) & (row8 < n_g), row8, no_row)
        el = lt[_ROUTER_EXPERT_LANE:_ROUTER_EXPERT_LANE + n_e, :]
        for g in range(1, n_g):
            lo = _ROUTER_EXPERT_LANE + g * n_e
            el = jnp.where(g_sel == g, lt[lo:lo + n_e, :], el)
        ee = jnp.exp(el - jnp.max(el, axis=0, keepdims=True))
        ep = ee / jnp.sum(ee, axis=0, keepdims=True)
        p1 = jnp.max(ep, axis=0, keepdims=True)
        i1 = _first_row_where(ep == p1, row8, no_row)
        ep2 = jnp.where(row8 == i1, -1.0, ep)
        p2 = jnp.max(ep2, axis=0, keepdims=True)
        i2 = _first_row_where(ep2 == p2, row8, no_row)
        denom = p1 + p2
        w0 = g_w * p1 / denom
        w1 = g_w * p2 / denom
        e0 = g_sel * n_e + i1
        e1 = g_sel * n_e + i2
        oh0 = jnp.where(row_e == e0, 1.0, 0.0)
        oh1 = jnp.where(row_e == e1, 1.0, 0.0)
        oh = oh0 + oh1
        local = jnp.dot(oh.astype(BF16), tri, preferred_element_type=F32)
        return (e0, e1, w0, w1), oh0, oh1, local, jnp.sum(oh, axis=1, keepdims=True)

    group = _PLAN_GROUP if (n_tok // tb) % _PLAN_GROUP == 0 else 1

    def route(it, carry):
        starts = [pl.multiple_of((it * group + k) * tb, tb) for k in range(group)]
        blocks = [route_block(r0) for r0 in starts]
        tiles = []
        for (e0, e1, w0, w1), oh0, oh1, local, total in blocks:
            before = local + carry
            rank0 = jnp.sum(oh0 * before, axis=0, keepdims=True)
            rank1 = jnp.sum(oh1 * before, axis=0, keepdims=True)
            tiles.append(_rows_to_tile([e0, e1, w0, w1, rank0, rank1], row8))
            carry = carry + total
        for r0, tile in zip(starts, tiles):
            info_ref[:, pl.ds(r0, tb)] = tile
        return carry

    counts_col = lax.fori_loop(0, n_tok // (tb * group), route, jnp.zeros((N_EXPERTS, 1), F32))
    counts = _row_from_col(jnp.concatenate([counts_col, jnp.zeros((LANES - N_EXPERTS, 1), F32)], axis=0))

    lane1 = lax.broadcasted_iota(I32, (1, LANES), 1)
    padded = jnp.floor((counts + (FFN_TILE - 1.0)) * (1.0 / FFN_TILE)) * FFN_TILE
    pad_end = padded
    sh = 1
    while sh < N_EXPERTS:
        pad_end = pad_end + jnp.where(lane1 >= sh, pltpu.roll(pad_end, sh, axis=1), 0.0)
        sh *= 2
    pad_start = pad_end - padded

    start_col = _col_from_row(pad_start)[0:N_EXPERTS, :]

    def place(it, _):
        starts = [pl.multiple_of((it * group + k) * tb, tb) for k in range(group)]
        tiles = [info_ref[:, pl.ds(r0, tb)] for r0 in starts]
        placed = []
        for tile in tiles:
            s0 = jnp.sum(jnp.where(row_e == tile[_L_E0:_L_E0 + 1, :], start_col, 0.0), axis=0, keepdims=True)
            s1 = jnp.sum(jnp.where(row_e == tile[_L_E1:_L_E1 + 1, :], start_col, 0.0), axis=0, keepdims=True)
            placed.append(tile + jnp.where(row8 == _L_D0, s0, jnp.where(row8 == _L_D1, s1, 0.0)))
        for r0, tile in zip(starts, placed):
            info_ref[:, pl.ds(r0, tb)] = tile
            dest_ref[:, pl.ds(r0, tb)] = tile.astype(I32)
        return 0

    lax.fori_loop(0, n_tok // (tb * group), place, 0)

    pe_col = _col_from_row(pad_end)
    mt = meta_ref.shape[1]
    t_start = lax.broadcasted_iota(I32, (LANES, mt), 1).astype(F32) * FFN_TILE
    e_row = lax.broadcasted_iota(I32, (LANES, mt), 0)
    ended = jnp.where((pe_col <= t_start) & (e_row < N_EXPERTS), 1.0, 0.0)
    tile_e = jnp.minimum(jnp.sum(ended, axis=0, keepdims=True), N_EXPERTS - 1.0)
    total = jnp.sum(jnp.where(lane1 == N_EXPERTS - 1, pad_end, 0.0), axis=1, keepdims=True)
    n_valid = jnp.broadcast_to(total * (1.0 / FFN_TILE), (1, mt))
    def per_expert(v):
        return v if mt == LANES else jnp.concatenate([v, jnp.zeros((1, mt - LANES), F32)], axis=1)

    mrow = lax.broadcasted_iota(I32, (SUBLANES, mt), 0)
    meta = jnp.where(mrow == _M_TILE_E, tile_e,
           jnp.where(mrow == _M_NVALID, n_valid,
           jnp.where(mrow == _M_NTILES, per_expert(padded * (1.0 / FFN_TILE)),
           jnp.where(mrow == _M_COUNT, per_expert(counts),
           jnp.where(mrow == _M_START, per_expert(pad_start), 0.0)))))
    meta_ref[...] = meta.astype(I32)


def _plan(logits, n_tiles):
    n_tok = logits.shape[0]
    mt = -(-n_tiles // LANES) * LANES
    kern = functools.partial(_plan_kernel, n_tok=n_tok)
    return pl.pallas_call(
        kern,
        out_shape=[jax.ShapeDtypeStruct((SUBLANES, n_tok), F32),
                   jax.ShapeDtypeStruct((SUBLANES, n_tok), I32),
                   jax.ShapeDtypeStruct((SUBLANES, mt), I32)],
        compiler_params=pltpu.CompilerParams(vmem_limit_bytes=VMEM_LIMIT),
        name="plan",
    )(logits)


_DMA_UNROLL = 8


def _invert_kernel(d0_ref, d1_ref, nv_ref, ntl_ref, cnt_ref, start_ref, tok_ref, *, n_tok, p_rows):
    def zero_rows(lo, hi):
        def zbody(u, _):
            for s in range(_DMA_UNROLL):
                tok_ref[jnp.minimum(lo + u * _DMA_UNROLL + s, hi - 1)] = 0
            return 0
        lax.fori_loop(0, (hi - lo + _DMA_UNROLL - 1) // _DMA_UNROLL, zbody, 0)

    def pad_body(e, _):
        zero_rows(start_ref[e] + cnt_ref[e], start_ref[e] + ntl_ref[e] * FFN_TILE)
        return 0

    lax.fori_loop(0, N_EXPERTS, pad_body, 0)
    zero_rows(nv_ref[0] * FFN_TILE, p_rows)

    def body(u, _):
        t0 = u * _DMA_UNROLL
        rows = [(d0_ref[t0 + s], d1_ref[t0 + s]) for s in range(_DMA_UNROLL)]
        for s, (r0, r1) in enumerate(rows):
            tok_ref[r0] = t0 + s
            tok_ref[r1] = t0 + s
        return 0

    lax.fori_loop(0, n_tok // _DMA_UNROLL, body, 0)


def _invert(d0, d1, n_valid, ntl, cnt, start, p_rows):
    n_tok = d0.shape[0]
    kern = functools.partial(_invert_kernel, n_tok=n_tok, p_rows=p_rows)
    return pl.pallas_call(
        kern,
        grid_spec=pltpu.PrefetchScalarGridSpec(
            num_scalar_prefetch=6,
            grid=(1,),
            in_specs=[],
            out_specs=pl.BlockSpec(memory_space=pltpu.SMEM)),
        out_shape=jax.ShapeDtypeStruct((p_rows,), I32),
        compiler_params=_cparams(("arbitrary",)),
        name="invert",
    )(d0, d1, n_valid, ntl, cnt, start)


_FFN_SLOTS = 3
_BULK_DMA_PRIORITY = 1


def _ffn_kernel(te_ref, nv_ref, ntl_ref, tok_ref, h_ref, w1_ref, w3_ref, w2_ref, y_ref,
                xbuf, xb_ref, w1f, w3f, w2f, w1b, w3b, w2b, gsem, wsem, run_ref, *, n_tiles):
    i = pl.program_id(0)
    nv = nv_ref[0]
    valid = i < nv
    slot = i % _FFN_SLOTS
    nc = xb_ref.shape[1] // LANES

    def row_copy(tile, r, dst_slot):
        tok = tok_ref[tile * FFN_TILE + r]
        src = pl.ds(pl.multiple_of(tok * nc, nc), nc)
        return pltpu.make_async_copy(h_ref.at[src], xbuf.at[dst_slot, pl.ds(r * nc, nc)], gsem.at[dst_slot])

    def weight_copies(e, ws):
        return (pltpu.make_async_copy(w1_ref.at[e], w1f.at[ws], wsem.at[ws]),
                pltpu.make_async_copy(w3_ref.at[e], w3f.at[ws], wsem.at[ws]),
                pltpu.make_async_copy(w2_ref.at[e], w2f.at[ws], wsem.at[ws]))

    @pl.when(i == 0)
    def _():
        run_ref[0] = 0
        for cp in weight_copies(te_ref[0], 0):
            cp.start(priority=_BULK_DMA_PRIORITY)

        def first_rows(u, _):
            for s in range(_DMA_UNROLL):
                for t in range(_FFN_SLOTS - 1):
                    row_copy(t, u * _DMA_UNROLL + s, t).start()
            return 0
        lax.fori_loop(0, FFN_TILE // _DMA_UNROLL, first_rows, 0)

    @pl.when(valid & ((i == 0) | (te_ref[i] != te_ref[jnp.maximum(i, 1) - 1])))
    def _():
        run = run_ref[0]
        ws = run % 2
        for cp in weight_copies(te_ref[i], ws):
            cp.wait()
        w1b[...] = w1f[ws].astype(BF16)
        w3b[...] = w3f[ws].astype(BF16)
        w2b[...] = w2f[ws].astype(BF16)
        nxt = i + ntl_ref[te_ref[i]]

        @pl.when(nxt < nv)
        def _():
            for cp in weight_copies(te_ref[jnp.minimum(nxt, n_tiles - 1)], 1 - ws):
                cp.start(priority=_BULK_DMA_PRIORITY)
        run_ref[0] = run + 1

    @pl.when(valid)
    def _():
        pltpu.make_async_copy(h_ref.at[pl.ds(0, FFN_TILE * nc)], xbuf.at[slot], gsem.at[slot]).wait()
        xb_ref[...] = _load_token_major(xbuf.at[slot], FFN_TILE, xb_ref.shape[1], BF16)
        ahead = _FFN_SLOTS - 1
        nxt_tile = jnp.minimum(i + ahead, n_tiles - 1)
        nxt_slot = (i + ahead) % _FFN_SLOTS
        for r in range(FFN_TILE):
            row_copy(nxt_tile, r, nxt_slot).start()
        x = xb_ref[...]
        a = jnp.dot(x, w1b[...], preferred_element_type=F32)
        b = jnp.dot(x, w3b[...], preferred_element_type=F32)
        hmid = (_silu(a) * b).astype(BF16)
        y_ref[...] = jnp.dot(hmid, w2b[...], preferred_element_type=F32)

    @pl.when(jnp.logical_not(valid))
    def _():
        y_ref[...] = jnp.zeros_like(y_ref)

    @pl.when(i == n_tiles - 1)
    def _():
        for t in range(_FFN_SLOTS - 1):
            s = (nv + t) % _FFN_SLOTS
            pltpu.make_async_copy(h_ref.at[pl.ds(0, FFN_TILE * nc)], xbuf.at[s], gsem.at[s]).wait()


def _ffn(tile_e, n_valid, ntl, row_tok, h2, w1, w3, w2):
    p_rows = row_tok.shape[0]
    _, d, de = w1.shape
    n_tiles = p_rows // FFN_TILE
    kern = functools.partial(_ffn_kernel, n_tiles=n_tiles)
    any_spec = pl.BlockSpec(memory_space=pl.ANY)
    return pl.pallas_call(
        kern,
        grid_spec=pltpu.PrefetchScalarGridSpec(
            num_scalar_prefetch=4,
            grid=(n_tiles,),
            in_specs=[any_spec, any_spec, any_spec, any_spec],
            out_specs=pl.BlockSpec((FFN_TILE, d), lambda i, *_: (i, 0)),
            scratch_shapes=[pltpu.VMEM((_FFN_SLOTS, FFN_TILE * (d // LANES), LANES), F32),
                            pltpu.VMEM((FFN_TILE, d), BF16),
                            pltpu.VMEM((2, d, de), F32), pltpu.VMEM((2, d, de), F32),
                            pltpu.VMEM((2, de, d), F32),
                            pltpu.VMEM((d, de), BF16), pltpu.VMEM((d, de), BF16),
                            pltpu.VMEM((de, d), BF16),
                            pltpu.SemaphoreType.DMA((_FFN_SLOTS,)), pltpu.SemaphoreType.DMA((2,)),
                            pltpu.SMEM((1,), I32)]),
        out_shape=jax.ShapeDtypeStruct((p_rows, d), F32),
        compiler_params=_cparams(("arbitrary",)),
        name="ffn",
    )(tile_e, n_valid, ntl, row_tok, h2, w1, w3, w2)


def _combine_kernel(d0_ref, d1_ref, x1_ref, info_ref, mod_ref, nw_ref, y_ref, o_ref,
                    *scratch, tp, n_phases):
    i = pl.program_id(0)
    ns = _COMBINE_SETS
    sem = scratch[-1]
    bufs = [(scratch[2 * q], scratch[2 * q + 1]) for q in range(ns)]

    def row_copies(phase, r, q):
        tok = phase * tp + r
        ya, yb = bufs[q]
        return (pltpu.make_async_copy(y_ref.at[pl.ds(d0_ref[tok], 1)], ya.at[pl.ds(r, 1)], sem.at[q]),
                pltpu.make_async_copy(y_ref.at[pl.ds(d1_ref[tok], 1)], yb.at[pl.ds(r, 1)], sem.at[q]))

    def wait_set(q):
        for buf in bufs[q]:
            pltpu.make_async_copy(y_ref.at[pl.ds(0, tp)], buf, sem.at[q]).wait()

    @pl.when(i == 0)
    def _():
        def first_rows(u, _):
            for s in range(_DMA_UNROLL):
                for q in range(ns - 1):
                    for cp in row_copies(q, u * _DMA_UNROLL + s, q):
                        cp.start()
            return 0
        lax.fori_loop(0, tp // _DMA_UNROLL, first_rows, 0)

    for q in range(ns):
        wait_set(q)
        ahead = jnp.minimum(i * ns + q + ns - 1, n_phases - 1)
        for r in range(tp):
            for prio, cp in enumerate(row_copies(ahead, r, (q + ns - 1) % ns)):
                cp.start(priority=prio)
        ya, yb = bufs[q]
        rows = pl.ds(q * tp, tp)
        w0 = _col_from_row(info_ref[_L_W0:_L_W0 + 1, q * tp:(q + 1) * tp])
        w1 = _col_from_row(info_ref[_L_W1:_L_W1 + 1, q * tp:(q + 1) * tp])
        moe = ya[...] * w0 + yb[...] * w1
        x2 = x1_ref[rows, :] + mod_ref[0][5:6] * moe
        o_ref[rows, :] = x2 * lax.rsqrt(jnp.mean(x2 * x2, axis=-1, keepdims=True) + NORM_EPS) * nw_ref[...]

    @pl.when(i == n_phases // ns - 1)
    def _():
        for q in range(ns - 1):
            wait_set(q)


_COMBINE_SETS = 4


def _combine(d0, d1, x1, info, mod3, nfw, y, seq, tp=LANES):
    assert tp == LANES, "one phase's combine weights must fill exactly one lane-dense row"
    n, d = x1.shape
    tc = _COMBINE_SETS * tp
    per_b = seq // tc
    kern = functools.partial(_combine_kernel, tp=tp, n_phases=n // tp)
    return pl.pallas_call(
        kern,
        grid_spec=pltpu.PrefetchScalarGridSpec(
            num_scalar_prefetch=2,
            grid=(n // tc,),
            in_specs=[pl.BlockSpec((tc, d), lambda i, d0, d1: (i, 0)),
                      pl.BlockSpec((SUBLANES, tc), lambda i, d0, d1: (0, i)),
                      pl.BlockSpec((1, N_MOD, d), lambda i, d0, d1: (i // per_b, 0, 0)),
                      pl.BlockSpec((1, d), lambda i, d0, d1: (0, 0)),
                      pl.BlockSpec(memory_space=pl.ANY)],
            out_specs=pl.BlockSpec((tc, d), lambda i, d0, d1: (i, 0)),
            scratch_shapes=[pltpu.VMEM((tp, d), F32)] * (2 * _COMBINE_SETS)
                           + [pltpu.SemaphoreType.DMA((_COMBINE_SETS,))]),
        out_shape=jax.ShapeDtypeStruct((n, d), F32),
        compiler_params=_cparams(("arbitrary",)),
        name="combine",
    )(d0, d1, x1, info, mod3, nfw, y)


def _split_hi_lo(w):
    hi = w.astype(BF16)
    lo = (w - hi.astype(F32)).astype(BF16)
    return hi, lo


def kernel(x, c, w_ada, b_ada, norm_mix_w, w_in, ret_norm_w, w_out, norm_ffn_w,
           w_group, b_group, w_router, b_router, w1, w3, w2, norm_final_w):
    batch, seq, d = x.shape
    n = batch * seq
    depth = w_ada.shape[0]
    assert depth == 1, "the final rmsnorm is fused into the (single) layer's last kernel"
    tm = min(INPROJ_ROWS, seq)
    half = LANES // 2

    slopes = jnp.exp2(-8.0 * jnp.arange(1, N_ATTN_HEADS + 1, dtype=F32) / N_ATTN_HEADS)
    log_gamma = jnp.log1p(-jnp.exp2(-5.0 - jnp.arange(RET_HEADS, dtype=F32)))
    n_tiles = (2 * n) // FFN_TILE + N_EXPERTS
    p_rows = n_tiles * FFN_TILE

    assert batch <= SUBLANES
    c_pad = jnp.zeros((SUBLANES, d), F32).at[:batch].set(c)
    x2d = x.reshape(n, d)
    for l in range(depth):
        mod = _adaln(c_pad, w_ada[l], b_ada[l].reshape(1, -1))
        mod3 = mod[:batch].reshape(batch, N_MOD, d)

        proj, kmean = _inproj(x2d, mod3, norm_mix_w[l].reshape(1, d), w_in[l], seq, tm)
        kmean = kmean.reshape(batch, seq // MOBA_BLOCK, ATTN_WIDTH)
        attn = _moba(proj, kmean, slopes, batch, seq)
        ret = _retention(proj, log_gamma, ret_norm_w[l].reshape(1, RET_WIDTH), batch, seq)

        gap = _ROUTER_EXPERT_LANE - N_GROUPS
        wr = jnp.concatenate([w_group[l], jnp.zeros((d, gap), F32),
                              jnp.transpose(w_router[l], (1, 0, 2)).reshape(d, N_EXPERTS)], axis=1)
        wr = jnp.pad(wr, ((0, 0), (0, half - wr.shape[1])))
        wr_hi, wr_lo = _split_hi_lo(wr)
        wr_cat = jnp.concatenate([jnp.concatenate([wr_hi, wr_lo], axis=1),
                                  jnp.concatenate([wr_hi, jnp.zeros_like(wr_lo)], axis=1)], axis=0)
        br = jnp.concatenate([b_group[l], jnp.zeros((gap,), F32), b_router[l].reshape(-1)])
        br = jnp.pad(br, (0, LANES - br.shape[0])).reshape(1, LANES)

        x1, h2p, logits = _outproj(attn, ret, w_out[l], x2d, mod3,
                                   norm_ffn_w[l].reshape(1, d), wr_cat, br, seq, min(OUTPROJ_ROWS, seq))

        info, dest, meta = _plan(logits, n_tiles)
        d0 = dest[_L_D0]
        d1 = dest[_L_D1]
        tile_e = meta[_M_TILE_E, :n_tiles]
        n_valid = meta[_M_NVALID, :1]
        ntl = meta[_M_NTILES, :N_EXPERTS]

        row_tok = _invert(d0, d1, n_valid, ntl, meta[_M_COUNT, :N_EXPERTS],
                          meta[_M_START, :N_EXPERTS], p_rows)
        y = _ffn(tile_e, n_valid, ntl, row_tok, h2p, w1[l], w3[l], w2[l])
        nfw = norm_final_w.reshape(1, d)
        x2d = _combine(d0, d1, x1, info, mod3, nfw, y, seq)
    return x2d.reshape(batch, seq, d)
```

```python
import functools

import jax
import jax.numpy as jnp
from jax import lax
from jax.experimental import pallas as pl
from jax.experimental.pallas import tpu as pltpu

F32 = jnp.float32
BF16 = jnp.bfloat16
I32 = jnp.int32

N_ATTN_HEADS = 8
ATTN_HEAD_DIM = 128
ATTN_WIDTH = N_ATTN_HEADS * ATTN_HEAD_DIM
MOBA_BLOCK = 256
MOBA_TOPK = 3
RET_HEADS = 4
RET_HEAD_DIM = 256
RET_WIDTH = RET_HEADS * RET_HEAD_DIM
N_GROUPS = 4
EXPERTS_PER_GROUP = 8
N_EXPERTS = N_GROUPS * EXPERTS_PER_GROUP
N_MOD = 6
NORM_EPS = 1e-6

LANES = 128
SUBLANES = 8
RET_CHUNK = 256
FFN_TILE = 256
INPROJ_ROWS = 1024
OUTPROJ_ROWS = 256
MASK_VALUE = -1e30
LOG2E = 1.4426950408889634
MOBA_QK_SCALE = ATTN_HEAD_DIM ** -0.5 * LOG2E
_MOBA_EXTRA_ROWS = 16
VMEM_LIMIT = 56 * 1024 * 1024

_DN_LAST = (((1,), (1,)), ((), ()))
_DN_FIRST = (((0,), (0,)), ((), ()))


def _cparams(sem):
    return pltpu.CompilerParams(dimension_semantics=sem, vmem_limit_bytes=VMEM_LIMIT)


def _silu(v):
    return v * jax.nn.sigmoid(v)


def _store_token_major(ref, value):
    rows, d = value.shape
    nc = d // LANES
    for c in range(nc):
        ref[pl.ds(c, rows, stride=nc), :] = value[:, c * LANES:(c + 1) * LANES]


def _load_token_major(ref, rows, d, dtype):
    nc = d // LANES
    return jnp.concatenate([ref[pl.ds(c, rows, stride=nc), :].astype(dtype) for c in range(nc)], axis=1)


def _adaln_kernel(c_ref, w_ref, b_ref, o_ref):
    ca = _silu(c_ref[...]).astype(BF16)
    o_ref[...] = jnp.dot(ca, w_ref[...].astype(BF16), preferred_element_type=F32) + b_ref[...]


def _adaln(c_pad, w_ada, b_ada, tn=1024):
    rows, d = c_pad.shape
    n = w_ada.shape[1]
    return pl.pallas_call(
        _adaln_kernel,
        grid=(n // tn,),
        in_specs=[pl.BlockSpec((rows, d), lambda j: (0, 0)),
                  pl.BlockSpec((d, tn), lambda j: (0, j)),
                  pl.BlockSpec((1, tn), lambda j: (0, j))],
        out_specs=pl.BlockSpec((rows, tn), lambda j: (0, j)),
        out_shape=jax.ShapeDtypeStruct((rows, n), F32),
        compiler_params=_cparams(("arbitrary",)),
        name="adaln",
    )(c_pad, w_ada, b_ada)


def _inproj_kernel(x_ref, mod_ref, nw_ref, w_ref, o_ref, km_ref, h_ref, *, tm, tn):
    j = pl.program_id(1)

    @pl.when(j == 0)
    def _():
        x = x_ref[...]
        y = x * lax.rsqrt(jnp.mean(x * x, axis=-1, keepdims=True) + NORM_EPS) * nw_ref[...]
        m = mod_ref[0]
        h_ref[...] = (y * (1.0 + m[1:2]) + m[0:1]).astype(BF16)

    acc = jnp.dot(h_ref[...], w_ref[...].astype(BF16), preferred_element_type=F32)
    o_ref[...] = (acc * jnp.where(j == 0, MOBA_QK_SCALE, 1.0)).astype(BF16)

    @pl.when(j == 1)
    def _():
        km_ref[0] = jnp.sum(acc.reshape(tm // MOBA_BLOCK, MOBA_BLOCK, tn), axis=1) * (1.0 / MOBA_BLOCK)


def _inproj(x2d, mod3, nw, w_bf, seq, tm, tn=ATTN_WIDTH):
    n, d = x2d.shape
    width = w_bf.shape[1]
    per_b = seq // tm
    kern = functools.partial(_inproj_kernel, tm=tm, tn=tn)
    return pl.pallas_call(
        kern,
        grid=(n // tm, width // tn),
        in_specs=[pl.BlockSpec((tm, d), lambda i, j: (i, 0)),
                  pl.BlockSpec((1, N_MOD, d), lambda i, j: (i // per_b, 0, 0)),
                  pl.BlockSpec((1, d), lambda i, j: (0, 0)),
                  pl.BlockSpec((d, tn), lambda i, j: (0, j))],
        out_specs=[pl.BlockSpec((tm, tn), lambda i, j: (i, j)),
                   pl.BlockSpec((1, tm // MOBA_BLOCK, tn), lambda i, j: (i, 0, 0))],
        out_shape=[jax.ShapeDtypeStruct((n, width), BF16),
                   jax.ShapeDtypeStruct((n // tm, tm // MOBA_BLOCK, tn), F32)],
        scratch_shapes=[pltpu.VMEM((tm, d), BF16)],
        compiler_params=_cparams(("arbitrary", "arbitrary")),
        name="inproj",
    )(x2d, mod3, nw, w_bf)


def _moba_kernel(slopes_ref, q_ref, k_ref, v_ref, km_ref, o_ref,
                 vt_ref, sel_ref, bias_ref, acc_ref, *, nb, hps):
    hg = pl.program_id(1)
    i = pl.program_id(2)
    blk = MOBA_BLOCK
    dh = ATTN_HEAD_DIM
    log2e = LOG2E

    qpos = lax.broadcasted_iota(I32, (blk, blk), 1)
    kpos = lax.broadcasted_iota(I32, (blk, blk), 0)

    vt_rows = dh + _MOBA_EXTRA_ROWS

    @pl.when(i == 0)
    def _():
        vt = v_ref[...].astype(F32).T.astype(BF16)
        extra = lax.broadcasted_iota(I32, (_MOBA_EXTRA_ROWS, vt.shape[1]), 0)
        ones_row = jnp.where(extra == 0, 1.0, 0.0).astype(BF16)
        key_pos = lax.broadcasted_iota(I32, (blk, LANES), 0).astype(F32)
        for g in range(hps):
            vt_ref[g * vt_rows:g * vt_rows + dh, :] = vt[g * dh:(g + 1) * dh, :]
            vt_ref[g * vt_rows + dh:(g + 1) * vt_rows, :] = ones_row
            bias_ref[g] = key_pos * (log2e * slopes_ref[hg * hps + g])

    bidx = lax.broadcasted_iota(I32, (nb, blk), 0)
    past = bidx < i
    i0 = pl.multiple_of(i * blk, blk)
    heads = [slice(g * dh, (g + 1) * dh) for g in range(hps)]
    vrows = [slice(g * vt_rows, (g + 1) * vt_rows) for g in range(hps)]
    gates, raw_own = [], []
    for cols in heads:
        q = q_ref[:, cols]
        km = km_ref[0, :, cols]
        km_hi = km.astype(BF16)
        km_lo = (km - km_hi.astype(F32)).astype(BF16)
        gates.append(lax.dot_general(km_hi, q, _DN_LAST, preferred_element_type=F32)
                     + lax.dot_general(km_lo, q, _DN_LAST, preferred_element_type=F32))
        raw_own.append(lax.dot_general(k_ref[pl.ds(i0, blk), cols], q, _DN_LAST,
                                       preferred_element_type=F32))
    def key_bias(g):
        kb = bias_ref[g]
        return jnp.concatenate([kb] * (blk // LANES), axis=1)

    own_bias = [key_bias(g) for g in range(hps)]

    carry0, acc0, sels = [], [], []
    for g, cols in enumerate(heads):
        gate = jnp.where(past, gates[g], -jnp.inf)
        rank = jnp.zeros((nb, blk), F32)
        for m in range(nb):
            gm = gate[m:m + 1, :]
            beats = jnp.where(gm > gate, 1.0, jnp.where((gm == gate) & (m < bidx), 1.0, 0.0))
            rank = rank + jnp.where(m < i, beats, 0.0)
        sels.append(jnp.where(past & (rank < MOBA_TOPK), 1.0, 0.0))

        s = jnp.where(qpos >= kpos, own_bias[g] + raw_own[g], MASK_VALUE)
        m0 = jnp.max(s, axis=0, keepdims=True)
        p = jnp.exp2(s - m0)
        acc0.append(jnp.dot(vt_ref[vrows[g], pl.ds(i0, blk)], p.astype(BF16), preferred_element_type=F32))
        carry0.append(m0)
    for g in range(hps):
        acc_ref[g] = acc0[g]
        sel_ref[g] = sels[g]

    def body(j, carry):
        j0 = pl.multiple_of(j * blk, blk)
        off = ((i - j) * blk).astype(F32) * log2e
        raw = [lax.dot_general(k_ref[pl.ds(j0, blk), cols], q_ref[:, cols], _DN_LAST,
                               preferred_element_type=F32) for cols in heads]
        chosen = [sel_ref[g, pl.ds(j, 1), :] > 0.0 for g in range(hps)]
        acc_old = [acc_ref[g] for g in range(hps)]
        out, acc_new = [], []
        for g in range(hps):
            m_run = carry[g]
            shift = -slopes_ref[hg * hps + g] * off
            s = key_bias(g) + raw[g]
            m_new = jnp.maximum(m_run, jnp.where(chosen[g], jnp.max(s, axis=0, keepdims=True) + shift,
                                                 MASK_VALUE))
            alpha = jnp.exp2(m_run - m_new)
            p = jnp.exp2(s - jnp.where(chosen[g], m_new - shift, -MASK_VALUE))
            acc_new.append(alpha * acc_old[g] + jnp.dot(vt_ref[vrows[g], pl.ds(j0, blk)], p.astype(BF16),
                                                        preferred_element_type=F32))
            out.append(m_new)
        for g in range(hps):
            acc_ref[g] = acc_new[g]
        return tuple(out)

    lax.fori_loop(0, i, body, tuple(carry0))
    for g in range(hps):
        acc = acc_ref[g]
        o_ref[:, g * dh:(g + 1) * dh] = (acc[:dh] / acc[dh:dh + 1]).T.astype(BF16)


MOBA_HEADS_PER_STEP = 8


def _moba(proj, kmean, slopes, batch, seq):
    n = proj.shape[0]
    nb = seq // MOBA_BLOCK
    hps = MOBA_HEADS_PER_STEP
    vt_rows = ATTN_HEAD_DIM + _MOBA_EXTRA_ROWS
    w = hps * ATTN_HEAD_DIM
    ng = N_ATTN_HEADS // hps
    kern = functools.partial(_moba_kernel, nb=nb, hps=hps)
    return pl.pallas_call(
        kern,
        grid=(batch, ng, nb),
        in_specs=[pl.BlockSpec(memory_space=pltpu.SMEM),
                  pl.BlockSpec((MOBA_BLOCK, w), lambda b, h, i: (b * nb + i, h)),
                  pl.BlockSpec((seq, w), lambda b, h, i: (b, ng + h)),
                  pl.BlockSpec((seq, w), lambda b, h, i: (b, 2 * ng + h)),
                  pl.BlockSpec((1, nb, w), lambda b, h, i: (b, 0, h))],
        out_specs=pl.BlockSpec((MOBA_BLOCK, w), lambda b, h, i: (b * nb + i, h)),
        out_shape=jax.ShapeDtypeStruct((n, ATTN_WIDTH), BF16),
        scratch_shapes=[pltpu.VMEM((hps * vt_rows, seq), BF16),
                        pltpu.VMEM((hps, nb, MOBA_BLOCK), F32),
                        pltpu.VMEM((hps, MOBA_BLOCK, LANES), F32),
                        pltpu.VMEM((hps, vt_rows, MOBA_BLOCK), F32)],
        compiler_params=_cparams(("arbitrary", "arbitrary", "arbitrary")),
        name="moba",
    )(slopes, proj, proj, proj, kmean)


def _ret_kernel(lg_ref, q_ref, k_ref, v_ref, g_ref, rnw_ref, o_ref,
                state_ref, dmask_ref, qdec_ref, kdec_ref, cdec_ref):
    n = pl.program_id(1)
    c = RET_CHUNK
    dk = RET_HEAD_DIM
    kscale = dk ** -0.5

    @pl.when(n == 0)
    def _():
        state_ref[...] = jnp.zeros_like(state_ref)
        ti = lax.broadcasted_iota(I32, (c, c), 0)
        si = lax.broadcasted_iota(I32, (c, c), 1)
        diff = jnp.maximum(ti - si, 0).astype(F32)
        pos = lax.broadcasted_iota(I32, (c, dk), 0).astype(F32)
        for h in range(RET_HEADS):
            lg = lg_ref[h]
            dmask_ref[h] = jnp.where(ti >= si, jnp.exp(diff * lg) * kscale, 0.0)
            qdec_ref[h] = jnp.exp((pos + 1.0) * lg)
            kdec_ref[h] = jnp.exp((c - 1.0 - pos) * lg) * kscale
            cdec_ref[h] = jnp.exp(jnp.full((1, dk), float(c), F32) * lg)

    heads = [slice(h * dk, (h + 1) * dk) for h in range(RET_HEADS)]
    qs = [q_ref[:, cols] for cols in heads]
    ks = [k_ref[:, cols] for cols in heads]
    vs = [v_ref[:, cols] for cols in heads]
    states = [state_ref[h] for h in range(RET_HEADS)]
    raw = [lax.dot_general(qs[h], ks[h], _DN_LAST, preferred_element_type=F32) for h in range(RET_HEADS)]
    cross = [jnp.dot(qs[h], states[h].astype(BF16), preferred_element_type=F32) for h in range(RET_HEADS)]
    outs, new_states = [], []
    for h, cols in enumerate(heads):
        inner = jnp.dot((raw[h] * dmask_ref[h]).astype(BF16), vs[h], preferred_element_type=F32)
        kd = (ks[h].astype(F32) * kdec_ref[h]).astype(BF16)
        kv = lax.dot_general(kd, vs[h], _DN_FIRST, preferred_element_type=F32)
        new_states.append(states[h] * cdec_ref[h] + kv)
        r = inner + cross[h] * qdec_ref[h]
        rn = r * lax.rsqrt(jnp.mean(r * r, axis=-1, keepdims=True) + NORM_EPS)
        g = g_ref[:, cols].astype(F32)
        outs.append((rn * rnw_ref[:, cols] * _silu(g)).astype(BF16))
    for h, cols in enumerate(heads):
        state_ref[h] = new_states[h]
        o_ref[:, cols] = outs[h]


def _retention(proj, log_gamma, rnw, batch, seq):
    n = proj.shape[0]
    c = RET_CHUNK
    nc = seq // c
    dk = RET_HEAD_DIM
    base = 3 * ATTN_WIDTH // RET_WIDTH

    def col(off):
        return lambda b, t: (b * nc + t, base + off)

    return pl.pallas_call(
        _ret_kernel,
        grid=(batch, nc),
        in_specs=[pl.BlockSpec(memory_space=pltpu.SMEM),
                  pl.BlockSpec((c, RET_WIDTH), col(0)),
                  pl.BlockSpec((c, RET_WIDTH), col(1)),
                  pl.BlockSpec((c, RET_WIDTH), col(2)),
                  pl.BlockSpec((c, RET_WIDTH), col(3)),
                  pl.BlockSpec((1, RET_WIDTH), lambda b, t: (0, 0))],
        out_specs=pl.BlockSpec((c, RET_WIDTH), lambda b, t: (b * nc + t, 0)),
        out_shape=jax.ShapeDtypeStruct((n, RET_WIDTH), BF16),
        scratch_shapes=[pltpu.VMEM((RET_HEADS, dk, dk), F32), pltpu.VMEM((RET_HEADS, c, c), F32),
                        pltpu.VMEM((RET_HEADS, c, dk), F32), pltpu.VMEM((RET_HEADS, c, dk), F32),
                        pltpu.VMEM((RET_HEADS, 1, dk), F32)],
        compiler_params=_cparams(("arbitrary", "arbitrary")),
        name="retention",
    )(log_gamma, proj, proj, proj, proj, rnw)


def _outproj_kernel(a_ref, r_ref, w_ref, x_ref, mod_ref, nw_ref, wr_ref, br_ref,
                    x1_ref, hp_ref, lg_ref, wb_ref):
    @pl.when(pl.program_id(0) == 0)
    def _():
        wb_ref[...] = w_ref[...].astype(BF16)

    mix = jnp.dot(jnp.concatenate([a_ref[...], r_ref[...]], axis=1), wb_ref[...],
                  preferred_element_type=F32)
    m = mod_ref[0]
    x1 = x_ref[...] + m[2:3] * mix
    x1_ref[...] = x1
    y = x1 * lax.rsqrt(jnp.mean(x1 * x1, axis=-1, keepdims=True) + NORM_EPS) * nw_ref[...]
    h2 = y * (1.0 + m[4:5]) + m[3:4]
    _store_token_major(hp_ref, h2)
    hi = h2.astype(BF16)
    lo = (h2 - hi.astype(F32)).astype(BF16)
    part = jnp.dot(jnp.concatenate([hi, lo], axis=1), wr_ref[...], preferred_element_type=F32)
    lg_ref[...] = part + pltpu.roll(part, LANES // 2, axis=1) + br_ref[...]


def _outproj(attn, ret, w_bf, x2d, mod3, nw, wr_cat, br, seq, tm):
    n, d = x2d.shape
    per_b = seq // tm
    return pl.pallas_call(
        _outproj_kernel,
        grid=(n // tm,),
        in_specs=[pl.BlockSpec((tm, ATTN_WIDTH), lambda i: (i, 0)),
                  pl.BlockSpec((tm, RET_WIDTH), lambda i: (i, 0)),
                  pl.BlockSpec((d, d), lambda i: (0, 0), pipeline_mode=pl.Buffered(1)),
                  pl.BlockSpec((tm, d), lambda i: (i, 0)),
                  pl.BlockSpec((1, N_MOD, d), lambda i: (i // per_b, 0, 0)),
                  pl.BlockSpec((1, d), lambda i: (0, 0)),
                  pl.BlockSpec((2 * d, LANES), lambda i: (0, 0)),
                  pl.BlockSpec((1, LANES), lambda i: (0, 0))],
        out_specs=[pl.BlockSpec((tm, d), lambda i: (i, 0)),
                   pl.BlockSpec((tm * (d // LANES), LANES), lambda i: (i, 0)),
                   pl.BlockSpec((tm, LANES), lambda i: (i, 0))],
        out_shape=[jax.ShapeDtypeStruct((n, d), F32),
                   jax.ShapeDtypeStruct((n * (d // LANES), LANES), F32),
                   jax.ShapeDtypeStruct((n, LANES), F32)],
        scratch_shapes=[pltpu.VMEM((d, d), BF16)],
        compiler_params=_cparams(("arbitrary",)),
        name="outproj",
    )(attn, ret, w_bf, x2d, mod3, nw, wr_cat, br)


_L_E0, _L_E1, _L_W0, _L_W1, _L_D0, _L_D1 = 0, 1, 2, 3, 4, 5
_ROUTER_EXPERT_LANE = 8
_M_TILE_E, _M_NVALID, _M_NTILES, _M_COUNT, _M_START = 0, 1, 2, 3, 4
_PLAN_ROWS = 256
_PLAN_GROUP = 4


def _first_row_where(cond, rows, limit):
    return jnp.min(jnp.where(cond, rows, limit), axis=0, keepdims=True)


def _col_from_row(v):
    r = lax.broadcasted_iota(I32, (LANES, LANES), 0)
    c = lax.broadcasted_iota(I32, (LANES, LANES), 1)
    return jnp.sum(jnp.where(r == c, v, 0.0), axis=1, keepdims=True)


def _row_from_col(v):
    r = lax.broadcasted_iota(I32, (LANES, LANES), 0)
    c = lax.broadcasted_iota(I32, (LANES, LANES), 1)
    return jnp.sum(jnp.where(r == c, v, 0.0), axis=0, keepdims=True)


def _rows_to_tile(rows, row8):
    tile = 0.0
    for k in reversed(range(len(rows))):
        tile = jnp.where(row8 == k, rows[k], tile)
    return tile


def _plan_kernel(lg_ref, info_ref, dest_ref, meta_ref, *, n_tok):
    tb = _PLAN_ROWS
    n_g, n_e = N_GROUPS, EXPERTS_PER_GROUP
    assert n_g <= SUBLANES and n_e == SUBLANES, "group and per-group logits each fill one sublane tile"
    no_row = float(SUBLANES)
    row8 = lax.broadcasted_iota(I32, (SUBLANES, tb), 0).astype(F32)
    row_e = lax.broadcasted_iota(I32, (N_EXPERTS, tb), 0).astype(F32)
    earlier = lax.broadcasted_iota(I32, (tb, tb), 0) < lax.broadcasted_iota(I32, (tb, tb), 1)
    tri = jnp.where(earlier, 1.0, 0.0).astype(BF16)

    def route_block(r0):
        lt = lg_ref[pl.ds(r0, tb), :].T
        gl = jnp.where(row8 < n_g, lt[0:SUBLANES, :], -jnp.inf)
        ge = jnp.exp(gl - jnp.max(gl, axis=0, keepdims=True))
        gp = ge / jnp.sum(ge, axis=0, keepdims=True)
        g_w = jnp.max(gp, axis=0, keepdims=True)
        g_sel = _first_row_where((gp == g_w) & (row8 < n_g), row8, no_row)
        el = lt[_ROUTER_EXPERT_LANE:_ROUTER_EXPERT_LANE + n_e, :]
        for g in range(1, n_g):
            lo = _ROUTER_EXPERT_LANE + g * n_e
            el = jnp.where(g_sel == g, lt[lo:lo + n_e, :], el)
        ee = jnp.exp(el - jnp.max(el, axis=0, keepdims=True))
        ep = ee / jnp.sum(ee, axis=0, keepdims=True)
        p1 = jnp.max(ep, axis=0, keepdims=True)
        i1 = _first_row_where(ep == p1, row8, no_row)
        ep2 = jnp.where(row8 == i1, -1.0, ep)
        p2 = jnp.max(ep2, axis=0, keepdims=True)
        i2 = _first_row_where(ep2 == p2, row8, no_row)
        denom = p1 + p2
        w0 = g_w * p1 / denom
        w1 = g_w * p2 / denom
        e0 = g_sel * n_e + i1
        e1 = g_sel * n_e + i2
        oh0 = jnp.where(row_e == e0, 1.0, 0.0)
        oh1 = jnp.where(row_e == e1, 1.0, 0.0)
        oh = oh0 + oh1
        local = jnp.dot(oh.astype(BF16), tri, preferred_element_type=F32)
        return (e0, e1, w0, w1), oh0, oh1, local, jnp.sum(oh, axis=1, keepdims=True)

    group = _PLAN_GROUP if (n_tok // tb) % _PLAN_GROUP == 0 else 1

    def route(it, carry):
        starts = [pl.multiple_of((it * group + k) * tb, tb) for k in range(group)]
        blocks = [route_block(r0) for r0 in starts]
        tiles = []
        for (e0, e1, w0, w1), oh0, oh1, local, total in blocks:
            before = local + carry
            rank0 = jnp.sum(oh0 * before, axis=0, keepdims=True)
            rank1 = jnp.sum(oh1 * before, axis=0, keepdims=True)
            tiles.append(_rows_to_tile([e0, e1, w0, w1, rank0, rank1], row8))
            carry = carry + total
        for r0, tile in zip(starts, tiles):
            info_ref[:, pl.ds(r0, tb)] = tile
        return carry

    counts_col = lax.fori_loop(0, n_tok // (tb * group), route, jnp.zeros((N_EXPERTS, 1), F32))
    counts = _row_from_col(jnp.concatenate([counts_col, jnp.zeros((LANES - N_EXPERTS, 1), F32)], axis=0))

    lane1 = lax.broadcasted_iota(I32, (1, LANES), 1)
    padded = jnp.floor((counts + (FFN_TILE - 1.0)) * (1.0 / FFN_TILE)) * FFN_TILE
    pad_end = padded
    sh = 1
    while sh < N_EXPERTS:
        pad_end = pad_end + jnp.where(lane1 >= sh, pltpu.roll(pad_end, sh, axis=1), 0.0)
        sh *= 2
    pad_start = pad_end - padded

    start_col = _col_from_row(pad_start)[0:N_EXPERTS, :]

    def place(it, _):
        starts = [pl.multiple_of((it * group + k) * tb, tb) for k in range(group)]
        tiles = [info_ref[:, pl.ds(r0, tb)] for r0 in starts]
        placed = []
        for tile in tiles:
            s0 = jnp.sum(jnp.where(row_e == tile[_L_E0:_L_E0 + 1, :], start_col, 0.0), axis=0, keepdims=True)
            s1 = jnp.sum(jnp.where(row_e == tile[_L_E1:_L_E1 + 1, :], start_col, 0.0), axis=0, keepdims=True)
            placed.append(tile + jnp.where(row8 == _L_D0, s0, jnp.where(row8 == _L_D1, s1, 0.0)))
        for r0, tile in zip(starts, placed):
            info_ref[:, pl.ds(r0, tb)] = tile
            dest_ref[:, pl.ds(r0, tb)] = tile.astype(I32)
        return 0

    lax.fori_loop(0, n_tok // (tb * group), place, 0)

    pe_col = _col_from_row(pad_end)
    mt = meta_ref.shape[1]
    t_start = lax.broadcasted_iota(I32, (LANES, mt), 1).astype(F32) * FFN_TILE
    e_row = lax.broadcasted_iota(I32, (LANES, mt), 0)
    ended = jnp.where((pe_col <= t_start) & (e_row < N_EXPERTS), 1.0, 0.0)
    tile_e = jnp.minimum(jnp.sum(ended, axis=0, keepdims=True), N_EXPERTS - 1.0)
    total = jnp.sum(jnp.where(lane1 == N_EXPERTS - 1, pad_end, 0.0), axis=1, keepdims=True)
    n_valid = jnp.broadcast_to(total * (1.0 / FFN_TILE), (1, mt))
    def per_expert(v):
        return v if mt == LANES else jnp.concatenate([v, jnp.zeros((1, mt - LANES), F32)], axis=1)

    mrow = lax.broadcasted_iota(I32, (SUBLANES, mt), 0)
    meta = jnp.where(mrow == _M_TILE_E, tile_e,
           jnp.where(mrow == _M_NVALID, n_valid,
           jnp.where(mrow == _M_NTILES, per_expert(padded * (1.0 / FFN_TILE)),
           jnp.where(mrow == _M_COUNT, per_expert(counts),
           jnp.where(mrow == _M_START, per_expert(pad_start), 0.0)))))
    meta_ref[...] = meta.astype(I32)


def _plan(logits, n_tiles):
    n_tok = logits.shape[0]
    mt = -(-n_tiles // LANES) * LANES
    kern = functools.partial(_plan_kernel, n_tok=n_tok)
    return pl.pallas_call(
        kern,
        out_shape=[jax.ShapeDtypeStruct((SUBLANES, n_tok), F32),
                   jax.ShapeDtypeStruct((SUBLANES, n_tok), I32),
                   jax.ShapeDtypeStruct((SUBLANES, mt), I32)],
        compiler_params=pltpu.CompilerParams(vmem_limit_bytes=VMEM_LIMIT),
        name="plan",
    )(logits)


_DMA_UNROLL = 8


def _invert_kernel(d0_ref, d1_ref, nv_ref, ntl_ref, cnt_ref, start_ref, tok_ref, *, n_tok, p_rows):
    def zero_rows(lo, hi):
        def zbody(u, _):
            for s in range(_DMA_UNROLL):
                tok_ref[jnp.minimum(lo + u * _DMA_UNROLL + s, hi - 1)] = 0
            return 0
        lax.fori_loop(0, (hi - lo + _DMA_UNROLL - 1) // _DMA_UNROLL, zbody, 0)

    def pad_body(e, _):
        zero_rows(start_ref[e] + cnt_ref[e], start_ref[e] + ntl_ref[e] * FFN_TILE)
        return 0

    lax.fori_loop(0, N_EXPERTS, pad_body, 0)
    zero_rows(nv_ref[0] * FFN_TILE, p_rows)

    def body(u, _):
        t0 = u * _DMA_UNROLL
        rows = [(d0_ref[t0 + s], d1_ref[t0 + s]) for s in range(_DMA_UNROLL)]
        for s, (r0, r1) in enumerate(rows):
            tok_ref[r0] = t0 + s
            tok_ref[r1] = t0 + s
        return 0

    lax.fori_loop(0, n_tok // _DMA_UNROLL, body, 0)


def _invert(d0, d1, n_valid, ntl, cnt, start, p_rows):
    n_tok = d0.shape[0]
    kern = functools.partial(_invert_kernel, n_tok=n_tok, p_rows=p_rows)
    return pl.pallas_call(
        kern,
        grid_spec=pltpu.PrefetchScalarGridSpec(
            num_scalar_prefetch=6,
            grid=(1,),
            in_specs=[],
            out_specs=pl.BlockSpec(memory_space=pltpu.SMEM)),
        out_shape=jax.ShapeDtypeStruct((p_rows,), I32),
        compiler_params=_cparams(("arbitrary",)),
        name="invert",
    )(d0, d1, n_valid, ntl, cnt, start)


_FFN_SLOTS = 3
_BULK_DMA_PRIORITY = 1


def _ffn_kernel(te_ref, nv_ref, ntl_ref, tok_ref, h_ref, w1_ref, w3_ref, w2_ref, y_ref,
                xbuf, xb_ref, w1f, w3f, w2f, w1b, w3b, w2b, gsem, wsem, run_ref, *, n_tiles):
    i = pl.program_id(0)
    nv = nv_ref[0]
    valid = i < nv
    slot = i % _FFN_SLOTS
    nc = xb_ref.shape[1] // LANES

    def row_copy(tile, r, dst_slot):
        tok = tok_ref[tile * FFN_TILE + r]
        src = pl.ds(pl.multiple_of(tok * nc, nc), nc)
        return pltpu.make_async_copy(h_ref.at[src], xbuf.at[dst_slot, pl.ds(r * nc, nc)], gsem.at[dst_slot])

    def weight_copies(e, ws):
        return (pltpu.make_async_copy(w1_ref.at[e], w1f.at[ws], wsem.at[ws]),
                pltpu.make_async_copy(w3_ref.at[e], w3f.at[ws], wsem.at[ws]),
                pltpu.make_async_copy(w2_ref.at[e], w2f.at[ws], wsem.at[ws]))

    @pl.when(i == 0)
    def _():
        run_ref[0] = 0
        for cp in weight_copies(te_ref[0], 0):
            cp.start(priority=_BULK_DMA_PRIORITY)

        def first_rows(u, _):
            for s in range(_DMA_UNROLL):
                for t in range(_FFN_SLOTS - 1):
                    row_copy(t, u * _DMA_UNROLL + s, t).start()
            return 0
        lax.fori_loop(0, FFN_TILE // _DMA_UNROLL, first_rows, 0)

    @pl.when(valid & ((i == 0) | (te_ref[i] != te_ref[jnp.maximum(i, 1) - 1])))
    def _():
        run = run_ref[0]
        ws = run % 2
        for cp in weight_copies(te_ref[i], ws):
            cp.wait()
        w1b[...] = w1f[ws].astype(BF16)
        w3b[...] = w3f[ws].astype(BF16)
        w2b[...] = w2f[ws].astype(BF16)
        nxt = i + ntl_ref[te_ref[i]]

        @pl.when(nxt < nv)
        def _():
            for cp in weight_copies(te_ref[jnp.minimum(nxt, n_tiles - 1)], 1 - ws):
                cp.start(priority=_BULK_DMA_PRIORITY)
        run_ref[0] = run + 1

    @pl.when(valid)
    def _():
        pltpu.make_async_copy(h_ref.at[pl.ds(0, FFN_TILE * nc)], xbuf.at[slot], gsem.at[slot]).wait()
        xb_ref[...] = _load_token_major(xbuf.at[slot], FFN_TILE, xb_ref.shape[1], BF16)
        ahead = _FFN_SLOTS - 1
        nxt_tile = jnp.minimum(i + ahead, n_tiles - 1)
        nxt_slot = (i + ahead) % _FFN_SLOTS
        for r in range(FFN_TILE):
            row_copy(nxt_tile, r, nxt_slot).start()
        x = xb_ref[...]
        a = jnp.dot(x, w1b[...], preferred_element_type=F32)
        b = jnp.dot(x, w3b[...], preferred_element_type=F32)
        hmid = (_silu(a) * b).astype(BF16)
        y_ref[...] = jnp.dot(hmid, w2b[...], preferred_element_type=F32)

    @pl.when(jnp.logical_not(valid))
    def _():
        y_ref[...] = jnp.zeros_like(y_ref)

    @pl.when(i == n_tiles - 1)
    def _():
        for t in range(_FFN_SLOTS - 1):
            s = (nv + t) % _FFN_SLOTS
            pltpu.make_async_copy(h_ref.at[pl.ds(0, FFN_TILE * nc)], xbuf.at[s], gsem.at[s]).wait()


def _ffn(tile_e, n_valid, ntl, row_tok, h2, w1, w3, w2):
    p_rows = row_tok.shape[0]
    _, d, de = w1.shape
    n_tiles = p_rows // FFN_TILE
    kern = functools.partial(_ffn_kernel, n_tiles=n_tiles)
    any_spec = pl.BlockSpec(memory_space=pl.ANY)
    return pl.pallas_call(
        kern,
        grid_spec=pltpu.PrefetchScalarGridSpec(
            num_scalar_prefetch=4,
            grid=(n_tiles,),
            in_specs=[any_spec, any_spec, any_spec, any_spec],
            out_specs=pl.BlockSpec((FFN_TILE, d), lambda i, *_: (i, 0)),
            scratch_shapes=[pltpu.VMEM((_FFN_SLOTS, FFN_TILE * (d // LANES), LANES), F32),
                            pltpu.VMEM((FFN_TILE, d), BF16),
                            pltpu.VMEM((2, d, de), F32), pltpu.VMEM((2, d, de), F32),
                            pltpu.VMEM((2, de, d), F32),
                            pltpu.VMEM((d, de), BF16), pltpu.VMEM((d, de), BF16),
                            pltpu.VMEM((de, d), BF16),
                            pltpu.SemaphoreType.DMA((_FFN_SLOTS,)), pltpu.SemaphoreType.DMA((2,)),
                            pltpu.SMEM((1,), I32)]),
        out_shape=jax.ShapeDtypeStruct((p_rows, d), F32),
        compiler_params=_cparams(("arbitrary",)),
        name="ffn",
    )(tile_e, n_valid, ntl, row_tok, h2, w1, w3, w2)


def _combine_kernel(d0_ref, d1_ref, x1_ref, info_ref, mod_ref, nw_ref, y_ref, o_ref,
                    *scratch, tp, n_phases):
    i = pl.program_id(0)
    ns = _COMBINE_SETS
    sem = scratch[-1]
    bufs = [(scratch[2 * q], scratch[2 * q + 1]) for q in range(ns)]

    def row_copies(phase, r, q):
        tok = phase * tp + r
        ya, yb = bufs[q]
        return (pltpu.make_async_copy(y_ref.at[pl.ds(d0_ref[tok], 1)], ya.at[pl.ds(r, 1)], sem.at[q]),
                pltpu.make_async_copy(y_ref.at[pl.ds(d1_ref[tok], 1)], yb.at[pl.ds(r, 1)], sem.at[q]))

    def wait_set(q):
        for buf in bufs[q]:
            pltpu.make_async_copy(y_ref.at[pl.ds(0, tp)], buf, sem.at[q]).wait()

    @pl.when(i == 0)
    def _():
        def first_rows(u, _):
            for s in range(_DMA_UNROLL):
                for q in range(ns - 1):
                    for cp in row_copies(q, u * _DMA_UNROLL + s, q):
                        cp.start()
            return 0
        lax.fori_loop(0, tp // _DMA_UNROLL, first_rows, 0)

    for q in range(ns):
        wait_set(q)
        ahead = jnp.minimum(i * ns + q + ns - 1, n_phases - 1)
        for r in range(tp):
            for prio, cp in enumerate(row_copies(ahead, r, (q + ns - 1) % ns)):
                cp.start(priority=prio)
        ya, yb = bufs[q]
        rows = pl.ds(q * tp, tp)
        w0 = _col_from_row(info_ref[_L_W0:_L_W0 + 1, q * tp:(q + 1) * tp])
        w1 = _col_from_row(info_ref[_L_W1:_L_W1 + 1, q * tp:(q + 1) * tp])
        moe = ya[...] * w0 + yb[...] * w1
        x2 = x1_ref[rows, :] + mod_ref[0][5:6] * moe
        o_ref[rows, :] = x2 * lax.rsqrt(jnp.mean(x2 * x2, axis=-1, keepdims=True) + NORM_EPS) * nw_ref[...]

    @pl.when(i == n_phases // ns - 1)
    def _():
        for q in range(ns - 1):
            wait_set(q)


_COMBINE_SETS = 4


def _combine(d0, d1, x1, info, mod3, nfw, y, seq, tp=LANES):
    assert tp == LANES, "one phase's combine weights must fill exactly one lane-dense row"
    n, d = x1.shape
    tc = _COMBINE_SETS * tp
    per_b = seq // tc
    kern = functools.partial(_combine_kernel, tp=tp, n_phases=n // tp)
    return pl.pallas_call(
        kern,
        grid_spec=pltpu.PrefetchScalarGridSpec(
            num_scalar_prefetch=2,
            grid=(n // tc,),
            in_specs=[pl.BlockSpec((tc, d), lambda i, d0, d1: (i, 0)),
                      pl.BlockSpec((SUBLANES, tc), lambda i, d0, d1: (0, i)),
                      pl.BlockSpec((1, N_MOD, d), lambda i, d0, d1: (i // per_b, 0, 0)),
                      pl.BlockSpec((1, d), lambda i, d0, d1: (0, 0)),
                      pl.BlockSpec(memory_space=pl.ANY)],
            out_specs=pl.BlockSpec((tc, d), lambda i, d0, d1: (i, 0)),
            scratch_shapes=[pltpu.VMEM((tp, d), F32)] * (2 * _COMBINE_SETS)
                           + [pltpu.SemaphoreType.DMA((_COMBINE_SETS,))]),
        out_shape=jax.ShapeDtypeStruct((n, d), F32),
        compiler_params=_cparams(("arbitrary",)),
        name="combine",
    )(d0, d1, x1, info, mod3, nfw, y)


def _split_hi_lo(w):
    hi = w.astype(BF16)
    lo = (w - hi.astype(F32)).astype(BF16)
    return hi, lo


def kernel(x, c, w_ada, b_ada, norm_mix_w, w_in, ret_norm_w, w_out, norm_ffn_w,
           w_group, b_group, w_router, b_router, w1, w3, w2, norm_final_w):
    batch, seq, d = x.shape
    n = batch * seq
    depth = w_ada.shape[0]
    assert depth == 1, "the final rmsnorm is fused into the (single) layer's last kernel"
    tm = min(INPROJ_ROWS, seq)
    half = LANES // 2

    slopes = jnp.exp2(-8.0 * jnp.arange(1, N_ATTN_HEADS + 1, dtype=F32) / N_ATTN_HEADS)
    log_gamma = jnp.log1p(-jnp.exp2(-5.0 - jnp.arange(RET_HEADS, dtype=F32)))
    n_tiles = (2 * n) // FFN_TILE + N_EXPERTS
    p_rows = n_tiles * FFN_TILE

    assert batch <= SUBLANES
    c_pad = jnp.zeros((SUBLANES, d), F32).at[:batch].set(c)
    x2d = x.reshape(n, d)
    for l in range(depth):
        mod = _adaln(c_pad, w_ada[l], b_ada[l].reshape(1, -1))
        mod3 = mod[:batch].reshape(batch, N_MOD, d)

        proj, kmean = _inproj(x2d, mod3, norm_mix_w[l].reshape(1, d), w_in[l], seq, tm)
        kmean = kmean.reshape(batch, seq // MOBA_BLOCK, ATTN_WIDTH)
        attn = _moba(proj, kmean, slopes, batch, seq)
        ret = _retention(proj, log_gamma, ret_norm_w[l].reshape(1, RET_WIDTH), batch, seq)

        gap = _ROUTER_EXPERT_LANE - N_GROUPS
        wr = jnp.concatenate([w_group[l], jnp.zeros((d, gap), F32),
                              jnp.transpose(w_router[l], (1, 0, 2)).reshape(d, N_EXPERTS)], axis=1)
        wr = jnp.pad(wr, ((0, 0), (0, half - wr.shape[1])))
        wr_hi, wr_lo = _split_hi_lo(wr)
        wr_cat = jnp.concatenate([jnp.concatenate([wr_hi, wr_lo], axis=1),
                                  jnp.concatenate([wr_hi, jnp.zeros_like(wr_lo)], axis=1)], axis=0)
        br = jnp.concatenate([b_group[l], jnp.zeros((gap,), F32), b_router[l].reshape(-1)])
        br = jnp.pad(br, (0, LANES - br.shape[0])).reshape(1, LANES)

        x1, h2p, logits = _outproj(attn, ret, w_out[l], x2d, mod3,
                                   norm_ffn_w[l].reshape(1, d), wr_cat, br, seq, min(OUTPROJ_ROWS, seq))

        info, dest, meta = _plan(logits, n_tiles)
        d0 = dest[_L_D0]
        d1 = dest[_L_D1]
        tile_e = meta[_M_TILE_E, :n_tiles]
        n_valid = meta[_M_NVALID, :1]
        ntl = meta[_M_NTILES, :N_EXPERTS]

        row_tok = _invert(d0, d1, n_valid, ntl, meta[_M_COUNT, :N_EXPERTS],
                          meta[_M_START, :N_EXPERTS], p_rows)
        y = _ffn(tile_e, n_valid, ntl, row_tok, h2p, w1[l], w3[l], w2[l])
        nfw = norm_final_w.reshape(1, d)
        x2d = _combine(d0, d1, x1, info, mod3, nfw, y, seq)
    return x2d.reshape(batch, seq, d)
```

```python
import functools

import jax
import jax.numpy as jnp
from jax import lax
from jax.experimental import pallas as pl
from jax.experimental.pallas import tpu as pltpu

F32 = jnp.float32
BF16 = jnp.bfloat16
I32 = jnp.int32

N_ATTN_HEADS = 8
ATTN_HEAD_DIM = 128
ATTN_WIDTH = N_ATTN_HEADS * ATTN_HEAD_DIM
MOBA_BLOCK = 256
MOBA_TOPK = 3
RET_HEADS = 4
RET_HEAD_DIM = 256
RET_WIDTH = RET_HEADS * RET_HEAD_DIM
N_GROUPS = 4
EXPERTS_PER_GROUP = 8
N_EXPERTS = N_GROUPS * EXPERTS_PER_GROUP
N_MOD = 6
NORM_EPS = 1e-6

LANES = 128
SUBLANES = 8
RET_CHUNK = 256
FFN_TILE = 256
INPROJ_ROWS = 1024
OUTPROJ_ROWS = 256
MASK_VALUE = -1e30
LOG2E = 1.4426950408889634
MOBA_QK_SCALE = ATTN_HEAD_DIM ** -0.5 * LOG2E
_MOBA_EXTRA_ROWS = 16
VMEM_LIMIT = 56 * 1024 * 1024

_DN_LAST = (((1,), (1,)), ((), ()))
_DN_FIRST = (((0,), (0,)), ((), ()))


def _cparams(sem):
    return pltpu.CompilerParams(dimension_semantics=sem, vmem_limit_bytes=VMEM_LIMIT)


def _silu(v):
    return v * jax.nn.sigmoid(v)


def _token_pitch(d):
    return d // LANES + 1


def _store_token_major(ref, value):
    rows, d = value.shape
    pitch = _token_pitch(d)
    for c in range(pitch - 1):
        ref[pl.ds(c, rows, stride=pitch), :] = value[:, c * LANES:(c + 1) * LANES]
    ref[pl.ds(pitch - 1, rows, stride=pitch), :] = jnp.zeros((rows, LANES), value.dtype)


def _load_token_major(ref, rows, d, dtype):
    pitch = _token_pitch(d)
    return jnp.concatenate([ref[pl.ds(c, rows, stride=pitch), :].astype(dtype) for c in range(pitch - 1)],
                           axis=1)


def _adaln_kernel(c_ref, w_ref, b_ref, o_ref):
    ca = _silu(c_ref[...]).astype(BF16)
    o_ref[...] = jnp.dot(ca, w_ref[...].astype(BF16), preferred_element_type=F32) + b_ref[...]


def _adaln(c_pad, w_ada, b_ada, tn=1024):
    rows, d = c_pad.shape
    n = w_ada.shape[1]
    return pl.pallas_call(
        _adaln_kernel,
        grid=(n // tn,),
        in_specs=[pl.BlockSpec((rows, d), lambda j: (0, 0)),
                  pl.BlockSpec((d, tn), lambda j: (0, j)),
                  pl.BlockSpec((1, tn), lambda j: (0, j))],
        out_specs=pl.BlockSpec((rows, tn), lambda j: (0, j)),
        out_shape=jax.ShapeDtypeStruct((rows, n), F32),
        compiler_params=_cparams(("arbitrary",)),
        name="adaln",
    )(c_pad, w_ada, b_ada)


def _inproj_kernel(x_ref, mod_ref, nw_ref, w_ref, o_ref, km_ref, h_ref, *, tm, tn):
    j = pl.program_id(1)

    @pl.when(j == 0)
    def _():
        x = x_ref[...]
        y = x * lax.rsqrt(jnp.mean(x * x, axis=-1, keepdims=True) + NORM_EPS) * nw_ref[...]
        m = mod_ref[0]
        h_ref[...] = (y * (1.0 + m[1:2]) + m[0:1]).astype(BF16)

    acc = jnp.dot(h_ref[...], w_ref[...].astype(BF16), preferred_element_type=F32)
    o_ref[...] = (acc * jnp.where(j == 0, MOBA_QK_SCALE, 1.0)).astype(BF16)

    @pl.when(j == 1)
    def _():
        km_ref[0] = jnp.sum(acc.reshape(tm // MOBA_BLOCK, MOBA_BLOCK, tn), axis=1) * (1.0 / MOBA_BLOCK)


def _inproj(x2d, mod3, nw, w_bf, seq, tm, tn=ATTN_WIDTH):
    n, d = x2d.shape
    width = w_bf.shape[1]
    per_b = seq // tm
    kern = functools.partial(_inproj_kernel, tm=tm, tn=tn)
    return pl.pallas_call(
        kern,
        grid=(n // tm, width // tn),
        in_specs=[pl.BlockSpec((tm, d), lambda i, j: (i, 0)),
                  pl.BlockSpec((1, N_MOD, d), lambda i, j: (i // per_b, 0, 0)),
                  pl.BlockSpec((1, d), lambda i, j: (0, 0)),
                  pl.BlockSpec((d, tn), lambda i, j: (0, j))],
        out_specs=[pl.BlockSpec((tm, tn), lambda i, j: (i, j)),
                   pl.BlockSpec((1, tm // MOBA_BLOCK, tn), lambda i, j: (i, 0, 0))],
        out_shape=[jax.ShapeDtypeStruct((n, width), BF16),
                   jax.ShapeDtypeStruct((n // tm, tm // MOBA_BLOCK, tn), F32)],
        scratch_shapes=[pltpu.VMEM((tm, d), BF16)],
        compiler_params=_cparams(("arbitrary", "arbitrary")),
        name="inproj",
    )(x2d, mod3, nw, w_bf)


def _moba_kernel(slopes_ref, q_ref, k_ref, v_ref, km_ref, o_ref,
                 vt_ref, sel_ref, bias_ref, acc_ref, *, nb, hps):
    hg = pl.program_id(1)
    i = pl.program_id(2)
    blk = MOBA_BLOCK
    dh = ATTN_HEAD_DIM
    log2e = LOG2E

    qpos = lax.broadcasted_iota(I32, (blk, blk), 1)
    kpos = lax.broadcasted_iota(I32, (blk, blk), 0)

    vt_rows = dh + _MOBA_EXTRA_ROWS

    @pl.when(i == 0)
    def _():
        vt = v_ref[...].astype(F32).T.astype(BF16)
        extra = lax.broadcasted_iota(I32, (_MOBA_EXTRA_ROWS, vt.shape[1]), 0)
        ones_row = jnp.where(extra == 0, 1.0, 0.0).astype(BF16)
        key_pos = lax.broadcasted_iota(I32, (blk, LANES), 0).astype(F32)
        for g in range(hps):
            vt_ref[g * vt_rows:g * vt_rows + dh, :] = vt[g * dh:(g + 1) * dh, :]
            vt_ref[g * vt_rows + dh:(g + 1) * vt_rows, :] = ones_row
            bias_ref[g] = key_pos * (log2e * slopes_ref[hg * hps + g])

    bidx = lax.broadcasted_iota(I32, (nb, blk), 0)
    past = bidx < i
    i0 = pl.multiple_of(i * blk, blk)
    heads = [slice(g * dh, (g + 1) * dh) for g in range(hps)]
    vrows = [slice(g * vt_rows, (g + 1) * vt_rows) for g in range(hps)]
    gates, raw_own = [], []
    for cols in heads:
        q = q_ref[:, cols]
        km = km_ref[0, :, cols]
        km_hi = km.astype(BF16)
        km_lo = (km - km_hi.astype(F32)).astype(BF16)
        gates.append(lax.dot_general(km_hi, q, _DN_LAST, preferred_element_type=F32)
                     + lax.dot_general(km_lo, q, _DN_LAST, preferred_element_type=F32))
        raw_own.append(lax.dot_general(k_ref[pl.ds(i0, blk), cols], q, _DN_LAST,
                                       preferred_element_type=F32))
    def key_bias(g):
        kb = bias_ref[g]
        return jnp.concatenate([kb] * (blk // LANES), axis=1)

    own_bias = [key_bias(g) for g in range(hps)]

    carry0, acc0, sels = [], [], []
    for g, cols in enumerate(heads):
        gate = jnp.where(past, gates[g], -jnp.inf)
        rank = jnp.zeros((nb, blk), F32)
        for m in range(nb):
            gm = gate[m:m + 1, :]
            beats = jnp.where(gm > gate, 1.0, jnp.where((gm == gate) & (m < bidx), 1.0, 0.0))
            rank = rank + jnp.where(m < i, beats, 0.0)
        sels.append(jnp.where(past & (rank < MOBA_TOPK), 1.0, 0.0))

        s = jnp.where(qpos >= kpos, own_bias[g] + raw_own[g], MASK_VALUE)
        m0 = jnp.max(s, axis=0, keepdims=True)
        p = jnp.exp2(s - m0)
        acc0.append(jnp.dot(vt_ref[vrows[g], pl.ds(i0, blk)], p.astype(BF16), preferred_element_type=F32))
        carry0.append(m0)
    for g in range(hps):
        acc_ref[g] = acc0[g]
        sel_ref[g] = sels[g]

    def body(j, carry):
        j0 = pl.multiple_of(j * blk, blk)
        off = ((i - j) * blk).astype(F32) * log2e
        raw = [lax.dot_general(k_ref[pl.ds(j0, blk), cols], q_ref[:, cols], _DN_LAST,
                               preferred_element_type=F32) for cols in heads]
        chosen = [sel_ref[g, pl.ds(j, 1), :] > 0.0 for g in range(hps)]
        acc_old = [acc_ref[g] for g in range(hps)]
        out, acc_new = [], []
        for g in range(hps):
            m_run = carry[g]
            shift = -slopes_ref[hg * hps + g] * off
            s = key_bias(g) + raw[g]
            m_new = jnp.maximum(m_run, jnp.where(chosen[g], jnp.max(s, axis=0, keepdims=True) + shift,
                                                 MASK_VALUE))
            alpha = jnp.exp2(m_run - m_new)
            p = jnp.exp2(s - jnp.where(chosen[g], m_new - shift, -MASK_VALUE))
            acc_new.append(alpha * acc_old[g] + jnp.dot(vt_ref[vrows[g], pl.ds(j0, blk)], p.astype(BF16),
                                                        preferred_element_type=F32))
            out.append(m_new)
        for g in range(hps):
            acc_ref[g] = acc_new[g]
        return tuple(out)

    lax.fori_loop(0, i, body, tuple(carry0))
    for g in range(hps):
        acc = acc_ref[g]
        o_ref[:, g * dh:(g + 1) * dh] = (acc[:dh] / acc[dh:dh + 1]).T.astype(BF16)


MOBA_HEADS_PER_STEP = 8


def _moba(proj, kmean, slopes, batch, seq):
    n = proj.shape[0]
    nb = seq // MOBA_BLOCK
    hps = MOBA_HEADS_PER_STEP
    vt_rows = ATTN_HEAD_DIM + _MOBA_EXTRA_ROWS
    w = hps * ATTN_HEAD_DIM
    ng = N_ATTN_HEADS // hps
    kern = functools.partial(_moba_kernel, nb=nb, hps=hps)
    return pl.pallas_call(
        kern,
        grid=(batch, ng, nb),
        in_specs=[pl.BlockSpec(memory_space=pltpu.SMEM),
                  pl.BlockSpec((MOBA_BLOCK, w), lambda b, h, i: (b * nb + i, h)),
                  pl.BlockSpec((seq, w), lambda b, h, i: (b, ng + h)),
                  pl.BlockSpec((seq, w), lambda b, h, i: (b, 2 * ng + h)),
                  pl.BlockSpec((1, nb, w), lambda b, h, i: (b, 0, h))],
        out_specs=pl.BlockSpec((MOBA_BLOCK, w), lambda b, h, i: (b * nb + i, h)),
        out_shape=jax.ShapeDtypeStruct((n, ATTN_WIDTH), BF16),
        scratch_shapes=[pltpu.VMEM((hps * vt_rows, seq), BF16),
                        pltpu.VMEM((hps, nb, MOBA_BLOCK), F32),
                        pltpu.VMEM((hps, MOBA_BLOCK, LANES), F32),
                        pltpu.VMEM((hps, vt_rows, MOBA_BLOCK), F32)],
        compiler_params=_cparams(("arbitrary", "arbitrary", "arbitrary")),
        name="moba",
    )(slopes, proj, proj, proj, kmean)


def _ret_kernel(lg_ref, q_ref, k_ref, v_ref, g_ref, rnw_ref, o_ref,
                state_ref, dmask_ref, qdec_ref, kdec_ref, cdec_ref):
    n = pl.program_id(1)
    c = RET_CHUNK
    dk = RET_HEAD_DIM
    kscale = dk ** -0.5

    @pl.when(n == 0)
    def _():
        state_ref[...] = jnp.zeros_like(state_ref)
        ti = lax.broadcasted_iota(I32, (c, c), 0)
        si = lax.broadcasted_iota(I32, (c, c), 1)
        diff = jnp.maximum(ti - si, 0).astype(F32)
        pos = lax.broadcasted_iota(I32, (c, dk), 0).astype(F32)
        for h in range(RET_HEADS):
            lg = lg_ref[h]
            dmask_ref[h] = jnp.where(ti >= si, jnp.exp(diff * lg) * kscale, 0.0)
            qdec_ref[h] = jnp.exp((pos + 1.0) * lg)
            kdec_ref[h] = jnp.exp((c - 1.0 - pos) * lg) * kscale
            cdec_ref[h] = jnp.exp(jnp.full((1, dk), float(c), F32) * lg)

    heads = [slice(h * dk, (h + 1) * dk) for h in range(RET_HEADS)]
    qs = [q_ref[:, cols] for cols in heads]
    ks = [k_ref[:, cols] for cols in heads]
    vs = [v_ref[:, cols] for cols in heads]
    states = [state_ref[h] for h in range(RET_HEADS)]
    raw = [lax.dot_general(qs[h], ks[h], _DN_LAST, preferred_element_type=F32) for h in range(RET_HEADS)]
    cross = [jnp.dot(qs[h], states[h].astype(BF16), preferred_element_type=F32) for h in range(RET_HEADS)]
    outs, new_states = [], []
    for h, cols in enumerate(heads):
        inner = jnp.dot((raw[h] * dmask_ref[h]).astype(BF16), vs[h], preferred_element_type=F32)
        kd = (ks[h].astype(F32) * kdec_ref[h]).astype(BF16)
        kv = lax.dot_general(kd, vs[h], _DN_FIRST, preferred_element_type=F32)
        new_states.append(states[h] * cdec_ref[h] + kv)
        r = inner + cross[h] * qdec_ref[h]
        rn = r * lax.rsqrt(jnp.mean(r * r, axis=-1, keepdims=True) + NORM_EPS)
        g = g_ref[:, cols].astype(F32)
        outs.append((rn * rnw_ref[:, cols] * _silu(g)).astype(BF16))
    for h, cols in enumerate(heads):
        state_ref[h] = new_states[h]
        o_ref[:, cols] = outs[h]


def _retention(proj, log_gamma, rnw, batch, seq):
    n = proj.shape[0]
    c = RET_CHUNK
    nc = seq // c
    dk = RET_HEAD_DIM
    base = 3 * ATTN_WIDTH // RET_WIDTH

    def col(off):
        return lambda b, t: (b * nc + t, base + off)

    return pl.pallas_call(
        _ret_kernel,
        grid=(batch, nc),
        in_specs=[pl.BlockSpec(memory_space=pltpu.SMEM),
                  pl.BlockSpec((c, RET_WIDTH), col(0)),
                  pl.BlockSpec((c, RET_WIDTH), col(1)),
                  pl.BlockSpec((c, RET_WIDTH), col(2)),
                  pl.BlockSpec((c, RET_WIDTH), col(3)),
                  pl.BlockSpec((1, RET_WIDTH), lambda b, t: (0, 0))],
        out_specs=pl.BlockSpec((c, RET_WIDTH), lambda b, t: (b * nc + t, 0)),
        out_shape=jax.ShapeDtypeStruct((n, RET_WIDTH), BF16),
        scratch_shapes=[pltpu.VMEM((RET_HEADS, dk, dk), F32), pltpu.VMEM((RET_HEADS, c, c), F32),
                        pltpu.VMEM((RET_HEADS, c, dk), F32), pltpu.VMEM((RET_HEADS, c, dk), F32),
                        pltpu.VMEM((RET_HEADS, 1, dk), F32)],
        compiler_params=_cparams(("arbitrary", "arbitrary")),
        name="retention",
    )(log_gamma, proj, proj, proj, proj, rnw)


def _outproj_kernel(a_ref, r_ref, w_ref, x_ref, mod_ref, nw_ref, wr_ref, br_ref,
                    x1_ref, hp_ref, lg_ref, wb_ref):
    @pl.when(pl.program_id(0) == 0)
    def _():
        wb_ref[...] = w_ref[...].astype(BF16)

    mix = jnp.dot(jnp.concatenate([a_ref[...], r_ref[...]], axis=1), wb_ref[...],
                  preferred_element_type=F32)
    m = mod_ref[0]
    x1 = x_ref[...] + m[2:3] * mix
    x1_ref[...] = x1
    y = x1 * lax.rsqrt(jnp.mean(x1 * x1, axis=-1, keepdims=True) + NORM_EPS) * nw_ref[...]
    h2 = y * (1.0 + m[4:5]) + m[3:4]
    _store_token_major(hp_ref, h2)
    hi = h2.astype(BF16)
    lo = (h2 - hi.astype(F32)).astype(BF16)
    part = jnp.dot(jnp.concatenate([hi, lo], axis=1), wr_ref[...], preferred_element_type=F32)
    lg_ref[...] = part + pltpu.roll(part, LANES // 2, axis=1) + br_ref[...]


def _outproj(attn, ret, w_bf, x2d, mod3, nw, wr_cat, br, seq, tm):
    n, d = x2d.shape
    per_b = seq // tm
    return pl.pallas_call(
        _outproj_kernel,
        grid=(n // tm,),
        in_specs=[pl.BlockSpec((tm, ATTN_WIDTH), lambda i: (i, 0)),
                  pl.BlockSpec((tm, RET_WIDTH), lambda i: (i, 0)),
                  pl.BlockSpec((d, d), lambda i: (0, 0), pipeline_mode=pl.Buffered(1)),
                  pl.BlockSpec((tm, d), lambda i: (i, 0)),
                  pl.BlockSpec((1, N_MOD, d), lambda i: (i // per_b, 0, 0)),
                  pl.BlockSpec((1, d), lambda i: (0, 0)),
                  pl.BlockSpec((2 * d, LANES), lambda i: (0, 0)),
                  pl.BlockSpec((1, LANES), lambda i: (0, 0))],
        out_specs=[pl.BlockSpec((tm, d), lambda i: (i, 0)),
                   pl.BlockSpec((tm * _token_pitch(d), LANES), lambda i: (i, 0)),
                   pl.BlockSpec((tm, LANES), lambda i: (i, 0))],
        out_shape=[jax.ShapeDtypeStruct((n, d), F32),
                   jax.ShapeDtypeStruct((n * _token_pitch(d), LANES), F32),
                   jax.ShapeDtypeStruct((n, LANES), F32)],
        scratch_shapes=[pltpu.VMEM((d, d), BF16)],
        compiler_params=_cparams(("arbitrary",)),
        name="outproj",
    )(attn, ret, w_bf, x2d, mod3, nw, wr_cat, br)


_L_E0, _L_E1, _L_W0, _L_W1, _L_D0, _L_D1 = 0, 1, 2, 3, 4, 5
_ROUTER_EXPERT_LANE = 8
_M_TILE_E, _M_NVALID, _M_NTILES, _M_COUNT, _M_START = 0, 1, 2, 3, 4
_PLAN_ROWS = 256
_PLAN_GROUP = 4


def _first_row_where(cond, rows, limit):
    return jnp.min(jnp.where(cond, rows, limit), axis=0, keepdims=True)


def _col_from_row(v):
    r = lax.broadcasted_iota(I32, (LANES, LANES), 0)
    c = lax.broadcasted_iota(I32, (LANES, LANES), 1)
    return jnp.sum(jnp.where(r == c, v, 0.0), axis=1, keepdims=True)


def _row_from_col(v):
    r = lax.broadcasted_iota(I32, (LANES, LANES), 0)
    c = lax.broadcasted_iota(I32, (LANES, LANES), 1)
    return jnp.sum(jnp.where(r == c, v, 0.0), axis=0, keepdims=True)


def _rows_to_tile(rows, row8):
    tile = 0.0
    for k in reversed(range(len(rows))):
        tile = jnp.where(row8 == k, rows[k], tile)
    return tile


def _plan_kernel(lg_ref, info_ref, dest_ref, meta_ref, *, n_tok):
    tb = _PLAN_ROWS
    n_g, n_e = N_GROUPS, EXPERTS_PER_GROUP
    assert n_g <= SUBLANES and n_e == SUBLANES, "group and per-group logits each fill one sublane tile"
    no_row = float(SUBLANES)
    row8 = lax.broadcasted_iota(I32, (SUBLANES, tb), 0).astype(F32)
    row_e = lax.broadcasted_iota(I32, (N_EXPERTS, tb), 0).astype(F32)
    earlier = lax.broadcasted_iota(I32, (tb, tb), 0) < lax.broadcasted_iota(I32, (tb, tb), 1)
    tri = jnp.where(earlier, 1.0, 0.0).astype(BF16)

    def route_block(r0):
        lt = lg_ref[pl.ds(r0, tb), :].T
        gl = jnp.where(row8 < n_g, lt[0:SUBLANES, :], -jnp.inf)
        ge = jnp.exp(gl - jnp.max(gl, axis=0, keepdims=True))
        gp = ge / jnp.sum(ge, axis=0, keepdims=True)
        g_w = jnp.max(gp, axis=0, keepdims=True)
        g_sel = _first_row_where((gp == g_w) & (row8 < n_g), row8, no_row)
        el = lt[_ROUTER_EXPERT_LANE:_ROUTER_EXPERT_LANE + n_e, :]
        for g in range(1, n_g):
            lo = _ROUTER_EXPERT_LANE + g * n_e
            el = jnp.where(g_sel == g, lt[lo:lo + n_e, :], el)
        ee = jnp.exp(el - jnp.max(el, axis=0, keepdims=True))
        ep = ee / jnp.sum(ee, axis=0, keepdims=True)
        p1 = jnp.max(ep, axis=0, keepdims=True)
        i1 = _first_row_where(ep == p1, row8, no_row)
        ep2 = jnp.where(row8 == i1, -1.0, ep)
        p2 = jnp.max(ep2, axis=0, keepdims=True)
        i2 = _first_row_where(ep2 == p2, row8, no_row)
        denom = p1 + p2
        w0 = g_w * p1 / denom
        w1 = g_w * p2 / denom
        e0 = g_sel * n_e + i1
        e1 = g_sel * n_e + i2
        oh0 = jnp.where(row_e == e0, 1.0, 0.0)
        oh1 = jnp.where(row_e == e1, 1.0, 0.0)
        oh = oh0 + oh1
        local = jnp.dot(oh.astype(BF16), tri, preferred_element_type=F32)
        return (e0, e1, w0, w1), oh0, oh1, local, jnp.sum(oh, axis=1, keepdims=True)

    group = _PLAN_GROUP if (n_tok // tb) % _PLAN_GROUP == 0 else 1

    def route(it, carry):
        starts = [pl.multiple_of((it * group + k) * tb, tb) for k in range(group)]
        blocks = [route_block(r0) for r0 in starts]
        tiles = []
        for (e0, e1, w0, w1), oh0, oh1, local, total in blocks:
            before = local + carry
            rank0 = jnp.sum(oh0 * before, axis=0, keepdims=True)
            rank1 = jnp.sum(oh1 * before, axis=0, keepdims=True)
            tiles.append(_rows_to_tile([e0, e1, w0, w1, rank0, rank1], row8))
            carry = carry + total
        for r0, tile in zip(starts, tiles):
            info_ref[:, pl.ds(r0, tb)] = tile
        return carry

    counts_col = lax.fori_loop(0, n_tok // (tb * group), route, jnp.zeros((N_EXPERTS, 1), F32))
    counts = _row_from_col(jnp.concatenate([counts_col, jnp.zeros((LANES - N_EXPERTS, 1), F32)], axis=0))

    lane1 = lax.broadcasted_iota(I32, (1, LANES), 1)
    padded = jnp.floor((counts + (FFN_TILE - 1.0)) * (1.0 / FFN_TILE)) * FFN_TILE
    pad_end = padded
    sh = 1
    while sh < N_EXPERTS:
        pad_end = pad_end + jnp.where(lane1 >= sh, pltpu.roll(pad_end, sh, axis=1), 0.0)
        sh *= 2
    pad_start = pad_end - padded

    start_col = _col_from_row(pad_start)[0:N_EXPERTS, :]

    def place(it, _):
        starts = [pl.multiple_of((it * group + k) * tb, tb) for k in range(group)]
        tiles = [info_ref[:, pl.ds(r0, tb)] for r0 in starts]
        placed = []
        for tile in tiles:
            s0 = jnp.sum(jnp.where(row_e == tile[_L_E0:_L_E0 + 1, :], start_col, 0.0), axis=0, keepdims=True)
            s1 = jnp.sum(jnp.where(row_e == tile[_L_E1:_L_E1 + 1, :], start_col, 0.0), axis=0, keepdims=True)
            placed.append(tile + jnp.where(row8 == _L_D0, s0, jnp.where(row8 == _L_D1, s1, 0.0)))
        for r0, tile in zip(starts, placed):
            info_ref[:, pl.ds(r0, tb)] = tile
            dest_ref[:, pl.ds(r0, tb)] = tile.astype(I32)
        return 0

    lax.fori_loop(0, n_tok // (tb * group), place, 0)

    pe_col = _col_from_row(pad_end)
    mt = meta_ref.shape[1]
    t_start = lax.broadcasted_iota(I32, (LANES, mt), 1).astype(F32) * FFN_TILE
    e_row = lax.broadcasted_iota(I32, (LANES, mt), 0)
    ended = jnp.where((pe_col <= t_start) & (e_row < N_EXPERTS), 1.0, 0.0)
    tile_e = jnp.minimum(jnp.sum(ended, axis=0, keepdims=True), N_EXPERTS - 1.0)
    total = jnp.sum(jnp.where(lane1 == N_EXPERTS - 1, pad_end, 0.0), axis=1, keepdims=True)
    n_valid = jnp.broadcast_to(total * (1.0 / FFN_TILE), (1, mt))
    def per_expert(v):
        return v if mt == LANES else jnp.concatenate([v, jnp.zeros((1, mt - LANES), F32)], axis=1)

    mrow = lax.broadcasted_iota(I32, (SUBLANES, mt), 0)
    meta = jnp.where(mrow == _M_TILE_E, tile_e,
           jnp.where(mrow == _M_NVALID, n_valid,
           jnp.where(mrow == _M_NTILES, per_expert(padded * (1.0 / FFN_TILE)),
           jnp.where(mrow == _M_COUNT, per_expert(counts),
           jnp.where(mrow == _M_START, per_expert(pad_start), 0.0)))))
    meta_ref[...] = meta.astype(I32)


def _plan(logits, n_tiles):
    n_tok = logits.shape[0]
    mt = -(-n_tiles // LANES) * LANES
    kern = functools.partial(_plan_kernel, n_tok=n_tok)
    return pl.pallas_call(
        kern,
        out_shape=[jax.ShapeDtypeStruct((SUBLANES, n_tok), F32),
                   jax.ShapeDtypeStruct((SUBLANES, n_tok), I32),
                   jax.ShapeDtypeStruct((SUBLANES, mt), I32)],
        compiler_params=pltpu.CompilerParams(vmem_limit_bytes=VMEM_LIMIT),
        name="plan",
    )(logits)


_DMA_UNROLL = 8


def _invert_kernel(d0_ref, d1_ref, nv_ref, ntl_ref, cnt_ref, start_ref, tok_ref, *, n_tok, p_rows):
    def zero_rows(lo, hi):
        def zbody(u, _):
            for s in range(_DMA_UNROLL):
                tok_ref[jnp.minimum(lo + u * _DMA_UNROLL + s, hi - 1)] = 0
            return 0
        lax.fori_loop(0, (hi - lo + _DMA_UNROLL - 1) // _DMA_UNROLL, zbody, 0)

    def pad_body(e, _):
        zero_rows(start_ref[e] + cnt_ref[e], start_ref[e] + ntl_ref[e] * FFN_TILE)
        return 0

    lax.fori_loop(0, N_EXPERTS, pad_body, 0)
    zero_rows(nv_ref[0] * FFN_TILE, p_rows)

    def body(u, _):
        t0 = u * _DMA_UNROLL
        rows = [(d0_ref[t0 + s], d1_ref[t0 + s]) for s in range(_DMA_UNROLL)]
        for s, (r0, r1) in enumerate(rows):
            tok_ref[r0] = t0 + s
            tok_ref[r1] = t0 + s
        return 0

    lax.fori_loop(0, n_tok // _DMA_UNROLL, body, 0)


def _invert(d0, d1, n_valid, ntl, cnt, start, p_rows):
    n_tok = d0.shape[0]
    kern = functools.partial(_invert_kernel, n_tok=n_tok, p_rows=p_rows)
    return pl.pallas_call(
        kern,
        grid_spec=pltpu.PrefetchScalarGridSpec(
            num_scalar_prefetch=6,
            grid=(1,),
            in_specs=[],
            out_specs=pl.BlockSpec(memory_space=pltpu.SMEM)),
        out_shape=jax.ShapeDtypeStruct((p_rows,), I32),
        compiler_params=_cparams(("arbitrary",)),
        name="invert",
    )(d0, d1, n_valid, ntl, cnt, start)


_FFN_SLOTS = 3
_BULK_DMA_PRIORITY = 1


def _ffn_kernel(te_ref, nv_ref, ntl_ref, tok_ref, h_ref, w1_ref, w3_ref, w2_ref, y_ref,
                xbuf, xb_ref, w1f, w3f, w2f, w1b, w3b, w2b, gsem, wsem, run_ref, *, n_tiles):
    i = pl.program_id(0)
    nv = nv_ref[0]
    valid = i < nv
    slot = i % _FFN_SLOTS
    nc = _token_pitch(xb_ref.shape[1])

    def row_copy(tile, r, dst_slot):
        tok = tok_ref[tile * FFN_TILE + r]
        src = pl.ds(tok * nc, nc)
        return pltpu.make_async_copy(h_ref.at[src], xbuf.at[dst_slot, pl.ds(r * nc, nc)], gsem.at[dst_slot])

    def weight_copies(e, ws):
        return (pltpu.make_async_copy(w1_ref.at[e], w1f.at[ws], wsem.at[ws]),
                pltpu.make_async_copy(w3_ref.at[e], w3f.at[ws], wsem.at[ws]),
                pltpu.make_async_copy(w2_ref.at[e], w2f.at[ws], wsem.at[ws]))

    @pl.when(i == 0)
    def _():
        run_ref[0] = 0
        for cp in weight_copies(te_ref[0], 0):
            cp.start(priority=_BULK_DMA_PRIORITY)

        def first_rows(u, _):
            for s in range(_DMA_UNROLL):
                for t in range(_FFN_SLOTS - 1):
                    row_copy(t, u * _DMA_UNROLL + s, t).start()
            return 0
        lax.fori_loop(0, FFN_TILE // _DMA_UNROLL, first_rows, 0)

    @pl.when(valid & ((i == 0) | (te_ref[i] != te_ref[jnp.maximum(i, 1) - 1])))
    def _():
        run = run_ref[0]
        ws = run % 2
        for cp in weight_copies(te_ref[i], ws):
            cp.wait()
        w1b[...] = w1f[ws].astype(BF16)
        w3b[...] = w3f[ws].astype(BF16)
        w2b[...] = w2f[ws].astype(BF16)
        nxt = i + ntl_ref[te_ref[i]]

        @pl.when(nxt < nv)
        def _():
            for cp in weight_copies(te_ref[jnp.minimum(nxt, n_tiles - 1)], 1 - ws):
                cp.start(priority=_BULK_DMA_PRIORITY)
        run_ref[0] = run + 1

    @pl.when(valid)
    def _():
        pltpu.make_async_copy(h_ref.at[pl.ds(0, FFN_TILE * nc)], xbuf.at[slot], gsem.at[slot]).wait()
        xb_ref[...] = _load_token_major(xbuf.at[slot], FFN_TILE, xb_ref.shape[1], BF16)
        ahead = _FFN_SLOTS - 1
        nxt_tile = jnp.minimum(i + ahead, n_tiles - 1)
        nxt_slot = (i + ahead) % _FFN_SLOTS
        for r in range(FFN_TILE):
            row_copy(nxt_tile, r, nxt_slot).start()
        x = xb_ref[...]
        a = jnp.dot(x, w1b[...], preferred_element_type=F32)
        b = jnp.dot(x, w3b[...], preferred_element_type=F32)
        hmid = (_silu(a) * b).astype(BF16)
        y_ref[...] = jnp.dot(hmid, w2b[...], preferred_element_type=F32)

    @pl.when(jnp.logical_not(valid))
    def _():
        y_ref[...] = jnp.zeros_like(y_ref)

    @pl.when(i == n_tiles - 1)
    def _():
        for t in range(_FFN_SLOTS - 1):
            s = (nv + t) % _FFN_SLOTS
            pltpu.make_async_copy(h_ref.at[pl.ds(0, FFN_TILE * nc)], xbuf.at[s], gsem.at[s]).wait()


def _ffn(tile_e, n_valid, ntl, row_tok, h2, w1, w3, w2):
    p_rows = row_tok.shape[0]
    _, d, de = w1.shape
    n_tiles = p_rows // FFN_TILE
    kern = functools.partial(_ffn_kernel, n_tiles=n_tiles)
    any_spec = pl.BlockSpec(memory_space=pl.ANY)
    return pl.pallas_call(
        kern,
        grid_spec=pltpu.PrefetchScalarGridSpec(
            num_scalar_prefetch=4,
            grid=(n_tiles,),
            in_specs=[any_spec, any_spec, any_spec, any_spec],
            out_specs=pl.BlockSpec((FFN_TILE, d), lambda i, *_: (i, 0)),
            scratch_shapes=[pltpu.VMEM((_FFN_SLOTS, FFN_TILE * _token_pitch(d), LANES), F32),
                            pltpu.VMEM((FFN_TILE, d), BF16),
                            pltpu.VMEM((2, d, de), F32), pltpu.VMEM((2, d, de), F32),
                            pltpu.VMEM((2, de, d), F32),
                            pltpu.VMEM((d, de), BF16), pltpu.VMEM((d, de), BF16),
                            pltpu.VMEM((de, d), BF16),
                            pltpu.SemaphoreType.DMA((_FFN_SLOTS,)), pltpu.SemaphoreType.DMA((2,)),
                            pltpu.SMEM((1,), I32)]),
        out_shape=jax.ShapeDtypeStruct((p_rows, d), F32),
        compiler_params=_cparams(("arbitrary",)),
        name="ffn",
    )(tile_e, n_valid, ntl, row_tok, h2, w1, w3, w2)


def _combine_kernel(d0_ref, d1_ref, x1_ref, info_ref, mod_ref, nw_ref, y_ref, o_ref,
                    *scratch, tp, n_phases):
    i = pl.program_id(0)
    ns = _COMBINE_SETS
    sem = scratch[-1]
    bufs = [(scratch[2 * q], scratch[2 * q + 1]) for q in range(ns)]

    def row_copies(phase, r, q):
        tok = phase * tp + r
        ya, yb = bufs[q]
        return (pltpu.make_async_copy(y_ref.at[pl.ds(d0_ref[tok], 1)], ya.at[pl.ds(r, 1)], sem.at[q]),
                pltpu.make_async_copy(y_ref.at[pl.ds(d1_ref[tok], 1)], yb.at[pl.ds(r, 1)], sem.at[q]))

    def wait_set(q):
        for buf in bufs[q]:
            pltpu.make_async_copy(y_ref.at[pl.ds(0, tp)], buf, sem.at[q]).wait()

    @pl.when(i == 0)
    def _():
        def first_rows(u, _):
            for s in range(_DMA_UNROLL):
                for q in range(ns - 1):
                    for cp in row_copies(q, u * _DMA_UNROLL + s, q):
                        cp.start()
            return 0
        lax.fori_loop(0, tp // _DMA_UNROLL, first_rows, 0)

    for q in range(ns):
        wait_set(q)
        ahead = jnp.minimum(i * ns + q + ns - 1, n_phases - 1)
        for r in range(tp):
            for prio, cp in enumerate(row_copies(ahead, r, (q + ns - 1) % ns)):
                cp.start(priority=prio)
        ya, yb = bufs[q]
        rows = pl.ds(q * tp, tp)
        w0 = _col_from_row(info_ref[_L_W0:_L_W0 + 1, q * tp:(q + 1) * tp])
        w1 = _col_from_row(info_ref[_L_W1:_L_W1 + 1, q * tp:(q + 1) * tp])
        moe = ya[...] * w0 + yb[...] * w1
        x2 = x1_ref[rows, :] + mod_ref[0][5:6] * moe
        o_ref[rows, :] = x2 * lax.rsqrt(jnp.mean(x2 * x2, axis=-1, keepdims=True) + NORM_EPS) * nw_ref[...]

    @pl.when(i == n_phases // ns - 1)
    def _():
        for q in range(ns - 1):
            wait_set(q)


_COMBINE_SETS = 4


def _combine(d0, d1, x1, info, mod3, nfw, y, seq, tp=LANES):
    assert tp == LANES, "one phase's combine weights must fill exactly one lane-dense row"
    n, d = x1.shape
    tc = _COMBINE_SETS * tp
    per_b = seq // tc
    kern = functools.partial(_combine_kernel, tp=tp, n_phases=n // tp)
    return pl.pallas_call(
        kern,
        grid_spec=pltpu.PrefetchScalarGridSpec(
            num_scalar_prefetch=2,
            grid=(n // tc,),
            in_specs=[pl.BlockSpec((tc, d), lambda i, d0, d1: (i, 0)),
                      pl.BlockSpec((SUBLANES, tc), lambda i, d0, d1: (0, i)),
                      pl.BlockSpec((1, N_MOD, d), lambda i, d0, d1: (i // per_b, 0, 0)),
                      pl.BlockSpec((1, d), lambda i, d0, d1: (0, 0)),
                      pl.BlockSpec(memory_space=pl.ANY)],
            out_specs=pl.BlockSpec((tc, d), lambda i, d0, d1: (i, 0)),
            scratch_shapes=[pltpu.VMEM((tp, d), F32)] * (2 * _COMBINE_SETS)
                           + [pltpu.SemaphoreType.DMA((_COMBINE_SETS,))]),
        out_shape=jax.ShapeDtypeStruct((n, d), F32),
        compiler_params=_cparams(("arbitrary",)),
        name="combine",
    )(d0, d1, x1, info, mod3, nfw, y)


def _split_hi_lo(w):
    hi = w.astype(BF16)
    lo = (w - hi.astype(F32)).astype(BF16)
    return hi, lo


def kernel(x, c, w_ada, b_ada, norm_mix_w, w_in, ret_norm_w, w_out, norm_ffn_w,
           w_group, b_group, w_router, b_router, w1, w3, w2, norm_final_w):
    batch, seq, d = x.shape
    n = batch * seq
    depth = w_ada.shape[0]
    assert depth == 1, "the final rmsnorm is fused into the (single) layer's last kernel"
    tm = min(INPROJ_ROWS, seq)
    half = LANES // 2

    slopes = jnp.exp2(-8.0 * jnp.arange(1, N_ATTN_HEADS + 1, dtype=F32) / N_ATTN_HEADS)
    log_gamma = jnp.log1p(-jnp.exp2(-5.0 - jnp.arange(RET_HEADS, dtype=F32)))
    n_tiles = (2 * n) // FFN_TILE + N_EXPERTS
    p_rows = n_tiles * FFN_TILE

    assert batch <= SUBLANES
    c_pad = jnp.zeros((SUBLANES, d), F32).at[:batch].set(c)
    x2d = x.reshape(n, d)
    for l in range(depth):
        mod = _adaln(c_pad, w_ada[l], b_ada[l].reshape(1, -1))
        mod3 = mod[:batch].reshape(batch, N_MOD, d)

        proj, kmean = _inproj(x2d, mod3, norm_mix_w[l].reshape(1, d), w_in[l], seq, tm)
        kmean = kmean.reshape(batch, seq // MOBA_BLOCK, ATTN_WIDTH)
        attn = _moba(proj, kmean, slopes, batch, seq)
        ret = _retention(proj, log_gamma, ret_norm_w[l].reshape(1, RET_WIDTH), batch, seq)

        gap = _ROUTER_EXPERT_LANE - N_GROUPS
        wr = jnp.concatenate([w_group[l], jnp.zeros((d, gap), F32),
                              jnp.transpose(w_router[l], (1, 0, 2)).reshape(d, N_EXPERTS)], axis=1)
        wr = jnp.pad(wr, ((0, 0), (0, half - wr.shape[1])))
        wr_hi, wr_lo = _split_hi_lo(wr)
        wr_cat = jnp.concatenate([jnp.concatenate([wr_hi, wr_lo], axis=1),
                                  jnp.concatenate([wr_hi, jnp.zeros_like(wr_lo)], axis=1)], axis=0)
        br = jnp.concatenate([b_group[l], jnp.zeros((gap,), F32), b_router[l].reshape(-1)])
        br = jnp.pad(br, (0, LANES - br.shape[0])).reshape(1, LANES)

        x1, h2p, logits = _outproj(attn, ret, w_out[l], x2d, mod3,
                                   norm_ffn_w[l].reshape(1, d), wr_cat, br, seq, min(OUTPROJ_ROWS, seq))

        info, dest, meta = _plan(logits, n_tiles)
        d0 = dest[_L_D0]
        d1 = dest[_L_D1]
        tile_e = meta[_M_TILE_E, :n_tiles]
        n_valid = meta[_M_NVALID, :1]
        ntl = meta[_M_NTILES, :N_EXPERTS]

        row_tok = _invert(d0, d1, n_valid, ntl, meta[_M_COUNT, :N_EXPERTS],
                          meta[_M_START, :N_EXPERTS], p_rows)
        y = _ffn(tile_e, n_valid, ntl, row_tok, h2p, w1[l], w3[l], w2[l])
        nfw = norm_final_w.reshape(1, d)
        x2d = _combine(d0, d1, x1, info, mod3, nfw, y, seq)
    return x2d.reshape(batch, seq, d)
```

```python
import functools

import jax
import jax.numpy as jnp
from jax import lax
from jax.experimental import pallas as pl
from jax.experimental.pallas import tpu as pltpu

F32 = jnp.float32
BF16 = jnp.bfloat16
I32 = jnp.int32

N_ATTN_HEADS = 8
ATTN_HEAD_DIM = 128
ATTN_WIDTH = N_ATTN_HEADS * ATTN_HEAD_DIM
MOBA_BLOCK = 256
MOBA_TOPK = 3
RET_HEADS = 4
RET_HEAD_DIM = 256
RET_WIDTH = RET_HEADS * RET_HEAD_DIM
N_GROUPS = 4
EXPERTS_PER_GROUP = 8
N_EXPERTS = N_GROUPS * EXPERTS_PER_GROUP
N_MOD = 6
NORM_EPS = 1e-6

LANES = 128
SUBLANES = 8
RET_CHUNK = 256
FFN_TILE = 256
INPROJ_ROWS = 1024
OUTPROJ_ROWS = 256
MASK_VALUE = -1e30
LOG2E = 1.4426950408889634
MOBA_QK_SCALE = ATTN_HEAD_DIM ** -0.5 * LOG2E
_MOBA_EXTRA_ROWS = 16
VMEM_LIMIT = 56 * 1024 * 1024

_DN_LAST = (((1,), (1,)), ((), ()))
_DN_FIRST = (((0,), (0,)), ((), ()))


def _cparams(sem):
    return pltpu.CompilerParams(dimension_semantics=sem, vmem_limit_bytes=VMEM_LIMIT)


def _silu(v):
    return v * jax.nn.sigmoid(v)


def _token_pitch(d):
    return d // LANES + 1


def _store_token_major(ref, value):
    rows, d = value.shape
    pitch = _token_pitch(d)
    for c in range(pitch - 1):
        ref[pl.ds(c, rows, stride=pitch), :] = value[:, c * LANES:(c + 1) * LANES]
    ref[pl.ds(pitch - 1, rows, stride=pitch), :] = jnp.zeros((rows, LANES), value.dtype)


def _load_token_major(ref, rows, d, dtype):
    pitch = _token_pitch(d)
    return jnp.concatenate([ref[pl.ds(c, rows, stride=pitch), :].astype(dtype) for c in range(pitch - 1)],
                           axis=1)


def _adaln_kernel(c_ref, w_ref, b_ref, o_ref):
    ca = _silu(c_ref[...]).astype(BF16)
    o_ref[...] = jnp.dot(ca, w_ref[...].astype(BF16), preferred_element_type=F32) + b_ref[...]


def _adaln(c_pad, w_ada, b_ada, tn=1024):
    rows, d = c_pad.shape
    n = w_ada.shape[1]
    return pl.pallas_call(
        _adaln_kernel,
        grid=(n // tn,),
        in_specs=[pl.BlockSpec((rows, d), lambda j: (0, 0)),
                  pl.BlockSpec((d, tn), lambda j: (0, j)),
                  pl.BlockSpec((1, tn), lambda j: (0, j))],
        out_specs=pl.BlockSpec((rows, tn), lambda j: (0, j)),
        out_shape=jax.ShapeDtypeStruct((rows, n), F32),
        compiler_params=_cparams(("arbitrary",)),
        name="adaln",
    )(c_pad, w_ada, b_ada)


def _inproj_kernel(x_ref, mod_ref, nw_ref, w_ref, o_ref, km_ref, h_ref, *, tm, tn):
    j = pl.program_id(1)

    @pl.when(j == 0)
    def _():
        x = x_ref[...]
        y = x * lax.rsqrt(jnp.mean(x * x, axis=-1, keepdims=True) + NORM_EPS) * nw_ref[...]
        m = mod_ref[0]
        h_ref[...] = (y * (1.0 + m[1:2]) + m[0:1]).astype(BF16)

    acc = jnp.dot(h_ref[...], w_ref[...].astype(BF16), preferred_element_type=F32)
    o_ref[...] = (acc * jnp.where(j == 0, MOBA_QK_SCALE, 1.0)).astype(BF16)

    @pl.when(j == 1)
    def _():
        km_ref[0] = jnp.sum(acc.reshape(tm // MOBA_BLOCK, MOBA_BLOCK, tn), axis=1) * (1.0 / MOBA_BLOCK)


def _inproj(x2d, mod3, nw, w_bf, seq, tm, tn=ATTN_WIDTH):
    n, d = x2d.shape
    width = w_bf.shape[1]
    per_b = seq // tm
    kern = functools.partial(_inproj_kernel, tm=tm, tn=tn)
    return pl.pallas_call(
        kern,
        grid=(n // tm, width // tn),
        in_specs=[pl.BlockSpec((tm, d), lambda i, j: (i, 0)),
                  pl.BlockSpec((1, N_MOD, d), lambda i, j: (i // per_b, 0, 0)),
                  pl.BlockSpec((1, d), lambda i, j: (0, 0)),
                  pl.BlockSpec((d, tn), lambda i, j: (0, j))],
        out_specs=[pl.BlockSpec((tm, tn), lambda i, j: (i, j)),
                   pl.BlockSpec((1, tm // MOBA_BLOCK, tn), lambda i, j: (i, 0, 0))],
        out_shape=[jax.ShapeDtypeStruct((n, width), BF16),
                   jax.ShapeDtypeStruct((n // tm, tm // MOBA_BLOCK, tn), F32)],
        scratch_shapes=[pltpu.VMEM((tm, d), BF16)],
        compiler_params=_cparams(("arbitrary", "arbitrary")),
        name="inproj",
    )(x2d, mod3, nw, w_bf)


def _moba_kernel(slopes_ref, q_ref, k_ref, v_ref, km_ref, o_ref,
                 vt_ref, sel_ref, bias_ref, acc_ref, *, nb, hps):
    hg = pl.program_id(1)
    i = pl.program_id(2)
    blk = MOBA_BLOCK
    dh = ATTN_HEAD_DIM
    log2e = LOG2E

    qpos = lax.broadcasted_iota(I32, (blk, blk), 1)
    kpos = lax.broadcasted_iota(I32, (blk, blk), 0)

    vt_rows = dh + _MOBA_EXTRA_ROWS

    @pl.when(i == 0)
    def _():
        vt = v_ref[...].astype(F32).T.astype(BF16)
        extra = lax.broadcasted_iota(I32, (_MOBA_EXTRA_ROWS, vt.shape[1]), 0)
        ones_row = jnp.where(extra == 0, 1.0, 0.0).astype(BF16)
        key_pos = lax.broadcasted_iota(I32, (blk, LANES), 0).astype(F32)
        for g in range(hps):
            vt_ref[g * vt_rows:g * vt_rows + dh, :] = vt[g * dh:(g + 1) * dh, :]
            vt_ref[g * vt_rows + dh:(g + 1) * vt_rows, :] = ones_row
            bias_ref[g] = key_pos * (log2e * slopes_ref[hg * hps + g])

    bidx = lax.broadcasted_iota(I32, (nb, blk), 0)
    past = bidx < i
    i0 = pl.multiple_of(i * blk, blk)
    heads = [slice(g * dh, (g + 1) * dh) for g in range(hps)]
    vrows = [slice(g * vt_rows, (g + 1) * vt_rows) for g in range(hps)]
    gates, raw_own = [], []
    for cols in heads:
        q = q_ref[:, cols]
        km = km_ref[0, :, cols]
        km_hi = km.astype(BF16)
        km_lo = (km - km_hi.astype(F32)).astype(BF16)
        gates.append(lax.dot_general(km_hi, q, _DN_LAST, preferred_element_type=F32)
                     + lax.dot_general(km_lo, q, _DN_LAST, preferred_element_type=F32))
        raw_own.append(lax.dot_general(k_ref[pl.ds(i0, blk), cols], q, _DN_LAST,
                                       preferred_element_type=F32))
    def key_bias(g):
        kb = bias_ref[g]
        return jnp.concatenate([kb] * (blk // LANES), axis=1)

    own_bias = [key_bias(g) for g in range(hps)]

    carry0, acc0, sels = [], [], []
    for g, cols in enumerate(heads):
        gate = jnp.where(past, gates[g], -jnp.inf)
        rank = jnp.zeros((nb, blk), F32)
        for m in range(nb):
            gm = gate[m:m + 1, :]
            beats = jnp.where(gm > gate, 1.0, jnp.where((gm == gate) & (m < bidx), 1.0, 0.0))
            rank = rank + jnp.where(m < i, beats, 0.0)
        sels.append(jnp.where(past & (rank < MOBA_TOPK), 1.0, 0.0))

        s = jnp.where(qpos >= kpos, own_bias[g] + raw_own[g], MASK_VALUE)
        m0 = jnp.max(s, axis=0, keepdims=True)
        p = jnp.exp2(s - m0)
        acc0.append(jnp.dot(vt_ref[vrows[g], pl.ds(i0, blk)], p.astype(BF16), preferred_element_type=F32))
        carry0.append(m0)
    for g in range(hps):
        acc_ref[g] = acc0[g]
        sel_ref[g] = sels[g]

    def body(j, carry):
        j0 = pl.multiple_of(j * blk, blk)
        off = ((i - j) * blk).astype(F32) * log2e
        raw = [lax.dot_general(k_ref[pl.ds(j0, blk), cols], q_ref[:, cols], _DN_LAST,
                               preferred_element_type=F32) for cols in heads]
        chosen = [sel_ref[g, pl.ds(j, 1), :] > 0.0 for g in range(hps)]
        acc_old = [acc_ref[g] for g in range(hps)]
        out, acc_new = [], []
        for g in range(hps):
            m_run = carry[g]
            shift = -slopes_ref[hg * hps + g] * off
            s = key_bias(g) + raw[g]
            m_new = jnp.maximum(m_run, jnp.where(chosen[g], jnp.max(s, axis=0, keepdims=True) + shift,
                                                 MASK_VALUE))
            alpha = jnp.exp2(m_run - m_new)
            p = jnp.exp2(s - jnp.where(chosen[g], m_new - shift, -MASK_VALUE))
            acc_new.append(alpha * acc_old[g] + jnp.dot(vt_ref[vrows[g], pl.ds(j0, blk)], p.astype(BF16),
                                                        preferred_element_type=F32))
            out.append(m_new)
        for g in range(hps):
            acc_ref[g] = acc_new[g]
        return tuple(out)

    lax.fori_loop(0, i, body, tuple(carry0))
    for g in range(hps):
        acc = acc_ref[g]
        o_ref[:, g * dh:(g + 1) * dh] = (acc[:dh] / acc[dh:dh + 1]).T.astype(BF16)


MOBA_HEADS_PER_STEP = 8


def _moba(proj, kmean, slopes, batch, seq):
    n = proj.shape[0]
    nb = seq // MOBA_BLOCK
    hps = MOBA_HEADS_PER_STEP
    vt_rows = ATTN_HEAD_DIM + _MOBA_EXTRA_ROWS
    w = hps * ATTN_HEAD_DIM
    ng = N_ATTN_HEADS // hps
    kern = functools.partial(_moba_kernel, nb=nb, hps=hps)
    return pl.pallas_call(
        kern,
        grid=(batch, ng, nb),
        in_specs=[pl.BlockSpec(memory_space=pltpu.SMEM),
                  pl.BlockSpec((MOBA_BLOCK, w), lambda b, h, i: (b * nb + i, h)),
                  pl.BlockSpec((seq, w), lambda b, h, i: (b, ng + h)),
                  pl.BlockSpec((seq, w), lambda b, h, i: (b, 2 * ng + h)),
                  pl.BlockSpec((1, nb, w), lambda b, h, i: (b, 0, h))],
        out_specs=pl.BlockSpec((MOBA_BLOCK, w), lambda b, h, i: (b * nb + i, h)),
        out_shape=jax.ShapeDtypeStruct((n, ATTN_WIDTH), BF16),
        scratch_shapes=[pltpu.VMEM((hps * vt_rows, seq), BF16),
                        pltpu.VMEM((hps, nb, MOBA_BLOCK), F32),
                        pltpu.VMEM((hps, MOBA_BLOCK, LANES), F32),
                        pltpu.VMEM((hps, vt_rows, MOBA_BLOCK), F32)],
        compiler_params=_cparams(("arbitrary", "arbitrary", "arbitrary")),
        name="moba",
    )(slopes, proj, proj, proj, kmean)


def _ret_kernel(lg_ref, q_ref, k_ref, v_ref, g_ref, rnw_ref, o_ref,
                state_ref, dmask_ref, qdec_ref, kdec_ref, cdec_ref):
    n = pl.program_id(1)
    c = RET_CHUNK
    dk = RET_HEAD_DIM
    kscale = dk ** -0.5

    @pl.when(n == 0)
    def _():
        state_ref[...] = jnp.zeros_like(state_ref)
        ti = lax.broadcasted_iota(I32, (c, c), 0)
        si = lax.broadcasted_iota(I32, (c, c), 1)
        diff = jnp.maximum(ti - si, 0).astype(F32)
        pos = lax.broadcasted_iota(I32, (c, dk), 0).astype(F32)
        for h in range(RET_HEADS):
            lg = lg_ref[h]
            dmask_ref[h] = jnp.where(ti >= si, jnp.exp(diff * lg) * kscale, 0.0)
            qdec_ref[h] = jnp.exp((pos + 1.0) * lg)
            kdec_ref[h] = jnp.exp((c - 1.0 - pos) * lg) * kscale
            cdec_ref[h] = jnp.exp(jnp.full((1, dk), float(c), F32) * lg)

    heads = [slice(h * dk, (h + 1) * dk) for h in range(RET_HEADS)]
    qs = [q_ref[:, cols] for cols in heads]
    ks = [k_ref[:, cols] for cols in heads]
    vs = [v_ref[:, cols] for cols in heads]
    states = [state_ref[h] for h in range(RET_HEADS)]
    raw = [lax.dot_general(qs[h], ks[h], _DN_LAST, preferred_element_type=F32) for h in range(RET_HEADS)]
    cross = [jnp.dot(qs[h], states[h].astype(BF16), preferred_element_type=F32) for h in range(RET_HEADS)]
    outs, new_states = [], []
    for h, cols in enumerate(heads):
        inner = jnp.dot((raw[h] * dmask_ref[h]).astype(BF16), vs[h], preferred_element_type=F32)
        kd = (ks[h].astype(F32) * kdec_ref[h]).astype(BF16)
        kv = lax.dot_general(kd, vs[h], _DN_FIRST, preferred_element_type=F32)
        new_states.append(states[h] * cdec_ref[h] + kv)
        r = inner + cross[h] * qdec_ref[h]
        rn = r * lax.rsqrt(jnp.mean(r * r, axis=-1, keepdims=True) + NORM_EPS)
        g = g_ref[:, cols].astype(F32)
        outs.append((rn * rnw_ref[:, cols] * _silu(g)).astype(BF16))
    for h, cols in enumerate(heads):
        state_ref[h] = new_states[h]
        o_ref[:, cols] = outs[h]


def _retention(proj, log_gamma, rnw, batch, seq):
    n = proj.shape[0]
    c = RET_CHUNK
    nc = seq // c
    dk = RET_HEAD_DIM
    base = 3 * ATTN_WIDTH // RET_WIDTH

    def col(off):
        return lambda b, t: (b * nc + t, base + off)

    return pl.pallas_call(
        _ret_kernel,
        grid=(batch, nc),
        in_specs=[pl.BlockSpec(memory_space=pltpu.SMEM),
                  pl.BlockSpec((c, RET_WIDTH), col(0)),
                  pl.BlockSpec((c, RET_WIDTH), col(1)),
                  pl.BlockSpec((c, RET_WIDTH), col(2)),
                  pl.BlockSpec((c, RET_WIDTH), col(3)),
                  pl.BlockSpec((1, RET_WIDTH), lambda b, t: (0, 0))],
        out_specs=pl.BlockSpec((c, RET_WIDTH), lambda b, t: (b * nc + t, 0)),
        out_shape=jax.ShapeDtypeStruct((n, RET_WIDTH), BF16),
        scratch_shapes=[pltpu.VMEM((RET_HEADS, dk, dk), F32), pltpu.VMEM((RET_HEADS, c, c), F32),
                        pltpu.VMEM((RET_HEADS, c, dk), F32), pltpu.VMEM((RET_HEADS, c, dk), F32),
                        pltpu.VMEM((RET_HEADS, 1, dk), F32)],
        compiler_params=_cparams(("arbitrary", "arbitrary")),
        name="retention",
    )(log_gamma, proj, proj, proj, proj, rnw)


def _outproj_kernel(a_ref, r_ref, w_ref, x_ref, mod_ref, nw_ref, wr_ref, br_ref,
                    x1_ref, hp_ref, lg_ref, wb_ref):
    @pl.when(pl.program_id(0) == 0)
    def _():
        wb_ref[...] = w_ref[...].astype(BF16)

    mix = jnp.dot(jnp.concatenate([a_ref[...], r_ref[...]], axis=1), wb_ref[...],
                  preferred_element_type=F32)
    m = mod_ref[0]
    x1 = x_ref[...] + m[2:3] * mix
    x1_ref[...] = x1
    y = x1 * lax.rsqrt(jnp.mean(x1 * x1, axis=-1, keepdims=True) + NORM_EPS) * nw_ref[...]
    h2 = y * (1.0 + m[4:5]) + m[3:4]
    _store_token_major(hp_ref, h2)
    hi = h2.astype(BF16)
    lo = (h2 - hi.astype(F32)).astype(BF16)
    part = jnp.dot(jnp.concatenate([hi, lo], axis=1), wr_ref[...], preferred_element_type=F32)
    lg_ref[...] = part + pltpu.roll(part, LANES // 2, axis=1) + br_ref[...]


def _outproj(attn, ret, w_bf, x2d, mod3, nw, wr_cat, br, seq, tm):
    n, d = x2d.shape
    per_b = seq // tm
    return pl.pallas_call(
        _outproj_kernel,
        grid=(n // tm,),
        in_specs=[pl.BlockSpec((tm, ATTN_WIDTH), lambda i: (i, 0)),
                  pl.BlockSpec((tm, RET_WIDTH), lambda i: (i, 0)),
                  pl.BlockSpec((d, d), lambda i: (0, 0), pipeline_mode=pl.Buffered(1)),
                  pl.BlockSpec((tm, d), lambda i: (i, 0)),
                  pl.BlockSpec((1, N_MOD, d), lambda i: (i // per_b, 0, 0)),
                  pl.BlockSpec((1, d), lambda i: (0, 0)),
                  pl.BlockSpec((2 * d, LANES), lambda i: (0, 0)),
                  pl.BlockSpec((1, LANES), lambda i: (0, 0))],
        out_specs=[pl.BlockSpec((tm, d), lambda i: (i, 0)),
                   pl.BlockSpec((tm * _token_pitch(d), LANES), lambda i: (i, 0)),
                   pl.BlockSpec((tm, LANES), lambda i: (i, 0))],
        out_shape=[jax.ShapeDtypeStruct((n, d), F32),
                   jax.ShapeDtypeStruct((n * _token_pitch(d), LANES), F32),
                   jax.ShapeDtypeStruct((n, LANES), F32)],
        scratch_shapes=[pltpu.VMEM((d, d), BF16)],
        compiler_params=_cparams(("arbitrary",)),
        name="outproj",
    )(attn, ret, w_bf, x2d, mod3, nw, wr_cat, br)


_L_E0, _L_E1, _L_W0, _L_W1, _L_D0, _L_D1 = 0, 1, 2, 3, 4, 5
_ROUTER_EXPERT_LANE = 8
_M_TILE_E, _M_NVALID, _M_NTILES, _M_COUNT, _M_START = 0, 1, 2, 3, 4
_PLAN_ROWS = 256
_PLAN_GROUP = 4


def _first_row_where(cond, rows, limit):
    return jnp.min(jnp.where(cond, rows, limit), axis=0, keepdims=True)


def _col_from_row(v):
    r = lax.broadcasted_iota(I32, (LANES, LANES), 0)
    c = lax.broadcasted_iota(I32, (LANES, LANES), 1)
    return jnp.sum(jnp.where(r == c, v, 0.0), axis=1, keepdims=True)


def _row_from_col(v):
    r = lax.broadcasted_iota(I32, (LANES, LANES), 0)
    c = lax.broadcasted_iota(I32, (LANES, LANES), 1)
    return jnp.sum(jnp.where(r == c, v, 0.0), axis=0, keepdims=True)


def _rows_to_tile(rows, row8):
    tile = 0.0
    for k in reversed(range(len(rows))):
        tile = jnp.where(row8 == k, rows[k], tile)
    return tile


def _plan_kernel(lg_ref, info_ref, dest_ref, meta_ref, *, n_tok):
    tb = _PLAN_ROWS
    n_g, n_e = N_GROUPS, EXPERTS_PER_GROUP
    assert n_g <= SUBLANES and n_e == SUBLANES, "group and per-group logits each fill one sublane tile"
    no_row = float(SUBLANES)
    row8 = lax.broadcasted_iota(I32, (SUBLANES, tb), 0).astype(F32)
    row_e = lax.broadcasted_iota(I32, (N_EXPERTS, tb), 0).astype(F32)
    earlier = lax.broadcasted_iota(I32, (tb, tb), 0) < lax.broadcasted_iota(I32, (tb, tb), 1)
    tri = jnp.where(earlier, 1.0, 0.0).astype(BF16)

    def route_block(r0):
        lt = lg_ref[pl.ds(r0, tb), :].T
        gl = jnp.where(row8 < n_g, lt[0:SUBLANES, :], -jnp.inf)
        ge = jnp.exp(gl - jnp.max(gl, axis=0, keepdims=True))
        gp = ge / jnp.sum(ge, axis=0, keepdims=True)
        g_w = jnp.max(gp, axis=0, keepdims=True)
        g_sel = _first_row_where((gp == g_w) & (row8 < n_g), row8, no_row)
        el = lt[_ROUTER_EXPERT_LANE:_ROUTER_EXPERT_LANE + n_e, :]
        for g in range(1, n_g):
            lo = _ROUTER_EXPERT_LANE + g * n_e
            el = jnp.where(g_sel == g, lt[lo:lo + n_e, :], el)
        ee = jnp.exp(el - jnp.max(el, axis=0, keepdims=True))
        ep = ee / jnp.sum(ee, axis=0, keepdims=True)
        p1 = jnp.max(ep, axis=0, keepdims=True)
        i1 = _first_row_where(ep == p1, row8, no_row)
        ep2 = jnp.where(row8 == i1, -1.0, ep)
        p2 = jnp.max(ep2, axis=0, keepdims=True)
        i2 = _first_row_where(ep2 == p2, row8, no_row)
        denom = p1 + p2
        w0 = g_w * p1 / denom
        w1 = g_w * p2 / denom
        e0 = g_sel * n_e + i1
        e1 = g_sel * n_e + i2
        oh0 = jnp.where(row_e == e0, 1.0, 0.0)
        oh1 = jnp.where(row_e == e1, 1.0, 0.0)
        oh = oh0 + oh1
        local = jnp.dot(oh.astype(BF16), tri, preferred_element_type=F32)
        return (e0, e1, w0, w1), oh0, oh1, local, jnp.sum(oh, axis=1, keepdims=True)

    group = _PLAN_GROUP if (n_tok // tb) % _PLAN_GROUP == 0 else 1

    def route(it, carry):
        starts = [pl.multiple_of((it * group + k) * tb, tb) for k in range(group)]
        blocks = [route_block(r0) for r0 in starts]
        tiles = []
        for (e0, e1, w0, w1), oh0, oh1, local, total in blocks:
            before = local + carry
            rank0 = jnp.sum(oh0 * before, axis=0, keepdims=True)
            rank1 = jnp.sum(oh1 * before, axis=0, keepdims=True)
            tiles.append(_rows_to_tile([e0, e1, w0, w1, rank0, rank1], row8))
            carry = carry + total
        for r0, tile in zip(starts, tiles):
            info_ref[:, pl.ds(r0, tb)] = tile
        return carry

    counts_col = lax.fori_loop(0, n_tok // (tb * group), route, jnp.zeros((N_EXPERTS, 1), F32))
    counts = _row_from_col(jnp.concatenate([counts_col, jnp.zeros((LANES - N_EXPERTS, 1), F32)], axis=0))

    lane1 = lax.broadcasted_iota(I32, (1, LANES), 1)
    padded = jnp.floor((counts + (FFN_TILE - 1.0)) * (1.0 / FFN_TILE)) * FFN_TILE
    pad_end = padded
    sh = 1
    while sh < N_EXPERTS:
        pad_end = pad_end + jnp.where(lane1 >= sh, pltpu.roll(pad_end, sh, axis=1), 0.0)
        sh *= 2
    pad_start = pad_end - padded

    start_col = _col_from_row(pad_start)[0:N_EXPERTS, :]

    def place(it, _):
        starts = [pl.multiple_of((it * group + k) * tb, tb) for k in range(group)]
        tiles = [info_ref[:, pl.ds(r0, tb)] for r0 in starts]
        placed = []
        for tile in tiles:
            s0 = jnp.sum(jnp.where(row_e == tile[_L_E0:_L_E0 + 1, :], start_col, 0.0), axis=0, keepdims=True)
            s1 = jnp.sum(jnp.where(row_e == tile[_L_E1:_L_E1 + 1, :], start_col, 0.0), axis=0, keepdims=True)
            placed.append(tile + jnp.where(row8 == _L_D0, s0, jnp.where(row8 == _L_D1, s1, 0.0)))
        for r0, tile in zip(starts, placed):
            info_ref[:, pl.ds(r0, tb)] = tile
            dest_ref[:, pl.ds(r0, tb)] = tile.astype(I32)
        return 0

    lax.fori_loop(0, n_tok // (tb * group), place, 0)

    pe_col = _col_from_row(pad_end)
    mt = meta_ref.shape[1]
    t_start = lax.broadcasted_iota(I32, (LANES, mt), 1).astype(F32) * FFN_TILE
    e_row = lax.broadcasted_iota(I32, (LANES, mt), 0)
    ended = jnp.where((pe_col <= t_start) & (e_row < N_EXPERTS), 1.0, 0.0)
    tile_e = jnp.minimum(jnp.sum(ended, axis=0, keepdims=True), N_EXPERTS - 1.0)
    total = jnp.sum(jnp.where(lane1 == N_EXPERTS - 1, pad_end, 0.0), axis=1, keepdims=True)
    n_valid = jnp.broadcast_to(total * (1.0 / FFN_TILE), (1, mt))
    def per_expert(v):
        return v if mt == LANES else jnp.concatenate([v, jnp.zeros((1, mt - LANES), F32)], axis=1)

    mrow = lax.broadcasted_iota(I32, (SUBLANES, mt), 0)
    meta = jnp.where(mrow == _M_TILE_E, tile_e,
           jnp.where(mrow == _M_NVALID, n_valid,
           jnp.where(mrow == _M_NTILES, per_expert(padded * (1.0 / FFN_TILE)),
           jnp.where(mrow == _M_COUNT, per_expert(counts),
           jnp.where(mrow == _M_START, per_expert(pad_start), 0.0)))))
    meta_ref[...] = meta.astype(I32)


def _plan(logits, n_tiles):
    n_tok = logits.shape[0]
    mt = -(-n_tiles // LANES) * LANES
    kern = functools.partial(_plan_kernel, n_tok=n_tok)
    return pl.pallas_call(
        kern,
        out_shape=[jax.ShapeDtypeStruct((SUBLANES, n_tok), F32),
                   jax.ShapeDtypeStruct((SUBLANES, n_tok), I32),
                   jax.ShapeDtypeStruct((SUBLANES, mt), I32)],
        compiler_params=pltpu.CompilerParams(vmem_limit_bytes=VMEM_LIMIT),
        name="plan",
    )(logits)


_DMA_UNROLL = 8


def _invert_kernel(d0_ref, d1_ref, nv_ref, ntl_ref, cnt_ref, start_ref, tok_ref, *, n_tok, p_rows):
    def zero_rows(lo, hi):
        def zbody(u, _):
            for s in range(_DMA_UNROLL):
                tok_ref[jnp.minimum(lo + u * _DMA_UNROLL + s, hi - 1)] = 0
            return 0
        lax.fori_loop(0, (hi - lo + _DMA_UNROLL - 1) // _DMA_UNROLL, zbody, 0)

    def pad_body(e, _):
        zero_rows(start_ref[e] + cnt_ref[e], start_ref[e] + ntl_ref[e] * FFN_TILE)
        return 0

    lax.fori_loop(0, N_EXPERTS, pad_body, 0)
    zero_rows(nv_ref[0] * FFN_TILE, p_rows)

    def body(u, _):
        t0 = u * _DMA_UNROLL
        rows = [(d0_ref[t0 + s], d1_ref[t0 + s]) for s in range(_DMA_UNROLL)]
        for s, (r0, r1) in enumerate(rows):
            tok_ref[r0] = t0 + s
            tok_ref[r1] = t0 + s
        return 0

    lax.fori_loop(0, n_tok // _DMA_UNROLL, body, 0)


def _invert(d0, d1, n_valid, ntl, cnt, start, p_rows):
    n_tok = d0.shape[0]
    kern = functools.partial(_invert_kernel, n_tok=n_tok, p_rows=p_rows)
    return pl.pallas_call(
        kern,
        grid_spec=pltpu.PrefetchScalarGridSpec(
            num_scalar_prefetch=6,
            grid=(1,),
            in_specs=[],
            out_specs=pl.BlockSpec(memory_space=pltpu.SMEM)),
        out_shape=jax.ShapeDtypeStruct((p_rows,), I32),
        compiler_params=_cparams(("arbitrary",)),
        name="invert",
    )(d0, d1, n_valid, ntl, cnt, start)


_FFN_SLOTS = 3
_BULK_DMA_PRIORITY = 1


def _ffn_kernel(te_ref, nv_ref, ntl_ref, tok_ref, h_ref, w1_ref, w3_ref, w2_ref, y_ref,
                xbuf, xb_ref, w1f, w3f, w2f, w1b, w3b, w2b, gsem, wsem, run_ref, *, n_tiles):
    i = pl.program_id(0)
    nv = nv_ref[0]
    valid = i < nv
    slot = i % _FFN_SLOTS
    nc = _token_pitch(xb_ref.shape[1])

    def row_copy(tile, r, dst_slot):
        tok = tok_ref[tile * FFN_TILE + r]
        src = pl.ds(tok * nc, nc)
        return pltpu.make_async_copy(h_ref.at[src], xbuf.at[dst_slot, pl.ds(r * nc, nc)], gsem.at[dst_slot])

    def weight_copies(e, ws):
        return (pltpu.make_async_copy(w1_ref.at[e], w1f.at[ws], wsem.at[ws]),
                pltpu.make_async_copy(w3_ref.at[e], w3f.at[ws], wsem.at[ws]),
                pltpu.make_async_copy(w2_ref.at[e], w2f.at[ws], wsem.at[ws]))

    @pl.when(i == 0)
    def _():
        run_ref[0] = 0
        for cp in weight_copies(te_ref[0], 0):
            cp.start(priority=_BULK_DMA_PRIORITY)

        def first_rows(u, _):
            for s in range(_DMA_UNROLL):
                for t in range(_FFN_SLOTS - 1):
                    row_copy(t, u * _DMA_UNROLL + s, t).start()
            return 0
        lax.fori_loop(0, FFN_TILE // _DMA_UNROLL, first_rows, 0)

    @pl.when(valid & ((i == 0) | (te_ref[i] != te_ref[jnp.maximum(i, 1) - 1])))
    def _():
        run = run_ref[0]
        ws = run % 2
        for cp in weight_copies(te_ref[i], ws):
            cp.wait()
        w1b[...] = w1f[ws].astype(BF16)
        w3b[...] = w3f[ws].astype(BF16)
        w2b[...] = w2f[ws].astype(BF16)
        nxt = i + ntl_ref[te_ref[i]]

        @pl.when(nxt < nv)
        def _():
            for cp in weight_copies(te_ref[jnp.minimum(nxt, n_tiles - 1)], 1 - ws):
                cp.start(priority=_BULK_DMA_PRIORITY)
        run_ref[0] = run + 1

    @pl.when(valid)
    def _():
        pltpu.make_async_copy(h_ref.at[pl.ds(0, FFN_TILE * nc)], xbuf.at[slot], gsem.at[slot]).wait()
        xb_ref[...] = _load_token_major(xbuf.at[slot], FFN_TILE, xb_ref.shape[1], BF16)
        ahead = _FFN_SLOTS - 1
        nxt_tile = jnp.minimum(i + ahead, n_tiles - 1)
        nxt_slot = (i + ahead) % _FFN_SLOTS
        for r in range(FFN_TILE):
            row_copy(nxt_tile, r, nxt_slot).start()
        x = xb_ref[...]
        a = jnp.dot(x, w1b[...], preferred_element_type=F32)
        b = jnp.dot(x, w3b[...], preferred_element_type=F32)
        hmid = (_silu(a) * b).astype(BF16)
        _store_token_major(y_ref, jnp.dot(hmid, w2b[...], preferred_element_type=F32))

    @pl.when(jnp.logical_not(valid))
    def _():
        y_ref[...] = jnp.zeros_like(y_ref)

    @pl.when(i == n_tiles - 1)
    def _():
        for t in range(_FFN_SLOTS - 1):
            s = (nv + t) % _FFN_SLOTS
            pltpu.make_async_copy(h_ref.at[pl.ds(0, FFN_TILE * nc)], xbuf.at[s], gsem.at[s]).wait()


def _ffn(tile_e, n_valid, ntl, row_tok, h2, w1, w3, w2):
    p_rows = row_tok.shape[0]
    _, d, de = w1.shape
    n_tiles = p_rows // FFN_TILE
    kern = functools.partial(_ffn_kernel, n_tiles=n_tiles)
    any_spec = pl.BlockSpec(memory_space=pl.ANY)
    return pl.pallas_call(
        kern,
        grid_spec=pltpu.PrefetchScalarGridSpec(
            num_scalar_prefetch=4,
            grid=(n_tiles,),
            in_specs=[any_spec, any_spec, any_spec, any_spec],
            out_specs=pl.BlockSpec((FFN_TILE * _token_pitch(d), LANES), lambda i, *_: (i, 0)),
            scratch_shapes=[pltpu.VMEM((_FFN_SLOTS, FFN_TILE * _token_pitch(d), LANES), F32),
                            pltpu.VMEM((FFN_TILE, d), BF16),
                            pltpu.VMEM((2, d, de), F32), pltpu.VMEM((2, d, de), F32),
                            pltpu.VMEM((2, de, d), F32),
                            pltpu.VMEM((d, de), BF16), pltpu.VMEM((d, de), BF16),
                            pltpu.VMEM((de, d), BF16),
                            pltpu.SemaphoreType.DMA((_FFN_SLOTS,)), pltpu.SemaphoreType.DMA((2,)),
                            pltpu.SMEM((1,), I32)]),
        out_shape=jax.ShapeDtypeStruct((p_rows * _token_pitch(d), LANES), F32),
        compiler_params=_cparams(("arbitrary",)),
        name="ffn",
    )(tile_e, n_valid, ntl, row_tok, h2, w1, w3, w2)


def _combine_kernel(d0_ref, d1_ref, x1_ref, info_ref, mod_ref, nw_ref, y_ref, o_ref,
                    *scratch, tp, n_phases):
    i = pl.program_id(0)
    ns = _COMBINE_SETS
    sem = scratch[-1]
    bufs = [(scratch[2 * q], scratch[2 * q + 1]) for q in range(ns)]
    pitch = _token_pitch(x1_ref.shape[1])

    def row_copies(phase, r, q):
        tok = phase * tp + r
        ya, yb = bufs[q]
        dst = pl.ds(r * pitch, pitch)
        return (pltpu.make_async_copy(y_ref.at[pl.ds(d0_ref[tok] * pitch, pitch)], ya.at[dst], sem.at[q]),
                pltpu.make_async_copy(y_ref.at[pl.ds(d1_ref[tok] * pitch, pitch)], yb.at[dst], sem.at[q]))

    def wait_set(q):
        for buf in bufs[q]:
            pltpu.make_async_copy(y_ref.at[pl.ds(0, tp * pitch)], buf, sem.at[q]).wait()

    @pl.when(i == 0)
    def _():
        def first_rows(u, _):
            for s in range(_DMA_UNROLL):
                for q in range(ns - 1):
                    for cp in row_copies(q, u * _DMA_UNROLL + s, q):
                        cp.start()
            return 0
        lax.fori_loop(0, tp // _DMA_UNROLL, first_rows, 0)

    for q in range(ns):
        wait_set(q)
        ahead = jnp.minimum(i * ns + q + ns - 1, n_phases - 1)
        for r in range(tp):
            for prio, cp in enumerate(row_copies(ahead, r, (q + ns - 1) % ns)):
                cp.start(priority=prio)
        ya, yb = bufs[q]
        rows = pl.ds(q * tp, tp)
        w0 = _col_from_row(info_ref[_L_W0:_L_W0 + 1, q * tp:(q + 1) * tp])
        w1 = _col_from_row(info_ref[_L_W1:_L_W1 + 1, q * tp:(q + 1) * tp])
        d = x1_ref.shape[1]
        moe = _load_token_major(ya, tp, d, F32) * w0 + _load_token_major(yb, tp, d, F32) * w1
        x2 = x1_ref[rows, :] + mod_ref[0][5:6] * moe
        o_ref[rows, :] = x2 * lax.rsqrt(jnp.mean(x2 * x2, axis=-1, keepdims=True) + NORM_EPS) * nw_ref[...]

    @pl.when(i == n_phases // ns - 1)
    def _():
        for q in range(ns - 1):
            wait_set(q)


_COMBINE_SETS = 4


def _combine(d0, d1, x1, info, mod3, nfw, y, seq, tp=LANES):
    assert tp == LANES, "one phase's combine weights must fill exactly one lane-dense row"
    n, d = x1.shape
    tc = _COMBINE_SETS * tp
    per_b = seq // tc
    kern = functools.partial(_combine_kernel, tp=tp, n_phases=n // tp)
    return pl.pallas_call(
        kern,
        grid_spec=pltpu.PrefetchScalarGridSpec(
            num_scalar_prefetch=2,
            grid=(n // tc,),
            in_specs=[pl.BlockSpec((tc, d), lambda i, d0, d1: (i, 0)),
                      pl.BlockSpec((SUBLANES, tc), lambda i, d0, d1: (0, i)),
                      pl.BlockSpec((1, N_MOD, d), lambda i, d0, d1: (i // per_b, 0, 0)),
                      pl.BlockSpec((1, d), lambda i, d0, d1: (0, 0)),
                      pl.BlockSpec(memory_space=pl.ANY)],
            out_specs=pl.BlockSpec((tc, d), lambda i, d0, d1: (i, 0)),
            scratch_shapes=[pltpu.VMEM((tp * _token_pitch(d), LANES), F32)] * (2 * _COMBINE_SETS)
                           + [pltpu.SemaphoreType.DMA((_COMBINE_SETS,))]),
        out_shape=jax.ShapeDtypeStruct((n, d), F32),
        compiler_params=_cparams(("arbitrary",)),
        name="combine",
    )(d0, d1, x1, info, mod3, nfw, y)


def _split_hi_lo(w):
    hi = w.astype(BF16)
    lo = (w - hi.astype(F32)).astype(BF16)
    return hi, lo


def kernel(x, c, w_ada, b_ada, norm_mix_w, w_in, ret_norm_w, w_out, norm_ffn_w,
           w_group, b_group, w_router, b_router, w1, w3, w2, norm_final_w):
    batch, seq, d = x.shape
    n = batch * seq
    depth = w_ada.shape[0]
    assert depth == 1, "the final rmsnorm is fused into the (single) layer's last kernel"
    tm = min(INPROJ_ROWS, seq)
    half = LANES // 2

    slopes = jnp.exp2(-8.0 * jnp.arange(1, N_ATTN_HEADS + 1, dtype=F32) / N_ATTN_HEADS)
    log_gamma = jnp.log1p(-jnp.exp2(-5.0 - jnp.arange(RET_HEADS, dtype=F32)))
    n_tiles = (2 * n) // FFN_TILE + N_EXPERTS
    p_rows = n_tiles * FFN_TILE

    assert batch <= SUBLANES
    c_pad = jnp.zeros((SUBLANES, d), F32).at[:batch].set(c)
    x2d = x.reshape(n, d)
    for l in range(depth):
        mod = _adaln(c_pad, w_ada[l], b_ada[l].reshape(1, -1))
        mod3 = mod[:batch].reshape(batch, N_MOD, d)

        proj, kmean = _inproj(x2d, mod3, norm_mix_w[l].reshape(1, d), w_in[l], seq, tm)
        kmean = kmean.reshape(batch, seq // MOBA_BLOCK, ATTN_WIDTH)
        attn = _moba(proj, kmean, slopes, batch, seq)
        ret = _retention(proj, log_gamma, ret_norm_w[l].reshape(1, RET_WIDTH), batch, seq)

        gap = _ROUTER_EXPERT_LANE - N_GROUPS
        wr = jnp.concatenate([w_group[l], jnp.zeros((d, gap), F32),
                              jnp.transpose(w_router[l], (1, 0, 2)).reshape(d, N_EXPERTS)], axis=1)
        wr = jnp.pad(wr, ((0, 0), (0, half - wr.shape[1])))
        wr_hi, wr_lo = _split_hi_lo(wr)
        wr_cat = jnp.concatenate([jnp.concatenate([wr_hi, wr_lo], axis=1),
                                  jnp.concatenate([wr_hi, jnp.zeros_like(wr_lo)], axis=1)], axis=0)
        br = jnp.concatenate([b_group[l], jnp.zeros((gap,), F32), b_router[l].reshape(-1)])
        br = jnp.pad(br, (0, LANES - br.shape[0])).reshape(1, LANES)

        x1, h2p, logits = _outproj(attn, ret, w_out[l], x2d, mod3,
                                   norm_ffn_w[l].reshape(1, d), wr_cat, br, seq, min(OUTPROJ_ROWS, seq))

        info, dest, meta = _plan(logits, n_tiles)
        d0 = dest[_L_D0]
        d1 = dest[_L_D1]
        tile_e = meta[_M_TILE_E, :n_tiles]
        n_valid = meta[_M_NVALID, :1]
        ntl = meta[_M_NTILES, :N_EXPERTS]

        row_tok = _invert(d0, d1, n_valid, ntl, meta[_M_COUNT, :N_EXPERTS],
                          meta[_M_START, :N_EXPERTS], p_rows)
        y = _ffn(tile_e, n_valid, ntl, row_tok, h2p, w1[l], w3[l], w2[l])
        nfw = norm_final_w.reshape(1, d)
        x2d = _combine(d0, d1, x1, info, mod3, nfw, y, seq)
    return x2d.reshape(batch, seq, d)
```

```python
import functools

import jax
import jax.numpy as jnp
from jax import lax
from jax.experimental import pallas as pl
from jax.experimental.pallas import tpu as pltpu

F32 = jnp.float32
BF16 = jnp.bfloat16
I32 = jnp.int32

N_ATTN_HEADS = 8
ATTN_HEAD_DIM = 128
ATTN_WIDTH = N_ATTN_HEADS * ATTN_HEAD_DIM
MOBA_BLOCK = 256
MOBA_TOPK = 3
RET_HEADS = 4
RET_HEAD_DIM = 256
RET_WIDTH = RET_HEADS * RET_HEAD_DIM
N_GROUPS = 4
EXPERTS_PER_GROUP = 8
N_EXPERTS = N_GROUPS * EXPERTS_PER_GROUP
N_MOD = 6
NORM_EPS = 1e-6

LANES = 128
SUBLANES = 8
RET_CHUNK = 256
FFN_TILE = 256
INPROJ_ROWS = 1024
OUTPROJ_ROWS = 256
MASK_VALUE = -1e30
LOG2E = 1.4426950408889634
MOBA_QK_SCALE = ATTN_HEAD_DIM ** -0.5 * LOG2E
_MOBA_EXTRA_ROWS = 16
VMEM_LIMIT = 56 * 1024 * 1024

_DN_LAST = (((1,), (1,)), ((), ()))
_DN_FIRST = (((0,), (0,)), ((), ()))


def _cparams(sem):
    return pltpu.CompilerParams(dimension_semantics=sem, vmem_limit_bytes=VMEM_LIMIT)


def _silu(v):
    return v * jax.nn.sigmoid(v)


def _token_pitch(d):
    return d // LANES + 1


def _store_token_major(ref, value):
    rows, d = value.shape
    pitch = _token_pitch(d)
    for c in range(pitch - 1):
        ref[pl.ds(c, rows, stride=pitch), :] = value[:, c * LANES:(c + 1) * LANES]
    ref[pl.ds(pitch - 1, rows, stride=pitch), :] = jnp.zeros((rows, LANES), value.dtype)


def _load_token_major(ref, rows, d, dtype):
    pitch = _token_pitch(d)
    return jnp.concatenate([ref[pl.ds(c, rows, stride=pitch), :].astype(dtype) for c in range(pitch - 1)],
                           axis=1)


def _adaln_kernel(c_ref, w_ref, b_ref, o_ref):
    ca = _silu(c_ref[...]).astype(BF16)
    o_ref[...] = jnp.dot(ca, w_ref[...].astype(BF16), preferred_element_type=F32) + b_ref[...]


def _adaln(c_pad, w_ada, b_ada, n, tn=1024):
    rows, d = c_pad.shape
    return pl.pallas_call(
        _adaln_kernel,
        grid=(n // tn,),
        in_specs=[pl.BlockSpec((rows, d), lambda j: (0, 0)),
                  pl.BlockSpec((d, tn), lambda j: (0, j)),
                  pl.BlockSpec((1, tn), lambda j: (0, j))],
        out_specs=pl.BlockSpec((rows, tn), lambda j: (0, j)),
        out_shape=jax.ShapeDtypeStruct((rows, n), F32),
        compiler_params=_cparams(("arbitrary",)),
        name="adaln",
    )(c_pad, w_ada, b_ada)


def _inproj_kernel(x_ref, mod_ref, nw_ref, w_ref, o_ref, km_ref, h_ref, *, tm, tn):
    j = pl.program_id(1)

    @pl.when(j == 0)
    def _():
        x = x_ref[...]
        y = x * lax.rsqrt(jnp.mean(x * x, axis=-1, keepdims=True) + NORM_EPS) * nw_ref[...]
        m = mod_ref[0]
        h_ref[...] = (y * (1.0 + m[1:2]) + m[0:1]).astype(BF16)

    acc = jnp.dot(h_ref[...], w_ref[...].astype(BF16), preferred_element_type=F32)
    o_ref[...] = (acc * jnp.where(j == 0, MOBA_QK_SCALE, 1.0)).astype(BF16)

    @pl.when(j == 1)
    def _():
        km_ref[0] = jnp.sum(acc.reshape(tm // MOBA_BLOCK, MOBA_BLOCK, tn), axis=1) * (1.0 / MOBA_BLOCK)


def _inproj(x2d, mod3, nw, w_bf, seq, tm, tn=ATTN_WIDTH):
    n, d = x2d.shape
    width = w_bf.shape[1]
    per_b = seq // tm
    kern = functools.partial(_inproj_kernel, tm=tm, tn=tn)
    return pl.pallas_call(
        kern,
        grid=(n // tm, width // tn),
        in_specs=[pl.BlockSpec((tm, d), lambda i, j: (i, 0)),
                  pl.BlockSpec((1, mod3.shape[1], d), lambda i, j: (i // per_b, 0, 0)),
                  pl.BlockSpec((1, d), lambda i, j: (0, 0)),
                  pl.BlockSpec((d, tn), lambda i, j: (0, j))],
        out_specs=[pl.BlockSpec((tm, tn), lambda i, j: (i, j)),
                   pl.BlockSpec((1, tm // MOBA_BLOCK, tn), lambda i, j: (i, 0, 0))],
        out_shape=[jax.ShapeDtypeStruct((n, width), BF16),
                   jax.ShapeDtypeStruct((n // tm, tm // MOBA_BLOCK, tn), F32)],
        scratch_shapes=[pltpu.VMEM((tm, d), BF16)],
        compiler_params=_cparams(("arbitrary", "arbitrary")),
        name="inproj",
    )(x2d, mod3, nw, w_bf)


def _moba_kernel(slopes_ref, q_ref, k_ref, v_ref, km_ref, o_ref,
                 vt_ref, sel_ref, bias_ref, acc_ref, *, nb, hps):
    hg = pl.program_id(1)
    i = pl.program_id(2)
    blk = MOBA_BLOCK
    dh = ATTN_HEAD_DIM
    log2e = LOG2E

    qpos = lax.broadcasted_iota(I32, (blk, blk), 1)
    kpos = lax.broadcasted_iota(I32, (blk, blk), 0)

    vt_rows = dh + _MOBA_EXTRA_ROWS

    @pl.when(i == 0)
    def _():
        vt = v_ref[...].astype(F32).T.astype(BF16)
        extra = lax.broadcasted_iota(I32, (_MOBA_EXTRA_ROWS, vt.shape[1]), 0)
        ones_row = jnp.where(extra == 0, 1.0, 0.0).astype(BF16)
        key_pos = lax.broadcasted_iota(I32, (blk, LANES), 0).astype(F32)
        for g in range(hps):
            vt_ref[g * vt_rows:g * vt_rows + dh, :] = vt[g * dh:(g + 1) * dh, :]
            vt_ref[g * vt_rows + dh:(g + 1) * vt_rows, :] = ones_row
            bias_ref[g] = key_pos * (log2e * slopes_ref[hg * hps + g])

    bidx = lax.broadcasted_iota(I32, (nb, blk), 0)
    past = bidx < i
    i0 = pl.multiple_of(i * blk, blk)
    heads = [slice(g * dh, (g + 1) * dh) for g in range(hps)]
    vrows = [slice(g * vt_rows, (g + 1) * vt_rows) for g in range(hps)]
    gates, raw_own = [], []
    for cols in heads:
        q = q_ref[:, cols]
        km = km_ref[0, :, cols]
        km_hi = km.astype(BF16)
        km_lo = (km - km_hi.astype(F32)).astype(BF16)
        gates.append(lax.dot_general(km_hi, q, _DN_LAST, preferred_element_type=F32)
                     + lax.dot_general(km_lo, q, _DN_LAST, preferred_element_type=F32))
        raw_own.append(lax.dot_general(k_ref[pl.ds(i0, blk), cols], q, _DN_LAST,
                                       preferred_element_type=F32))
    def key_bias(g):
        kb = bias_ref[g]
        return jnp.concatenate([kb] * (blk // LANES), axis=1)

    own_bias = [key_bias(g) for g in range(hps)]

    carry0, acc0, sels = [], [], []
    for g, cols in enumerate(heads):
        gate = jnp.where(past, gates[g], -jnp.inf)
        rank = jnp.zeros((nb, blk), F32)
        for m in range(nb):
            gm = gate[m:m + 1, :]
            beats = jnp.where(gm > gate, 1.0, jnp.where((gm == gate) & (m < bidx), 1.0, 0.0))
            rank = rank + jnp.where(m < i, beats, 0.0)
        sels.append(jnp.where(past & (rank < MOBA_TOPK), 1.0, 0.0))

        s = jnp.where(qpos >= kpos, own_bias[g] + raw_own[g], MASK_VALUE)
        m0 = jnp.max(s, axis=0, keepdims=True)
        p = jnp.exp2(s - m0)
        acc0.append(jnp.dot(vt_ref[vrows[g], pl.ds(i0, blk)], p.astype(BF16), preferred_element_type=F32))
        carry0.append(m0)
    for g in range(hps):
        acc_ref[g] = acc0[g]
        sel_ref[g] = sels[g]

    def body(j, carry):
        j0 = pl.multiple_of(j * blk, blk)
        off = ((i - j) * blk).astype(F32) * log2e
        raw = [lax.dot_general(k_ref[pl.ds(j0, blk), cols], q_ref[:, cols], _DN_LAST,
                               preferred_element_type=F32) for cols in heads]
        chosen = [sel_ref[g, pl.ds(j, 1), :] > 0.0 for g in range(hps)]
        acc_old = [acc_ref[g] for g in range(hps)]
        out, acc_new = [], []
        for g in range(hps):
            m_run = carry[g]
            shift = -slopes_ref[hg * hps + g] * off
            s = key_bias(g) + raw[g]
            m_new = jnp.maximum(m_run, jnp.where(chosen[g], jnp.max(s, axis=0, keepdims=True) + shift,
                                                 MASK_VALUE))
            alpha = jnp.exp2(m_run - m_new)
            p = jnp.exp2(s - jnp.where(chosen[g], m_new - shift, -MASK_VALUE))
            acc_new.append(alpha * acc_old[g] + jnp.dot(vt_ref[vrows[g], pl.ds(j0, blk)], p.astype(BF16),
                                                        preferred_element_type=F32))
            out.append(m_new)
        for g in range(hps):
            acc_ref[g] = acc_new[g]
        return tuple(out)

    lax.fori_loop(0, i, body, tuple(carry0))
    for g in range(hps):
        acc = acc_ref[g]
        o_ref[:, g * dh:(g + 1) * dh] = (acc[:dh] / acc[dh:dh + 1]).T.astype(BF16)


MOBA_HEADS_PER_STEP = 8


def _moba(proj, kmean, slopes, batch, seq):
    n = proj.shape[0]
    nb = seq // MOBA_BLOCK
    hps = MOBA_HEADS_PER_STEP
    vt_rows = ATTN_HEAD_DIM + _MOBA_EXTRA_ROWS
    w = hps * ATTN_HEAD_DIM
    ng = N_ATTN_HEADS // hps
    kern = functools.partial(_moba_kernel, nb=nb, hps=hps)
    return pl.pallas_call(
        kern,
        grid=(batch, ng, nb),
        in_specs=[pl.BlockSpec(memory_space=pltpu.SMEM),
                  pl.BlockSpec((MOBA_BLOCK, w), lambda b, h, i: (b * nb + i, h)),
                  pl.BlockSpec((seq, w), lambda b, h, i: (b, ng + h)),
                  pl.BlockSpec((seq, w), lambda b, h, i: (b, 2 * ng + h)),
                  pl.BlockSpec((1, nb, w), lambda b, h, i: (b, 0, h))],
        out_specs=pl.BlockSpec((MOBA_BLOCK, w), lambda b, h, i: (b * nb + i, h)),
        out_shape=jax.ShapeDtypeStruct((n, ATTN_WIDTH), BF16),
        scratch_shapes=[pltpu.VMEM((hps * vt_rows, seq), BF16),
                        pltpu.VMEM((hps, nb, MOBA_BLOCK), F32),
                        pltpu.VMEM((hps, MOBA_BLOCK, LANES), F32),
                        pltpu.VMEM((hps, vt_rows, MOBA_BLOCK), F32)],
        compiler_params=_cparams(("arbitrary", "arbitrary", "arbitrary")),
        name="moba",
    )(slopes, proj, proj, proj, kmean)


def _ret_kernel(lg_ref, q_ref, k_ref, v_ref, g_ref, rnw_ref, c_ref, wa_ref, ba_ref, o_ref, mod_ref,
                state_ref, dmask_ref, qdec_ref, kdec_ref, cdec_ref):
    mod_ref[...] = jnp.dot(_silu(c_ref[...]).astype(BF16), wa_ref[...].astype(BF16),
                           preferred_element_type=F32) + ba_ref[...]
    n = pl.program_id(1)
    c = RET_CHUNK
    dk = RET_HEAD_DIM
    kscale = dk ** -0.5

    @pl.when(n == 0)
    def _():
        state_ref[...] = jnp.zeros_like(state_ref)
        ti = lax.broadcasted_iota(I32, (c, c), 0)
        si = lax.broadcasted_iota(I32, (c, c), 1)
        diff = jnp.maximum(ti - si, 0).astype(F32)
        pos = lax.broadcasted_iota(I32, (c, dk), 0).astype(F32)
        for h in range(RET_HEADS):
            lg = lg_ref[h]
            dmask_ref[h] = jnp.where(ti >= si, jnp.exp(diff * lg) * kscale, 0.0)
            qdec_ref[h] = jnp.exp((pos + 1.0) * lg)
            kdec_ref[h] = jnp.exp((c - 1.0 - pos) * lg) * kscale
            cdec_ref[h] = jnp.exp(jnp.full((1, dk), float(c), F32) * lg)

    heads = [slice(h * dk, (h + 1) * dk) for h in range(RET_HEADS)]
    qs = [q_ref[:, cols] for cols in heads]
    ks = [k_ref[:, cols] for cols in heads]
    vs = [v_ref[:, cols] for cols in heads]
    states = [state_ref[h] for h in range(RET_HEADS)]
    raw = [lax.dot_general(qs[h], ks[h], _DN_LAST, preferred_element_type=F32) for h in range(RET_HEADS)]
    cross = [jnp.dot(qs[h], states[h].astype(BF16), preferred_element_type=F32) for h in range(RET_HEADS)]
    outs, new_states = [], []
    for h, cols in enumerate(heads):
        inner = jnp.dot((raw[h] * dmask_ref[h]).astype(BF16), vs[h], preferred_element_type=F32)
        kd = (ks[h].astype(F32) * kdec_ref[h]).astype(BF16)
        kv = lax.dot_general(kd, vs[h], _DN_FIRST, preferred_element_type=F32)
        new_states.append(states[h] * cdec_ref[h] + kv)
        r = inner + cross[h] * qdec_ref[h]
        rn = r * lax.rsqrt(jnp.mean(r * r, axis=-1, keepdims=True) + NORM_EPS)
        g = g_ref[:, cols].astype(F32)
        outs.append((rn * rnw_ref[:, cols] * _silu(g)).astype(BF16))
    for h, cols in enumerate(heads):
        state_ref[h] = new_states[h]
        o_ref[:, cols] = outs[h]


def _retention(proj, log_gamma, rnw, c_pad, w_ada, b_ada, mod_done, batch, seq):
    n = proj.shape[0]
    c = RET_CHUNK
    nc = seq // c
    dk = RET_HEAD_DIM
    base = 3 * ATTN_WIDTH // RET_WIDTH
    rows, d = c_pad.shape
    slab = (w_ada.shape[1] - mod_done) // (batch * nc)
    assert slab % LANES == 0 and slab * batch * nc == w_ada.shape[1] - mod_done
    first = mod_done // slab

    def col(off):
        return lambda b, t: (b * nc + t, base + off)

    def mod_col(b, t):
        return (0, first + b * nc + t)

    return pl.pallas_call(
        _ret_kernel,
        grid=(batch, nc),
        in_specs=[pl.BlockSpec(memory_space=pltpu.SMEM),
                  pl.BlockSpec((c, RET_WIDTH), col(0)),
                  pl.BlockSpec((c, RET_WIDTH), col(1)),
                  pl.BlockSpec((c, RET_WIDTH), col(2)),
                  pl.BlockSpec((c, RET_WIDTH), col(3)),
                  pl.BlockSpec((1, RET_WIDTH), lambda b, t: (0, 0)),
                  pl.BlockSpec((rows, d), lambda b, t: (0, 0)),
                  pl.BlockSpec((d, slab), mod_col),
                  pl.BlockSpec((1, slab), mod_col)],
        out_specs=[pl.BlockSpec((c, RET_WIDTH), lambda b, t: (b * nc + t, 0)),
                   pl.BlockSpec((rows, slab), lambda b, t: (0, b * nc + t))],
        out_shape=[jax.ShapeDtypeStruct((n, RET_WIDTH), BF16),
                   jax.ShapeDtypeStruct((rows, w_ada.shape[1] - mod_done), F32)],
        scratch_shapes=[pltpu.VMEM((RET_HEADS, dk, dk), F32), pltpu.VMEM((RET_HEADS, c, c), F32),
                        pltpu.VMEM((RET_HEADS, c, dk), F32), pltpu.VMEM((RET_HEADS, c, dk), F32),
                        pltpu.VMEM((RET_HEADS, 1, dk), F32)],
        compiler_params=_cparams(("arbitrary", "arbitrary")),
        name="retention",
    )(log_gamma, proj, proj, proj, proj, rnw, c_pad, w_ada, b_ada)


def _outproj_kernel(a_ref, r_ref, w_ref, x_ref, mod_ref, nw_ref, wr_ref, br_ref,
                    x1_ref, hp_ref, lg_ref, wb_ref):
    @pl.when(pl.program_id(0) == 0)
    def _():
        wb_ref[...] = w_ref[...].astype(BF16)

    mix = jnp.dot(jnp.concatenate([a_ref[...], r_ref[...]], axis=1), wb_ref[...],
                  preferred_element_type=F32)
    m = mod_ref[0]
    x1 = x_ref[...] + m[2:3] * mix
    x1_ref[...] = x1
    y = x1 * lax.rsqrt(jnp.mean(x1 * x1, axis=-1, keepdims=True) + NORM_EPS) * nw_ref[...]
    h2 = y * (1.0 + m[4:5]) + m[3:4]
    _store_token_major(hp_ref, h2)
    hi = h2.astype(BF16)
    lo = (h2 - hi.astype(F32)).astype(BF16)
    part = jnp.dot(jnp.concatenate([hi, lo], axis=1), wr_ref[...], preferred_element_type=F32)
    lg_ref[...] = part + pltpu.roll(part, LANES // 2, axis=1) + br_ref[...]


def _outproj(attn, ret, w_bf, x2d, mod3, nw, wr_cat, br, seq, tm):
    n, d = x2d.shape
    per_b = seq // tm
    return pl.pallas_call(
        _outproj_kernel,
        grid=(n // tm,),
        in_specs=[pl.BlockSpec((tm, ATTN_WIDTH), lambda i: (i, 0)),
                  pl.BlockSpec((tm, RET_WIDTH), lambda i: (i, 0)),
                  pl.BlockSpec((d, d), lambda i: (0, 0), pipeline_mode=pl.Buffered(1)),
                  pl.BlockSpec((tm, d), lambda i: (i, 0)),
                  pl.BlockSpec((1, N_MOD, d), lambda i: (i // per_b, 0, 0)),
                  pl.BlockSpec((1, d), lambda i: (0, 0)),
                  pl.BlockSpec((2 * d, LANES), lambda i: (0, 0)),
                  pl.BlockSpec((1, LANES), lambda i: (0, 0))],
        out_specs=[pl.BlockSpec((tm, d), lambda i: (i, 0)),
                   pl.BlockSpec((tm * _token_pitch(d), LANES), lambda i: (i, 0)),
                   pl.BlockSpec((tm, LANES), lambda i: (i, 0))],
        out_shape=[jax.ShapeDtypeStruct((n, d), F32),
                   jax.ShapeDtypeStruct((n * _token_pitch(d), LANES), F32),
                   jax.ShapeDtypeStruct((n, LANES), F32)],
        scratch_shapes=[pltpu.VMEM((d, d), BF16)],
        compiler_params=_cparams(("arbitrary",)),
        name="outproj",
    )(attn, ret, w_bf, x2d, mod3, nw, wr_cat, br)


_L_E0, _L_E1, _L_W0, _L_W1, _L_D0, _L_D1 = 0, 1, 2, 3, 4, 5
_ROUTER_EXPERT_LANE = 8
_M_TILE_E, _M_NVALID, _M_NTILES, _M_COUNT, _M_START = 0, 1, 2, 3, 4
_PLAN_ROWS = 256
_PLAN_GROUP = 4


def _first_row_where(cond, rows, limit):
    return jnp.min(jnp.where(cond, rows, limit), axis=0, keepdims=True)


def _col_from_row(v):
    r = lax.broadcasted_iota(I32, (LANES, LANES), 0)
    c = lax.broadcasted_iota(I32, (LANES, LANES), 1)
    return jnp.sum(jnp.where(r == c, v, 0.0), axis=1, keepdims=True)


def _row_from_col(v):
    r = lax.broadcasted_iota(I32, (LANES, LANES), 0)
    c = lax.broadcasted_iota(I32, (LANES, LANES), 1)
    return jnp.sum(jnp.where(r == c, v, 0.0), axis=0, keepdims=True)


def _rows_to_tile(rows, row8):
    tile = 0.0
    for k in reversed(range(len(rows))):
        tile = jnp.where(row8 == k, rows[k], tile)
    return tile


def _plan_kernel(lg_ref, info_ref, dest_ref, meta_ref, *, n_tok):
    tb = _PLAN_ROWS
    n_g, n_e = N_GROUPS, EXPERTS_PER_GROUP
    assert n_g <= SUBLANES and n_e == SUBLANES, "group and per-group logits each fill one sublane tile"
    no_row = float(SUBLANES)
    row8 = lax.broadcasted_iota(I32, (SUBLANES, tb), 0).astype(F32)
    row_e = lax.broadcasted_iota(I32, (N_EXPERTS, tb), 0).astype(F32)
    earlier = lax.broadcasted_iota(I32, (tb, tb), 0) < lax.broadcasted_iota(I32, (tb, tb), 1)
    tri = jnp.where(earlier, 1.0, 0.0).astype(BF16)

    def route_block(r0):
        lt = lg_ref[pl.ds(r0, tb), :].T
        gl = jnp.where(row8 < n_g, lt[0:SUBLANES, :], -jnp.inf)
        ge = jnp.exp(gl - jnp.max(gl, axis=0, keepdims=True))
        gp = ge / jnp.sum(ge, axis=0, keepdims=True)
        g_w = jnp.max(gp, axis=0, keepdims=True)
        g_sel = _first_row_where((gp == g_w) & (row8 < n_g), row8, no_row)
        el = lt[_ROUTER_EXPERT_LANE:_ROUTER_EXPERT_LANE + n_e, :]
        for g in range(1, n_g):
            lo = _ROUTER_EXPERT_LANE + g * n_e
            el = jnp.where(g_sel == g, lt[lo:lo + n_e, :], el)
        ee = jnp.exp(el - jnp.max(el, axis=0, keepdims=True))
        ep = ee / jnp.sum(ee, axis=0, keepdims=True)
        p1 = jnp.max(ep, axis=0, keepdims=True)
        i1 = _first_row_where(ep == p1, row8, no_row)
        ep2 = jnp.where(row8 == i1, -1.0, ep)
        p2 = jnp.max(ep2, axis=0, keepdims=True)
        i2 = _first_row_where(ep2 == p2, row8, no_row)
        denom = p1 + p2
        w0 = g_w * p1 / denom
        w1 = g_w * p2 / denom
        e0 = g_sel * n_e + i1
        e1 = g_sel * n_e + i2
        oh0 = jnp.where(row_e == e0, 1.0, 0.0)
        oh1 = jnp.where(row_e == e1, 1.0, 0.0)
        oh = oh0 + oh1
        local = jnp.dot(oh.astype(BF16), tri, preferred_element_type=F32)
        return (e0, e1, w0, w1), oh0, oh1, local, jnp.sum(oh, axis=1, keepdims=True)

    group = _PLAN_GROUP if (n_tok // tb) % _PLAN_GROUP == 0 else 1

    def route(it, carry):
        starts = [pl.multiple_of((it * group + k) * tb, tb) for k in range(group)]
        blocks = [route_block(r0) for r0 in starts]
        tiles = []
        for (e0, e1, w0, w1), oh0, oh1, local, total in blocks:
            before = local + carry
            rank0 = jnp.sum(oh0 * before, axis=0, keepdims=True)
            rank1 = jnp.sum(oh1 * before, axis=0, keepdims=True)
            tiles.append(_rows_to_tile([e0, e1, w0, w1, rank0, rank1], row8))
            carry = carry + total
        for r0, tile in zip(starts, tiles):
            info_ref[:, pl.ds(r0, tb)] = tile
        return carry

    counts_col = lax.fori_loop(0, n_tok // (tb * group), route, jnp.zeros((N_EXPERTS, 1), F32))
    counts = _row_from_col(jnp.concatenate([counts_col, jnp.zeros((LANES - N_EXPERTS, 1), F32)], axis=0))

    lane1 = lax.broadcasted_iota(I32, (1, LANES), 1)
    padded = jnp.floor((counts + (FFN_TILE - 1.0)) * (1.0 / FFN_TILE)) * FFN_TILE
    pad_end = padded
    sh = 1
    while sh < N_EXPERTS:
        pad_end = pad_end + jnp.where(lane1 >= sh, pltpu.roll(pad_end, sh, axis=1), 0.0)
        sh *= 2
    pad_start = pad_end - padded

    start_col = _col_from_row(pad_start)[0:N_EXPERTS, :]

    def place(it, _):
        starts = [pl.multiple_of((it * group + k) * tb, tb) for k in range(group)]
        tiles = [info_ref[:, pl.ds(r0, tb)] for r0 in starts]
        placed = []
        for tile in tiles:
            s0 = jnp.sum(jnp.where(row_e == tile[_L_E0:_L_E0 + 1, :], start_col, 0.0), axis=0, keepdims=True)
            s1 = jnp.sum(jnp.where(row_e == tile[_L_E1:_L_E1 + 1, :], start_col, 0.0), axis=0, keepdims=True)
            placed.append(tile + jnp.where(row8 == _L_D0, s0, jnp.where(row8 == _L_D1, s1, 0.0)))
        for r0, tile in zip(starts, placed):
            info_ref[:, pl.ds(r0, tb)] = tile
            dest_ref[:, pl.ds(r0, tb)] = tile.astype(I32)
        return 0

    lax.fori_loop(0, n_tok // (tb * group), place, 0)

    pe_col = _col_from_row(pad_end)
    mt = meta_ref.shape[1]
    t_start = lax.broadcasted_iota(I32, (LANES, mt), 1).astype(F32) * FFN_TILE
    e_row = lax.broadcasted_iota(I32, (LANES, mt), 0)
    ended = jnp.where((pe_col <= t_start) & (e_row < N_EXPERTS), 1.0, 0.0)
    tile_e = jnp.minimum(jnp.sum(ended, axis=0, keepdims=True), N_EXPERTS - 1.0)
    total = jnp.sum(jnp.where(lane1 == N_EXPERTS - 1, pad_end, 0.0), axis=1, keepdims=True)
    n_valid = jnp.broadcast_to(total * (1.0 / FFN_TILE), (1, mt))
    def per_expert(v):
        return v if mt == LANES else jnp.concatenate([v, jnp.zeros((1, mt - LANES), F32)], axis=1)

    mrow = lax.broadcasted_iota(I32, (SUBLANES, mt), 0)
    meta = jnp.where(mrow == _M_TILE_E, tile_e,
           jnp.where(mrow == _M_NVALID, n_valid,
           jnp.where(mrow == _M_NTILES, per_expert(padded * (1.0 / FFN_TILE)),
           jnp.where(mrow == _M_COUNT, per_expert(counts),
           jnp.where(mrow == _M_START, per_expert(pad_start), 0.0)))))
    meta_ref[...] = meta.astype(I32)


def _plan(logits, n_tiles):
    n_tok = logits.shape[0]
    mt = -(-n_tiles // LANES) * LANES
    kern = functools.partial(_plan_kernel, n_tok=n_tok)
    return pl.pallas_call(
        kern,
        out_shape=[jax.ShapeDtypeStruct((SUBLANES, n_tok), F32),
                   jax.ShapeDtypeStruct((SUBLANES, n_tok), I32),
                   jax.ShapeDtypeStruct((SUBLANES, mt), I32)],
        compiler_params=pltpu.CompilerParams(vmem_limit_bytes=VMEM_LIMIT),
        name="plan",
    )(logits)


_DMA_UNROLL = 8


def _invert_kernel(d0_ref, d1_ref, nv_ref, ntl_ref, cnt_ref, start_ref, tok_ref, *, n_tok, p_rows):
    def zero_rows(lo, hi):
        def zbody(u, _):
            for s in range(_DMA_UNROLL):
                tok_ref[jnp.minimum(lo + u * _DMA_UNROLL + s, hi - 1)] = 0
            return 0
        lax.fori_loop(0, (hi - lo + _DMA_UNROLL - 1) // _DMA_UNROLL, zbody, 0)

    def pad_body(e, _):
        zero_rows(start_ref[e] + cnt_ref[e], start_ref[e] + ntl_ref[e] * FFN_TILE)
        return 0

    lax.fori_loop(0, N_EXPERTS, pad_body, 0)
    zero_rows(nv_ref[0] * FFN_TILE, p_rows)

    def body(u, _):
        t0 = u * _DMA_UNROLL
        rows = [(d0_ref[t0 + s], d1_ref[t0 + s]) for s in range(_DMA_UNROLL)]
        for s, (r0, r1) in enumerate(rows):
            tok_ref[r0] = t0 + s
            tok_ref[r1] = t0 + s
        return 0

    lax.fori_loop(0, n_tok // _DMA_UNROLL, body, 0)


def _invert(d0, d1, n_valid, ntl, cnt, start, p_rows):
    n_tok = d0.shape[0]
    kern = functools.partial(_invert_kernel, n_tok=n_tok, p_rows=p_rows)
    return pl.pallas_call(
        kern,
        grid_spec=pltpu.PrefetchScalarGridSpec(
            num_scalar_prefetch=6,
            grid=(1,),
            in_specs=[],
            out_specs=pl.BlockSpec(memory_space=pltpu.SMEM)),
        out_shape=jax.ShapeDtypeStruct((p_rows,), I32),
        compiler_params=_cparams(("arbitrary",)),
        name="invert",
    )(d0, d1, n_valid, ntl, cnt, start)


_FFN_SLOTS = 3
_BULK_DMA_PRIORITY = 1


def _ffn_kernel(te_ref, nv_ref, ntl_ref, tok_ref, h_ref, w1_ref, w3_ref, w2_ref, y_ref,
                xbuf, xb_ref, w1f, w3f, w2f, w1b, w3b, w2b, gsem, wsem, run_ref, *, n_tiles):
    i = pl.program_id(0)
    nv = nv_ref[0]
    valid = i < nv
    slot = i % _FFN_SLOTS
    nc = _token_pitch(xb_ref.shape[1])

    def row_copy(tile, r, dst_slot):
        tok = tok_ref[tile * FFN_TILE + r]
        src = pl.ds(tok * nc, nc)
        return pltpu.make_async_copy(h_ref.at[src], xbuf.at[dst_slot, pl.ds(r * nc, nc)], gsem.at[dst_slot])

    def weight_copies(e, ws):
        return (pltpu.make_async_copy(w1_ref.at[e], w1f.at[ws], wsem.at[ws]),
                pltpu.make_async_copy(w3_ref.at[e], w3f.at[ws], wsem.at[ws]),
                pltpu.make_async_copy(w2_ref.at[e], w2f.at[ws], wsem.at[ws]))

    @pl.when(i == 0)
    def _():
        run_ref[0] = 0
        for cp in weight_copies(te_ref[0], 0):
            cp.start(priority=_BULK_DMA_PRIORITY)

        def first_rows(u, _):
            for s in range(_DMA_UNROLL):
                for t in range(_FFN_SLOTS - 1):
                    row_copy(t, u * _DMA_UNROLL + s, t).start()
            return 0
        lax.fori_loop(0, FFN_TILE // _DMA_UNROLL, first_rows, 0)

    @pl.when(valid & ((i == 0) | (te_ref[i] != te_ref[jnp.maximum(i, 1) - 1])))
    def _():
        run = run_ref[0]
        ws = run % 2
        for cp in weight_copies(te_ref[i], ws):
            cp.wait()
        w1b[...] = w1f[ws].astype(BF16)
        w3b[...] = w3f[ws].astype(BF16)
        w2b[...] = w2f[ws].astype(BF16)
        nxt = i + ntl_ref[te_ref[i]]

        @pl.when(nxt < nv)
        def _():
            for cp in weight_copies(te_ref[jnp.minimum(nxt, n_tiles - 1)], 1 - ws):
                cp.start(priority=_BULK_DMA_PRIORITY)
        run_ref[0] = run + 1

    @pl.when(valid)
    def _():
        pltpu.make_async_copy(h_ref.at[pl.ds(0, FFN_TILE * nc)], xbuf.at[slot], gsem.at[slot]).wait()
        xb_ref[...] = _load_token_major(xbuf.at[slot], FFN_TILE, xb_ref.shape[1], BF16)
        ahead = _FFN_SLOTS - 1
        nxt_tile = jnp.minimum(i + ahead, n_tiles - 1)
        nxt_slot = (i + ahead) % _FFN_SLOTS
        for r in range(FFN_TILE):
            row_copy(nxt_tile, r, nxt_slot).start()
        x = xb_ref[...]
        a = jnp.dot(x, w1b[...], preferred_element_type=F32)
        b = jnp.dot(x, w3b[...], preferred_element_type=F32)
        hmid = (_silu(a) * b).astype(BF16)
        y_ref[...] = jnp.dot(hmid, w2b[...], preferred_element_type=F32)

    @pl.when(jnp.logical_not(valid))
    def _():
        y_ref[...] = jnp.zeros_like(y_ref)

    @pl.when(i == n_tiles - 1)
    def _():
        for t in range(_FFN_SLOTS - 1):
            s = (nv + t) % _FFN_SLOTS
            pltpu.make_async_copy(h_ref.at[pl.ds(0, FFN_TILE * nc)], xbuf.at[s], gsem.at[s]).wait()


def _ffn(tile_e, n_valid, ntl, row_tok, h2, w1, w3, w2):
    p_rows = row_tok.shape[0]
    _, d, de = w1.shape
    n_tiles = p_rows // FFN_TILE
    kern = functools.partial(_ffn_kernel, n_tiles=n_tiles)
    any_spec = pl.BlockSpec(memory_space=pl.ANY)
    return pl.pallas_call(
        kern,
        grid_spec=pltpu.PrefetchScalarGridSpec(
            num_scalar_prefetch=4,
            grid=(n_tiles,),
            in_specs=[any_spec, any_spec, any_spec, any_spec],
            out_specs=pl.BlockSpec((FFN_TILE, d), lambda i, *_: (i, 0)),
            scratch_shapes=[pltpu.VMEM((_FFN_SLOTS, FFN_TILE * _token_pitch(d), LANES), F32),
                            pltpu.VMEM((FFN_TILE, d), BF16),
                            pltpu.VMEM((2, d, de), F32), pltpu.VMEM((2, d, de), F32),
                            pltpu.VMEM((2, de, d), F32),
                            pltpu.VMEM((d, de), BF16), pltpu.VMEM((d, de), BF16),
                            pltpu.VMEM((de, d), BF16),
                            pltpu.SemaphoreType.DMA((_FFN_SLOTS,)), pltpu.SemaphoreType.DMA((2,)),
                            pltpu.SMEM((1,), I32)]),
        out_shape=jax.ShapeDtypeStruct((p_rows, d), F32),
        compiler_params=_cparams(("arbitrary",)),
        name="ffn",
    )(tile_e, n_valid, ntl, row_tok, h2, w1, w3, w2)


def _combine_kernel(d0_ref, d1_ref, x1_ref, info_ref, mod_ref, nw_ref, y_ref, o_ref,
                    *scratch, tp, n_phases):
    i = pl.program_id(0)
    ns = _COMBINE_SETS
    sem = scratch[-1]
    bufs = [(scratch[2 * q], scratch[2 * q + 1]) for q in range(ns)]

    def row_copies(phase, r, q):
        tok = phase * tp + r
        ya, yb = bufs[q]
        return (pltpu.make_async_copy(y_ref.at[pl.ds(d0_ref[tok], 1)], ya.at[pl.ds(r, 1)], sem.at[q]),
                pltpu.make_async_copy(y_ref.at[pl.ds(d1_ref[tok], 1)], yb.at[pl.ds(r, 1)], sem.at[q]))

    def wait_set(q):
        for buf in bufs[q]:
            pltpu.make_async_copy(y_ref.at[pl.ds(0, tp)], buf, sem.at[q]).wait()

    @pl.when(i == 0)
    def _():
        def first_rows(u, _):
            for s in range(_DMA_UNROLL):
                for q in range(ns - 1):
                    for cp in row_copies(q, u * _DMA_UNROLL + s, q):
                        cp.start()
            return 0
        lax.fori_loop(0, tp // _DMA_UNROLL, first_rows, 0)

    for q in range(ns):
        wait_set(q)
        ahead = jnp.minimum(i * ns + q + ns - 1, n_phases - 1)
        for r in range(tp):
            for prio, cp in enumerate(row_copies(ahead, r, (q + ns - 1) % ns)):
                cp.start(priority=prio)
        ya, yb = bufs[q]
        rows = pl.ds(q * tp, tp)
        w0 = _col_from_row(info_ref[_L_W0:_L_W0 + 1, q * tp:(q + 1) * tp])
        w1 = _col_from_row(info_ref[_L_W1:_L_W1 + 1, q * tp:(q + 1) * tp])
        moe = ya[...] * w0 + yb[...] * w1
        x2 = x1_ref[rows, :] + mod_ref[0][5:6] * moe
        o_ref[rows, :] = x2 * lax.rsqrt(jnp.mean(x2 * x2, axis=-1, keepdims=True) + NORM_EPS) * nw_ref[...]

    @pl.when(i == n_phases // ns - 1)
    def _():
        for q in range(ns - 1):
            wait_set(q)


_COMBINE_SETS = 4


def _combine(d0, d1, x1, info, mod3, nfw, y, seq, tp=LANES):
    assert tp == LANES, "one phase's combine weights must fill exactly one lane-dense row"
    n, d = x1.shape
    tc = _COMBINE_SETS * tp
    per_b = seq // tc
    kern = functools.partial(_combine_kernel, tp=tp, n_phases=n // tp)
    return pl.pallas_call(
        kern,
        grid_spec=pltpu.PrefetchScalarGridSpec(
            num_scalar_prefetch=2,
            grid=(n // tc,),
            in_specs=[pl.BlockSpec((tc, d), lambda i, d0, d1: (i, 0)),
                      pl.BlockSpec((SUBLANES, tc), lambda i, d0, d1: (0, i)),
                      pl.BlockSpec((1, N_MOD, d), lambda i, d0, d1: (i // per_b, 0, 0)),
                      pl.BlockSpec((1, d), lambda i, d0, d1: (0, 0)),
                      pl.BlockSpec(memory_space=pl.ANY)],
            out_specs=pl.BlockSpec((tc, d), lambda i, d0, d1: (i, 0)),
            scratch_shapes=[pltpu.VMEM((tp, d), F32)] * (2 * _COMBINE_SETS)
                           + [pltpu.SemaphoreType.DMA((_COMBINE_SETS,))]),
        out_shape=jax.ShapeDtypeStruct((n, d), F32),
        compiler_params=_cparams(("arbitrary",)),
        name="combine",
    )(d0, d1, x1, info, mod3, nfw, y)


def _split_hi_lo(w):
    hi = w.astype(BF16)
    lo = (w - hi.astype(F32)).astype(BF16)
    return hi, lo


def kernel(x, c, w_ada, b_ada, norm_mix_w, w_in, ret_norm_w, w_out, norm_ffn_w,
           w_group, b_group, w_router, b_router, w1, w3, w2, norm_final_w):
    batch, seq, d = x.shape
    n = batch * seq
    depth = w_ada.shape[0]
    assert depth == 1, "the final rmsnorm is fused into the (single) layer's last kernel"
    tm = min(INPROJ_ROWS, seq)
    half = LANES // 2

    slopes = jnp.exp2(-8.0 * jnp.arange(1, N_ATTN_HEADS + 1, dtype=F32) / N_ATTN_HEADS)
    log_gamma = jnp.log1p(-jnp.exp2(-5.0 - jnp.arange(RET_HEADS, dtype=F32)))
    n_tiles = (2 * n) // FFN_TILE + N_EXPERTS
    p_rows = n_tiles * FFN_TILE

    assert batch <= SUBLANES
    c_pad = jnp.zeros((SUBLANES, d), F32).at[:batch].set(c)
    x2d = x.reshape(n, d)
    for l in range(depth):
        b_ada_row = b_ada[l].reshape(1, -1)
        mod_mix = _adaln(c_pad, w_ada[l], b_ada_row, 2 * d)
        mod_in = mod_mix[:batch].reshape(batch, 2, d)

        proj, kmean = _inproj(x2d, mod_in, norm_mix_w[l].reshape(1, d), w_in[l], seq, tm)
        kmean = kmean.reshape(batch, seq // MOBA_BLOCK, ATTN_WIDTH)
        attn = _moba(proj, kmean, slopes, batch, seq)
        ret, mod_rest = _retention(proj, log_gamma, ret_norm_w[l].reshape(1, RET_WIDTH),
                                   c_pad, w_ada[l], b_ada_row, 2 * d, batch, seq)
        mod3 = jnp.concatenate([mod_mix[:batch], mod_rest[:batch]], axis=1).reshape(batch, N_MOD, d)

        gap = _ROUTER_EXPERT_LANE - N_GROUPS
        wr = jnp.concatenate([w_group[l], jnp.zeros((d, gap), F32),
                              jnp.transpose(w_router[l], (1, 0, 2)).reshape(d, N_EXPERTS)], axis=1)
        wr = jnp.pad(wr, ((0, 0), (0, half - wr.shape[1])))
        wr_hi, wr_lo = _split_hi_lo(wr)
        wr_cat = jnp.concatenate([jnp.concatenate([wr_hi, wr_lo], axis=1),
                                  jnp.concatenate([wr_hi, jnp.zeros_like(wr_lo)], axis=1)], axis=0)
        br = jnp.concatenate([b_group[l], jnp.zeros((gap,), F32), b_router[l].reshape(-1)])
        br = jnp.pad(br, (0, LANES - br.shape[0])).reshape(1, LANES)

        x1, h2p, logits = _outproj(attn, ret, w_out[l], x2d, mod3,
                                   norm_ffn_w[l].reshape(1, d), wr_cat, br, seq, min(OUTPROJ_ROWS, seq))

        info, dest, meta = _plan(logits, n_tiles)
        d0 = dest[_L_D0]
        d1 = dest[_L_D1]
        tile_e = meta[_M_TILE_E, :n_tiles]
        n_valid = meta[_M_NVALID, :1]
        ntl = meta[_M_NTILES, :N_EXPERTS]

        row_tok = _invert(d0, d1, n_valid, ntl, meta[_M_COUNT, :N_EXPERTS],
                          meta[_M_START, :N_EXPERTS], p_rows)
        y = _ffn(tile_e, n_valid, ntl, row_tok, h2p, w1[l], w3[l], w2[l])
        nfw = norm_final_w.reshape(1, d)
        x2d = _combine(d0, d1, x1, info, mod3, nfw, y, seq)
    return x2d.reshape(batch, seq, d)
```

```python
import functools

import jax
import jax.numpy as jnp
from jax import lax
from jax.experimental import pallas as pl
from jax.experimental.pallas import tpu as pltpu

F32 = jnp.float32
BF16 = jnp.bfloat16
I32 = jnp.int32

N_ATTN_HEADS = 8
ATTN_HEAD_DIM = 128
ATTN_WIDTH = N_ATTN_HEADS * ATTN_HEAD_DIM
MOBA_BLOCK = 256
MOBA_TOPK = 3
RET_HEADS = 4
RET_HEAD_DIM = 256
RET_WIDTH = RET_HEADS * RET_HEAD_DIM
N_GROUPS = 4
EXPERTS_PER_GROUP = 8
N_EXPERTS = N_GROUPS * EXPERTS_PER_GROUP
N_MOD = 6
NORM_EPS = 1e-6

LANES = 128
SUBLANES = 8
RET_CHUNK = 256
RET_CHUNKS_PER_STEP = 2
FFN_TILE = 256
INPROJ_ROWS = 1024
OUTPROJ_ROWS = 256
MASK_VALUE = -1e30
LOG2E = 1.4426950408889634
MOBA_QK_SCALE = ATTN_HEAD_DIM ** -0.5 * LOG2E
_MOBA_EXTRA_ROWS = 16
VMEM_LIMIT = 56 * 1024 * 1024

_DN_LAST = (((1,), (1,)), ((), ()))
_DN_FIRST = (((0,), (0,)), ((), ()))


def _cparams(sem):
    return pltpu.CompilerParams(dimension_semantics=sem, vmem_limit_bytes=VMEM_LIMIT)


def _silu(v):
    return v * jax.nn.sigmoid(v)


def _token_pitch(d):
    return d // LANES + 1


def _store_token_major(ref, value):
    rows, d = value.shape
    pitch = _token_pitch(d)
    for c in range(pitch - 1):
        ref[pl.ds(c, rows, stride=pitch), :] = value[:, c * LANES:(c + 1) * LANES]
    ref[pl.ds(pitch - 1, rows, stride=pitch), :] = jnp.zeros((rows, LANES), value.dtype)


def _load_token_major(ref, rows, d, dtype):
    pitch = _token_pitch(d)
    return jnp.concatenate([ref[pl.ds(c, rows, stride=pitch), :].astype(dtype) for c in range(pitch - 1)],
                           axis=1)


def _adaln_kernel(c_ref, w_ref, b_ref, o_ref):
    ca = _silu(c_ref[...]).astype(BF16)
    o_ref[...] = jnp.dot(ca, w_ref[...].astype(BF16), preferred_element_type=F32) + b_ref[...]


def _adaln(c_pad, w_ada, b_ada, tn=1024):
    rows, d = c_pad.shape
    n = w_ada.shape[1]
    return pl.pallas_call(
        _adaln_kernel,
        grid=(n // tn,),
        in_specs=[pl.BlockSpec((rows, d), lambda j: (0, 0)),
                  pl.BlockSpec((d, tn), lambda j: (0, j)),
                  pl.BlockSpec((1, tn), lambda j: (0, j))],
        out_specs=pl.BlockSpec((rows, tn), lambda j: (0, j)),
        out_shape=jax.ShapeDtypeStruct((rows, n), F32),
        compiler_params=_cparams(("arbitrary",)),
        name="adaln",
    )(c_pad, w_ada, b_ada)


def _inproj_kernel(x_ref, mod_ref, nw_ref, w_ref, o_ref, km_ref, h_ref, *, tm, tn):
    j = pl.program_id(1)

    @pl.when(j == 0)
    def _():
        x = x_ref[...]
        y = x * lax.rsqrt(jnp.mean(x * x, axis=-1, keepdims=True) + NORM_EPS) * nw_ref[...]
        m = mod_ref[0]
        h_ref[...] = (y * (1.0 + m[1:2]) + m[0:1]).astype(BF16)

    acc = jnp.dot(h_ref[...], w_ref[...].astype(BF16), preferred_element_type=F32)
    o_ref[...] = (acc * jnp.where(j == 0, MOBA_QK_SCALE, 1.0)).astype(BF16)

    @pl.when(j == 1)
    def _():
        km_ref[0] = jnp.sum(acc.reshape(tm // MOBA_BLOCK, MOBA_BLOCK, tn), axis=1) * (1.0 / MOBA_BLOCK)


def _inproj(x2d, mod3, nw, w_bf, seq, tm, tn=ATTN_WIDTH):
    n, d = x2d.shape
    width = w_bf.shape[1]
    per_b = seq // tm
    kern = functools.partial(_inproj_kernel, tm=tm, tn=tn)
    return pl.pallas_call(
        kern,
        grid=(n // tm, width // tn),
        in_specs=[pl.BlockSpec((tm, d), lambda i, j: (i, 0)),
                  pl.BlockSpec((1, N_MOD, d), lambda i, j: (i // per_b, 0, 0)),
                  pl.BlockSpec((1, d), lambda i, j: (0, 0)),
                  pl.BlockSpec((d, tn), lambda i, j: (0, j))],
        out_specs=[pl.BlockSpec((tm, tn), lambda i, j: (i, j)),
                   pl.BlockSpec((1, tm // MOBA_BLOCK, tn), lambda i, j: (i, 0, 0))],
        out_shape=[jax.ShapeDtypeStruct((n, width), BF16),
                   jax.ShapeDtypeStruct((n // tm, tm // MOBA_BLOCK, tn), F32)],
        scratch_shapes=[pltpu.VMEM((tm, d), BF16)],
        compiler_params=_cparams(("arbitrary", "arbitrary")),
        name="inproj",
    )(x2d, mod3, nw, w_bf)


def _moba_kernel(slopes_ref, q_ref, k_ref, v_ref, km_ref, o_ref,
                 vt_ref, sel_ref, bias_ref, acc_ref, *, nb, hps):
    hg = pl.program_id(1)
    i = pl.program_id(2)
    blk = MOBA_BLOCK
    dh = ATTN_HEAD_DIM
    log2e = LOG2E

    qpos = lax.broadcasted_iota(I32, (blk, blk), 1)
    kpos = lax.broadcasted_iota(I32, (blk, blk), 0)

    vt_rows = dh + _MOBA_EXTRA_ROWS

    @pl.when(i == 0)
    def _():
        vt = v_ref[...].astype(F32).T.astype(BF16)
        extra = lax.broadcasted_iota(I32, (_MOBA_EXTRA_ROWS, vt.shape[1]), 0)
        ones_row = jnp.where(extra == 0, 1.0, 0.0).astype(BF16)
        key_pos = lax.broadcasted_iota(I32, (blk, LANES), 0).astype(F32)
        for g in range(hps):
            vt_ref[g * vt_rows:g * vt_rows + dh, :] = vt[g * dh:(g + 1) * dh, :]
            vt_ref[g * vt_rows + dh:(g + 1) * vt_rows, :] = ones_row
            bias_ref[g] = key_pos * (log2e * slopes_ref[hg * hps + g])

    bidx = lax.broadcasted_iota(I32, (nb, blk), 0)
    past = bidx < i
    i0 = pl.multiple_of(i * blk, blk)
    heads = [slice(g * dh, (g + 1) * dh) for g in range(hps)]
    vrows = [slice(g * vt_rows, (g + 1) * vt_rows) for g in range(hps)]
    gates, raw_own = [], []
    for cols in heads:
        q = q_ref[:, cols]
        km = km_ref[0, :, cols]
        km_hi = km.astype(BF16)
        km_lo = (km - km_hi.astype(F32)).astype(BF16)
        gates.append(lax.dot_general(km_hi, q, _DN_LAST, preferred_element_type=F32)
                     + lax.dot_general(km_lo, q, _DN_LAST, preferred_element_type=F32))
        raw_own.append(lax.dot_general(k_ref[pl.ds(i0, blk), cols], q, _DN_LAST,
                                       preferred_element_type=F32))
    def key_bias(g):
        kb = bias_ref[g]
        return jnp.concatenate([kb] * (blk // LANES), axis=1)

    own_bias = [key_bias(g) for g in range(hps)]

    carry0, acc0, sels = [], [], []
    for g, cols in enumerate(heads):
        gate = jnp.where(past, gates[g], -jnp.inf)
        rank = jnp.zeros((nb, blk), F32)
        for m in range(nb):
            gm = gate[m:m + 1, :]
            beats = jnp.where(gm > gate, 1.0, jnp.where((gm == gate) & (m < bidx), 1.0, 0.0))
            rank = rank + jnp.where(m < i, beats, 0.0)
        sels.append(jnp.where(past & (rank < MOBA_TOPK), 1.0, 0.0))

        s = jnp.where(qpos >= kpos, own_bias[g] + raw_own[g], MASK_VALUE)
        m0 = jnp.max(s, axis=0, keepdims=True)
        p = jnp.exp2(s - m0)
        acc0.append(jnp.dot(vt_ref[vrows[g], pl.ds(i0, blk)], p.astype(BF16), preferred_element_type=F32))
        carry0.append(m0)
    for g in range(hps):
        acc_ref[g] = acc0[g]
        sel_ref[g] = sels[g]

    def body(j, carry):
        j0 = pl.multiple_of(j * blk, blk)
        off = ((i - j) * blk).astype(F32) * log2e
        raw = [lax.dot_general(k_ref[pl.ds(j0, blk), cols], q_ref[:, cols], _DN_LAST,
                               preferred_element_type=F32) for cols in heads]
        chosen = [sel_ref[g, pl.ds(j, 1), :] > 0.0 for g in range(hps)]
        acc_old = [acc_ref[g] for g in range(hps)]
        out, acc_new = [], []
        for g in range(hps):
            m_run = carry[g]
            shift = -slopes_ref[hg * hps + g] * off
            s = key_bias(g) + raw[g]
            m_new = jnp.maximum(m_run, jnp.where(chosen[g], jnp.max(s, axis=0, keepdims=True) + shift,
                                                 MASK_VALUE))
            alpha = jnp.exp2(m_run - m_new)
            p = jnp.exp2(s - jnp.where(chosen[g], m_new - shift, -MASK_VALUE))
            acc_new.append(alpha * acc_old[g] + jnp.dot(vt_ref[vrows[g], pl.ds(j0, blk)], p.astype(BF16),
                                                        preferred_element_type=F32))
            out.append(m_new)
        for g in range(hps):
            acc_ref[g] = acc_new[g]
        return tuple(out)

    lax.fori_loop(0, i, body, tuple(carry0))
    for g in range(hps):
        acc = acc_ref[g]
        o_ref[:, g * dh:(g + 1) * dh] = (acc[:dh] / acc[dh:dh + 1]).T.astype(BF16)


MOBA_HEADS_PER_STEP = 8


def _moba(proj, kmean, slopes, batch, seq):
    n = proj.shape[0]
    nb = seq // MOBA_BLOCK
    hps = MOBA_HEADS_PER_STEP
    vt_rows = ATTN_HEAD_DIM + _MOBA_EXTRA_ROWS
    w = hps * ATTN_HEAD_DIM
    ng = N_ATTN_HEADS // hps
    kern = functools.partial(_moba_kernel, nb=nb, hps=hps)
    return pl.pallas_call(
        kern,
        grid=(batch, ng, nb),
        in_specs=[pl.BlockSpec(memory_space=pltpu.SMEM),
                  pl.BlockSpec((MOBA_BLOCK, w), lambda b, h, i: (b * nb + i, h)),
                  pl.BlockSpec((seq, w), lambda b, h, i: (b, ng + h)),
                  pl.BlockSpec((seq, w), lambda b, h, i: (b, 2 * ng + h)),
                  pl.BlockSpec((1, nb, w), lambda b, h, i: (b, 0, h))],
        out_specs=pl.BlockSpec((MOBA_BLOCK, w), lambda b, h, i: (b * nb + i, h)),
        out_shape=jax.ShapeDtypeStruct((n, ATTN_WIDTH), BF16),
        scratch_shapes=[pltpu.VMEM((hps * vt_rows, seq), BF16),
                        pltpu.VMEM((hps, nb, MOBA_BLOCK), F32),
                        pltpu.VMEM((hps, MOBA_BLOCK, LANES), F32),
                        pltpu.VMEM((hps, vt_rows, MOBA_BLOCK), F32)],
        compiler_params=_cparams(("arbitrary", "arbitrary", "arbitrary")),
        name="moba",
    )(slopes, proj, proj, proj, kmean)


def _ret_kernel(lg_ref, q_ref, k_ref, v_ref, g_ref, rnw_ref, o_ref,
                state_ref, dmask_ref, qdec_ref, kdec_ref, cdec_ref):
    n = pl.program_id(1)
    c = RET_CHUNK
    dk = RET_HEAD_DIM
    kscale = dk ** -0.5

    @pl.when(n == 0)
    def _():
        state_ref[...] = jnp.zeros_like(state_ref)
        ti = lax.broadcasted_iota(I32, (c, c), 0)
        si = lax.broadcasted_iota(I32, (c, c), 1)
        diff = jnp.maximum(ti - si, 0).astype(F32)
        pos = lax.broadcasted_iota(I32, (c, dk), 0).astype(F32)
        for h in range(RET_HEADS):
            lg = lg_ref[h]
            dmask_ref[h] = jnp.where(ti >= si, jnp.exp(diff * lg) * kscale, 0.0)
            qdec_ref[h] = jnp.exp((pos + 1.0) * lg)
            kdec_ref[h] = jnp.exp((c - 1.0 - pos) * lg) * kscale
            cdec_ref[h] = jnp.exp(jnp.full((1, dk), float(c), F32) * lg)

    heads = [slice(h * dk, (h + 1) * dk) for h in range(RET_HEADS)]
    chunks = [slice(u * c, (u + 1) * c) for u in range(RET_CHUNKS_PER_STEP)]
    qs = [[q_ref[rows, cols] for cols in heads] for rows in chunks]
    ks = [[k_ref[rows, cols] for cols in heads] for rows in chunks]
    vs = [[v_ref[rows, cols] for cols in heads] for rows in chunks]
    states = [state_ref[h] for h in range(RET_HEADS)]
    raw = [[lax.dot_general(q, k, _DN_LAST, preferred_element_type=F32) for q, k in zip(qu, ku)]
           for qu, ku in zip(qs, ks)]
    outs, new_states = [], []
    for h, cols in enumerate(heads):
        state = states[h]
        for u, rows in enumerate(chunks):
            cross = jnp.dot(qs[u][h], state.astype(BF16), preferred_element_type=F32)
            inner = jnp.dot((raw[u][h] * dmask_ref[h]).astype(BF16), vs[u][h], preferred_element_type=F32)
            kd = (ks[u][h].astype(F32) * kdec_ref[h]).astype(BF16)
            kv = lax.dot_general(kd, vs[u][h], _DN_FIRST, preferred_element_type=F32)
            state = state * cdec_ref[h] + kv
            r = inner + cross * qdec_ref[h]
            rn = r * lax.rsqrt(jnp.mean(r * r, axis=-1, keepdims=True) + NORM_EPS)
            g = g_ref[rows, cols].astype(F32)
            outs.append((rows, cols, (rn * rnw_ref[:, cols] * _silu(g)).astype(BF16)))
        new_states.append(state)
    for h in range(RET_HEADS):
        state_ref[h] = new_states[h]
    for rows, cols, out in outs:
        o_ref[rows, cols] = out


def _retention(proj, log_gamma, rnw, batch, seq):
    n = proj.shape[0]
    c = RET_CHUNK
    rows = c * RET_CHUNKS_PER_STEP
    nc = seq // rows
    dk = RET_HEAD_DIM
    base = 3 * ATTN_WIDTH // RET_WIDTH

    def col(off):
        return lambda b, t: (b * nc + t, base + off)

    return pl.pallas_call(
        _ret_kernel,
        grid=(batch, nc),
        in_specs=[pl.BlockSpec(memory_space=pltpu.SMEM),
                  pl.BlockSpec((rows, RET_WIDTH), col(0)),
                  pl.BlockSpec((rows, RET_WIDTH), col(1)),
                  pl.BlockSpec((rows, RET_WIDTH), col(2)),
                  pl.BlockSpec((rows, RET_WIDTH), col(3)),
                  pl.BlockSpec((1, RET_WIDTH), lambda b, t: (0, 0))],
        out_specs=pl.BlockSpec((rows, RET_WIDTH), lambda b, t: (b * nc + t, 0)),
        out_shape=jax.ShapeDtypeStruct((n, RET_WIDTH), BF16),
        scratch_shapes=[pltpu.VMEM((RET_HEADS, dk, dk), F32), pltpu.VMEM((RET_HEADS, c, c), F32),
                        pltpu.VMEM((RET_HEADS, c, dk), F32), pltpu.VMEM((RET_HEADS, c, dk), F32),
                        pltpu.VMEM((RET_HEADS, 1, dk), F32)],
        compiler_params=_cparams(("arbitrary", "arbitrary")),
        name="retention",
    )(log_gamma, proj, proj, proj, proj, rnw)


def _outproj_kernel(a_ref, r_ref, w_ref, x_ref, mod_ref, nw_ref, wr_ref, br_ref,
                    x1_ref, hp_ref, lg_ref, wb_ref):
    @pl.when(pl.program_id(0) == 0)
    def _():
        wb_ref[...] = w_ref[...].astype(BF16)

    mix = jnp.dot(jnp.concatenate([a_ref[...], r_ref[...]], axis=1), wb_ref[...],
                  preferred_element_type=F32)
    m = mod_ref[0]
    x1 = x_ref[...] + m[2:3] * mix
    x1_ref[...] = x1
    y = x1 * lax.rsqrt(jnp.mean(x1 * x1, axis=-1, keepdims=True) + NORM_EPS) * nw_ref[...]
    h2 = y * (1.0 + m[4:5]) + m[3:4]
    _store_token_major(hp_ref, h2)
    hi = h2.astype(BF16)
    lo = (h2 - hi.astype(F32)).astype(BF16)
    part = jnp.dot(jnp.concatenate([hi, lo], axis=1), wr_ref[...], preferred_element_type=F32)
    lg_ref[...] = part + pltpu.roll(part, LANES // 2, axis=1) + br_ref[...]


def _outproj(attn, ret, w_bf, x2d, mod3, nw, wr_cat, br, seq, tm):
    n, d = x2d.shape
    per_b = seq // tm
    return pl.pallas_call(
        _outproj_kernel,
        grid=(n // tm,),
        in_specs=[pl.BlockSpec((tm, ATTN_WIDTH), lambda i: (i, 0)),
                  pl.BlockSpec((tm, RET_WIDTH), lambda i: (i, 0)),
                  pl.BlockSpec((d, d), lambda i: (0, 0), pipeline_mode=pl.Buffered(1)),
                  pl.BlockSpec((tm, d), lambda i: (i, 0)),
                  pl.BlockSpec((1, N_MOD, d), lambda i: (i // per_b, 0, 0)),
                  pl.BlockSpec((1, d), lambda i: (0, 0)),
                  pl.BlockSpec((2 * d, LANES), lambda i: (0, 0)),
                  pl.BlockSpec((1, LANES), lambda i: (0, 0))],
        out_specs=[pl.BlockSpec((tm, d), lambda i: (i, 0)),
                   pl.BlockSpec((tm * _token_pitch(d), LANES), lambda i: (i, 0)),
                   pl.BlockSpec((tm, LANES), lambda i: (i, 0))],
        out_shape=[jax.ShapeDtypeStruct((n, d), F32),
                   jax.ShapeDtypeStruct((n * _token_pitch(d), LANES), F32),
                   jax.ShapeDtypeStruct((n, LANES), F32)],
        scratch_shapes=[pltpu.VMEM((d, d), BF16)],
        compiler_params=_cparams(("arbitrary",)),
        name="outproj",
    )(attn, ret, w_bf, x2d, mod3, nw, wr_cat, br)


_L_E0, _L_E1, _L_W0, _L_W1, _L_D0, _L_D1 = 0, 1, 2, 3, 4, 5
_ROUTER_EXPERT_LANE = 8
_M_TILE_E, _M_NVALID, _M_NTILES, _M_COUNT, _M_START = 0, 1, 2, 3, 4
_PLAN_ROWS = 256
_PLAN_GROUP = 4


def _first_row_where(cond, rows, limit):
    return jnp.min(jnp.where(cond, rows, limit), axis=0, keepdims=True)


def _col_from_row(v):
    r = lax.broadcasted_iota(I32, (LANES, LANES), 0)
    c = lax.broadcasted_iota(I32, (LANES, LANES), 1)
    return jnp.sum(jnp.where(r == c, v, 0.0), axis=1, keepdims=True)


def _row_from_col(v):
    r = lax.broadcasted_iota(I32, (LANES, LANES), 0)
    c = lax.broadcasted_iota(I32, (LANES, LANES), 1)
    return jnp.sum(jnp.where(r == c, v, 0.0), axis=0, keepdims=True)


def _rows_to_tile(rows, row8):
    tile = 0.0
    for k in reversed(range(len(rows))):
        tile = jnp.where(row8 == k, rows[k], tile)
    return tile


def _plan_kernel(lg_ref, info_ref, dest_ref, meta_ref, *, n_tok):
    tb = _PLAN_ROWS
    n_g, n_e = N_GROUPS, EXPERTS_PER_GROUP
    assert n_g <= SUBLANES and n_e == SUBLANES, "group and per-group logits each fill one sublane tile"
    no_row = float(SUBLANES)
    row8 = lax.broadcasted_iota(I32, (SUBLANES, tb), 0).astype(F32)
    row_e = lax.broadcasted_iota(I32, (N_EXPERTS, tb), 0).astype(F32)
    earlier = lax.broadcasted_iota(I32, (tb, tb), 0) < lax.broadcasted_iota(I32, (tb, tb), 1)
    tri = jnp.where(earlier, 1.0, 0.0).astype(BF16)

    def route_block(r0):
        lt = lg_ref[pl.ds(r0, tb), :].T
        gl = jnp.where(row8 < n_g, lt[0:SUBLANES, :], -jnp.inf)
        ge = jnp.exp(gl - jnp.max(gl, axis=0, keepdims=True))
        gp = ge / jnp.sum(ge, axis=0, keepdims=True)
        g_w = jnp.max(gp, axis=0, keepdims=True)
        g_sel = _first_row_where((gp == g_w) & (row8 < n_g), row8, no_row)
        el = lt[_ROUTER_EXPERT_LANE:_ROUTER_EXPERT_LANE + n_e, :]
        for g in range(1, n_g):
            lo = _ROUTER_EXPERT_LANE + g * n_e
            el = jnp.where(g_sel == g, lt[lo:lo + n_e, :], el)
        ee = jnp.exp(el - jnp.max(el, axis=0, keepdims=True))
        ep = ee / jnp.sum(ee, axis=0, keepdims=True)
        p1 = jnp.max(ep, axis=0, keepdims=True)
        i1 = _first_row_where(ep == p1, row8, no_row)
        ep2 = jnp.where(row8 == i1, -1.0, ep)
        p2 = jnp.max(ep2, axis=0, keepdims=True)
        i2 = _first_row_where(ep2 == p2, row8, no_row)
        denom = p1 + p2
        w0 = g_w * p1 / denom
        w1 = g_w * p2 / denom
        e0 = g_sel * n_e + i1
        e1 = g_sel * n_e + i2
        oh0 = jnp.where(row_e == e0, 1.0, 0.0)
        oh1 = jnp.where(row_e == e1, 1.0, 0.0)
        oh = oh0 + oh1
        local = jnp.dot(oh.astype(BF16), tri, preferred_element_type=F32)
        return (e0, e1, w0, w1), oh0, oh1, local, jnp.sum(oh, axis=1, keepdims=True)

    group = _PLAN_GROUP if (n_tok // tb) % _PLAN_GROUP == 0 else 1

    def route(it, carry):
        starts = [pl.multiple_of((it * group + k) * tb, tb) for k in range(group)]
        blocks = [route_block(r0) for r0 in starts]
        tiles = []
        for (e0, e1, w0, w1), oh0, oh1, local, total in blocks:
            before = local + carry
            rank0 = jnp.sum(oh0 * before, axis=0, keepdims=True)
            rank1 = jnp.sum(oh1 * before, axis=0, keepdims=True)
            tiles.append(_rows_to_tile([e0, e1, w0, w1, rank0, rank1], row8))
            carry = carry + total
        for r0, tile in zip(starts, tiles):
            info_ref[:, pl.ds(r0, tb)] = tile
        return carry

    counts_col = lax.fori_loop(0, n_tok // (tb * group), route, jnp.zeros((N_EXPERTS, 1), F32))
    counts = _row_from_col(jnp.concatenate([counts_col, jnp.zeros((LANES - N_EXPERTS, 1), F32)], axis=0))

    lane1 = lax.broadcasted_iota(I32, (1, LANES), 1)
    padded = jnp.floor((counts + (FFN_TILE - 1.0)) * (1.0 / FFN_TILE)) * FFN_TILE
    pad_end = padded
    sh = 1
    while sh < N_EXPERTS:
        pad_end = pad_end + jnp.where(lane1 >= sh, pltpu.roll(pad_end, sh, axis=1), 0.0)
        sh *= 2
    pad_start = pad_end - padded

    start_col = _col_from_row(pad_start)[0:N_EXPERTS, :]

    def place(it, _):
        starts = [pl.multiple_of((it * group + k) * tb, tb) for k in range(group)]
        tiles = [info_ref[:, pl.ds(r0, tb)] for r0 in starts]
        placed = []
        for tile in tiles:
            s0 = jnp.sum(jnp.where(row_e == tile[_L_E0:_L_E0 + 1, :], start_col, 0.0), axis=0, keepdims=True)
            s1 = jnp.sum(jnp.where(row_e == tile[_L_E1:_L_E1 + 1, :], start_col, 0.0), axis=0, keepdims=True)
            placed.append(tile + jnp.where(row8 == _L_D0, s0, jnp.where(row8 == _L_D1, s1, 0.0)))
        for r0, tile in zip(starts, placed):
            info_ref[:, pl.ds(r0, tb)] = tile
            dest_ref[:, pl.ds(r0, tb)] = tile.astype(I32)
        return 0

    lax.fori_loop(0, n_tok // (tb * group), place, 0)

    pe_col = _col_from_row(pad_end)
    mt = meta_ref.shape[1]
    t_start = lax.broadcasted_iota(I32, (LANES, mt), 1).astype(F32) * FFN_TILE
    e_row = lax.broadcasted_iota(I32, (LANES, mt), 0)
    ended = jnp.where((pe_col <= t_start) & (e_row < N_EXPERTS), 1.0, 0.0)
    tile_e = jnp.minimum(jnp.sum(ended, axis=0, keepdims=True), N_EXPERTS - 1.0)
    total = jnp.sum(jnp.where(lane1 == N_EXPERTS - 1, pad_end, 0.0), axis=1, keepdims=True)
    n_valid = jnp.broadcast_to(total * (1.0 / FFN_TILE), (1, mt))
    def per_expert(v):
        return v if mt == LANES else jnp.concatenate([v, jnp.zeros((1, mt - LANES), F32)], axis=1)

    mrow = lax.broadcasted_iota(I32, (SUBLANES, mt), 0)
    meta = jnp.where(mrow == _M_TILE_E, tile_e,
           jnp.where(mrow == _M_NVALID, n_valid,
           jnp.where(mrow == _M_NTILES, per_expert(padded * (1.0 / FFN_TILE)),
           jnp.where(mrow == _M_COUNT, per_expert(counts),
           jnp.where(mrow == _M_START, per_expert(pad_start), 0.0)))))
    meta_ref[...] = meta.astype(I32)


def _plan(logits, n_tiles):
    n_tok = logits.shape[0]
    mt = -(-n_tiles // LANES) * LANES
    kern = functools.partial(_plan_kernel, n_tok=n_tok)
    return pl.pallas_call(
        kern,
        out_shape=[jax.ShapeDtypeStruct((SUBLANES, n_tok), F32),
                   jax.ShapeDtypeStruct((SUBLANES, n_tok), I32),
                   jax.ShapeDtypeStruct((SUBLANES, mt), I32)],
        compiler_params=pltpu.CompilerParams(vmem_limit_bytes=VMEM_LIMIT),
        name="plan",
    )(logits)


_DMA_UNROLL = 8


def _invert_kernel(d0_ref, d1_ref, nv_ref, ntl_ref, cnt_ref, start_ref, tok_ref, *, n_tok, p_rows):
    def zero_rows(lo, hi):
        def zbody(u, _):
            for s in range(_DMA_UNROLL):
                tok_ref[jnp.minimum(lo + u * _DMA_UNROLL + s, hi - 1)] = 0
            return 0
        lax.fori_loop(0, (hi - lo + _DMA_UNROLL - 1) // _DMA_UNROLL, zbody, 0)

    def pad_body(e, _):
        zero_rows(start_ref[e] + cnt_ref[e], start_ref[e] + ntl_ref[e] * FFN_TILE)
        return 0

    lax.fori_loop(0, N_EXPERTS, pad_body, 0)
    zero_rows(nv_ref[0] * FFN_TILE, p_rows)

    def body(u, _):
        t0 = u * _DMA_UNROLL
        rows = [(d0_ref[t0 + s], d1_ref[t0 + s]) for s in range(_DMA_UNROLL)]
        for s, (r0, r1) in enumerate(rows):
            tok_ref[r0] = t0 + s
            tok_ref[r1] = t0 + s
        return 0

    lax.fori_loop(0, n_tok // _DMA_UNROLL, body, 0)


def _invert(d0, d1, n_valid, ntl, cnt, start, p_rows):
    n_tok = d0.shape[0]
    kern = functools.partial(_invert_kernel, n_tok=n_tok, p_rows=p_rows)
    return pl.pallas_call(
        kern,
        grid_spec=pltpu.PrefetchScalarGridSpec(
            num_scalar_prefetch=6,
            grid=(1,),
            in_specs=[],
            out_specs=pl.BlockSpec(memory_space=pltpu.SMEM)),
        out_shape=jax.ShapeDtypeStruct((p_rows,), I32),
        compiler_params=_cparams(("arbitrary",)),
        name="invert",
    )(d0, d1, n_valid, ntl, cnt, start)


_FFN_SLOTS = 3
_BULK_DMA_PRIORITY = 1


def _ffn_kernel(te_ref, nv_ref, ntl_ref, tok_ref, h_ref, w1_ref, w3_ref, w2_ref, y_ref,
                xbuf, xb_ref, w1f, w3f, w2f, w1b, w3b, w2b, gsem, wsem, run_ref, *, n_tiles):
    i = pl.program_id(0)
    nv = nv_ref[0]
    valid = i < nv
    slot = i % _FFN_SLOTS
    nc = _token_pitch(xb_ref.shape[1])

    def row_copy(tile, r, dst_slot):
        tok = tok_ref[tile * FFN_TILE + r]
        src = pl.ds(tok * nc, nc)
        return pltpu.make_async_copy(h_ref.at[src], xbuf.at[dst_slot, pl.ds(r * nc, nc)], gsem.at[dst_slot])

    def weight_copies(e, ws):
        return (pltpu.make_async_copy(w1_ref.at[e], w1f.at[ws], wsem.at[ws]),
                pltpu.make_async_copy(w3_ref.at[e], w3f.at[ws], wsem.at[ws]),
                pltpu.make_async_copy(w2_ref.at[e], w2f.at[ws], wsem.at[ws]))

    @pl.when(i == 0)
    def _():
        run_ref[0] = 0
        for cp in weight_copies(te_ref[0], 0):
            cp.start(priority=_BULK_DMA_PRIORITY)

        def first_rows(u, _):
            for s in range(_DMA_UNROLL):
                for t in range(_FFN_SLOTS - 1):
                    row_copy(t, u * _DMA_UNROLL + s, t).start()
            return 0
        lax.fori_loop(0, FFN_TILE // _DMA_UNROLL, first_rows, 0)

    @pl.when(valid & ((i == 0) | (te_ref[i] != te_ref[jnp.maximum(i, 1) - 1])))
    def _():
        run = run_ref[0]
        ws = run % 2
        for cp in weight_copies(te_ref[i], ws):
            cp.wait()
        w1b[...] = w1f[ws].astype(BF16)
        w3b[...] = w3f[ws].astype(BF16)
        w2b[...] = w2f[ws].astype(BF16)
        nxt = i + ntl_ref[te_ref[i]]

        @pl.when(nxt < nv)
        def _():
            for cp in weight_copies(te_ref[jnp.minimum(nxt, n_tiles - 1)], 1 - ws):
                cp.start(priority=_BULK_DMA_PRIORITY)
        run_ref[0] = run + 1

    @pl.when(valid)
    def _():
        pltpu.make_async_copy(h_ref.at[pl.ds(0, FFN_TILE * nc)], xbuf.at[slot], gsem.at[slot]).wait()
        xb_ref[...] = _load_token_major(xbuf.at[slot], FFN_TILE, xb_ref.shape[1], BF16)
        ahead = _FFN_SLOTS - 1
        nxt_tile = jnp.minimum(i + ahead, n_tiles - 1)
        nxt_slot = (i + ahead) % _FFN_SLOTS
        for r in range(FFN_TILE):
            row_copy(nxt_tile, r, nxt_slot).start()
        x = xb_ref[...]
        a = jnp.dot(x, w1b[...], preferred_element_type=F32)
        b = jnp.dot(x, w3b[...], preferred_element_type=F32)
        hmid = (_silu(a) * b).astype(BF16)
        y_ref[...] = jnp.dot(hmid, w2b[...], preferred_element_type=F32)

    @pl.when(jnp.logical_not(valid))
    def _():
        y_ref[...] = jnp.zeros_like(y_ref)

    @pl.when(i == n_tiles - 1)
    def _():
        for t in range(_FFN_SLOTS - 1):
            s = (nv + t) % _FFN_SLOTS
            pltpu.make_async_copy(h_ref.at[pl.ds(0, FFN_TILE * nc)], xbuf.at[s], gsem.at[s]).wait()


def _ffn(tile_e, n_valid, ntl, row_tok, h2, w1, w3, w2):
    p_rows = row_tok.shape[0]
    _, d, de = w1.shape
    n_tiles = p_rows // FFN_TILE
    kern = functools.partial(_ffn_kernel, n_tiles=n_tiles)
    any_spec = pl.BlockSpec(memory_space=pl.ANY)
    return pl.pallas_call(
        kern,
        grid_spec=pltpu.PrefetchScalarGridSpec(
            num_scalar_prefetch=4,
            grid=(n_tiles,),
            in_specs=[any_spec, any_spec, any_spec, any_spec],
            out_specs=pl.BlockSpec((FFN_TILE, d), lambda i, *_: (i, 0)),
            scratch_shapes=[pltpu.VMEM((_FFN_SLOTS, FFN_TILE * _token_pitch(d), LANES), F32),
                            pltpu.VMEM((FFN_TILE, d), BF16),
                            pltpu.VMEM((2, d, de), F32), pltpu.VMEM((2, d, de), F32),
                            pltpu.VMEM((2, de, d), F32),
                            pltpu.VMEM((d, de), BF16), pltpu.VMEM((d, de), BF16),
                            pltpu.VMEM((de, d), BF16),
                            pltpu.SemaphoreType.DMA((_FFN_SLOTS,)), pltpu.SemaphoreType.DMA((2,)),
                            pltpu.SMEM((1,), I32)]),
        out_shape=jax.ShapeDtypeStruct((p_rows, d), F32),
        compiler_params=_cparams(("arbitrary",)),
        name="ffn",
    )(tile_e, n_valid, ntl, row_tok, h2, w1, w3, w2)


def _combine_kernel(d0_ref, d1_ref, x1_ref, info_ref, mod_ref, nw_ref, y_ref, o_ref,
                    *scratch, tp, n_phases):
    i = pl.program_id(0)
    ns = _COMBINE_SETS
    sem = scratch[-1]
    bufs = [(scratch[2 * q], scratch[2 * q + 1]) for q in range(ns)]

    def row_copies(phase, r, q):
        tok = phase * tp + r
        ya, yb = bufs[q]
        return (pltpu.make_async_copy(y_ref.at[pl.ds(d0_ref[tok], 1)], ya.at[pl.ds(r, 1)], sem.at[q]),
                pltpu.make_async_copy(y_ref.at[pl.ds(d1_ref[tok], 1)], yb.at[pl.ds(r, 1)], sem.at[q]))

    def wait_set(q):
        for buf in bufs[q]:
            pltpu.make_async_copy(y_ref.at[pl.ds(0, tp)], buf, sem.at[q]).wait()

    @pl.when(i == 0)
    def _():
        def first_rows(u, _):
            for s in range(_DMA_UNROLL):
                for q in range(ns - 1):
                    for cp in row_copies(q, u * _DMA_UNROLL + s, q):
                        cp.start()
            return 0
        lax.fori_loop(0, tp // _DMA_UNROLL, first_rows, 0)

    for q in range(ns):
        wait_set(q)
        ahead = jnp.minimum(i * ns + q + ns - 1, n_phases - 1)
        for r in range(tp):
            for prio, cp in enumerate(row_copies(ahead, r, (q + ns - 1) % ns)):
                cp.start(priority=prio)
        ya, yb = bufs[q]
        rows = pl.ds(q * tp, tp)
        w0 = _col_from_row(info_ref[_L_W0:_L_W0 + 1, q * tp:(q + 1) * tp])
        w1 = _col_from_row(info_ref[_L_W1:_L_W1 + 1, q * tp:(q + 1) * tp])
        moe = ya[...] * w0 + yb[...] * w1
        x2 = x1_ref[rows, :] + mod_ref[0][5:6] * moe
        o_ref[rows, :] = x2 * lax.rsqrt(jnp.mean(x2 * x2, axis=-1, keepdims=True) + NORM_EPS) * nw_ref[...]

    @pl.when(i == n_phases // ns - 1)
    def _():
        for q in range(ns - 1):
            wait_set(q)


_COMBINE_SETS = 4


def _combine(d0, d1, x1, info, mod3, nfw, y, seq, tp=LANES):
    assert tp == LANES, "one phase's combine weights must fill exactly one lane-dense row"
    n, d = x1.shape
    tc = _COMBINE_SETS * tp
    per_b = seq // tc
    kern = functools.partial(_combine_kernel, tp=tp, n_phases=n // tp)
    return pl.pallas_call(
        kern,
        grid_spec=pltpu.PrefetchScalarGridSpec(
            num_scalar_prefetch=2,
            grid=(n // tc,),
            in_specs=[pl.BlockSpec((tc, d), lambda i, d0, d1: (i, 0)),
                      pl.BlockSpec((SUBLANES, tc), lambda i, d0, d1: (0, i)),
                      pl.BlockSpec((1, N_MOD, d), lambda i, d0, d1: (i // per_b, 0, 0)),
                      pl.BlockSpec((1, d), lambda i, d0, d1: (0, 0)),
                      pl.BlockSpec(memory_space=pl.ANY)],
            out_specs=pl.BlockSpec((tc, d), lambda i, d0, d1: (i, 0)),
            scratch_shapes=[pltpu.VMEM((tp, d), F32)] * (2 * _COMBINE_SETS)
                           + [pltpu.SemaphoreType.DMA((_COMBINE_SETS,))]),
        out_shape=jax.ShapeDtypeStruct((n, d), F32),
        compiler_params=_cparams(("arbitrary",)),
        name="combine",
    )(d0, d1, x1, info, mod3, nfw, y)


def _split_hi_lo(w):
    hi = w.astype(BF16)
    lo = (w - hi.astype(F32)).astype(BF16)
    return hi, lo


def kernel(x, c, w_ada, b_ada, norm_mix_w, w_in, ret_norm_w, w_out, norm_ffn_w,
           w_group, b_group, w_router, b_router, w1, w3, w2, norm_final_w):
    batch, seq, d = x.shape
    n = batch * seq
    depth = w_ada.shape[0]
    assert depth == 1, "the final rmsnorm is fused into the (single) layer's last kernel"
    tm = min(INPROJ_ROWS, seq)
    half = LANES // 2

    slopes = jnp.exp2(-8.0 * jnp.arange(1, N_ATTN_HEADS + 1, dtype=F32) / N_ATTN_HEADS)
    log_gamma = jnp.log1p(-jnp.exp2(-5.0 - jnp.arange(RET_HEADS, dtype=F32)))
    n_tiles = (2 * n) // FFN_TILE + N_EXPERTS
    p_rows = n_tiles * FFN_TILE

    assert batch <= SUBLANES
    c_pad = jnp.zeros((SUBLANES, d), F32).at[:batch].set(c)
    x2d = x.reshape(n, d)
    for l in range(depth):
        mod = _adaln(c_pad, w_ada[l], b_ada[l].reshape(1, -1))
        mod3 = mod[:batch].reshape(batch, N_MOD, d)

        proj, kmean = _inproj(x2d, mod3, norm_mix_w[l].reshape(1, d), w_in[l], seq, tm)
        kmean = kmean.reshape(batch, seq // MOBA_BLOCK, ATTN_WIDTH)
        attn = _moba(proj, kmean, slopes, batch, seq)
        ret = _retention(proj, log_gamma, ret_norm_w[l].reshape(1, RET_WIDTH), batch, seq)

        gap = _ROUTER_EXPERT_LANE - N_GROUPS
        wr = jnp.concatenate([w_group[l], jnp.zeros((d, gap), F32),
                              jnp.transpose(w_router[l], (1, 0, 2)).reshape(d, N_EXPERTS)], axis=1)
        wr = jnp.pad(wr, ((0, 0), (0, half - wr.shape[1])))
        wr_hi, wr_lo = _split_hi_lo(wr)
        wr_cat = jnp.concatenate([jnp.concatenate([wr_hi, wr_lo], axis=1),
                                  jnp.concatenate([wr_hi, jnp.zeros_like(wr_lo)], axis=1)], axis=0)
        br = jnp.concatenate([b_group[l], jnp.zeros((gap,), F32), b_router[l].reshape(-1)])
        br = jnp.pad(br, (0, LANES - br.shape[0])).reshape(1, LANES)

        x1, h2p, logits = _outproj(attn, ret, w_out[l], x2d, mod3,
                                   norm_ffn_w[l].reshape(1, d), wr_cat, br, seq, min(OUTPROJ_ROWS, seq))

        info, dest, meta = _plan(logits, n_tiles)
        d0 = dest[_L_D0]
        d1 = dest[_L_D1]
        tile_e = meta[_M_TILE_E, :n_tiles]
        n_valid = meta[_M_NVALID, :1]
        ntl = meta[_M_NTILES, :N_EXPERTS]

        row_tok = _invert(d0, d1, n_valid, ntl, meta[_M_COUNT, :N_EXPERTS],
                          meta[_M_START, :N_EXPERTS], p_rows)
        y = _ffn(tile_e, n_valid, ntl, row_tok, h2p, w1[l], w3[l], w2[l])
        nfw = norm_final_w.reshape(1, d)
        x2d = _combine(d0, d1, x1, info, mod3, nfw, y, seq)
    return x2d.reshape(batch, seq, d)
```
